```python
import math
import jax, jax.numpy as jnp
from jax import lax
import numpy as np

D_MODEL = 1024
BATCH = 16
SEQ = 2048
DEPTH = 4
DEC_BATCH = 8
DEC_SEQ = 32
PAST_LEN = 4096

CHUNK = 64
N_A = DEPTH // 2
N_B = DEPTH - N_A
HEAD_DIM = 64
N_HEADS = D_MODEL // HEAD_DIM
DECAY_LORA = 64
AAA_LORA = 64
GATE_LORA = 128
N_GROUPS = 4
EXP_PER_GROUP = 4
N_EXPERTS = N_GROUPS * EXP_PER_GROUP
EXP_HIDDEN = 256
TOP_K_INNER = 2
Q_BLOCK = 128
FORGET_BIAS = 2.0
ALPHA = (2.0 * DEPTH) ** 0.25
BETA = (8.0 * DEPTH) ** -0.25
LN_EPS = 1e-5
GN_EPS = 64e-5
F32 = jnp.float32

kernel_name = 'yoco_rwkv7_fox_hmoe_step'


def _layer_norm(z, g, b):
    zf = z.astype(F32)
    mu = jnp.mean(zf, -1, keepdims=True)
    var = jnp.mean(jnp.square(zf - mu), -1, keepdims=True)
    return ((zf - mu) * lax.rsqrt(var + LN_EPS) * g + b).astype(z.dtype)


def _adaln(c, w, b):
    m = jax.nn.silu(c) @ w + b
    return jnp.split(m[:, None, :], 6, axis=-1)


def _rwkv7(h, shift_prev, s0, mu, w_rkv, w0, w1, w2, a0, a1, a2, g1, g2, k_k, k_a, r_k,
           lnx_g, lnx_b, w_o):
    bsz, t, d = h.shape
    xx = jnp.concatenate([shift_prev[:, None, :], h[:, :-1]], axis=1) - h
    mix = lambda i: h + xx * mu[i]
    r = mix(0) @ w_rkv[0]
    k = mix(1) @ w_rkv[1]
    v = mix(2) @ w_rkv[2]
    w_log = -jax.nn.softplus(-(w0 + jnp.tanh(mix(3) @ w1) @ w2).astype(F32)) - 0.5
    decay = jnp.exp(-jnp.exp(w_log))
    a = jax.nn.sigmoid((a0 + (mix(4) @ a1) @ a2).astype(F32))
    g = jax.nn.sigmoid(mix(5) @ g1) @ g2
    heads = lambda z: z.reshape(bsz, t, N_HEADS, HEAD_DIM).astype(F32)
    kk = heads(k * k_k)
    kk = kk * lax.rsqrt(jnp.maximum(jnp.sum(kk * kk, -1, keepdims=True), 1e-24))
    k = k.astype(F32) * (1.0 + (a - 1.0) * k_a)
    r, k, v, a, decay = heads(r), heads(k), heads(v), heads(a), heads(decay)

    def step(s, inp):
        r_t, w_t, k_t, v_t, kk_t, b_t = inp
        sa = jnp.einsum('bhvk,bhk->bhv', s, -kk_t)
        s = s * w_t[:, :, None, :] + sa[..., None] * b_t[:, :, None, :] + v_t[..., None] * k_t[:, :, None, :]
        return s, jnp.einsum('bhvk,bhk->bhv', s, r_t)

    seq = tuple(jnp.swapaxes(z, 0, 1) for z in (r, decay, k, v, kk, kk * a))
    s_t, ys = lax.scan(step, s0.astype(F32), seq)
    y = jnp.swapaxes(ys, 0, 1)
    ym = jnp.mean(y, -1, keepdims=True)
    yv = jnp.mean(jnp.square(y - ym), -1, keepdims=True)
    y = ((y - ym) * lax.rsqrt(yv + GN_EPS)).reshape(bsz, t, d) * lnx_g + lnx_b
    bonus = jnp.sum(r * k * r_k, -1, keepdims=True) * v
    y = (y + bonus.reshape(bsz, t, d)) * g
    return y.astype(h.dtype) @ w_o, h[:, -1], s_t.astype(s0.dtype)


def _hmoe(h, rg_w, rg_b, re_w, re_b, wg, wu, wd):
    bsz, t, d = h.shape
    xf = h.reshape(-1, d)
    n = xf.shape[0]
    gl = (xf @ rg_w + rg_b).astype(F32)
    gsel = jnp.argmax(gl, -1)
    gprob = jnp.max(jax.nn.softmax(gl, -1), -1, keepdims=True)
    el = (xf @ re_w + re_b).astype(F32).reshape(n, N_GROUPS, EXP_PER_GROUP)
    el_sel = el[jnp.arange(n), gsel]
    top_v, top_i = lax.top_k(el_sel, TOP_K_INNER)
    top_w = jax.nn.softmax(top_v, -1) * gprob
    eid = gsel[:, None] * EXP_PER_GROUP + top_i
    comb = jnp.sum(jax.nn.one_hot(eid, N_EXPERTS, dtype=F32) * top_w[..., None], 1)
    hid = jax.nn.silu(jnp.einsum('nd,edf->nef', xf, wg)) * jnp.einsum('nd,edf->nef', xf, wu)
    hid = hid * comb[:, :, None].astype(hid.dtype)
    return jnp.einsum('nef,efd->nd', hid, wd).reshape(bsz, t, d)


def _fox_attend(q, k, v, fq, fk, q_pos, k_pos):
    s = jnp.einsum('bqhd,bkhd->bhqk', q, k).astype(F32) * (HEAD_DIM ** -0.5)
    s = s + jnp.swapaxes(fq, 1, 2)[:, :, :, None] - jnp.swapaxes(fk, 1, 2)[:, :, None, :]
    s = jnp.where(k_pos[None, :] <= q_pos[:, None], s, -jnp.inf)
    p = jax.nn.softmax(s, axis=-1)
    return jnp.einsum('bhqk,bkhd->bqhd', p.astype(v.dtype), v)


def _fox_prompt(q, k, v, logf):
    t = q.shape[1]
    fcum = jnp.cumsum(logf, axis=1)
    pos = jnp.arange(t)
    outs = []
    for i in range(t // Q_BLOCK):
        lo, hi = i * Q_BLOCK, (i + 1) * Q_BLOCK
        outs.append(_fox_attend(q[:, lo:hi], k[:, :hi], v[:, :hi], fcum[:, lo:hi], fcum[:, :hi],
                                pos[lo:hi], pos[:hi]))
    return jnp.concatenate(outs, axis=1)


def _fox_sample(q, k_new, v_new, logf_new, ck, cv, clf):
    p_len, t = ck.shape[1], q.shape[1]
    clf = clf.astype(F32)
    f_new = jnp.cumsum(logf_new, axis=1)
    suf = jnp.flip(jnp.cumsum(jnp.flip(clf, 1), 1), 1) - clf
    fk = jnp.concatenate([-suf, f_new], axis=1)
    k = jnp.concatenate([ck.astype(k_new.dtype), k_new], axis=1)
    v = jnp.concatenate([cv.astype(v_new.dtype), v_new], axis=1)
    return _fox_attend(q, k, v, f_new, fk, p_len + jnp.arange(t), jnp.arange(p_len + t))


def _trunk(x, c, shift_in, wkv_in, cache, p):
    bsz, t, d = x.shape
    shifts, states = [], []
    k_sh = v_sh = lf_sh = None
    for l in range(DEPTH):
        sh1, sc1, gt1, sh2, sc2, gt2 = _adaln(c, p['ada_w'][l], p['ada_b'][l])
        h = x * (1 + sc1) + sh1
        if l < N_A:
            y, last, s_new = _rwkv7(h, shift_in[l], wkv_in[l], p['a_mu'][l], p['a_w_rkv'][l],
                                    p['a_w0'][l], p['a_w1'][l], p['a_w2'][l], p['a_a0'][l],
                                    p['a_a1'][l], p['a_a2'][l], p['a_g1'][l], p['a_g2'][l],
                                    p['a_k_k'][l], p['a_k_a'][l], p['a_r_k'][l],
                                    p['a_lnx_g'][l], p['a_lnx_b'][l], p['a_w_o'][l])
            shifts.append(last)
            states.append(s_new)
        else:
            if k_sh is None:
                kv = x @ p['kv_w']
                k_sh = kv[..., :d].reshape(bsz, t, N_HEADS, HEAD_DIM)
                v_sh = kv[..., d:].reshape(bsz, t, N_HEADS, HEAD_DIM)
                lf_sh = jax.nn.log_sigmoid((x @ p['f_w'] + p['f_b']).astype(F32))
            j = l - N_A
            q = (h @ p['b_wq'][j]).reshape(bsz, t, N_HEADS, HEAD_DIM)
            if cache is None:
                o = _fox_prompt(q, k_sh, v_sh, lf_sh)
            else:
                o = _fox_sample(q, k_sh, v_sh, lf_sh, cache[0], cache[1], cache[2])
            y = o.reshape(bsz, t, d) @ p['b_wo'][j]
        x = _layer_norm(ALPHA * x + gt1 * y, p['ln_g'][l, 0], p['ln_b'][l, 0])
        h = x * (1 + sc2) + sh2
        y = _hmoe(h, p['rg_w'][l], p['rg_b'][l], p['re_w'][l], p['re_b'][l],
                  p['exp_wg'][l], p['exp_wu'][l], p['exp_wd'][l])
        x = _layer_norm(ALPHA * x + gt2 * y, p['ln_g'][l, 1], p['ln_b'][l, 1])
    return x, jnp.stack(shifts), jnp.stack(states), k_sh, v_sh, lf_sh.astype(x.dtype)


def setup_inputs(seed: int = 0) -> dict:
    key = jax.random.key(seed)
    it = iter(jax.random.split(key, 64))
    d, h, n = D_MODEL, N_HEADS, HEAD_DIM

    def nrm(shape, scale):
        return scale * jax.random.normal(next(it), shape, F32)

    return {
        'x_prompt': nrm((BATCH, SEQ, d), 1.0),
        'x_sample': nrm((DEC_BATCH, DEC_SEQ, d), 1.0),
        'state_shift': nrm((N_A, DEC_BATCH, d), 1.0),
        'state_wkv': nrm((N_A, DEC_BATCH, h, n, n), 0.1),
        'cache_k': nrm((DEC_BATCH, PAST_LEN, h, n), 1.0),
        'cache_v': nrm((DEC_BATCH, PAST_LEN, h, n), BETA),
        'cache_logf': jax.nn.log_sigmoid(FORGET_BIAS + nrm((DEC_BATCH, PAST_LEN, h), 1.0)),
        'c_prompt': nrm((BATCH, d), 1.0),
        'c_sample': nrm((DEC_BATCH, d), 1.0),
        'ada_w': nrm((DEPTH, d, 6 * d), 0.5 * d ** -0.5),
        'ada_b': nrm((DEPTH, 6 * d), 0.02),
        'ln_g': 1.0 + nrm((DEPTH, 2, d), 0.02),
        'ln_b': nrm((DEPTH, 2, d), 0.02),
        'a_mu': jax.random.uniform(next(it), (N_A, 6, d), F32),
        'a_w_rkv': nrm((N_A, 3, d, d), d ** -0.5) * jnp.array([1.0, 1.0, BETA], F32)[:, None, None],
        'a_w0': jax.random.uniform(next(it), (N_A, d), F32, minval=-5.0, maxval=0.0),
        'a_w1': nrm((N_A, d, DECAY_LORA), d ** -0.5),
        'a_w2': nrm((N_A, DECAY_LORA, d), 0.5 * DECAY_LORA ** -0.5),
        'a_a0': nrm((N_A, d), 0.1),
        'a_a1': nrm((N_A, d, AAA_LORA), d ** -0.5),
        'a_a2': nrm((N_A, AAA_LORA, d), 0.5 * AAA_LORA ** -0.5),
        'a_g1': nrm((N_A, d, GATE_LORA), d ** -0.5),
        'a_g2': nrm((N_A, GATE_LORA, d), GATE_LORA ** -0.5),
        'a_k_k': 0.85 + nrm((N_A, d), 0.05),
        'a_k_a': 1.0 + nrm((N_A, d), 0.05),
        'a_r_k': nrm((N_A, h, n), 0.1),
        'a_lnx_g': 1.0 + nrm((N_A, d), 0.02),
        'a_lnx_b': nrm((N_A, d), 0.02),
        'a_w_o': nrm((N_A, d, d), BETA * d ** -0.5),
        'kv_w': nrm((d, 2 * d), d ** -0.5) * jnp.concatenate([jnp.ones((d,), F32), jnp.full((d,), BETA, F32)])[None, :],
        'f_w': nrm((d, h), d ** -0.5),
        'f_b': FORGET_BIAS + nrm((h,), 0.5),
        'b_wq': nrm((N_B, d, d), d ** -0.5),
        'b_wo': nrm((N_B, d, d), BETA * d ** -0.5),
        'rg_w': nrm((DEPTH, d, N_GROUPS), d ** -0.5),
        'rg_b': nrm((DEPTH, N_GROUPS), 0.01),
        're_w': nrm((DEPTH, d, N_EXPERTS), d ** -0.5),
        're_b': nrm((DEPTH, N_EXPERTS), 0.01),
        'exp_wg': nrm((DEPTH, N_EXPERTS, d, EXP_HIDDEN), d ** -0.5),
        'exp_wu': nrm((DEPTH, N_EXPERTS, d, EXP_HIDDEN), d ** -0.5),
        'exp_wd': nrm((DEPTH, N_EXPERTS, EXP_HIDDEN, d), BETA * EXP_HIDDEN ** -0.5),
    }


def reference(x_prompt, x_sample, state_shift, state_wkv, cache_k, cache_v, cache_logf,
              c_prompt, c_sample, ada_w, ada_b, ln_g, ln_b, a_mu, a_w_rkv, a_w0, a_w1, a_w2,
              a_a0, a_a1, a_a2, a_g1, a_g2, a_k_k, a_k_a, a_r_k, a_lnx_g, a_lnx_b, a_w_o,
              kv_w, f_w, f_b, b_wq, b_wo, rg_w, rg_b, re_w, re_b, exp_wg, exp_wu, exp_wd):
    p = dict(ada_w=ada_w, ada_b=ada_b, ln_g=ln_g, ln_b=ln_b, a_mu=a_mu, a_w_rkv=a_w_rkv,
             a_w0=a_w0, a_w1=a_w1, a_w2=a_w2, a_a0=a_a0, a_a1=a_a1, a_a2=a_a2, a_g1=a_g1,
             a_g2=a_g2, a_k_k=a_k_k, a_k_a=a_k_a, a_r_k=a_r_k, a_lnx_g=a_lnx_g,
             a_lnx_b=a_lnx_b, a_w_o=a_w_o, kv_w=kv_w, f_w=f_w, f_b=f_b, b_wq=b_wq, b_wo=b_wo,
             rg_w=rg_w, rg_b=rg_b, re_w=re_w, re_b=re_b, exp_wg=exp_wg, exp_wu=exp_wu,
             exp_wd=exp_wd)
    bp = x_prompt.shape[0]
    zero_shift = jnp.zeros((N_A, bp, D_MODEL), x_prompt.dtype)
    zero_wkv = jnp.zeros((N_A, bp, N_HEADS, HEAD_DIM, HEAD_DIM), x_prompt.dtype)
    y_prompt, p_shift, p_wkv, p_k, p_v, p_logf = _trunk(x_prompt, c_prompt, zero_shift, zero_wkv, None, p)
    y_sample, s_shift, s_wkv, s_k, s_v, s_logf = _trunk(x_sample, c_sample, state_shift, state_wkv,
                                                      (cache_k, cache_v, cache_logf), p)
    return (y_prompt, y_sample, p_shift, p_wkv, p_k, p_v, p_logf, s_shift, s_wkv, s_k, s_v, s_logf)
```

```python
import functools

import jax
import jax.numpy as jnp
from jax import lax
from jax.experimental import pallas as pl
from jax.experimental.pallas import tpu as pltpu

F32 = jnp.float32
BF16 = jnp.bfloat16

HEAD_DIM = 64
LANES = 128
N_GROUPS = 4
EXP_PER_GROUP = 4
N_EXPERTS = N_GROUPS * EXP_PER_GROUP
LN_EPS = 1e-5
GN_EPS = 64e-5
WKV_CHUNK = 64
VMEM_LIMIT = 56 * 1024 * 1024
NEG_BIG = -1e30


def _params(sem):
    return pltpu.CompilerParams(dimension_semantics=sem, vmem_limit_bytes=VMEM_LIMIT)


def _dg(a, b, ca, cb):
    return lax.dot_general(a, b, (((ca,), (cb,)), ((), ())), preferred_element_type=F32)


def _bdot(a, b, ca=1, cb=0):
    return _dg(a.astype(BF16), b.astype(BF16), ca, cb)


def _split3(x):
    hi = x.astype(BF16)
    r1 = x - hi.astype(F32)
    mid = r1.astype(BF16)
    lo = (r1 - mid.astype(F32)).astype(BF16)
    return hi, mid, lo


def _dot_exact_rhs(a, b_exact, ca=1, cb=0):
    hi, mid, lo = _split3(a)
    bb = b_exact.astype(BF16)
    return _dg(hi, bb, ca, cb) + _dg(mid, bb, ca, cb) + _dg(lo, bb, ca, cb)


def _dot_exact_lhs(a_exact, b, ca=1, cb=0):
    hi, mid, lo = _split3(b)
    aa = a_exact.astype(BF16)
    return _dg(aa, hi, ca, cb) + _dg(aa, mid, ca, cb) + _dg(aa, lo, ca, cb)


def _dot3(a, b, ca=1, cb=0):
    ah = a.astype(BF16)
    al = (a - ah.astype(F32)).astype(BF16)
    bh = b.astype(BF16)
    bl = (b - bh.astype(F32)).astype(BF16)
    return _dg(ah, bh, ca, cb) + _dg(ah, bl, ca, cb) + _dg(al, bh, ca, cb)


def _layer_norm(z, g, b):
    mu = jnp.mean(z, -1, keepdims=True)
    d = z - mu
    var = jnp.mean(d * d, -1, keepdims=True)
    return d * lax.rsqrt(var + LN_EPS) * g + b


def _softplus(z):
    return jnp.maximum(z, 0.0) + jnp.log(1.0 + jnp.exp(-jnp.abs(z)))


def _sigmoid(z):
    return 1.0 / (1.0 + jnp.exp(-z))


def _silu(z):
    return z * _sigmoid(z)


def _ada_kernel(c_ref, w_ref, b_ref, o_ref):
    o_ref[...] = _dot3(_silu(c_ref[...]), w_ref[...]) + b_ref[...]


def _ada_mods(c_all, ada_w, ada_b):
    depth, d, d6 = ada_w.shape
    bsz = c_all.shape[0]
    tn = d
    out = pl.pallas_call(
        _ada_kernel,
        grid=(depth, d6 // tn),
        in_specs=[
            pl.BlockSpec((bsz, d), lambda l, j: (0, 0)),
            pl.BlockSpec((None, d, tn), lambda l, j: (l, 0, j)),
            pl.BlockSpec((None, 1, tn), lambda l, j: (l, 0, j)),
        ],
        out_specs=pl.BlockSpec((None, bsz, tn), lambda l, j: (l, 0, j)),
        out_shape=jax.ShapeDtypeStruct((depth, bsz, d6), F32),
        compiler_params=_params(("parallel", "parallel")),
        name="ada_mods",
    )(c_all, ada_w, ada_b.reshape(depth, 1, d6))
    return out.reshape(depth, bsz, 6, d)


def _rwkv_proj_kernel(x_ref, xp_ref, shift_ref, mods_ref, mu_ref, vec_ref, wrkv_ref, w1_ref, w2_ref,
                      a1_ref, a2_ref, g1_ref, g2_ref, hd_ref, hu_ref,
                      r_ref, lw_ref, k_ref, v_ref, kk_ref, b_ref, g_ref, last_ref):
    tm = x_ref.shape[0]
    sh1 = mods_ref[0:1, :]
    sc1 = mods_ref[1:2, :]
    h = x_ref[...] * (1.0 + sc1) + sh1
    h_prev_tile = xp_ref[7:8, :] * (1.0 + sc1) + sh1
    prev_row = jnp.where(pl.program_id(1) == 0, shift_ref[...], h_prev_tile)
    row = lax.broadcasted_iota(jnp.int32, (tm, 1), 0)
    xx = jnp.where(row == 0, prev_row, pltpu.roll(h, 1, axis=0)) - h

    def mix(i):
        return h + xx * mu_ref[i:i + 1, :]

    w0, a0 = vec_ref[0:1, :], vec_ref[1:2, :]
    k_k, k_a = vec_ref[2:3, :], vec_ref[3:4, :]
    r = _bdot(mix(0), wrkv_ref[0])
    k = _bdot(mix(1), wrkv_ref[1])
    v = _bdot(mix(2), wrkv_ref[2])
    ww = w0 + _bdot(jnp.tanh(_bdot(mix(3), w1_ref[...])), w2_ref[...])
    w_log = -_softplus(-ww) - 0.5
    a = _sigmoid(a0 + _bdot(_bdot(mix(4), a1_ref[...]), a2_ref[...]))
    g = _bdot(_sigmoid(_bdot(mix(5), g1_ref[...])), g2_ref[...])
    kk = k * k_k
    ss = _dot_exact_rhs(_dot_exact_rhs(kk * kk, hd_ref[...]), hu_ref[...])
    kk = kk * lax.rsqrt(jnp.maximum(ss, 1e-24))
    r_ref[...] = r
    lw_ref[...] = -jnp.exp(w_log)
    k_ref[...] = k * (1.0 + (a - 1.0) * k_a)
    v_ref[...] = v
    kk_ref[...] = kk
    b_ref[...] = kk * a
    g_ref[...] = g
    last_ref[...] = h[tm - 1:tm, :]


def _rwkv_proj(x2d, shift_prev, mods_l, wl, bsz, t, tm):
    n, d = x2d.shape
    nt = t // tm
    tok = pl.BlockSpec((tm, d), lambda b, i: (b * nt + i, 0))
    full = lambda a: pl.BlockSpec(a.shape, lambda b, i: (0,) * a.ndim)
    weights = [wl["mu"], wl["vec"], wl["w_rkv"], wl["w1"], wl["w2"], wl["a1"], wl["a2"], wl["g1"],
               wl["g2"], wl["head_down"], wl["head_up"]]
    outs = pl.pallas_call(
        _rwkv_proj_kernel,
        grid=(bsz, nt),
        in_specs=[
            tok,
            pl.BlockSpec((8, d), lambda b, i: (jnp.maximum((b * nt + i) * (tm // 8) - 1, 0), 0)),
            pl.BlockSpec((None, 1, d), lambda b, i: (b, 0, 0)),
            pl.BlockSpec((None, 6, d), lambda b, i: (b, 0, 0)),
        ] + [full(a) for a in weights],
        out_specs=[tok] * 7 + [pl.BlockSpec((None, 1, d), lambda b, i: (b, 0, 0))],
        out_shape=[jax.ShapeDtypeStruct((n, d), F32)] * 7 + [jax.ShapeDtypeStruct((bsz, 1, d), F32)],
        compiler_params=_params(("parallel", "arbitrary")),
        name="rwkv_proj",
    )(x2d, x2d, shift_prev.reshape(bsz, 1, d), mods_l, *weights)
    return outs[:7], outs[7].reshape(bsz, d)


def _wkv_kernel(r_ref, lw_ref, k_ref, v_ref, kk_ref, b_ref, s0_ref, vec_ref, tri_ref, ones_ref,
                y_ref, s_out_ref, state_ref):
    c = r_ref.shape[0]
    n_pairs = r_ref.shape[1] // LANES
    first = pl.program_id(1) == 0

    @pl.when(first)
    def _():
        state_ref[...] = s0_ref[...]

    lane = lax.broadcasted_iota(jnp.int32, (c, LANES), 1)
    low = lane < HEAD_DIM
    ri = lax.broadcasted_iota(jnp.int32, (2 * c, 2 * c), 0)
    ci = lax.broadcasted_iota(jnp.int32, (2 * c, 2 * c), 1)
    same = (ri >= c) == (ci >= c)
    strict = same & (ci < ri)
    incl = same & (ci <= ri)
    eye = (ri == ci).astype(F32)
    tri = tri_ref[...]
    ones_bd = ones_ref[...]

    def stack(z):
        return jnp.concatenate([jnp.where(low, z, 0.0), jnp.where(low, 0.0, z)], axis=0)

    for p in range(n_pairs):
        sl = slice(p * LANES, (p + 1) * LANES)
        r, lw, k, v = r_ref[:, sl], lw_ref[:, sl], k_ref[:, sl], v_ref[:, sl]
        kk, b = kk_ref[:, sl], b_ref[:, sl]
        a = -kk
        cum = _dot_exact_lhs(tri, lw)
        cum_prev = cum - lw
        mid = cum[c // 2 - 1:c // 2, :]
        end = cum[c - 1:c, :]
        e_in = jnp.exp(mid - cum)
        rows_q = jnp.concatenate([stack(a * jnp.exp(cum_prev - mid)), stack(r * jnp.exp(cum - mid))], 0)
        rows_k = jnp.concatenate([stack(b * e_in), stack(k * e_in)], 0)
        gram = _bdot(rows_q, rows_k, 1, 1)
        a_ab = jnp.where(strict, gram[:2 * c, :2 * c], 0.0)
        a_ak = jnp.where(strict, gram[:2 * c, 2 * c:], 0.0)
        a_rb = jnp.where(incl, gram[2 * c:, :2 * c], 0.0)
        a_rk = jnp.where(incl, gram[2 * c:, 2 * c:], 0.0)
        tinv = eye + a_ab
        pw = a_ab
        steps = max(c.bit_length() - 2, 0)
        for it in range(steps):
            pw = _dot3(pw, pw)
            tinv = tinv + _dot3(tinv, pw)
        s_prev = state_ref[p]
        rows_0 = jnp.concatenate([stack(a * jnp.exp(cum_prev)), stack(r * jnp.exp(cum))], 0)
        from_state = _bdot(rows_0, s_prev, 1, 1)
        v_st = stack(v)
        u = _dot3(tinv, from_state[:2 * c] + _bdot(a_ak, v_st))
        uv = jnp.concatenate([u, v_st], 0)
        o_st = from_state[2 * c:] + _bdot(jnp.concatenate([a_rb, a_rk], 1), uv)
        o = o_st[:c] + o_st[c:]
        e_out = jnp.exp(end - cum)
        rows_e = jnp.concatenate([stack(b * e_out), stack(k * e_out)], 0)
        state_ref[p] = s_prev * jnp.exp(end) + _bdot(uv, rows_e, 0, 0)
        mean = _dot_exact_rhs(o, ones_bd) * (1.0 / HEAD_DIM)
        dev = o - mean
        var = _dot_exact_rhs(dev * dev, ones_bd) * (1.0 / HEAD_DIM)
        r_k, lnx_g, lnx_b = vec_ref[0:1, sl], vec_ref[1:2, sl], vec_ref[2:3, sl]
        bonus = _dot_exact_rhs(r * k * r_k, ones_bd) * v
        y_ref[:, sl] = dev * lax.rsqrt(var + GN_EPS) * lnx_g + lnx_b + bonus

    @pl.when(pl.program_id(1) == pl.num_programs(1) - 1)
    def _():
        s_out_ref[...] = state_ref[...]


def _pair_states(s):
    bsz, nh, n, _ = s.shape
    s = s.reshape(bsz, nh // 2, 2, n, n)
    z = jnp.zeros_like(s[:, :, 0])
    top = jnp.concatenate([s[:, :, 0], z], -1)
    bot = jnp.concatenate([z, s[:, :, 1]], -1)
    return jnp.concatenate([top, bot], -2)


def _unpair_states(sp):
    bsz, npair, _, _ = sp.shape
    n = HEAD_DIM
    return jnp.stack([sp[:, :, :n, :n], sp[:, :, n:, n:]], 2).reshape(bsz, 2 * npair, n, n)


def _wkv(proj, s0, wl, bsz, t):
    r, lw, k, v, kk, b = proj
    n, d = r.shape
    c = min(WKV_CHUNK, t)
    nc = t // c
    npair = d // LANES
    tok = pl.BlockSpec((c, d), lambda bb, i: (bb * nc + i, 0))
    st = pl.BlockSpec((None, npair, LANES, LANES), lambda bb, i: (bb, 0, 0, 0))
    tri = jnp.tril(jnp.ones((c, c), BF16))
    hid = jnp.arange(LANES) // HEAD_DIM
    ones_bd = (hid[:, None] == hid[None, :]).astype(BF16)
    y, s_out = pl.pallas_call(
        _wkv_kernel,
        grid=(bsz, nc),
        in_specs=[tok] * 6 + [st,
                              pl.BlockSpec((8, d), lambda bb, i: (0, 0)),
                              pl.BlockSpec((c, c), lambda bb, i: (0, 0)),
                              pl.BlockSpec((LANES, LANES), lambda bb, i: (0, 0))],
        out_specs=[tok, st],
        out_shape=[jax.ShapeDtypeStruct((n, d), F32),
                   jax.ShapeDtypeStruct((bsz, npair, LANES, LANES), F32)],
        scratch_shapes=[pltpu.VMEM((npair, LANES, LANES), F32)],
        compiler_params=_params(("parallel", "arbitrary")),
        name="wkv_scan",
    )(r, lw, k, v, kk, b, _pair_states(s0.astype(F32)), wl["scan_vec"], tri, ones_bd)
    return y, _unpair_states(s_out)


def _router(logits):
    lane_i = lax.broadcasted_iota(jnp.int32, logits.shape, 1)
    lane = lane_i.astype(F32)
    far = 1e9
    is_g = lane_i < N_GROUPS
    gl = jnp.where(is_g, logits, NEG_BIG)
    gmax = jnp.max(gl, -1, keepdims=True)
    gsel = jnp.min(jnp.where(gl == gmax, lane, far), -1, keepdims=True)
    gprob = 1.0 / jnp.sum(jnp.where(is_g, jnp.exp(gl - gmax), 0.0), -1, keepdims=True)
    group_of = lax.shift_right_arithmetic(lane_i - N_GROUPS, 2).astype(F32)
    in_group = (lane_i >= N_GROUPS) & (lane_i < N_GROUPS + N_EXPERTS) & (group_of == gsel)
    el = jnp.where(in_group, logits, NEG_BIG)
    v1 = jnp.max(el, -1, keepdims=True)
    i1 = jnp.min(jnp.where(el == v1, lane, far), -1, keepdims=True)
    el2 = jnp.where(lane == i1, NEG_BIG, el)
    v2 = jnp.max(el2, -1, keepdims=True)
    i2 = jnp.min(jnp.where(el2 == v2, lane, far), -1, keepdims=True)
    e2 = jnp.exp(v2 - v1)
    w1 = gprob / (1.0 + e2)
    w2 = gprob * e2 / (1.0 + e2)
    return jnp.where(lane == i1, w1, 0.0) + jnp.where(lane == i2, w2, 0.0)


def _post_kernel(gated, alpha, *refs):
    if gated:
        x_ref, y_ref, g_ref, mods_ref, wo_ref, ln_ref, rw_ref, rb_ref, x1_ref, comb_ref = refs
        y = y_ref[...] * g_ref[...]
    else:
        x_ref, y_ref, mods_ref, wo_ref, ln_ref, rw_ref, rb_ref, x1_ref, comb_ref = refs
        y = y_ref[...]
    gt1, sh2, sc2 = mods_ref[2:3, :], mods_ref[3:4, :], mods_ref[4:5, :]
    x1 = _layer_norm(alpha * x_ref[...] + gt1 * _bdot(y, wo_ref[...]), ln_ref[0:1, :], ln_ref[1:2, :])
    x1_ref[...] = x1
    h2 = x1 * (1.0 + sc2) + sh2
    comb_ref[...] = _router(_dot3(h2, rw_ref[...]) + rb_ref[...])


def _post_mixer(x2d, y2d, g2d, mods_l, w_o, ln_pack, router_w, router_b, alpha, bsz, t, tm):
    n, d = x2d.shape
    nt = t // tm
    tok = pl.BlockSpec((tm, d), lambda b, i: (b * nt + i, 0))
    full = lambda a: pl.BlockSpec(a.shape, lambda b, i: (0,) * a.ndim)
    gated = g2d is not None
    acts = [x2d, y2d] + ([g2d] if gated else [])
    consts = [w_o, ln_pack, router_w, router_b]
    return pl.pallas_call(
        functools.partial(_post_kernel, gated, alpha),
        grid=(bsz, nt),
        in_specs=[tok] * len(acts) + [pl.BlockSpec((None, 6, d), lambda b, i: (b, 0, 0))]
        + [full(a) for a in consts],
        out_specs=[tok, pl.BlockSpec((tm, LANES), lambda b, i: (b * nt + i, 0))],
        out_shape=[jax.ShapeDtypeStruct((n, d), F32), jax.ShapeDtypeStruct((n, LANES), F32)],
        compiler_params=_params(("parallel", "parallel")),
        name="post_mixer",
    )(*acts, mods_l, *consts)


def _moe_kernel(alpha, x1_ref, comb_ref, mods_ref, ln_ref, wg_ref, wu_ref, wd_ref, x2_ref, h2_ref, acc_ref):
    e = pl.program_id(1)

    @pl.when(e == 0)
    def _():
        sh2, sc2 = mods_ref[3:4, :], mods_ref[4:5, :]
        h2_ref[...] = (x1_ref[...] * (1.0 + sc2) + sh2).astype(BF16)
        acc_ref[...] = jnp.zeros_like(acc_ref)

    h2 = h2_ref[...]
    hg = _dg(h2, wg_ref[...], 1, 0)
    hu = _dg(h2, wu_ref[...], 1, 0)
    lane = lax.broadcasted_iota(jnp.int32, comb_ref.shape, 1)
    ce = jnp.sum(jnp.where(lane == e + N_GROUPS, comb_ref[...], 0.0), -1, keepdims=True)
    hid = _silu(hg) * hu * ce
    acc_ref[...] += _bdot(hid, wd_ref[...])

    @pl.when(e == pl.num_programs(1) - 1)
    def _():
        gt2 = mods_ref[5:6, :]
        x2_ref[...] = _layer_norm(alpha * x1_ref[...] + gt2 * acc_ref[...], ln_ref[0:1, :], ln_ref[1:2, :])


def _moe(x1, comb, mods_l, ln_pack, wg, wu, wd, alpha, bsz, t, tm):
    n, d = x1.shape
    nt = t // tm
    ne, _, f = wg.shape
    return pl.pallas_call(
        functools.partial(_moe_kernel, alpha),
        grid=(n // tm, ne),
        in_specs=[
            pl.BlockSpec((tm, d), lambda i, e: (i, 0)),
            pl.BlockSpec((tm, LANES), lambda i, e: (i, 0)),
            pl.BlockSpec((None, 6, d), lambda i, e: (i // nt, 0, 0)),
            pl.BlockSpec(ln_pack.shape, lambda i, e: (0, 0)),
            pl.BlockSpec((None, d, f), lambda i, e: (e, 0, 0)),
            pl.BlockSpec((None, d, f), lambda i, e: (e, 0, 0)),
            pl.BlockSpec((None, f, d), lambda i, e: (e, 0, 0)),
        ],
        out_specs=pl.BlockSpec((tm, d), lambda i, e: (i, 0)),
        out_shape=jax.ShapeDtypeStruct((n, d), F32),
        scratch_shapes=[pltpu.VMEM((tm, d), BF16), pltpu.VMEM((tm, d), F32)],
        compiler_params=_params(("parallel", "arbitrary")),
        name="hmoe",
    )(x1, comb, mods_l, ln_pack, wg, wu, wd)


def _kv_kernel(x_ref, kvw_ref, fw_ref, fb_ref, tri_ref, k_ref, v_ref, kb_ref, vb_ref, lf_ref, fc_ref,
               carry_ref):
    d = x_ref.shape[1]

    @pl.when(pl.program_id(1) == 0)
    def _():
        carry_ref[...] = jnp.zeros_like(carry_ref)

    x = x_ref[...]
    kv = _bdot(x, kvw_ref[...])
    k, v = kv[:, :d], kv[:, d:]
    k_ref[...] = k
    v_ref[...] = v
    kb_ref[...] = k.astype(BF16)
    vb_ref[...] = v.astype(BF16)
    z = _dot3(x, fw_ref[...]) + fb_ref[...]
    lf = -_softplus(-z)
    lf_ref[...] = lf
    fc = _dot_exact_lhs(tri_ref[...], lf) + carry_ref[...]
    fc_ref[...] = fc
    carry_ref[...] = fc[fc.shape[0] - 1:, :]


def _kv_proj(x2d, kv_w, f_w, f_b, bsz, t, tm):
    n, d = x2d.shape
    nt = t // tm
    tok = pl.BlockSpec((tm, d), lambda b, i: (b * nt + i, 0))
    nar = pl.BlockSpec((tm, LANES), lambda b, i: (b * nt + i, 0))
    tri = jnp.tril(jnp.ones((tm, tm), BF16))
    full = lambda a: pl.BlockSpec(a.shape, lambda b, i: (0,) * a.ndim)
    return pl.pallas_call(
        _kv_kernel,
        grid=(bsz, nt),
        in_specs=[tok, full(kv_w), full(f_w), full(f_b), full(tri)],
        out_specs=[tok, tok, tok, tok, nar, nar],
        out_shape=[jax.ShapeDtypeStruct((n, d), F32), jax.ShapeDtypeStruct((n, d), F32),
                   jax.ShapeDtypeStruct((n, d), BF16), jax.ShapeDtypeStruct((n, d), BF16),
                   jax.ShapeDtypeStruct((n, LANES), F32), jax.ShapeDtypeStruct((n, LANES), F32)],
        scratch_shapes=[pltpu.VMEM((1, LANES), F32)],
        compiler_params=_params(("parallel", "arbitrary")),
        name="kv_proj",
    )(x2d, kv_w, f_w, f_b, tri)


def _q_kernel(x_ref, mods_ref, wq_ref, q_ref):
    h = x_ref[...] * (1.0 + mods_ref[1:2, :]) + mods_ref[0:1, :]
    q_ref[...] = (_bdot(h, wq_ref[...]) * (HEAD_DIM ** -0.5)).astype(BF16)


def _q_proj(x2d, mods_l, wq, bsz, t, tm):
    n, d = x2d.shape
    nt = t // tm
    tok = pl.BlockSpec((tm, d), lambda b, i: (b * nt + i, 0))
    return pl.pallas_call(
        _q_kernel,
        grid=(bsz, nt),
        in_specs=[tok, pl.BlockSpec((None, 6, d), lambda b, i: (b, 0, 0)),
                  pl.BlockSpec(wq.shape, lambda b, i: (0, 0))],
        out_specs=tok,
        out_shape=jax.ShapeDtypeStruct((n, d), BF16),
        compiler_params=_params(("parallel", "parallel")),
        name="q_proj",
    )(x2d, mods_l, wq)


def _fox_prompt_kernel(q_ref, k_ref, v_ref, fq_ref, fk_ref, o_ref, m_ref, l_ref, acc_ref):
    qi, ki = pl.program_id(2), pl.program_id(3)
    tq, tk = q_ref.shape[0], k_ref.shape[0]

    @pl.when(ki == 0)
    def _():
        m_ref[...] = jnp.full_like(m_ref, NEG_BIG)
        l_ref[...] = jnp.zeros_like(l_ref)
        acc_ref[...] = jnp.zeros_like(acc_ref)

    def update(diag):
        q, k, v = q_ref[...], k_ref[...], v_ref[...]
        lane = lax.broadcasted_iota(jnp.int32, q.shape, 1)
        for hh in range(2):
            qm = jnp.where((lane < HEAD_DIM) == (hh == 0), q, jnp.zeros_like(q))
            s = _dg(qm, k, 1, 1) + fq_ref[:, hh:hh + 1] - fk_ref[hh:hh + 1, :]
            if diag:
                rr = lax.broadcasted_iota(jnp.int32, (tq, tk), 0)
                cc = lax.broadcasted_iota(jnp.int32, (tq, tk), 1)
                s = jnp.where(cc <= rr, s, NEG_BIG)
            m_old = m_ref[hh]
            m_new = jnp.maximum(m_old, jnp.max(s, -1, keepdims=True))
            alpha = jnp.exp(m_old - m_new)
            p = jnp.exp(s - m_new)
            l_ref[hh] = alpha * l_ref[hh] + jnp.sum(p, -1, keepdims=True)
            acc_ref[hh] = alpha * acc_ref[hh] + _dg(p.astype(BF16), v, 1, 0)
            m_ref[hh] = m_new

    @pl.when(ki < qi)
    def _():
        update(False)

    @pl.when(ki == qi)
    def _():
        update(True)
        lane = lax.broadcasted_iota(jnp.int32, o_ref.shape, 1)
        o_ref[...] = jnp.where(lane < HEAD_DIM, acc_ref[0] / l_ref[0], acc_ref[1] / l_ref[1])


def _fox_prompt(q, kb, vb, fcum, bsz, t, tq):
    n, d = q.shape
    npair = d // LANES
    nq = t // tq
    nh = 2 * npair
    f = fcum[:, :nh].reshape(bsz, t, npair, 2)
    fq = f.transpose(0, 2, 1, 3)
    fk = f.transpose(0, 2, 3, 1)
    qspec = pl.BlockSpec((tq, LANES), lambda b, p, i, j: (b * nq + i, p))
    kspec = pl.BlockSpec((tq, LANES), lambda b, p, i, j: (b * nq + jnp.minimum(i, j), p))
    return pl.pallas_call(
        _fox_prompt_kernel,
        grid=(bsz, npair, nq, nq),
        in_specs=[qspec, kspec, kspec,
                  pl.BlockSpec((None, None, tq, 2), lambda b, p, i, j: (b, p, i, 0)),
                  pl.BlockSpec((None, None, 2, tq), lambda b, p, i, j: (b, p, 0, jnp.minimum(i, j)))],
        out_specs=qspec,
        out_shape=jax.ShapeDtypeStruct((n, d), F32),
        scratch_shapes=[pltpu.VMEM((2, tq, 1), F32), pltpu.VMEM((2, tq, 1), F32),
                        pltpu.VMEM((2, tq, LANES), F32)],
        compiler_params=_params(("parallel", "parallel", "parallel", "arbitrary")),
        name="fox_prompt",
    )(q, kb, vb, fq, fk)


def _fox_sample_kernel(q_ref, kn_ref, vn_ref, fn_ref, fnt_ref, ck_ref, cv_ref, clf_ref, upper_ref,
                       o_ref, m_ref, l_ref, acc_ref, carry_ref):
    step = pl.program_id(1)
    t, d = q_ref.shape
    nh = d // HEAD_DIM

    @pl.when(step == 0)
    def _():
        m_ref[...] = jnp.full_like(m_ref, NEG_BIG)
        l_ref[...] = jnp.zeros_like(l_ref)
        acc_ref[...] = jnp.zeros_like(acc_ref)
        carry_ref[...] = jnp.zeros_like(carry_ref)

    q = q_ref[...]
    lane = lax.broadcasted_iota(jnp.int32, (t, LANES), 1)

    def attend(h, k_slab, v_slab, bias, mask):
        sl = slice((h // 2) * LANES, (h // 2 + 1) * LANES)
        qs = q[:, sl]
        qm = jnp.where((lane < HEAD_DIM) == (h % 2 == 0), qs, jnp.zeros_like(qs))
        s = _dg(qm, k_slab, 1, 1) + bias
        if mask is not None:
            s = jnp.where(mask, s, NEG_BIG)
        m_old = m_ref[h]
        m_new = jnp.maximum(m_old, jnp.max(s, -1, keepdims=True))
        alpha = jnp.exp(m_old - m_new)
        p = jnp.exp(s - m_new)
        l_ref[h] = alpha * l_ref[h] + jnp.sum(p, -1, keepdims=True)
        acc_ref[h] = alpha * acc_ref[h] + _dg(p.astype(BF16), v_slab, 1, 0)
        m_ref[h] = m_new

    clf = clf_ref[...]
    suf = _dot_exact_rhs(clf, upper_ref[...]) + carry_ref[...]
    carry_ref[...] = carry_ref[...] + jnp.sum(clf, -1, keepdims=True)
    fn = fn_ref[...]
    for h in range(nh):
        sl = slice((h // 2) * LANES, (h // 2 + 1) * LANES)
        attend(h, ck_ref[:, sl].astype(BF16), cv_ref[:, sl].astype(BF16),
               fn[:, h:h + 1] + suf[h:h + 1, :], None)

    @pl.when(step == pl.num_programs(1) - 1)
    def _():
        rr = lax.broadcasted_iota(jnp.int32, (t, t), 0)
        cc = lax.broadcasted_iota(jnp.int32, (t, t), 1)
        fnt = fnt_ref[...]
        for h in range(nh):
            sl = slice((h // 2) * LANES, (h // 2 + 1) * LANES)
            attend(h, kn_ref[:, sl], vn_ref[:, sl], fn[:, h:h + 1] - fnt[h:h + 1, :], cc <= rr)
        for pr in range(nh // 2):
            sl = slice(pr * LANES, (pr + 1) * LANES)
            o_ref[:, sl] = jnp.where(lane < HEAD_DIM, acc_ref[2 * pr] / l_ref[2 * pr],
                                     acc_ref[2 * pr + 1] / l_ref[2 * pr + 1])


def _fox_sample(q, kb, vb, fcum, cache_k, cache_v, cache_logf, bsz, t, tk):
    n, d = q.shape
    nh = d // HEAD_DIM
    plen = cache_k.shape[1]
    nk = plen // tk
    ck = cache_k.reshape(bsz * plen, d)
    cv = cache_v.reshape(bsz * plen, d)
    clf_t = cache_logf.astype(F32).transpose(0, 2, 1)
    fn_t = fcum[:, :nh].reshape(bsz, t, nh).transpose(0, 2, 1)
    upper = (jnp.arange(tk)[:, None] > jnp.arange(tk)[None, :]).astype(BF16)
    tok = pl.BlockSpec((t, d), lambda b, j: (b, 0))
    past = pl.BlockSpec((tk, d), lambda b, j: (b * nk + nk - 1 - j, 0))
    return pl.pallas_call(
        _fox_sample_kernel,
        grid=(bsz, nk),
        in_specs=[tok, tok, tok,
                  pl.BlockSpec((t, LANES), lambda b, j: (b, 0)),
                  pl.BlockSpec((None, nh, t), lambda b, j: (b, 0, 0)),
                  past, past,
                  pl.BlockSpec((None, nh, tk), lambda b, j: (b, 0, nk - 1 - j)),
                  pl.BlockSpec((tk, tk), lambda b, j: (0, 0))],
        out_specs=tok,
        out_shape=jax.ShapeDtypeStruct((n, d), F32),
        scratch_shapes=[pltpu.VMEM((nh, t, 1), F32), pltpu.VMEM((nh, t, 1), F32),
                        pltpu.VMEM((nh, t, LANES), F32), pltpu.VMEM((nh, 1), F32)],
        compiler_params=_params(("parallel", "arbitrary")),
        name="fox_sample",
    )(q, kb, vb, fcum, fn_t, ck, cv, clf_t, upper)


def _tile(t, cap):
    return min(t, cap)


def _trunk(x, mods, shift_in, wkv_in, cache, w):
    bsz, t, d = x.shape
    depth = mods.shape[0]
    n_a = w["n_a"]
    nh = d // HEAD_DIM
    x2d = x.reshape(bsz * t, d)
    shifts, states = [], []
    kv = None
    for l in range(depth):
        mods_l = mods[l]
        if l < n_a:
            wl = w["rwkv"][l]
            proj, last = _rwkv_proj(x2d, shift_in[l], mods_l, wl, bsz, t, _tile(t, 256))
            y, s_new = _wkv(proj[:6], wkv_in[l], wl, bsz, t)
            shifts.append(last)
            states.append(s_new.astype(wkv_in.dtype))
            mixer_out, gate, w_o = y, proj[6], wl["w_o"]
        else:
            if kv is None:
                kv = _kv_proj(x2d, w["kv_w"], w["f_w"], w["f_b"], bsz, t, _tile(t, 512))
            k_sh, v_sh, kb, vb, lf, fcum = kv
            j = l - n_a
            q = _q_proj(x2d, mods_l, w["b_wq"][j], bsz, t, _tile(t, 512))
            if cache is None:
                mixer_out = _fox_prompt(q, kb, vb, fcum, bsz, t, _tile(t, 512))
            else:
                mixer_out = _fox_sample(q, kb, vb, fcum, cache[0], cache[1], cache[2], bsz, t,
                                        _tile(cache[0].shape[1], 512))
            gate, w_o = None, w["b_wo"][j]
        x1, comb = _post_mixer(x2d, mixer_out, gate, mods_l, w_o, w["ln"][l][0], w["router_w"][l],
                               w["router_b"][l], w["alpha"], bsz, t, _tile(t, 512))
        x2d = _moe(x1, comb, mods_l, w["ln"][l][1], w["exp_wg"][l], w["exp_wu"][l], w["exp_wd"][l],
                   w["alpha"], bsz, t, _tile(t, 1024))
    k_sh, v_sh, _, _, lf, _ = kv
    return (x2d.reshape(bsz, t, d), jnp.stack(shifts), jnp.stack(states),
            k_sh.reshape(bsz, t, nh, HEAD_DIM), v_sh.reshape(bsz, t, nh, HEAD_DIM),
            lf[:, :nh].reshape(bsz, t, nh).astype(x.dtype))


def _prepare_weights(ln_g, ln_b, a_mu, a_w_rkv, a_w0, a_w1, a_w2, a_a0, a_a1, a_a2, a_g1, a_g2, a_k_k,
                     a_k_a, a_r_k, a_lnx_g, a_lnx_b, a_w_o, kv_w, f_w, f_b, b_wq, b_wo, rg_w, rg_b,
                     re_w, re_b, exp_wg, exp_wu, exp_wd):
    depth, _, d = ln_g.shape
    n_a = a_mu.shape[0]
    nh = d // HEAD_DIM
    alpha = (2.0 * depth) ** 0.25
    zrow = jnp.zeros((d,), F32)
    head_of = jnp.arange(d) // HEAD_DIM
    head_down = (head_of[:, None] == jnp.arange(LANES)[None, :]).astype(BF16)
    rwkv = []
    for l in range(n_a):
        rwkv.append(dict(
            mu=a_mu[l],
            vec=jnp.stack([a_w0[l], a_a0[l], a_k_k[l], a_k_a[l], zrow, zrow, zrow, zrow]),
            scan_vec=jnp.stack([a_r_k[l].reshape(d), a_lnx_g[l], a_lnx_b[l], zrow, zrow, zrow, zrow, zrow]),
            w_rkv=a_w_rkv[l].astype(BF16), w1=a_w1[l].astype(BF16), w2=a_w2[l].astype(BF16),
            a1=a_a1[l].astype(BF16), a2=a_a2[l].astype(BF16), g1=a_g1[l].astype(BF16),
            g2=a_g2[l].astype(BF16), w_o=a_w_o[l].astype(BF16),
            head_down=head_down, head_up=head_down.T))
    ln = [[jnp.stack([ln_g[l, s], ln_b[l, s], zrow, zrow, zrow, zrow, zrow, zrow])
           for s in range(2)] for l in range(depth)]
    pad = LANES - N_GROUPS - N_EXPERTS
    router_w = [jnp.concatenate([rg_w[l], re_w[l], jnp.zeros((d, pad), F32)], 1) for l in range(depth)]
    router_b = [jnp.concatenate([rg_b[l], re_b[l], jnp.zeros((pad,), F32)])[None, :] for l in range(depth)]
    return dict(
        n_a=n_a, alpha=alpha, rwkv=rwkv, ln=ln, router_w=router_w, router_b=router_b,
        kv_w=kv_w.astype(BF16),
        f_w=jnp.concatenate([f_w, jnp.zeros((d, LANES - nh), F32)], 1),
        f_b=jnp.concatenate([f_b, jnp.zeros((LANES - nh,), F32)])[None, :],
        b_wq=b_wq.astype(BF16), b_wo=b_wo.astype(BF16),
        exp_wg=exp_wg.astype(BF16), exp_wu=exp_wu.astype(BF16), exp_wd=exp_wd.astype(BF16))


def kernel(x_prompt, x_sample, state_shift, state_wkv, cache_k, cache_v, cache_logf, c_prompt, c_sample,
           ada_w, ada_b, ln_g, ln_b, a_mu, a_w_rkv, a_w0, a_w1, a_w2, a_a0, a_a1, a_a2, a_g1, a_g2, a_k_k,
           a_k_a, a_r_k, a_lnx_g, a_lnx_b, a_w_o, kv_w, f_w, f_b, b_wq, b_wo, rg_w, rg_b, re_w, re_b,
           exp_wg, exp_wu, exp_wd):
    w = _prepare_weights(ln_g, ln_b, a_mu, a_w_rkv, a_w0, a_w1, a_w2, a_a0, a_a1, a_a2, a_g1, a_g2,
                         a_k_k, a_k_a, a_r_k, a_lnx_g, a_lnx_b, a_w_o, kv_w, f_w, f_b, b_wq, b_wo,
                         rg_w, rg_b, re_w, re_b, exp_wg, exp_wu, exp_wd)
    bp, _, d = x_prompt.shape
    n_a = a_mu.shape[0]
    nh = d // HEAD_DIM
    mods = _ada_mods(jnp.concatenate([c_prompt, c_sample], 0), ada_w, ada_b)
    zero_shift = jnp.zeros((n_a, bp, d), x_prompt.dtype)
    zero_wkv = jnp.zeros((n_a, bp, nh, HEAD_DIM, HEAD_DIM), x_prompt.dtype)
    outs_p = _trunk(x_prompt, mods[:, :bp], zero_shift, zero_wkv, None, w)
    outs_s = _trunk(x_sample, mods[:, bp:], state_shift, state_wkv, (cache_k, cache_v, cache_logf), w)
    y_p, p_shift, p_wkv, p_k, p_v, p_logf = outs_p
    y_s, s_shift, s_wkv, s_k, s_v, s_logf = outs_s
    return (y_p, y_s, p_shift, p_wkv, p_k, p_v, p_logf, s_shift, s_wkv, s_k, s_v, s_logf)
```

```python
import functools

import jax
import jax.numpy as jnp
from jax import lax
from jax.experimental import pallas as pl
from jax.experimental.pallas import tpu as pltpu

F32 = jnp.float32
BF16 = jnp.bfloat16

HEAD_DIM = 64
LANES = 128
N_GROUPS = 4
EXP_PER_GROUP = 4
N_EXPERTS = N_GROUPS * EXP_PER_GROUP
LN_EPS = 1e-5
GN_EPS = 64e-5
WKV_CHUNK = 64
VMEM_LIMIT = 56 * 1024 * 1024
NEG_BIG = -1e30


def _params(sem):
    return pltpu.CompilerParams(dimension_semantics=sem, vmem_limit_bytes=VMEM_LIMIT)


def _dg(a, b, ca, cb):
    return lax.dot_general(a, b, (((ca,), (cb,)), ((), ())), preferred_element_type=F32)


def _bdot(a, b, ca=1, cb=0):
    return _dg(a.astype(BF16), b.astype(BF16), ca, cb)


def _split3(x):
    hi = x.astype(BF16)
    r1 = x - hi.astype(F32)
    mid = r1.astype(BF16)
    lo = (r1 - mid.astype(F32)).astype(BF16)
    return hi, mid, lo


def _dot_exact_rhs(a, b_exact, ca=1, cb=0):
    hi, mid, lo = _split3(a)
    bb = b_exact.astype(BF16)
    return _dg(hi, bb, ca, cb) + _dg(mid, bb, ca, cb) + _dg(lo, bb, ca, cb)


def _dot_exact_lhs(a_exact, b, ca=1, cb=0):
    hi, mid, lo = _split3(b)
    aa = a_exact.astype(BF16)
    return _dg(aa, hi, ca, cb) + _dg(aa, mid, ca, cb) + _dg(aa, lo, ca, cb)


def _dot3(a, b, ca=1, cb=0):
    ah = a.astype(BF16)
    al = (a - ah.astype(F32)).astype(BF16)
    bh = b.astype(BF16)
    bl = (b - bh.astype(F32)).astype(BF16)
    return _dg(ah, bh, ca, cb) + _dg(ah, bl, ca, cb) + _dg(al, bh, ca, cb)


_inv_dot = _bdot


def _layer_norm(z, g, b):
    mu = jnp.mean(z, -1, keepdims=True)
    d = z - mu
    var = jnp.mean(d * d, -1, keepdims=True)
    return d * lax.rsqrt(var + LN_EPS) * g + b


def _softplus(z):
    return jnp.maximum(z, 0.0) + jnp.log(1.0 + jnp.exp(-jnp.abs(z)))


def _sigmoid(z):
    return 1.0 / (1.0 + jnp.exp(-z))


def _silu(z):
    return z * _sigmoid(z)


def _ada_kernel(c_ref, w_ref, b_ref, o_ref):
    o_ref[...] = _dot3(_silu(c_ref[...]), w_ref[...]) + b_ref[...]


def _ada_mods(c_all, ada_w, ada_b):
    depth, d, d6 = ada_w.shape
    bsz = c_all.shape[0]
    tn = d
    out = pl.pallas_call(
        _ada_kernel,
        grid=(depth, d6 // tn),
        in_specs=[
            pl.BlockSpec((bsz, d), lambda l, j: (0, 0)),
            pl.BlockSpec((None, d, tn), lambda l, j: (l, 0, j)),
            pl.BlockSpec((None, 1, tn), lambda l, j: (l, 0, j)),
        ],
        out_specs=pl.BlockSpec((None, bsz, tn), lambda l, j: (l, 0, j)),
        out_shape=jax.ShapeDtypeStruct((depth, bsz, d6), F32),
        compiler_params=_params(("parallel", "parallel")),
        name="ada_mods",
    )(c_all, ada_w, ada_b.reshape(depth, 1, d6))
    return out.reshape(depth, bsz, 6, d)


def _rwkv_proj_kernel(x_ref, xp_ref, shift_ref, mods_ref, mu_ref, vec_ref, wrkv_ref, w1_ref, w2_ref,
                      a1_ref, a2_ref, g1_ref, g2_ref, hd_ref, hu_ref,
                      r_ref, lw_ref, k_ref, v_ref, kk_ref, b_ref, g_ref, last_ref):
    tm = x_ref.shape[0]
    sh1 = mods_ref[0:1, :]
    sc1 = mods_ref[1:2, :]
    h = x_ref[...] * (1.0 + sc1) + sh1
    h_prev_tile = xp_ref[7:8, :] * (1.0 + sc1) + sh1
    prev_row = jnp.where(pl.program_id(1) == 0, shift_ref[...], h_prev_tile)
    row = lax.broadcasted_iota(jnp.int32, (tm, 1), 0)
    xx = jnp.where(row == 0, prev_row, pltpu.roll(h, 1, axis=0)) - h

    def mix(i):
        return h + xx * mu_ref[i:i + 1, :]

    w0, a0 = vec_ref[0:1, :], vec_ref[1:2, :]
    k_k, k_a = vec_ref[2:3, :], vec_ref[3:4, :]
    r = _bdot(mix(0), wrkv_ref[0])
    k = _bdot(mix(1), wrkv_ref[1])
    v = _bdot(mix(2), wrkv_ref[2])
    ww = w0 + _bdot(jnp.tanh(_bdot(mix(3), w1_ref[...])), w2_ref[...])
    w_log = -_softplus(-ww) - 0.5
    a = _sigmoid(a0 + _bdot(_bdot(mix(4), a1_ref[...]), a2_ref[...]))
    g = _bdot(_sigmoid(_bdot(mix(5), g1_ref[...])), g2_ref[...])
    kk = k * k_k
    ss = _dot_exact_rhs(_dot_exact_rhs(kk * kk, hd_ref[...]), hu_ref[...])
    kk = kk * lax.rsqrt(jnp.maximum(ss, 1e-24))
    r_ref[...] = r
    lw_ref[...] = -jnp.exp(w_log)
    k_ref[...] = k * (1.0 + (a - 1.0) * k_a)
    v_ref[...] = v
    kk_ref[...] = kk
    b_ref[...] = kk * a
    g_ref[...] = g
    last_ref[...] = h[tm - 1:tm, :]


def _rwkv_proj(x2d, shift_prev, mods_l, wl, bsz, t, tm):
    n, d = x2d.shape
    nt = t // tm
    tok = pl.BlockSpec((tm, d), lambda b, i: (b * nt + i, 0))
    full = lambda a: pl.BlockSpec(a.shape, lambda b, i: (0,) * a.ndim)
    weights = [wl["mu"], wl["vec"], wl["w_rkv"], wl["w1"], wl["w2"], wl["a1"], wl["a2"], wl["g1"],
               wl["g2"], wl["head_down"], wl["head_up"]]
    outs = pl.pallas_call(
        _rwkv_proj_kernel,
        grid=(bsz, nt),
        in_specs=[
            tok,
            pl.BlockSpec((8, d), lambda b, i: (jnp.maximum((b * nt + i) * (tm // 8) - 1, 0), 0)),
            pl.BlockSpec((None, 1, d), lambda b, i: (b, 0, 0)),
            pl.BlockSpec((None, 6, d), lambda b, i: (b, 0, 0)),
        ] + [full(a) for a in weights],
        out_specs=[tok] * 7 + [pl.BlockSpec((None, 1, d), lambda b, i: (b, 0, 0))],
        out_shape=[jax.ShapeDtypeStruct((n, d), F32)] * 7 + [jax.ShapeDtypeStruct((bsz, 1, d), F32)],
        compiler_params=_params(("parallel", "arbitrary")),
        name="rwkv_proj",
    )(x2d, x2d, shift_prev.reshape(bsz, 1, d), mods_l, *weights)
    return outs[:7], outs[7].reshape(bsz, d)


def _wkv_kernel(r_ref, lw_ref, k_ref, v_ref, kk_ref, b_ref, s0_ref, vec_ref, tri_ref, ones_ref,
                y_ref, s_out_ref, state_ref):
    c = r_ref.shape[0]
    n_pairs = r_ref.shape[1] // LANES
    first = pl.program_id(1) == 0

    @pl.when(first)
    def _():
        state_ref[...] = s0_ref[...]

    lane = lax.broadcasted_iota(jnp.int32, (c, LANES), 1)
    low = lane < HEAD_DIM
    ri = lax.broadcasted_iota(jnp.int32, (2 * c, 2 * c), 0)
    ci = lax.broadcasted_iota(jnp.int32, (2 * c, 2 * c), 1)
    same = (ri >= c) == (ci >= c)
    strict = same & (ci < ri)
    incl = same & (ci <= ri)
    eye = (ri == ci).astype(F32)
    tri = tri_ref[...]
    ones_bd = ones_ref[...]

    def stack(z):
        return jnp.concatenate([jnp.where(low, z, 0.0), jnp.where(low, 0.0, z)], axis=0)

    pairs = range(n_pairs)
    slabs = [slice(p * LANES, (p + 1) * LANES) for p in pairs]
    lw_all = lw_ref[...]
    cum_all = _dot_exact_lhs(tri, lw_all)
    prev_all = cum_all - lw_all
    mid_all = cum_all[c // 2 - 1:c // 2, :]
    end_all = cum_all[c - 1:c, :]
    e_in = jnp.exp(mid_all - cum_all)
    e_out = jnp.exp(end_all - cum_all)
    e_end = jnp.exp(end_all)
    r_all, k_all, v_all = r_ref[...], k_ref[...], v_ref[...]
    a_all, b_all = -kk_ref[...], b_ref[...]
    aq, rq = a_all * jnp.exp(prev_all - mid_all), r_all * jnp.exp(cum_all - mid_all)
    bi, ki = b_all * e_in, k_all * e_in
    a0, r0 = a_all * jnp.exp(prev_all), r_all * jnp.exp(cum_all)
    bo, ko = b_all * e_out, k_all * e_out

    gram = [_bdot(jnp.concatenate([stack(aq[:, s]), stack(rq[:, s])], 0),
                  jnp.concatenate([stack(bi[:, s]), stack(ki[:, s])], 0), 1, 1) for s in slabs]
    a_ab = [jnp.where(strict, g[:2 * c, :2 * c], 0.0) for g in gram]
    a_ak = [jnp.where(strict, g[:2 * c, 2 * c:], 0.0) for g in gram]
    a_r = [jnp.concatenate([jnp.where(incl, g[2 * c:, :2 * c], 0.0),
                            jnp.where(incl, g[2 * c:, 2 * c:], 0.0)], 1) for g in gram]
    tinv = [eye + m for m in a_ab]
    pw = a_ab
    for _ in range(max(c.bit_length() - 2, 0)):
        pw = [_inv_dot(m, m) for m in pw]
        tinv = [t_ + _inv_dot(t_, m) for t_, m in zip(tinv, pw)]
    s_prev = [state_ref[p] for p in pairs]
    from_state = [_bdot(jnp.concatenate([stack(a0[:, s]), stack(r0[:, s])], 0), sp, 1, 1)
                  for s, sp in zip(slabs, s_prev)]
    v_st = [stack(v_all[:, s]) for s in slabs]
    rhs = [fs[:2 * c] + _bdot(m, vs) for fs, m, vs in zip(from_state, a_ak, v_st)]
    u = [_inv_dot(t_, x) for t_, x in zip(tinv, rhs)]
    uv = [jnp.concatenate([u_, vs], 0) for u_, vs in zip(u, v_st)]
    o_st = [fs[2 * c:] + _bdot(m, x) for fs, m, x in zip(from_state, a_r, uv)]
    o = [x[:c] + x[c:] for x in o_st]
    new_state = [sp * e_end[:, s] + _bdot(x, jnp.concatenate([stack(bo[:, s]), stack(ko[:, s])], 0), 0, 0)
                 for s, sp, x in zip(slabs, s_prev, uv)]
    state_ref[...] = jnp.stack(new_state)
    mean = [_dot_exact_rhs(x, ones_bd) * (1.0 / HEAD_DIM) for x in o]
    dev = [x - m for x, m in zip(o, mean)]
    var = [_dot_exact_rhs(x * x, ones_bd) * (1.0 / HEAD_DIM) for x in dev]
    rkk = r_all * k_all * vec_ref[0:1, :]
    bonus = [_dot_exact_rhs(rkk[:, s], ones_bd) for s in slabs]
    normed = jnp.concatenate([x * lax.rsqrt(vv + GN_EPS) for x, vv in zip(dev, var)], 1)
    y_ref[...] = normed * vec_ref[1:2, :] + vec_ref[2:3, :] + jnp.concatenate(bonus, 1) * v_all

    @pl.when(pl.program_id(1) == pl.num_programs(1) - 1)
    def _():
        s_out_ref[...] = state_ref[...]


def _pair_states(s):
    bsz, nh, n, _ = s.shape
    s = s.reshape(bsz, nh // 2, 2, n, n)
    z = jnp.zeros_like(s[:, :, 0])
    top = jnp.concatenate([s[:, :, 0], z], -1)
    bot = jnp.concatenate([z, s[:, :, 1]], -1)
    return jnp.concatenate([top, bot], -2)


def _unpair_states(sp):
    bsz, npair, _, _ = sp.shape
    n = HEAD_DIM
    return jnp.stack([sp[:, :, :n, :n], sp[:, :, n:, n:]], 2).reshape(bsz, 2 * npair, n, n)


def _wkv(proj, s0, wl, bsz, t):
    r, lw, k, v, kk, b = proj
    n, d = r.shape
    c = min(WKV_CHUNK, t)
    nc = t // c
    npair = d // LANES
    tok = pl.BlockSpec((c, d), lambda bb, i: (bb * nc + i, 0))
    st = pl.BlockSpec((None, npair, LANES, LANES), lambda bb, i: (bb, 0, 0, 0))
    tri = jnp.tril(jnp.ones((c, c), BF16))
    hid = jnp.arange(LANES) // HEAD_DIM
    ones_bd = (hid[:, None] == hid[None, :]).astype(BF16)
    y, s_out = pl.pallas_call(
        _wkv_kernel,
        grid=(bsz, nc),
        in_specs=[tok] * 6 + [st,
                              pl.BlockSpec((8, d), lambda bb, i: (0, 0)),
                              pl.BlockSpec((c, c), lambda bb, i: (0, 0)),
                              pl.BlockSpec((LANES, LANES), lambda bb, i: (0, 0))],
        out_specs=[tok, st],
        out_shape=[jax.ShapeDtypeStruct((n, d), F32),
                   jax.ShapeDtypeStruct((bsz, npair, LANES, LANES), F32)],
        scratch_shapes=[pltpu.VMEM((npair, LANES, LANES), F32)],
        compiler_params=_params(("parallel", "arbitrary")),
        name="wkv_scan",
    )(r, lw, k, v, kk, b, _pair_states(s0.astype(F32)), wl["scan_vec"], tri, ones_bd)
    return y, _unpair_states(s_out)


def _router(logits):
    lane_i = lax.broadcasted_iota(jnp.int32, logits.shape, 1)
    lane = lane_i.astype(F32)
    far = 1e9
    is_g = lane_i < N_GROUPS
    gl = jnp.where(is_g, logits, NEG_BIG)
    gmax = jnp.max(gl, -1, keepdims=True)
    gsel = jnp.min(jnp.where(gl == gmax, lane, far), -1, keepdims=True)
    gprob = 1.0 / jnp.sum(jnp.where(is_g, jnp.exp(gl - gmax), 0.0), -1, keepdims=True)
    group_of = lax.shift_right_arithmetic(lane_i - N_GROUPS, 2).astype(F32)
    in_group = (lane_i >= N_GROUPS) & (lane_i < N_GROUPS + N_EXPERTS) & (group_of == gsel)
    el = jnp.where(in_group, logits, NEG_BIG)
    v1 = jnp.max(el, -1, keepdims=True)
    i1 = jnp.min(jnp.where(el == v1, lane, far), -1, keepdims=True)
    el2 = jnp.where(lane == i1, NEG_BIG, el)
    v2 = jnp.max(el2, -1, keepdims=True)
    i2 = jnp.min(jnp.where(el2 == v2, lane, far), -1, keepdims=True)
    e2 = jnp.exp(v2 - v1)
    w1 = gprob / (1.0 + e2)
    w2 = gprob * e2 / (1.0 + e2)
    return jnp.where(lane == i1, w1, 0.0) + jnp.where(lane == i2, w2, 0.0)


def _post_kernel(gated, alpha, *refs):
    if gated:
        x_ref, y_ref, g_ref, mods_ref, wo_ref, ln_ref, rw_ref, rb_ref, x1_ref, comb_ref = refs
        y = y_ref[...] * g_ref[...]
    else:
        x_ref, y_ref, mods_ref, wo_ref, ln_ref, rw_ref, rb_ref, x1_ref, comb_ref = refs
        y = y_ref[...]
    gt1, sh2, sc2 = mods_ref[2:3, :], mods_ref[3:4, :], mods_ref[4:5, :]
    x1 = _layer_norm(alpha * x_ref[...] + gt1 * _bdot(y, wo_ref[...]), ln_ref[0:1, :], ln_ref[1:2, :])
    x1_ref[...] = x1
    h2 = x1 * (1.0 + sc2) + sh2
    comb_ref[...] = _router(_dot3(h2, rw_ref[...]) + rb_ref[...])


def _post_mixer(x2d, y2d, g2d, mods_l, w_o, ln_pack, router_w, router_b, alpha, bsz, t, tm):
    n, d = x2d.shape
    nt = t // tm
    tok = pl.BlockSpec((tm, d), lambda b, i: (b * nt + i, 0))
    full = lambda a: pl.BlockSpec(a.shape, lambda b, i: (0,) * a.ndim)
    gated = g2d is not None
    acts = [x2d, y2d] + ([g2d] if gated else [])
    consts = [w_o, ln_pack, router_w, router_b]
    return pl.pallas_call(
        functools.partial(_post_kernel, gated, alpha),
        grid=(bsz, nt),
        in_specs=[tok] * len(acts) + [pl.BlockSpec((None, 6, d), lambda b, i: (b, 0, 0))]
        + [full(a) for a in consts],
        out_specs=[tok, pl.BlockSpec((tm, LANES), lambda b, i: (b * nt + i, 0))],
        out_shape=[jax.ShapeDtypeStruct((n, d), F32), jax.ShapeDtypeStruct((n, LANES), F32)],
        compiler_params=_params(("parallel", "parallel")),
        name="post_mixer",
    )(*acts, mods_l, *consts)


def _moe_kernel(alpha, x1_ref, comb_ref, mods_ref, ln_ref, wg_ref, wu_ref, wd_ref, x2_ref, h2_ref, acc_ref):
    e = pl.program_id(1)

    @pl.when(e == 0)
    def _():
        sh2, sc2 = mods_ref[3:4, :], mods_ref[4:5, :]
        h2_ref[...] = (x1_ref[...] * (1.0 + sc2) + sh2).astype(BF16)
        acc_ref[...] = jnp.zeros_like(acc_ref)

    h2 = h2_ref[...]
    hg = _dg(h2, wg_ref[...], 1, 0)
    hu = _dg(h2, wu_ref[...], 1, 0)
    lane = lax.broadcasted_iota(jnp.int32, comb_ref.shape, 1)
    ce = jnp.sum(jnp.where(lane == e + N_GROUPS, comb_ref[...], 0.0), -1, keepdims=True)
    hid = _silu(hg) * hu * ce
    acc_ref[...] += _bdot(hid, wd_ref[...])

    @pl.when(e == pl.num_programs(1) - 1)
    def _():
        gt2 = mods_ref[5:6, :]
        x2_ref[...] = _layer_norm(alpha * x1_ref[...] + gt2 * acc_ref[...], ln_ref[0:1, :], ln_ref[1:2, :])


def _moe(x1, comb, mods_l, ln_pack, wg, wu, wd, alpha, bsz, t, tm):
    n, d = x1.shape
    nt = t // tm
    ne, _, f = wg.shape
    return pl.pallas_call(
        functools.partial(_moe_kernel, alpha),
        grid=(n // tm, ne),
        in_specs=[
            pl.BlockSpec((tm, d), lambda i, e: (i, 0)),
            pl.BlockSpec((tm, LANES), lambda i, e: (i, 0)),
            pl.BlockSpec((None, 6, d), lambda i, e: (i // nt, 0, 0)),
            pl.BlockSpec(ln_pack.shape, lambda i, e: (0, 0)),
            pl.BlockSpec((None, d, f), lambda i, e: (e, 0, 0)),
            pl.BlockSpec((None, d, f), lambda i, e: (e, 0, 0)),
            pl.BlockSpec((None, f, d), lambda i, e: (e, 0, 0)),
        ],
        out_specs=pl.BlockSpec((tm, d), lambda i, e: (i, 0)),
        out_shape=jax.ShapeDtypeStruct((n, d), F32),
        scratch_shapes=[pltpu.VMEM((tm, d), BF16), pltpu.VMEM((tm, d), F32)],
        compiler_params=_params(("parallel", "arbitrary")),
        name="hmoe",
    )(x1, comb, mods_l, ln_pack, wg, wu, wd)


def _kv_kernel(x_ref, kvw_ref, fw_ref, fb_ref, tri_ref, k_ref, v_ref, kb_ref, vb_ref, lf_ref, fc_ref,
               carry_ref):
    d = x_ref.shape[1]

    @pl.when(pl.program_id(1) == 0)
    def _():
        carry_ref[...] = jnp.zeros_like(carry_ref)

    x = x_ref[...]
    kv = _bdot(x, kvw_ref[...])
    k, v = kv[:, :d], kv[:, d:]
    k_ref[...] = k
    v_ref[...] = v
    kb_ref[...] = k.astype(BF16)
    vb_ref[...] = v.astype(BF16)
    z = _dot3(x, fw_ref[...]) + fb_ref[...]
    lf = -_softplus(-z)
    lf_ref[...] = lf
    fc = _dot_exact_lhs(tri_ref[...], lf) + carry_ref[...]
    fc_ref[...] = fc
    carry_ref[...] = fc[fc.shape[0] - 1:, :]


def _kv_proj(x2d, kv_w, f_w, f_b, bsz, t, tm):
    n, d = x2d.shape
    nt = t // tm
    tok = pl.BlockSpec((tm, d), lambda b, i: (b * nt + i, 0))
    nar = pl.BlockSpec((tm, LANES), lambda b, i: (b * nt + i, 0))
    tri = jnp.tril(jnp.ones((tm, tm), BF16))
    full = lambda a: pl.BlockSpec(a.shape, lambda b, i: (0,) * a.ndim)
    return pl.pallas_call(
        _kv_kernel,
        grid=(bsz, nt),
        in_specs=[tok, full(kv_w), full(f_w), full(f_b), full(tri)],
        out_specs=[tok, tok, tok, tok, nar, nar],
        out_shape=[jax.ShapeDtypeStruct((n, d), F32), jax.ShapeDtypeStruct((n, d), F32),
                   jax.ShapeDtypeStruct((n, d), BF16), jax.ShapeDtypeStruct((n, d), BF16),
                   jax.ShapeDtypeStruct((n, LANES), F32), jax.ShapeDtypeStruct((n, LANES), F32)],
        scratch_shapes=[pltpu.VMEM((1, LANES), F32)],
        compiler_params=_params(("parallel", "arbitrary")),
        name="kv_proj",
    )(x2d, kv_w, f_w, f_b, tri)


def _q_kernel(x_ref, mods_ref, wq_ref, q_ref):
    h = x_ref[...] * (1.0 + mods_ref[1:2, :]) + mods_ref[0:1, :]
    q_ref[...] = (_bdot(h, wq_ref[...]) * (HEAD_DIM ** -0.5)).astype(BF16)


def _q_proj(x2d, mods_l, wq, bsz, t, tm):
    n, d = x2d.shape
    nt = t // tm
    tok = pl.BlockSpec((tm, d), lambda b, i: (b * nt + i, 0))
    return pl.pallas_call(
        _q_kernel,
        grid=(bsz, nt),
        in_specs=[tok, pl.BlockSpec((None, 6, d), lambda b, i: (b, 0, 0)),
                  pl.BlockSpec(wq.shape, lambda b, i: (0, 0))],
        out_specs=tok,
        out_shape=jax.ShapeDtypeStruct((n, d), BF16),
        compiler_params=_params(("parallel", "parallel")),
        name="q_proj",
    )(x2d, mods_l, wq)


def _fox_prompt_kernel(tq, q_ref, k_ref, v_ref, fq_ref, fk_ref, o_ref):
    t = q_ref.shape[0]
    lane = lax.broadcasted_iota(jnp.int32, (tq, LANES), 1)
    rr = lax.broadcasted_iota(jnp.int32, (tq, tq), 0)
    cc = lax.broadcasted_iota(jnp.int32, (tq, tq), 1)
    for qi in range(t // tq):
        lo, hi = qi * tq, (qi + 1) * tq
        q = q_ref[lo:hi, :]
        k, v = k_ref[:hi, :], v_ref[:hi, :]
        heads = []
        for hh in range(2):
            qm = jnp.where((lane < HEAD_DIM) == (hh == 0), q, jnp.zeros_like(q))
            s = _dg(qm, k, 1, 1) + fq_ref[lo:hi, hh:hh + 1] - fk_ref[hh:hh + 1, :hi]
            s_diag = jnp.where(cc <= rr, s[:, lo:], NEG_BIG)
            s = s_diag if qi == 0 else jnp.concatenate([s[:, :lo], s_diag], 1)
            p = jnp.exp(s - jnp.max(s, -1, keepdims=True))
            heads.append(_dg(p.astype(BF16), v, 1, 0) / jnp.sum(p, -1, keepdims=True))
        o_ref[lo:hi, :] = jnp.where(lane < HEAD_DIM, heads[0], heads[1])


def _fox_prompt(q, kb, vb, fcum, bsz, t, tq):
    n, d = q.shape
    npair = d // LANES
    nh = 2 * npair
    f = fcum[:, :nh].reshape(bsz, t, npair, 2)
    fq = f.transpose(0, 2, 1, 3)
    fk = f.transpose(0, 2, 3, 1)
    seq = pl.BlockSpec((t, LANES), lambda b, p: (b, p))
    return pl.pallas_call(
        functools.partial(_fox_prompt_kernel, tq),
        grid=(bsz, npair),
        in_specs=[seq, seq, seq,
                  pl.BlockSpec((None, None, t, 2), lambda b, p: (b, p, 0, 0)),
                  pl.BlockSpec((None, None, 2, t), lambda b, p: (b, p, 0, 0))],
        out_specs=seq,
        out_shape=jax.ShapeDtypeStruct((n, d), F32),
        compiler_params=_params(("parallel", "parallel")),
        name="fox_prompt",
    )(q, kb, vb, fq, fk)


def _fox_sample_kernel(q_ref, kn_ref, vn_ref, fn_ref, fnt_ref, ck_ref, cv_ref, clf_ref, upper_ref,
                       o_ref, m_ref, l_ref, acc_ref, carry_ref):
    step = pl.program_id(1)
    t, d = q_ref.shape
    nh = d // HEAD_DIM

    @pl.when(step == 0)
    def _():
        m_ref[...] = jnp.full_like(m_ref, NEG_BIG)
        l_ref[...] = jnp.zeros_like(l_ref)
        acc_ref[...] = jnp.zeros_like(acc_ref)
        carry_ref[...] = jnp.zeros_like(carry_ref)

    q = q_ref[...]
    lane = lax.broadcasted_iota(jnp.int32, (t, LANES), 1)

    def attend(h, k_slab, v_slab, bias, mask):
        sl = slice((h // 2) * LANES, (h // 2 + 1) * LANES)
        qs = q[:, sl]
        qm = jnp.where((lane < HEAD_DIM) == (h % 2 == 0), qs, jnp.zeros_like(qs))
        s = _dg(qm, k_slab, 1, 1) + bias
        if mask is not None:
            s = jnp.where(mask, s, NEG_BIG)
        m_old = m_ref[h]
        m_new = jnp.maximum(m_old, jnp.max(s, -1, keepdims=True))
        alpha = jnp.exp(m_old - m_new)
        p = jnp.exp(s - m_new)
        l_ref[h] = alpha * l_ref[h] + jnp.sum(p, -1, keepdims=True)
        acc_ref[h] = alpha * acc_ref[h] + _dg(p.astype(BF16), v_slab, 1, 0)
        m_ref[h] = m_new

    clf = clf_ref[...]
    suf = _dot_exact_rhs(clf, upper_ref[...]) + carry_ref[...]
    carry_ref[...] = carry_ref[...] + jnp.sum(clf, -1, keepdims=True)
    fn = fn_ref[...]
    for h in range(nh):
        sl = slice((h // 2) * LANES, (h // 2 + 1) * LANES)
        attend(h, ck_ref[:, sl].astype(BF16), cv_ref[:, sl].astype(BF16),
               fn[:, h:h + 1] + suf[h:h + 1, :], None)

    @pl.when(step == pl.num_programs(1) - 1)
    def _():
        rr = lax.broadcasted_iota(jnp.int32, (t, t), 0)
        cc = lax.broadcasted_iota(jnp.int32, (t, t), 1)
        fnt = fnt_ref[...]
        for h in range(nh):
            sl = slice((h // 2) * LANES, (h // 2 + 1) * LANES)
            attend(h, kn_ref[:, sl], vn_ref[:, sl], fn[:, h:h + 1] - fnt[h:h + 1, :], cc <= rr)
        for pr in range(nh // 2):
            sl = slice(pr * LANES, (pr + 1) * LANES)
            o_ref[:, sl] = jnp.where(lane < HEAD_DIM, acc_ref[2 * pr] / l_ref[2 * pr],
                                     acc_ref[2 * pr + 1] / l_ref[2 * pr + 1])


def _fox_sample(q, kb, vb, fcum, cache_k, cache_v, cache_logf, bsz, t, tk):
    n, d = q.shape
    nh = d // HEAD_DIM
    plen = cache_k.shape[1]
    nk = plen // tk
    ck = cache_k.reshape(bsz * plen, d)
    cv = cache_v.reshape(bsz * plen, d)
    clf_t = cache_logf.astype(F32).transpose(0, 2, 1)
    fn_t = fcum[:, :nh].reshape(bsz, t, nh).transpose(0, 2, 1)
    upper = (jnp.arange(tk)[:, None] > jnp.arange(tk)[None, :]).astype(BF16)
    tok = pl.BlockSpec((t, d), lambda b, j: (b, 0))
    past = pl.BlockSpec((tk, d), lambda b, j: (b * nk + nk - 1 - j, 0))
    return pl.pallas_call(
        _fox_sample_kernel,
        grid=(bsz, nk),
        in_specs=[tok, tok, tok,
                  pl.BlockSpec((t, LANES), lambda b, j: (b, 0)),
                  pl.BlockSpec((None, nh, t), lambda b, j: (b, 0, 0)),
                  past, past,
                  pl.BlockSpec((None, nh, tk), lambda b, j: (b, 0, nk - 1 - j)),
                  pl.BlockSpec((tk, tk), lambda b, j: (0, 0))],
        out_specs=tok,
        out_shape=jax.ShapeDtypeStruct((n, d), F32),
        scratch_shapes=[pltpu.VMEM((nh, t, 1), F32), pltpu.VMEM((nh, t, 1), F32),
                        pltpu.VMEM((nh, t, LANES), F32), pltpu.VMEM((nh, 1), F32)],
        compiler_params=_params(("parallel", "arbitrary")),
        name="fox_sample",
    )(q, kb, vb, fcum, fn_t, ck, cv, clf_t, upper)


def _tile(t, cap):
    return min(t, cap)


def _trunk(x, mods, shift_in, wkv_in, cache, w):
    bsz, t, d = x.shape
    depth = mods.shape[0]
    n_a = w["n_a"]
    nh = d // HEAD_DIM
    x2d = x.reshape(bsz * t, d)
    shifts, states = [], []
    kv = None
    for l in range(depth):
        mods_l = mods[l]
        if l < n_a:
            wl = w["rwkv"][l]
            proj, last = _rwkv_proj(x2d, shift_in[l], mods_l, wl, bsz, t, _tile(t, 256))
            y, s_new = _wkv(proj[:6], wkv_in[l], wl, bsz, t)
            shifts.append(last)
            states.append(s_new.astype(wkv_in.dtype))
            mixer_out, gate, w_o = y, proj[6], wl["w_o"]
        else:
            if kv is None:
                kv = _kv_proj(x2d, w["kv_w"], w["f_w"], w["f_b"], bsz, t, _tile(t, 512))
            k_sh, v_sh, kb, vb, lf, fcum = kv
            j = l - n_a
            q = _q_proj(x2d, mods_l, w["b_wq"][j], bsz, t, _tile(t, 512))
            if cache is None:
                mixer_out = _fox_prompt(q, kb, vb, fcum, bsz, t, _tile(t, 512))
            else:
                mixer_out = _fox_sample(q, kb, vb, fcum, cache[0], cache[1], cache[2], bsz, t,
                                        _tile(cache[0].shape[1], 512))
            gate, w_o = None, w["b_wo"][j]
        x1, comb = _post_mixer(x2d, mixer_out, gate, mods_l, w_o, w["ln"][l][0], w["router_w"][l],
                               w["router_b"][l], w["alpha"], bsz, t, _tile(t, 512))
        x2d = _moe(x1, comb, mods_l, w["ln"][l][1], w["exp_wg"][l], w["exp_wu"][l], w["exp_wd"][l],
                   w["alpha"], bsz, t, _tile(t, 1024))
    k_sh, v_sh, _, _, lf, _ = kv
    return (x2d.reshape(bsz, t, d), jnp.stack(shifts), jnp.stack(states),
            k_sh.reshape(bsz, t, nh, HEAD_DIM), v_sh.reshape(bsz, t, nh, HEAD_DIM),
            lf[:, :nh].reshape(bsz, t, nh).astype(x.dtype))


def _prepare_weights(ln_g, ln_b, a_mu, a_w_rkv, a_w0, a_w1, a_w2, a_a0, a_a1, a_a2, a_g1, a_g2, a_k_k,
                     a_k_a, a_r_k, a_lnx_g, a_lnx_b, a_w_o, kv_w, f_w, f_b, b_wq, b_wo, rg_w, rg_b,
                     re_w, re_b, exp_wg, exp_wu, exp_wd):
    depth, _, d = ln_g.shape
    n_a = a_mu.shape[0]
    nh = d // HEAD_DIM
    alpha = (2.0 * depth) ** 0.25
    zrow = jnp.zeros((d,), F32)
    head_of = jnp.arange(d) // HEAD_DIM
    head_down = (head_of[:, None] == jnp.arange(LANES)[None, :]).astype(BF16)
    rwkv = []
    for l in range(n_a):
        rwkv.append(dict(
            mu=a_mu[l],
            vec=jnp.stack([a_w0[l], a_a0[l], a_k_k[l], a_k_a[l], zrow, zrow, zrow, zrow]),
            scan_vec=jnp.stack([a_r_k[l].reshape(d), a_lnx_g[l], a_lnx_b[l], zrow, zrow, zrow, zrow, zrow]),
            w_rkv=a_w_rkv[l].astype(BF16), w1=a_w1[l].astype(BF16), w2=a_w2[l].astype(BF16),
            a1=a_a1[l].astype(BF16), a2=a_a2[l].astype(BF16), g1=a_g1[l].astype(BF16),
            g2=a_g2[l].astype(BF16), w_o=a_w_o[l].astype(BF16),
            head_down=head_down, head_up=head_down.T))
    ln = [[jnp.stack([ln_g[l, s], ln_b[l, s], zrow, zrow, zrow, zrow, zrow, zrow])
           for s in range(2)] for l in range(depth)]
    pad = LANES - N_GROUPS - N_EXPERTS
    router_w = [jnp.concatenate([rg_w[l], re_w[l], jnp.zeros((d, pad), F32)], 1) for l in range(depth)]
    router_b = [jnp.concatenate([rg_b[l], re_b[l], jnp.zeros((pad,), F32)])[None, :] for l in range(depth)]
    return dict(
        n_a=n_a, alpha=alpha, rwkv=rwkv, ln=ln, router_w=router_w, router_b=router_b,
        kv_w=kv_w.astype(BF16),
        f_w=jnp.concatenate([f_w, jnp.zeros((d, LANES - nh), F32)], 1),
        f_b=jnp.concatenate([f_b, jnp.zeros((LANES - nh,), F32)])[None, :],
        b_wq=b_wq.astype(BF16), b_wo=b_wo.astype(BF16),
        exp_wg=exp_wg.astype(BF16), exp_wu=exp_wu.astype(BF16), exp_wd=exp_wd.astype(BF16))


def kernel(x_prompt, x_sample, state_shift, state_wkv, cache_k, cache_v, cache_logf, c_prompt, c_sample,
           ada_w, ada_b, ln_g, ln_b, a_mu, a_w_rkv, a_w0, a_w1, a_w2, a_a0, a_a1, a_a2, a_g1, a_g2, a_k_k,
           a_k_a, a_r_k, a_lnx_g, a_lnx_b, a_w_o, kv_w, f_w, f_b, b_wq, b_wo, rg_w, rg_b, re_w, re_b,
           exp_wg, exp_wu, exp_wd):
    w = _prepare_weights(ln_g, ln_b, a_mu, a_w_rkv, a_w0, a_w1, a_w2, a_a0, a_a1, a_a2, a_g1, a_g2,
                         a_k_k, a_k_a, a_r_k, a_lnx_g, a_lnx_b, a_w_o, kv_w, f_w, f_b, b_wq, b_wo,
                         rg_w, rg_b, re_w, re_b, exp_wg, exp_wu, exp_wd)
    bp, _, d = x_prompt.shape
    n_a = a_mu.shape[0]
    nh = d // HEAD_DIM
    mods = _ada_mods(jnp.concatenate([c_prompt, c_sample], 0), ada_w, ada_b)
    zero_shift = jnp.zeros((n_a, bp, d), x_prompt.dtype)
    zero_wkv = jnp.zeros((n_a, bp, nh, HEAD_DIM, HEAD_DIM), x_prompt.dtype)
    outs_p = _trunk(x_prompt, mods[:, :bp], zero_shift, zero_wkv, None, w)
    outs_s = _trunk(x_sample, mods[:, bp:], state_shift, state_wkv, (cache_k, cache_v, cache_logf), w)
    y_p, p_shift, p_wkv, p_k, p_v, p_logf = outs_p
    y_s, s_shift, s_wkv, s_k, s_v, s_logf = outs_s
    return (y_p, y_s, p_shift, p_wkv, p_k, p_v, p_logf, s_shift, s_wkv, s_k, s_v, s_logf)
```

```python
import functools

import jax
import jax.numpy as jnp
from jax import lax
from jax.experimental import pallas as pl
from jax.experimental.pallas import tpu as pltpu

F32 = jnp.float32
BF16 = jnp.bfloat16

HEAD_DIM = 64
LANES = 128
N_GROUPS = 4
EXP_PER_GROUP = 4
N_EXPERTS = N_GROUPS * EXP_PER_GROUP
LN_EPS = 1e-5
GN_EPS = 64e-5
WKV_CHUNK = 64
MOE_EXPERTS_PER_STEP = 2
LOG2E = 1.4426950408889634
VMEM_LIMIT = 56 * 1024 * 1024
NEG_BIG = -1e30


def _params(sem):
    return pltpu.CompilerParams(dimension_semantics=sem, vmem_limit_bytes=VMEM_LIMIT)


def _dg(a, b, ca, cb):
    return lax.dot_general(a, b, (((ca,), (cb,)), ((), ())), preferred_element_type=F32)


def _bdot(a, b, ca=1, cb=0):
    return _dg(a.astype(BF16), b.astype(BF16), ca, cb)


def _split3(x):
    hi = x.astype(BF16)
    r1 = x - hi.astype(F32)
    mid = r1.astype(BF16)
    lo = (r1 - mid.astype(F32)).astype(BF16)
    return hi, mid, lo


def _dot_exact_rhs(a, b_exact, ca=1, cb=0):
    hi, mid, lo = _split3(a)
    bb = b_exact.astype(BF16)
    return _dg(hi, bb, ca, cb) + _dg(mid, bb, ca, cb) + _dg(lo, bb, ca, cb)


def _dot_exact_lhs(a_exact, b, ca=1, cb=0):
    hi, mid, lo = _split3(b)
    aa = a_exact.astype(BF16)
    return _dg(aa, hi, ca, cb) + _dg(aa, mid, ca, cb) + _dg(aa, lo, ca, cb)


def _dot3(a, b, ca=1, cb=0):
    ah = a.astype(BF16)
    al = (a - ah.astype(F32)).astype(BF16)
    bh = b.astype(BF16)
    bl = (b - bh.astype(F32)).astype(BF16)
    return _dg(ah, bh, ca, cb) + _dg(ah, bl, ca, cb) + _dg(al, bh, ca, cb)


_inv_dot = _bdot


def _layer_norm(z, g, b):
    mu = jnp.mean(z, -1, keepdims=True)
    d = z - mu
    var = jnp.mean(d * d, -1, keepdims=True)
    return d * lax.rsqrt(var + LN_EPS) * g + b


def _softplus(z):
    return jnp.maximum(z, 0.0) + jnp.log(1.0 + jnp.exp(-jnp.abs(z)))


def _sigmoid(z):
    return 1.0 / (1.0 + jnp.exp(-z))


def _silu(z):
    return z * _sigmoid(z)


def _ada_kernel(c_ref, w_ref, b_ref, o_ref):
    o_ref[...] = _dot3(_silu(c_ref[...]), w_ref[...]) + b_ref[...]


def _ada_mods(c_all, ada_w, ada_b):
    depth, d, d6 = ada_w.shape
    bsz = c_all.shape[0]
    tn = d
    out = pl.pallas_call(
        _ada_kernel,
        grid=(depth, d6 // tn),
        in_specs=[
            pl.BlockSpec((bsz, d), lambda l, j: (0, 0)),
            pl.BlockSpec((None, d, tn), lambda l, j: (l, 0, j)),
            pl.BlockSpec((None, 1, tn), lambda l, j: (l, 0, j)),
        ],
        out_specs=pl.BlockSpec((None, bsz, tn), lambda l, j: (l, 0, j)),
        out_shape=jax.ShapeDtypeStruct((depth, bsz, d6), F32),
        compiler_params=_params(("parallel", "parallel")),
        name="ada_mods",
    )(c_all, ada_w, ada_b.reshape(depth, 1, d6))
    return out.reshape(depth, bsz, 6, d)


def _rwkv_proj_kernel(x_ref, xp_ref, shift_ref, mods_ref, mu_ref, vec_ref, wrkv_ref, w1_ref, w2_ref,
                      a1_ref, a2_ref, g1_ref, g2_ref, hd_ref, hu_ref,
                      r_ref, lw_ref, k_ref, v_ref, kk_ref, b_ref, g_ref, last_ref):
    tm = x_ref.shape[0]
    sh1 = mods_ref[0:1, :]
    sc1 = mods_ref[1:2, :]
    h = x_ref[...] * (1.0 + sc1) + sh1
    h_prev_tile = xp_ref[7:8, :] * (1.0 + sc1) + sh1
    prev_row = jnp.where(pl.program_id(1) == 0, shift_ref[...], h_prev_tile)
    row = lax.broadcasted_iota(jnp.int32, (tm, 1), 0)
    xx = jnp.where(row == 0, prev_row, pltpu.roll(h, 1, axis=0)) - h

    def mix(i):
        return h + xx * mu_ref[i:i + 1, :]

    w0, a0 = vec_ref[0:1, :], vec_ref[1:2, :]
    k_k, k_a = vec_ref[2:3, :], vec_ref[3:4, :]
    r = _bdot(mix(0), wrkv_ref[0])
    k = _bdot(mix(1), wrkv_ref[1])
    v = _bdot(mix(2), wrkv_ref[2])
    ww = w0 + _bdot(jnp.tanh(_bdot(mix(3), w1_ref[...])), w2_ref[...])
    w_log = -_softplus(-ww) - 0.5
    a = _sigmoid(a0 + _bdot(_bdot(mix(4), a1_ref[...]), a2_ref[...]))
    g = _bdot(_sigmoid(_bdot(mix(5), g1_ref[...])), g2_ref[...])
    kk = k * k_k
    ss = _dot_exact_rhs(_dot_exact_rhs(kk * kk, hd_ref[...]), hu_ref[...])
    kk = kk * lax.rsqrt(jnp.maximum(ss, 1e-24))
    r_ref[...] = r.astype(BF16)
    lw_ref[...] = -jnp.exp(w_log)
    k_ref[...] = (k * (1.0 + (a - 1.0) * k_a)).astype(BF16)
    v_ref[...] = v.astype(BF16)
    kk_ref[...] = kk.astype(BF16)
    b_ref[...] = (kk * a).astype(BF16)
    g_ref[...] = g.astype(BF16)
    last_ref[...] = h[tm - 1:tm, :]


def _rwkv_proj(x2d, shift_prev, mods_l, wl, bsz, t, tm):
    n, d = x2d.shape
    nt = t // tm
    tok = pl.BlockSpec((tm, d), lambda b, i: (b * nt + i, 0))
    full = lambda a: pl.BlockSpec(a.shape, lambda b, i: (0,) * a.ndim)
    weights = [wl["mu"], wl["vec"], wl["w_rkv"], wl["w1"], wl["w2"], wl["a1"], wl["a2"], wl["g1"],
               wl["g2"], wl["head_down"], wl["head_up"]]
    outs = pl.pallas_call(
        _rwkv_proj_kernel,
        grid=(bsz, nt),
        in_specs=[
            tok,
            pl.BlockSpec((8, d), lambda b, i: (jnp.maximum((b * nt + i) * (tm // 8) - 1, 0), 0)),
            pl.BlockSpec((None, 1, d), lambda b, i: (b, 0, 0)),
            pl.BlockSpec((None, 6, d), lambda b, i: (b, 0, 0)),
        ] + [full(a) for a in weights],
        out_specs=[tok] * 7 + [pl.BlockSpec((None, 1, d), lambda b, i: (b, 0, 0))],
        out_shape=[jax.ShapeDtypeStruct((n, d), F32 if i == 1 else BF16) for i in range(7)]
        + [jax.ShapeDtypeStruct((bsz, 1, d), F32)],
        compiler_params=_params(("parallel", "arbitrary")),
        name="rwkv_proj",
    )(x2d, x2d, shift_prev.reshape(bsz, 1, d), mods_l, *weights)
    return outs[:7], outs[7].reshape(bsz, d)


def _wkv_kernel(r_ref, lw_ref, k_ref, v_ref, kk_ref, b_ref, s0_ref, vec_ref, tri_ref, ones_ref,
                y_ref, s_out_ref, state_ref):
    c = r_ref.shape[0]
    n_pairs = r_ref.shape[1] // LANES
    first = pl.program_id(1) == 0

    @pl.when(first)
    def _():
        state_ref[...] = s0_ref[...]

    lane = lax.broadcasted_iota(jnp.int32, (c, LANES), 1)
    low = lane < HEAD_DIM
    ri = lax.broadcasted_iota(jnp.int32, (2 * c, 2 * c), 0)
    ci = lax.broadcasted_iota(jnp.int32, (2 * c, 2 * c), 1)
    same = (ri >= c) == (ci >= c)
    strict = same & (ci < ri)
    incl = same & (ci <= ri)
    eye = (ri == ci).astype(F32)
    tri = tri_ref[...]
    ones_bd = ones_ref[...]

    def stack(z):
        return jnp.concatenate([jnp.where(low, z, 0.0), jnp.where(low, 0.0, z)], axis=0)

    pairs = range(n_pairs)
    slabs = [slice(p * LANES, (p + 1) * LANES) for p in pairs]
    lw_all = lw_ref[...]
    cum_all = _dot_exact_lhs(tri, lw_all)
    prev_all = cum_all - lw_all
    mid_all = cum_all[c // 2 - 1:c // 2, :]
    end_all = cum_all[c - 1:c, :]
    e_in = jnp.exp(mid_all - cum_all)
    e_out = jnp.exp(end_all - cum_all)
    e_end = jnp.exp(end_all)
    r_all, k_all, v_all = r_ref[...].astype(F32), k_ref[...].astype(F32), v_ref[...].astype(F32)
    a_all, b_all = -kk_ref[...].astype(F32), b_ref[...].astype(F32)
    aq, rq = a_all * jnp.exp(prev_all - mid_all), r_all * jnp.exp(cum_all - mid_all)
    bi, ki = b_all * e_in, k_all * e_in
    a0, r0 = a_all * jnp.exp(prev_all), r_all * jnp.exp(cum_all)
    bo, ko = b_all * e_out, k_all * e_out

    gram = [_bdot(jnp.concatenate([stack(aq[:, s]), stack(rq[:, s])], 0),
                  jnp.concatenate([stack(bi[:, s]), stack(ki[:, s])], 0), 1, 1) for s in slabs]
    a_ab = [jnp.where(strict, g[:2 * c, :2 * c], 0.0) for g in gram]
    a_ak = [jnp.where(strict, g[:2 * c, 2 * c:], 0.0) for g in gram]
    a_r = [jnp.concatenate([jnp.where(incl, g[2 * c:, :2 * c], 0.0),
                            jnp.where(incl, g[2 * c:, 2 * c:], 0.0)], 1) for g in gram]
    tinv = [eye + m for m in a_ab]
    pw = a_ab
    for _ in range(max(c.bit_length() - 2, 0)):
        pw = [_inv_dot(m, m) for m in pw]
        tinv = [t_ + _inv_dot(t_, m) for t_, m in zip(tinv, pw)]
    s_prev = [state_ref[p] for p in pairs]
    from_state = [_bdot(jnp.concatenate([stack(a0[:, s]), stack(r0[:, s])], 0), sp, 1, 1)
                  for s, sp in zip(slabs, s_prev)]
    v_st = [stack(v_all[:, s]) for s in slabs]
    rhs = [fs[:2 * c] + _bdot(m, vs) for fs, m, vs in zip(from_state, a_ak, v_st)]
    u = [_inv_dot(t_, x) for t_, x in zip(tinv, rhs)]
    uv = [jnp.concatenate([u_, vs], 0) for u_, vs in zip(u, v_st)]
    o_st = [fs[2 * c:] + _bdot(m, x) for fs, m, x in zip(from_state, a_r, uv)]
    o = [x[:c] + x[c:] for x in o_st]
    new_state = [sp * e_end[:, s] + _bdot(x, jnp.concatenate([stack(bo[:, s]), stack(ko[:, s])], 0), 0, 0)
                 for s, sp, x in zip(slabs, s_prev, uv)]
    state_ref[...] = jnp.stack(new_state)
    def rows(z):
        return jnp.concatenate([z[:, s] for s in slabs], 0)

    def lanes(z):
        return jnp.concatenate([z[p * c:(p + 1) * c] for p in pairs], 1)

    o_rows = jnp.concatenate(o, 0)
    dev = o_rows - _dot_exact_rhs(o_rows, ones_bd) * (1.0 / HEAD_DIM)
    var = _dot_exact_rhs(dev * dev, ones_bd) * (1.0 / HEAD_DIM)
    bonus = _dot_exact_rhs(rows(r_all * k_all * vec_ref[0:1, :]), ones_bd)
    y_ref[...] = (lanes(dev * lax.rsqrt(var + GN_EPS)) * vec_ref[1:2, :] + vec_ref[2:3, :]
                  + lanes(bonus) * v_all)

    @pl.when(pl.program_id(1) == pl.num_programs(1) - 1)
    def _():
        s_out_ref[...] = state_ref[...]


def _pair_states(s):
    bsz, nh, n, _ = s.shape
    s = s.reshape(bsz, nh // 2, 2, n, n)
    z = jnp.zeros_like(s[:, :, 0])
    top = jnp.concatenate([s[:, :, 0], z], -1)
    bot = jnp.concatenate([z, s[:, :, 1]], -1)
    return jnp.concatenate([top, bot], -2)


def _unpair_states(sp):
    bsz, npair, _, _ = sp.shape
    n = HEAD_DIM
    return jnp.stack([sp[:, :, :n, :n], sp[:, :, n:, n:]], 2).reshape(bsz, 2 * npair, n, n)


def _wkv(proj, s0, wl, bsz, t):
    r, lw, k, v, kk, b = proj
    n, d = r.shape
    c = min(WKV_CHUNK, t)
    nc = t // c
    npair = d // LANES
    tok = pl.BlockSpec((c, d), lambda bb, i: (bb * nc + i, 0))
    st = pl.BlockSpec((None, npair, LANES, LANES), lambda bb, i: (bb, 0, 0, 0))
    tri = jnp.tril(jnp.ones((c, c), BF16))
    hid = jnp.arange(LANES) // HEAD_DIM
    ones_bd = (hid[:, None] == hid[None, :]).astype(BF16)
    y, s_out = pl.pallas_call(
        _wkv_kernel,
        grid=(bsz, nc),
        in_specs=[tok] * 6 + [st,
                              pl.BlockSpec((8, d), lambda bb, i: (0, 0)),
                              pl.BlockSpec((c, c), lambda bb, i: (0, 0)),
                              pl.BlockSpec((LANES, LANES), lambda bb, i: (0, 0))],
        out_specs=[tok, st],
        out_shape=[jax.ShapeDtypeStruct((n, d), F32),
                   jax.ShapeDtypeStruct((bsz, npair, LANES, LANES), F32)],
        scratch_shapes=[pltpu.VMEM((npair, LANES, LANES), F32)],
        compiler_params=_params(("parallel", "arbitrary")),
        name="wkv_scan",
    )(r, lw, k, v, kk, b, _pair_states(s0.astype(F32)), wl["scan_vec"], tri, ones_bd)
    return y, _unpair_states(s_out)


def _router(logits):
    lane_i = lax.broadcasted_iota(jnp.int32, logits.shape, 1)
    lane = lane_i.astype(F32)
    far = 1e9
    is_g = lane_i < N_GROUPS
    gl = jnp.where(is_g, logits, NEG_BIG)
    gmax = jnp.max(gl, -1, keepdims=True)
    gsel = jnp.min(jnp.where(gl == gmax, lane, far), -1, keepdims=True)
    gprob = 1.0 / jnp.sum(jnp.where(is_g, jnp.exp(gl - gmax), 0.0), -1, keepdims=True)
    group_of = lax.shift_right_arithmetic(lane_i - N_GROUPS, 2).astype(F32)
    in_group = (lane_i >= N_GROUPS) & (lane_i < N_GROUPS + N_EXPERTS) & (group_of == gsel)
    el = jnp.where(in_group, logits, NEG_BIG)
    v1 = jnp.max(el, -1, keepdims=True)
    i1 = jnp.min(jnp.where(el == v1, lane, far), -1, keepdims=True)
    el2 = jnp.where(lane == i1, NEG_BIG, el)
    v2 = jnp.max(el2, -1, keepdims=True)
    i2 = jnp.min(jnp.where(el2 == v2, lane, far), -1, keepdims=True)
    e2 = jnp.exp(v2 - v1)
    w1 = gprob / (1.0 + e2)
    w2 = gprob * e2 / (1.0 + e2)
    return jnp.where(lane == i1, w1, 0.0) + jnp.where(lane == i2, w2, 0.0)


def _post_kernel(gated, alpha, *refs):
    if gated:
        x_ref, y_ref, g_ref, mods_ref, wo_ref, ln_ref, rw_ref, rb_ref, x1_ref, comb_ref = refs
        y = y_ref[...] * g_ref[...].astype(F32)
    else:
        x_ref, y_ref, mods_ref, wo_ref, ln_ref, rw_ref, rb_ref, x1_ref, comb_ref = refs
        y = y_ref[...]
    gt1, sh2, sc2 = mods_ref[2:3, :], mods_ref[3:4, :], mods_ref[4:5, :]
    x1 = _layer_norm(alpha * x_ref[...] + gt1 * _bdot(y, wo_ref[...]), ln_ref[0:1, :], ln_ref[1:2, :])
    x1_ref[...] = x1
    h2 = x1 * (1.0 + sc2) + sh2
    comb_ref[...] = _router(_dot3(h2, rw_ref[...]) + rb_ref[...])


def _post_mixer(x2d, y2d, g2d, mods_l, w_o, ln_pack, router_w, router_b, alpha, bsz, t, tm):
    n, d = x2d.shape
    nt = t // tm
    tok = pl.BlockSpec((tm, d), lambda b, i: (b * nt + i, 0))
    full = lambda a: pl.BlockSpec(a.shape, lambda b, i: (0,) * a.ndim)
    gated = g2d is not None
    acts = [x2d, y2d] + ([g2d] if gated else [])
    consts = [w_o, ln_pack, router_w, router_b]
    return pl.pallas_call(
        functools.partial(_post_kernel, gated, alpha),
        grid=(bsz, nt),
        in_specs=[tok] * len(acts) + [pl.BlockSpec((None, 6, d), lambda b, i: (b, 0, 0))]
        + [full(a) for a in consts],
        out_specs=[tok, pl.BlockSpec((tm, LANES), lambda b, i: (b * nt + i, 0))],
        out_shape=[jax.ShapeDtypeStruct((n, d), F32), jax.ShapeDtypeStruct((n, LANES), F32)],
        compiler_params=_params(("parallel", "parallel")),
        name="post_mixer",
    )(*acts, mods_l, *consts)


def _moe_kernel(alpha, x1_ref, comb_ref, mods_ref, ln_ref, wg_ref, wu_ref, wd_ref, x2_ref, h2_ref, acc_ref):
    e = pl.program_id(1)

    @pl.when(e == 0)
    def _():
        sh2, sc2 = mods_ref[3:4, :], mods_ref[4:5, :]
        h2_ref[...] = (x1_ref[...] * (1.0 + sc2) + sh2).astype(BF16)
        acc_ref[...] = jnp.zeros_like(acc_ref)

    h2 = h2_ref[...]
    per_step, f, d = wd_ref.shape
    lane = lax.broadcasted_iota(jnp.int32, comb_ref.shape, 1)
    comb = comb_ref[...]
    hidden = []
    for i in range(per_step):
        hg = _dg(h2, wg_ref[i].astype(BF16), 1, 0)
        hu = _dg(h2, wu_ref[i].astype(BF16), 1, 0)
        ce = jnp.sum(jnp.where(lane == e * per_step + (i + N_GROUPS), comb, 0.0), -1, keepdims=True)
        hidden.append((_silu(hg) * hu * ce).astype(BF16))
    acc_ref[...] += _dg(jnp.concatenate(hidden, 1), wd_ref[...].reshape(per_step * f, d).astype(BF16), 1, 0)

    @pl.when(e == pl.num_programs(1) - 1)
    def _():
        gt2 = mods_ref[5:6, :]
        x2_ref[...] = _layer_norm(alpha * x1_ref[...] + gt2 * acc_ref[...], ln_ref[0:1, :], ln_ref[1:2, :])


def _moe(x1, comb, mods_l, ln_pack, wg, wu, wd, alpha, bsz, t, tm):
    n, d = x1.shape
    nt = t // tm
    ne, _, f = wg.shape
    per_step = MOE_EXPERTS_PER_STEP
    return pl.pallas_call(
        functools.partial(_moe_kernel, alpha),
        grid=(n // tm, ne // per_step),
        in_specs=[
            pl.BlockSpec((tm, d), lambda i, e: (i, 0)),
            pl.BlockSpec((tm, LANES), lambda i, e: (i, 0)),
            pl.BlockSpec((None, 6, d), lambda i, e: (i // nt, 0, 0)),
            pl.BlockSpec(ln_pack.shape, lambda i, e: (0, 0)),
            pl.BlockSpec((per_step, d, f), lambda i, e: (e, 0, 0)),
            pl.BlockSpec((per_step, d, f), lambda i, e: (e, 0, 0)),
            pl.BlockSpec((per_step, f, d), lambda i, e: (e, 0, 0)),
        ],
        out_specs=pl.BlockSpec((tm, d), lambda i, e: (i, 0)),
        out_shape=jax.ShapeDtypeStruct((n, d), F32),
        scratch_shapes=[pltpu.VMEM((tm, d), BF16), pltpu.VMEM((tm, d), F32)],
        compiler_params=_params(("parallel", "arbitrary")),
        name="hmoe",
    )(x1, comb, mods_l, ln_pack, wg, wu, wd)


def _kv_kernel(x_ref, kvw_ref, fw_ref, fb_ref, tri_ref, k_ref, v_ref, kb_ref, vb_ref, lf_ref, fc_ref,
               carry_ref):
    d = x_ref.shape[1]

    @pl.when(pl.program_id(1) == 0)
    def _():
        carry_ref[...] = jnp.zeros_like(carry_ref)

    x = x_ref[...]
    kv = _bdot(x, kvw_ref[...])
    k, v = kv[:, :d], kv[:, d:]
    k_ref[...] = k
    v_ref[...] = v
    kb_ref[...] = k.astype(BF16)
    vb_ref[...] = v.astype(BF16)
    z = _dot3(x, fw_ref[...]) + fb_ref[...]
    lf = -_softplus(-z)
    lf_ref[...] = lf
    fc = _dot_exact_lhs(tri_ref[...], lf) + carry_ref[...]
    fc_ref[...] = fc
    carry_ref[...] = fc[fc.shape[0] - 1:, :]


def _kv_proj(x2d, kv_w, f_w, f_b, bsz, t, tm):
    n, d = x2d.shape
    nt = t // tm
    tok = pl.BlockSpec((tm, d), lambda b, i: (b * nt + i, 0))
    nar = pl.BlockSpec((tm, LANES), lambda b, i: (b * nt + i, 0))
    tri = jnp.tril(jnp.ones((tm, tm), BF16))
    full = lambda a: pl.BlockSpec(a.shape, lambda b, i: (0,) * a.ndim)
    return pl.pallas_call(
        _kv_kernel,
        grid=(bsz, nt),
        in_specs=[tok, full(kv_w), full(f_w), full(f_b), full(tri)],
        out_specs=[tok, tok, tok, tok, nar, nar],
        out_shape=[jax.ShapeDtypeStruct((n, d), F32), jax.ShapeDtypeStruct((n, d), F32),
                   jax.ShapeDtypeStruct((n, d), BF16), jax.ShapeDtypeStruct((n, d), BF16),
                   jax.ShapeDtypeStruct((n, LANES), F32), jax.ShapeDtypeStruct((n, LANES), F32)],
        scratch_shapes=[pltpu.VMEM((1, LANES), F32)],
        compiler_params=_params(("parallel", "arbitrary")),
        name="kv_proj",
    )(x2d, kv_w, f_w, f_b, tri)


def _q_kernel(scale, x_ref, mods_ref, wq_ref, q_ref):
    h = x_ref[...] * (1.0 + mods_ref[1:2, :]) + mods_ref[0:1, :]
    q_ref[...] = (_bdot(h, wq_ref[...]) * scale).astype(BF16)


def _q_proj(x2d, mods_l, wq, scale, bsz, t, tm):
    n, d = x2d.shape
    nt = t // tm
    tok = pl.BlockSpec((tm, d), lambda b, i: (b * nt + i, 0))
    return pl.pallas_call(
        functools.partial(_q_kernel, scale),
        grid=(bsz, nt),
        in_specs=[tok, pl.BlockSpec((None, 6, d), lambda b, i: (b, 0, 0)),
                  pl.BlockSpec(wq.shape, lambda b, i: (0, 0))],
        out_specs=tok,
        out_shape=jax.ShapeDtypeStruct((n, d), BF16),
        compiler_params=_params(("parallel", "parallel")),
        name="q_proj",
    )(x2d, mods_l, wq)


def _fox_prompt_kernel(tq, q_ref, k_ref, v_ref, qb_ref, kb_ref, o_ref):
    t = q_ref.shape[0]
    lane = lax.broadcasted_iota(jnp.int32, (1, LANES), 1)
    rr = lax.broadcasted_iota(jnp.int32, (tq, tq), 0)
    cc = lax.broadcasted_iota(jnp.int32, (tq, tq), 1)
    own = [lane < HEAD_DIM, lane >= HEAD_DIM]
    k_aug = [jnp.where(own[hh], k_ref[...], kb_ref[...]) for hh in range(2)]
    v = v_ref[...]
    for qi in range(t // tq):
        lo, hi = qi * tq, (qi + 1) * tq
        heads = []
        for hh in range(2):
            q = jnp.where(own[hh], q_ref[lo:hi, :], qb_ref[lo:hi, :])
            s_diag = jnp.where(cc <= rr, _dg(q, k_aug[hh][lo:hi], 1, 1), NEG_BIG)
            m = jnp.max(s_diag, -1, keepdims=True)
            if qi > 0:
                s_past = _dg(q, k_aug[hh][:lo], 1, 1)
                m = jnp.maximum(m, jnp.max(s_past, -1, keepdims=True))
            p = jnp.exp2(s_diag - m)
            num = _dg(p.astype(BF16), v[lo:hi], 1, 0)
            den = jnp.sum(p, -1, keepdims=True)
            if qi > 0:
                p = jnp.exp2(s_past - m)
                num = num + _dg(p.astype(BF16), v[:lo], 1, 0)
                den = den + jnp.sum(p, -1, keepdims=True)
            heads.append(num / den)
        o_ref[lo:hi, :] = jnp.where(own[0], heads[0], heads[1])


def _fox_bias_slabs(fcum, bsz, t, npair):
    f2 = (fcum[:, :2 * npair] * LOG2E).reshape(bsz, t, npair, 2)
    hi = f2.astype(BF16)
    r1 = f2 - hi.astype(F32)
    mid = r1.astype(BF16)
    lo = (r1 - mid.astype(F32)).astype(BF16)
    terms = jnp.stack([hi, mid, lo], -1)
    ones = jnp.ones_like(terms)
    pad = jnp.zeros((bsz, t, npair, HEAD_DIM - 6), BF16)

    def slab(first, second):
        return jnp.concatenate([first[..., 1, :], second[..., 1, :], pad,
                                first[..., 0, :], second[..., 0, :], pad], -1).transpose(0, 2, 1, 3)

    return slab(terms, ones), slab(ones, -terms)


def _fox_prompt(q, kb, vb, bias_slabs, bsz, t, tq):
    n, d = q.shape
    npair = d // LANES
    seq = pl.BlockSpec((t, LANES), lambda b, p: (b, p))
    slab = pl.BlockSpec((None, None, t, LANES), lambda b, p: (b, p, 0, 0))
    return pl.pallas_call(
        functools.partial(_fox_prompt_kernel, tq),
        grid=(bsz, npair),
        in_specs=[seq, seq, seq, slab, slab],
        out_specs=seq,
        out_shape=jax.ShapeDtypeStruct((n, d), F32),
        compiler_params=_params(("parallel", "parallel")),
        name="fox_prompt",
    )(q, kb, vb, *bias_slabs)


def _fox_sample_kernel(q_ref, kn_ref, vn_ref, fn_ref, fnt_ref, ck_ref, cv_ref, clf_ref, upper_ref,
                       o_ref, m_ref, l_ref, acc_ref, carry_ref):
    step = pl.program_id(1)
    t, d = q_ref.shape
    nh = d // HEAD_DIM

    @pl.when(step == 0)
    def _():
        m_ref[...] = jnp.full_like(m_ref, NEG_BIG)
        l_ref[...] = jnp.zeros_like(l_ref)
        acc_ref[...] = jnp.zeros_like(acc_ref)
        carry_ref[...] = jnp.zeros_like(carry_ref)

    q = q_ref[...]
    lane = lax.broadcasted_iota(jnp.int32, (t, LANES), 1)

    def attend(h, k_slab, v_slab, bias, mask):
        sl = slice((h // 2) * LANES, (h // 2 + 1) * LANES)
        qs = q[:, sl]
        qm = jnp.where((lane < HEAD_DIM) == (h % 2 == 0), qs, jnp.zeros_like(qs))
        s = _dg(qm, k_slab, 1, 1) + bias
        if mask is not None:
            s = jnp.where(mask, s, NEG_BIG)
        m_old = m_ref[h]
        m_new = jnp.maximum(m_old, jnp.max(s, -1, keepdims=True))
        alpha = jnp.exp(m_old - m_new)
        p = jnp.exp(s - m_new)
        l_ref[h] = alpha * l_ref[h] + jnp.sum(p, -1, keepdims=True)
        acc_ref[h] = alpha * acc_ref[h] + _dg(p.astype(BF16), v_slab, 1, 0)
        m_ref[h] = m_new

    clf = clf_ref[...]
    suf = _dot_exact_rhs(clf, upper_ref[...]) + carry_ref[...]
    carry_ref[...] = carry_ref[...] + jnp.sum(clf, -1, keepdims=True)
    fn = fn_ref[...]
    for h in range(nh):
        sl = slice((h // 2) * LANES, (h // 2 + 1) * LANES)
        attend(h, ck_ref[:, sl].astype(BF16), cv_ref[:, sl].astype(BF16),
               fn[:, h:h + 1] + suf[h:h + 1, :], None)

    @pl.when(step == pl.num_programs(1) - 1)
    def _():
        rr = lax.broadcasted_iota(jnp.int32, (t, t), 0)
        cc = lax.broadcasted_iota(jnp.int32, (t, t), 1)
        fnt = fnt_ref[...]
        for h in range(nh):
            sl = slice((h // 2) * LANES, (h // 2 + 1) * LANES)
            attend(h, kn_ref[:, sl], vn_ref[:, sl], fn[:, h:h + 1] - fnt[h:h + 1, :], cc <= rr)
        for pr in range(nh // 2):
            sl = slice(pr * LANES, (pr + 1) * LANES)
            o_ref[:, sl] = jnp.where(lane < HEAD_DIM, acc_ref[2 * pr] / l_ref[2 * pr],
                                     acc_ref[2 * pr + 1] / l_ref[2 * pr + 1])


def _fox_sample(q, kb, vb, fcum, cache_k, cache_v, cache_logf, bsz, t, tk):
    n, d = q.shape
    nh = d // HEAD_DIM
    plen = cache_k.shape[1]
    nk = plen // tk
    ck = cache_k.reshape(bsz * plen, d)
    cv = cache_v.reshape(bsz * plen, d)
    clf_t = cache_logf.astype(F32).transpose(0, 2, 1)
    fn_t = fcum[:, :nh].reshape(bsz, t, nh).transpose(0, 2, 1)
    upper = (jnp.arange(tk)[:, None] > jnp.arange(tk)[None, :]).astype(BF16)
    tok = pl.BlockSpec((t, d), lambda b, j: (b, 0))
    past = pl.BlockSpec((tk, d), lambda b, j: (b * nk + nk - 1 - j, 0))
    return pl.pallas_call(
        _fox_sample_kernel,
        grid=(bsz, nk),
        in_specs=[tok, tok, tok,
                  pl.BlockSpec((t, LANES), lambda b, j: (b, 0)),
                  pl.BlockSpec((None, nh, t), lambda b, j: (b, 0, 0)),
                  past, past,
                  pl.BlockSpec((None, nh, tk), lambda b, j: (b, 0, nk - 1 - j)),
                  pl.BlockSpec((tk, tk), lambda b, j: (0, 0))],
        out_specs=tok,
        out_shape=jax.ShapeDtypeStruct((n, d), F32),
        scratch_shapes=[pltpu.VMEM((nh, t, 1), F32), pltpu.VMEM((nh, t, 1), F32),
                        pltpu.VMEM((nh, t, LANES), F32), pltpu.VMEM((nh, 1), F32)],
        compiler_params=_params(("parallel", "arbitrary")),
        name="fox_sample",
    )(q, kb, vb, fcum, fn_t, ck, cv, clf_t, upper)


def _tile(t, cap):
    return min(t, cap)


def _trunk(x, mods, shift_in, wkv_in, cache, w):
    bsz, t, d = x.shape
    depth = mods.shape[0]
    n_a = w["n_a"]
    nh = d // HEAD_DIM
    x2d = x.reshape(bsz * t, d)
    shifts, states = [], []
    kv = None
    for l in range(depth):
        mods_l = mods[l]
        if l < n_a:
            wl = w["rwkv"][l]
            proj, last = _rwkv_proj(x2d, shift_in[l], mods_l, wl, bsz, t, _tile(t, 256))
            y, s_new = _wkv(proj[:6], wkv_in[l], wl, bsz, t)
            shifts.append(last)
            states.append(s_new.astype(wkv_in.dtype))
            mixer_out, gate, w_o = y, proj[6], wl["w_o"]
        else:
            if kv is None:
                kv = _kv_proj(x2d, w["kv_w"], w["f_w"], w["f_b"], bsz, t, _tile(t, 512))
                if cache is None:
                    bias_slabs = _fox_bias_slabs(kv[5], bsz, t, d // LANES)
            k_sh, v_sh, kb, vb, lf, fcum = kv
            j = l - n_a
            q_scale = HEAD_DIM ** -0.5 * (LOG2E if cache is None else 1.0)
            q = _q_proj(x2d, mods_l, w["b_wq"][j], q_scale, bsz, t, _tile(t, 512))
            if cache is None:
                mixer_out = _fox_prompt(q, kb, vb, bias_slabs, bsz, t, _tile(t, 512))
            else:
                mixer_out = _fox_sample(q, kb, vb, fcum, cache[0], cache[1], cache[2], bsz, t,
                                        _tile(cache[0].shape[1], 512))
            gate, w_o = None, w["b_wo"][j]
        x1, comb = _post_mixer(x2d, mixer_out, gate, mods_l, w_o, w["ln"][l][0], w["router_w"][l],
                               w["router_b"][l], w["alpha"], bsz, t, _tile(t, 512))
        x2d = _moe(x1, comb, mods_l, w["ln"][l][1], w["exp_wg"][l], w["exp_wu"][l], w["exp_wd"][l],
                   w["alpha"], bsz, t, _tile(t, 1024))
    k_sh, v_sh, _, _, lf, _ = kv
    return (x2d.reshape(bsz, t, d), jnp.stack(shifts), jnp.stack(states),
            k_sh.reshape(bsz, t, nh, HEAD_DIM), v_sh.reshape(bsz, t, nh, HEAD_DIM),
            lf[:, :nh].reshape(bsz, t, nh).astype(x.dtype))


def _prepare_weights(ln_g, ln_b, a_mu, a_w_rkv, a_w0, a_w1, a_w2, a_a0, a_a1, a_a2, a_g1, a_g2, a_k_k,
                     a_k_a, a_r_k, a_lnx_g, a_lnx_b, a_w_o, kv_w, f_w, f_b, b_wq, b_wo, rg_w, rg_b,
                     re_w, re_b, exp_wg, exp_wu, exp_wd):
    depth, _, d = ln_g.shape
    n_a = a_mu.shape[0]
    nh = d // HEAD_DIM
    alpha = (2.0 * depth) ** 0.25
    zrow = jnp.zeros((d,), F32)
    head_of = jnp.arange(d) // HEAD_DIM
    head_down = (head_of[:, None] == jnp.arange(LANES)[None, :]).astype(BF16)
    rwkv = []
    for l in range(n_a):
        rwkv.append(dict(
            mu=a_mu[l],
            vec=jnp.stack([a_w0[l], a_a0[l], a_k_k[l], a_k_a[l], zrow, zrow, zrow, zrow]),
            scan_vec=jnp.stack([a_r_k[l].reshape(d), a_lnx_g[l], a_lnx_b[l], zrow, zrow, zrow, zrow, zrow]),
            w_rkv=a_w_rkv[l].astype(BF16), w1=a_w1[l].astype(BF16), w2=a_w2[l].astype(BF16),
            a1=a_a1[l].astype(BF16), a2=a_a2[l].astype(BF16), g1=a_g1[l].astype(BF16),
            g2=a_g2[l].astype(BF16), w_o=a_w_o[l].astype(BF16),
            head_down=head_down, head_up=head_down.T))
    ln = [[jnp.stack([ln_g[l, s], ln_b[l, s], zrow, zrow, zrow, zrow, zrow, zrow])
           for s in range(2)] for l in range(depth)]
    pad = LANES - N_GROUPS - N_EXPERTS
    router_w = [jnp.concatenate([rg_w[l], re_w[l], jnp.zeros((d, pad), F32)], 1) for l in range(depth)]
    router_b = [jnp.concatenate([rg_b[l], re_b[l], jnp.zeros((pad,), F32)])[None, :] for l in range(depth)]
    return dict(
        n_a=n_a, alpha=alpha, rwkv=rwkv, ln=ln, router_w=router_w, router_b=router_b,
        kv_w=kv_w.astype(BF16),
        f_w=jnp.concatenate([f_w, jnp.zeros((d, LANES - nh), F32)], 1),
        f_b=jnp.concatenate([f_b, jnp.zeros((LANES - nh,), F32)])[None, :],
        b_wq=b_wq.astype(BF16), b_wo=b_wo.astype(BF16),
        exp_wg=exp_wg, exp_wu=exp_wu, exp_wd=exp_wd)


def kernel(x_prompt, x_sample, state_shift, state_wkv, cache_k, cache_v, cache_logf, c_prompt, c_sample,
           ada_w, ada_b, ln_g, ln_b, a_mu, a_w_rkv, a_w0, a_w1, a_w2, a_a0, a_a1, a_a2, a_g1, a_g2, a_k_k,
           a_k_a, a_r_k, a_lnx_g, a_lnx_b, a_w_o, kv_w, f_w, f_b, b_wq, b_wo, rg_w, rg_b, re_w, re_b,
           exp_wg, exp_wu, exp_wd):
    w = _prepare_weights(ln_g, ln_b, a_mu, a_w_rkv, a_w0, a_w1, a_w2, a_a0, a_a1, a_a2, a_g1, a_g2,
                         a_k_k, a_k_a, a_r_k, a_lnx_g, a_lnx_b, a_w_o, kv_w, f_w, f_b, b_wq, b_wo,
                         rg_w, rg_b, re_w, re_b, exp_wg, exp_wu, exp_wd)
    bp, _, d = x_prompt.shape
    n_a = a_mu.shape[0]
    nh = d // HEAD_DIM
    mods = _ada_mods(jnp.concatenate([c_prompt, c_sample], 0), ada_w, ada_b)
    zero_shift = jnp.zeros((n_a, bp, d), x_prompt.dtype)
    zero_wkv = jnp.zeros((n_a, bp, nh, HEAD_DIM, HEAD_DIM), x_prompt.dtype)
    outs_p = _trunk(x_prompt, mods[:, :bp], zero_shift, zero_wkv, None, w)
    outs_s = _trunk(x_sample, mods[:, bp:], state_shift, state_wkv, (cache_k, cache_v, cache_logf), w)
    y_p, p_shift, p_wkv, p_k, p_v, p_logf = outs_p
    y_s, s_shift, s_wkv, s_k, s_v, s_logf = outs_s
    return (y_p, y_s, p_shift, p_wkv, p_k, p_v, p_logf, s_shift, s_wkv, s_k, s_v, s_logf)
```

```python
import functools

import jax
import jax.numpy as jnp
import numpy as np
from jax import lax
from jax.experimental import pallas as pl
from jax.experimental.pallas import tpu as pltpu

F32 = jnp.float32
BF16 = jnp.bfloat16

HEAD_DIM = 64
LANES = 128
N_GROUPS = 4
EXP_PER_GROUP = 4
N_EXPERTS = N_GROUPS * EXP_PER_GROUP
LN_EPS = 1e-5
GN_EPS = 64e-5
WKV_CHUNK = 64
MOE_EXPERTS_PER_STEP = 2
LOG2E = 1.4426950408889634
VMEM_LIMIT = 56 * 1024 * 1024
NEG_BIG = -1e30


def _params(sem):
    return pltpu.CompilerParams(dimension_semantics=sem, vmem_limit_bytes=VMEM_LIMIT)


def _dg(a, b, ca, cb):
    return lax.dot_general(a, b, (((ca,), (cb,)), ((), ())), preferred_element_type=F32)


def _bdot(a, b, ca=1, cb=0):
    return _dg(a.astype(BF16), b.astype(BF16), ca, cb)


def _split3(x):
    hi = x.astype(BF16)
    r1 = x - hi.astype(F32)
    mid = r1.astype(BF16)
    lo = (r1 - mid.astype(F32)).astype(BF16)
    return hi, mid, lo


def _dot_exact_rhs(a, b_exact, ca=1, cb=0):
    hi, mid, lo = _split3(a)
    bb = b_exact.astype(BF16)
    return _dg(hi, bb, ca, cb) + _dg(mid, bb, ca, cb) + _dg(lo, bb, ca, cb)


def _dot_exact_lhs(a_exact, b, ca=1, cb=0):
    hi, mid, lo = _split3(b)
    aa = a_exact.astype(BF16)
    return _dg(aa, hi, ca, cb) + _dg(aa, mid, ca, cb) + _dg(aa, lo, ca, cb)


def _dot3(a, b, ca=1, cb=0):
    ah = a.astype(BF16)
    al = (a - ah.astype(F32)).astype(BF16)
    bh = b.astype(BF16)
    bl = (b - bh.astype(F32)).astype(BF16)
    return _dg(ah, bh, ca, cb) + _dg(ah, bl, ca, cb) + _dg(al, bh, ca, cb)


_inv_dot = _bdot


def _layer_norm(z, g, b):
    mu = jnp.mean(z, -1, keepdims=True)
    d = z - mu
    var = jnp.mean(d * d, -1, keepdims=True)
    return d * lax.rsqrt(var + LN_EPS) * g + b


def _softplus(z):
    return jnp.maximum(z, 0.0) + jnp.log(1.0 + jnp.exp(-jnp.abs(z)))


def _sigmoid(z):
    return 1.0 / (1.0 + jnp.exp(-z))


def _silu(z):
    return z * _sigmoid(z)


def _ada_kernel(c_ref, w_ref, b_ref, o_ref):
    o_ref[...] = _dot3(_silu(c_ref[...]), w_ref[...]) + b_ref[...]


def _ada_mods(c_all, ada_w, ada_b):
    depth, d, d6 = ada_w.shape
    bsz = c_all.shape[0]
    tn = d
    out = pl.pallas_call(
        _ada_kernel,
        grid=(depth, d6 // tn),
        in_specs=[
            pl.BlockSpec((bsz, d), lambda l, j: (0, 0)),
            pl.BlockSpec((None, d, tn), lambda l, j: (l, 0, j)),
            pl.BlockSpec((None, 1, tn), lambda l, j: (l, 0, j)),
        ],
        out_specs=pl.BlockSpec((None, bsz, tn), lambda l, j: (l, 0, j)),
        out_shape=jax.ShapeDtypeStruct((depth, bsz, d6), F32),
        compiler_params=_params(("parallel", "parallel")),
        name="ada_mods",
    )(c_all, ada_w, ada_b.reshape(depth, 1, d6))
    return out.reshape(depth, bsz, 6, d)


def _rwkv_proj_kernel(x_ref, xp_ref, shift_ref, mods_ref, mu_ref, vec_ref, wrkv_ref, w1_ref, w2_ref,
                      a1_ref, a2_ref, g1_ref, g2_ref, hd_ref, hu_ref,
                      r_ref, lw_ref, k_ref, v_ref, kk_ref, b_ref, g_ref, last_ref):
    tm = x_ref.shape[0]
    sh1 = mods_ref[0:1, :]
    sc1 = mods_ref[1:2, :]
    h = x_ref[...] * (1.0 + sc1) + sh1
    h_prev_tile = xp_ref[7:8, :] * (1.0 + sc1) + sh1
    prev_row = jnp.where(pl.program_id(1) == 0, shift_ref[...], h_prev_tile)
    row = lax.broadcasted_iota(jnp.int32, (tm, 1), 0)
    xx = jnp.where(row == 0, prev_row, pltpu.roll(h, 1, axis=0)) - h

    def mix(i):
        return h + xx * mu_ref[i:i + 1, :]

    w0, a0 = vec_ref[0:1, :], vec_ref[1:2, :]
    k_k, k_a = vec_ref[2:3, :], vec_ref[3:4, :]
    r = _bdot(mix(0), wrkv_ref[0])
    k = _bdot(mix(1), wrkv_ref[1])
    v = _bdot(mix(2), wrkv_ref[2])
    ww = w0 + _bdot(jnp.tanh(_bdot(mix(3), w1_ref[...])), w2_ref[...])
    w_log = -_softplus(-ww) - 0.5
    a = _sigmoid(a0 + _bdot(_bdot(mix(4), a1_ref[...]), a2_ref[...]))
    g = _bdot(_sigmoid(_bdot(mix(5), g1_ref[...])), g2_ref[...])
    kk = k * k_k
    ss = _dot_exact_rhs(_dot_exact_rhs(kk * kk, hd_ref[...]), hu_ref[...])
    kk = kk * lax.rsqrt(jnp.maximum(ss, 1e-24))
    r_ref[...] = r.astype(BF16)
    lw_ref[...] = -jnp.exp(w_log)
    k_ref[...] = (k * (1.0 + (a - 1.0) * k_a)).astype(BF16)
    v_ref[...] = v.astype(BF16)
    kk_ref[...] = kk.astype(BF16)
    b_ref[...] = (kk * a).astype(BF16)
    g_ref[...] = g.astype(BF16)
    last_ref[...] = h[tm - 1:tm, :]


def _rwkv_proj(x2d, shift_prev, mods_l, wl, bsz, t, tm):
    n, d = x2d.shape
    nt = t // tm
    tok = pl.BlockSpec((tm, d), lambda b, i: (b * nt + i, 0))
    full = lambda a: pl.BlockSpec(a.shape, lambda b, i: (0,) * a.ndim)
    weights = [wl["mu"], wl["vec"], wl["w_rkv"], wl["w1"], wl["w2"], wl["a1"], wl["a2"], wl["g1"],
               wl["g2"], wl["head_down"], wl["head_up"]]
    outs = pl.pallas_call(
        _rwkv_proj_kernel,
        grid=(bsz, nt),
        in_specs=[
            tok,
            pl.BlockSpec((8, d), lambda b, i: (jnp.maximum((b * nt + i) * (tm // 8) - 1, 0), 0)),
            pl.BlockSpec((None, 1, d), lambda b, i: (b, 0, 0)),
            pl.BlockSpec((None, 6, d), lambda b, i: (b, 0, 0)),
        ] + [full(a) for a in weights],
        out_specs=[tok] * 7 + [pl.BlockSpec((None, 1, d), lambda b, i: (b, 0, 0))],
        out_shape=[jax.ShapeDtypeStruct((n, d), F32 if i == 1 else BF16) for i in range(7)]
        + [jax.ShapeDtypeStruct((bsz, 1, d), F32)],
        compiler_params=_params(("parallel", "arbitrary")),
        name="rwkv_proj",
    )(x2d, x2d, shift_prev.reshape(bsz, 1, d), mods_l, *weights)
    return outs[:7], outs[7].reshape(bsz, d)


def _wkv_kernel(r_ref, lw_ref, k_ref, v_ref, kk_ref, b_ref, s0_ref, vec_ref, tri_ref, ones_ref,
                y_ref, s_out_ref, state_ref):
    c = r_ref.shape[0]
    n_pairs = r_ref.shape[1] // LANES
    first = pl.program_id(1) == 0

    @pl.when(first)
    def _():
        state_ref[...] = s0_ref[...]

    lane = lax.broadcasted_iota(jnp.int32, (c, LANES), 1)
    low = lane < HEAD_DIM
    ri = lax.broadcasted_iota(jnp.int32, (2 * c, 2 * c), 0)
    ci = lax.broadcasted_iota(jnp.int32, (2 * c, 2 * c), 1)
    same = (ri >= c) == (ci >= c)
    strict = same & (ci < ri)
    incl = same & (ci <= ri)
    eye = (ri == ci).astype(F32)
    tri = tri_ref[...]
    ones_bd = ones_ref[...]

    def stack(z):
        return jnp.concatenate([jnp.where(low, z, 0.0), jnp.where(low, 0.0, z)], axis=0)

    pairs = range(n_pairs)
    slabs = [slice(p * LANES, (p + 1) * LANES) for p in pairs]
    lw_all = lw_ref[...]
    cum_all = _dot_exact_lhs(tri, lw_all)
    prev_all = cum_all - lw_all
    mid_all = cum_all[c // 2 - 1:c // 2, :]
    end_all = cum_all[c - 1:c, :]
    e_in = jnp.exp(mid_all - cum_all)
    e_out = jnp.exp(end_all - cum_all)
    e_end = jnp.exp(end_all)
    r_all, k_all, v_all = r_ref[...].astype(F32), k_ref[...].astype(F32), v_ref[...].astype(F32)
    a_all, b_all = -kk_ref[...].astype(F32), b_ref[...].astype(F32)
    aq, rq = a_all * jnp.exp(prev_all - mid_all), r_all * jnp.exp(cum_all - mid_all)
    bi, ki = b_all * e_in, k_all * e_in
    a0, r0 = a_all * jnp.exp(prev_all), r_all * jnp.exp(cum_all)
    bo, ko = b_all * e_out, k_all * e_out

    gram = [_bdot(jnp.concatenate([stack(aq[:, s]), stack(rq[:, s])], 0),
                  jnp.concatenate([stack(bi[:, s]), stack(ki[:, s])], 0), 1, 1) for s in slabs]
    a_ab = [jnp.where(strict, g[:2 * c, :2 * c], 0.0) for g in gram]
    a_ak = [jnp.where(strict, g[:2 * c, 2 * c:], 0.0) for g in gram]
    a_r = [jnp.concatenate([jnp.where(incl, g[2 * c:, :2 * c], 0.0),
                            jnp.where(incl, g[2 * c:, 2 * c:], 0.0)], 1) for g in gram]
    tinv = [eye + m for m in a_ab]
    pw = a_ab
    for _ in range(max(c.bit_length() - 2, 0)):
        pw = [_inv_dot(m, m) for m in pw]
        tinv = [t_ + _inv_dot(t_, m) for t_, m in zip(tinv, pw)]
    s_prev = [state_ref[p] for p in pairs]
    from_state = [_bdot(jnp.concatenate([stack(a0[:, s]), stack(r0[:, s])], 0), sp, 1, 1)
                  for s, sp in zip(slabs, s_prev)]
    v_st = [stack(v_all[:, s]) for s in slabs]
    rhs = [fs[:2 * c] + _bdot(m, vs) for fs, m, vs in zip(from_state, a_ak, v_st)]
    u = [_inv_dot(t_, x) for t_, x in zip(tinv, rhs)]
    uv = [jnp.concatenate([u_, vs], 0) for u_, vs in zip(u, v_st)]
    o_st = [fs[2 * c:] + _bdot(m, x) for fs, m, x in zip(from_state, a_r, uv)]
    o = [x[:c] + x[c:] for x in o_st]
    new_state = [sp * e_end[:, s] + _bdot(x, jnp.concatenate([stack(bo[:, s]), stack(ko[:, s])], 0), 0, 0)
                 for s, sp, x in zip(slabs, s_prev, uv)]
    state_ref[...] = jnp.stack(new_state)
    def rows(z):
        return jnp.concatenate([z[:, s] for s in slabs], 0)

    def lanes(z):
        return jnp.concatenate([z[p * c:(p + 1) * c] for p in pairs], 1)

    o_rows = jnp.concatenate(o, 0)
    dev = o_rows - _dot_exact_rhs(o_rows, ones_bd) * (1.0 / HEAD_DIM)
    var = _dot_exact_rhs(dev * dev, ones_bd) * (1.0 / HEAD_DIM)
    bonus = _dot_exact_rhs(rows(r_all * k_all * vec_ref[0:1, :]), ones_bd)
    y_ref[...] = (lanes(dev * lax.rsqrt(var + GN_EPS)) * vec_ref[1:2, :] + vec_ref[2:3, :]
                  + lanes(bonus) * v_all)

    @pl.when(pl.program_id(1) == pl.num_programs(1) - 1)
    def _():
        s_out_ref[...] = state_ref[...]


def _pair_states(s):
    bsz, nh, n, _ = s.shape
    s = s.reshape(bsz, nh // 2, 2, n, n)
    z = jnp.zeros_like(s[:, :, 0])
    top = jnp.concatenate([s[:, :, 0], z], -1)
    bot = jnp.concatenate([z, s[:, :, 1]], -1)
    return jnp.concatenate([top, bot], -2)


def _unpair_states(sp):
    bsz, npair, _, _ = sp.shape
    n = HEAD_DIM
    return jnp.stack([sp[:, :, :n, :n], sp[:, :, n:, n:]], 2).reshape(bsz, 2 * npair, n, n)


def _wkv(proj, s0, wl, bsz, t):
    r, lw, k, v, kk, b = proj
    n, d = r.shape
    c = min(WKV_CHUNK, t)
    nc = t // c
    npair = d // LANES
    tok = pl.BlockSpec((c, d), lambda bb, i: (bb * nc + i, 0))
    st = pl.BlockSpec((None, npair, LANES, LANES), lambda bb, i: (bb, 0, 0, 0))
    tri = jnp.tril(jnp.ones((c, c), BF16))
    hid = jnp.arange(LANES) // HEAD_DIM
    ones_bd = (hid[:, None] == hid[None, :]).astype(BF16)
    y, s_out = pl.pallas_call(
        _wkv_kernel,
        grid=(bsz, nc),
        in_specs=[tok] * 6 + [st,
                              pl.BlockSpec((8, d), lambda bb, i: (0, 0)),
                              pl.BlockSpec((c, c), lambda bb, i: (0, 0)),
                              pl.BlockSpec((LANES, LANES), lambda bb, i: (0, 0))],
        out_specs=[tok, st],
        out_shape=[jax.ShapeDtypeStruct((n, d), F32),
                   jax.ShapeDtypeStruct((bsz, npair, LANES, LANES), F32)],
        scratch_shapes=[pltpu.VMEM((npair, LANES, LANES), F32)],
        compiler_params=_params(("parallel", "arbitrary")),
        name="wkv_scan",
    )(r, lw, k, v, kk, b, _pair_states(s0.astype(F32)), wl["scan_vec"], tri, ones_bd)
    return y, _unpair_states(s_out)


def _router(logits):
    lane_i = lax.broadcasted_iota(jnp.int32, logits.shape, 1)
    lane = lane_i.astype(F32)
    far = 1e9
    is_g = lane_i < N_GROUPS
    gl = jnp.where(is_g, logits, NEG_BIG)
    gmax = jnp.max(gl, -1, keepdims=True)
    gsel = jnp.min(jnp.where(gl == gmax, lane, far), -1, keepdims=True)
    gprob = 1.0 / jnp.sum(jnp.where(is_g, jnp.exp(gl - gmax), 0.0), -1, keepdims=True)
    group_of = lax.shift_right_arithmetic(lane_i - N_GROUPS, 2).astype(F32)
    in_group = (lane_i >= N_GROUPS) & (lane_i < N_GROUPS + N_EXPERTS) & (group_of == gsel)
    el = jnp.where(in_group, logits, NEG_BIG)
    v1 = jnp.max(el, -1, keepdims=True)
    i1 = jnp.min(jnp.where(el == v1, lane, far), -1, keepdims=True)
    el2 = jnp.where(lane == i1, NEG_BIG, el)
    v2 = jnp.max(el2, -1, keepdims=True)
    i2 = jnp.min(jnp.where(el2 == v2, lane, far), -1, keepdims=True)
    e2 = jnp.exp(v2 - v1)
    w1 = gprob / (1.0 + e2)
    w2 = gprob * e2 / (1.0 + e2)
    return jnp.where(lane == i1, w1, 0.0) + jnp.where(lane == i2, w2, 0.0)


def _post_kernel(gated, alpha, *refs):
    if gated:
        x_ref, y_ref, g_ref, mods_ref, wo_ref, ln_ref, rw_ref, rb_ref, x1_ref, comb_ref = refs
        y = y_ref[...] * g_ref[...].astype(F32)
    else:
        x_ref, y_ref, mods_ref, wo_ref, ln_ref, rw_ref, rb_ref, x1_ref, comb_ref = refs
        y = y_ref[...]
    gt1, sh2, sc2 = mods_ref[2:3, :], mods_ref[3:4, :], mods_ref[4:5, :]
    x1 = _layer_norm(alpha * x_ref[...] + gt1 * _bdot(y, wo_ref[...]), ln_ref[0:1, :], ln_ref[1:2, :])
    x1_ref[...] = x1
    h2 = x1 * (1.0 + sc2) + sh2
    comb_ref[...] = _router(_dot3(h2, rw_ref[...]) + rb_ref[...])


def _post_mixer(x2d, y2d, g2d, mods_l, w_o, ln_pack, router_w, router_b, alpha, bsz, t, tm):
    n, d = x2d.shape
    nt = t // tm
    tok = pl.BlockSpec((tm, d), lambda b, i: (b * nt + i, 0))
    full = lambda a: pl.BlockSpec(a.shape, lambda b, i: (0,) * a.ndim)
    gated = g2d is not None
    acts = [x2d, y2d] + ([g2d] if gated else [])
    consts = [w_o, ln_pack, router_w, router_b]
    return pl.pallas_call(
        functools.partial(_post_kernel, gated, alpha),
        grid=(bsz, nt),
        in_specs=[tok] * len(acts) + [pl.BlockSpec((None, 6, d), lambda b, i: (b, 0, 0))]
        + [full(a) for a in consts],
        out_specs=[tok, pl.BlockSpec((tm, LANES), lambda b, i: (b * nt + i, 0))],
        out_shape=[jax.ShapeDtypeStruct((n, d), F32), jax.ShapeDtypeStruct((n, LANES), F32)],
        compiler_params=_params(("parallel", "parallel")),
        name="post_mixer",
    )(*acts, mods_l, *consts)


def _moe_kernel(alpha, x1_ref, comb_ref, mods_ref, ln_ref, wg_ref, wu_ref, wd_ref, x2_ref, h2_ref, acc_ref):
    e = pl.program_id(1)

    @pl.when(e == 0)
    def _():
        sh2, sc2 = mods_ref[3:4, :], mods_ref[4:5, :]
        h2_ref[...] = (x1_ref[...] * (1.0 + sc2) + sh2).astype(BF16)
        acc_ref[...] = jnp.zeros_like(acc_ref)

    h2 = h2_ref[...]
    per_step, f, d = wd_ref.shape
    lane = lax.broadcasted_iota(jnp.int32, comb_ref.shape, 1)
    comb = comb_ref[...]
    hidden = []
    for i in range(per_step):
        hg = _dg(h2, wg_ref[i].astype(BF16), 1, 0)
        hu = _dg(h2, wu_ref[i].astype(BF16), 1, 0)
        ce = jnp.sum(jnp.where(lane == e * per_step + (i + N_GROUPS), comb, 0.0), -1, keepdims=True)
        hidden.append((_silu(hg) * hu * ce).astype(BF16))
    acc_ref[...] += _dg(jnp.concatenate(hidden, 1), wd_ref[...].reshape(per_step * f, d).astype(BF16), 1, 0)

    @pl.when(e == pl.num_programs(1) - 1)
    def _():
        gt2 = mods_ref[5:6, :]
        x2_ref[...] = _layer_norm(alpha * x1_ref[...] + gt2 * acc_ref[...], ln_ref[0:1, :], ln_ref[1:2, :])


def _moe(x1, comb, mods_l, ln_pack, wg, wu, wd, alpha, bsz, t, tm):
    n, d = x1.shape
    nt = t // tm
    ne, _, f = wg.shape
    per_step = MOE_EXPERTS_PER_STEP
    return pl.pallas_call(
        functools.partial(_moe_kernel, alpha),
        grid=(n // tm, ne // per_step),
        in_specs=[
            pl.BlockSpec((tm, d), lambda i, e: (i, 0)),
            pl.BlockSpec((tm, LANES), lambda i, e: (i, 0)),
            pl.BlockSpec((None, 6, d), lambda i, e: (i // nt, 0, 0)),
            pl.BlockSpec(ln_pack.shape, lambda i, e: (0, 0)),
            pl.BlockSpec((per_step, d, f), lambda i, e: (e, 0, 0)),
            pl.BlockSpec((per_step, d, f), lambda i, e: (e, 0, 0)),
            pl.BlockSpec((per_step, f, d), lambda i, e: (e, 0, 0)),
        ],
        out_specs=pl.BlockSpec((tm, d), lambda i, e: (i, 0)),
        out_shape=jax.ShapeDtypeStruct((n, d), F32),
        scratch_shapes=[pltpu.VMEM((tm, d), BF16), pltpu.VMEM((tm, d), F32)],
        compiler_params=_params(("parallel", "arbitrary")),
        name="hmoe",
    )(x1, comb, mods_l, ln_pack, wg, wu, wd)


def _fox_bias_placement(d):
    place_q = np.zeros((3 * LANES, d), np.float32)
    place_k = np.zeros((3 * LANES, d), np.float32)
    const_q = np.zeros((1, d), np.float32)
    const_k = np.zeros((1, d), np.float32)
    for h in range(d // HEAD_DIM):
        base = (h // 2) * LANES + (HEAD_DIM if h % 2 == 0 else 0)
        for term in range(3):
            place_q[term * LANES + h, base + term] = 1.0
            place_k[term * LANES + h, base + 3 + term] = -1.0
            const_q[0, base + 3 + term] = 1.0
            const_k[0, base + term] = 1.0
    return (jnp.asarray(place_q, BF16), jnp.asarray(place_k, BF16), jnp.asarray(const_q),
            jnp.asarray(const_k))


def _kv_kernel(x_ref, kvw_ref, fw_ref, fb_ref, tri_ref, pq_ref, pk_ref, cq_ref, ck_ref,
               k_ref, v_ref, kb_ref, vb_ref, lf_ref, fc_ref, qbias_ref, kbias_ref, carry_ref):
    d = x_ref.shape[1]

    @pl.when(pl.program_id(1) == 0)
    def _():
        carry_ref[...] = jnp.zeros_like(carry_ref)

    x = x_ref[...]
    kv = _bdot(x, kvw_ref[...])
    k, v = kv[:, :d], kv[:, d:]
    k_ref[...] = k
    v_ref[...] = v
    kb_ref[...] = k.astype(BF16)
    vb_ref[...] = v.astype(BF16)
    z = _dot3(x, fw_ref[...]) + fb_ref[...]
    lf = -_softplus(-z)
    lf_ref[...] = lf
    fc = _dot_exact_lhs(tri_ref[...], lf) + carry_ref[...]
    fc_ref[...] = fc
    carry_ref[...] = fc[fc.shape[0] - 1:, :]
    terms = jnp.concatenate(_split3(fc * LOG2E), 1)
    qbias_ref[...] = (_dg(terms, pq_ref[...], 1, 0) + cq_ref[...]).astype(BF16)
    kbias_ref[...] = (_dg(terms, pk_ref[...], 1, 0) + ck_ref[...]).astype(BF16)


def _kv_proj(x2d, kv_w, f_w, f_b, bsz, t, tm):
    n, d = x2d.shape
    nt = t // tm
    tok = pl.BlockSpec((tm, d), lambda b, i: (b * nt + i, 0))
    nar = pl.BlockSpec((tm, LANES), lambda b, i: (b * nt + i, 0))
    tri = jnp.tril(jnp.ones((tm, tm), BF16))
    full = lambda a: pl.BlockSpec(a.shape, lambda b, i: (0,) * a.ndim)
    consts = [kv_w, f_w, f_b, tri, *_fox_bias_placement(d)]
    return pl.pallas_call(
        _kv_kernel,
        grid=(bsz, nt),
        in_specs=[tok] + [full(a) for a in consts],
        out_specs=[tok, tok, tok, tok, nar, nar, tok, tok],
        out_shape=[jax.ShapeDtypeStruct((n, d), F32), jax.ShapeDtypeStruct((n, d), F32),
                   jax.ShapeDtypeStruct((n, d), BF16), jax.ShapeDtypeStruct((n, d), BF16),
                   jax.ShapeDtypeStruct((n, LANES), F32), jax.ShapeDtypeStruct((n, LANES), F32),
                   jax.ShapeDtypeStruct((n, d), BF16), jax.ShapeDtypeStruct((n, d), BF16)],
        scratch_shapes=[pltpu.VMEM((1, LANES), F32)],
        compiler_params=_params(("parallel", "arbitrary")),
        name="kv_proj",
    )(x2d, *consts)


def _q_kernel(scale, x_ref, mods_ref, wq_ref, q_ref):
    h = x_ref[...] * (1.0 + mods_ref[1:2, :]) + mods_ref[0:1, :]
    q_ref[...] = (_bdot(h, wq_ref[...]) * scale).astype(BF16)


def _q_proj(x2d, mods_l, wq, scale, bsz, t, tm):
    n, d = x2d.shape
    nt = t // tm
    tok = pl.BlockSpec((tm, d), lambda b, i: (b * nt + i, 0))
    return pl.pallas_call(
        functools.partial(_q_kernel, scale),
        grid=(bsz, nt),
        in_specs=[tok, pl.BlockSpec((None, 6, d), lambda b, i: (b, 0, 0)),
                  pl.BlockSpec(wq.shape, lambda b, i: (0, 0))],
        out_specs=tok,
        out_shape=jax.ShapeDtypeStruct((n, d), BF16),
        compiler_params=_params(("parallel", "parallel")),
        name="q_proj",
    )(x2d, mods_l, wq)


def _fox_prompt_kernel(tq, q_ref, k_ref, v_ref, qb_ref, kb_ref, o_ref):
    t = q_ref.shape[0]
    lane = lax.broadcasted_iota(jnp.int32, (1, LANES), 1)
    rr = lax.broadcasted_iota(jnp.int32, (tq, tq), 0)
    cc = lax.broadcasted_iota(jnp.int32, (tq, tq), 1)
    own = [lane < HEAD_DIM, lane >= HEAD_DIM]
    k_aug = [jnp.where(own[hh], k_ref[...], kb_ref[...]) for hh in range(2)]
    v = v_ref[...]
    for qi in range(t // tq):
        lo, hi = qi * tq, (qi + 1) * tq
        heads = []
        for hh in range(2):
            q = jnp.where(own[hh], q_ref[lo:hi, :], qb_ref[lo:hi, :])
            s_diag = jnp.where(cc <= rr, _dg(q, k_aug[hh][lo:hi], 1, 1), NEG_BIG)
            m = jnp.max(s_diag, -1, keepdims=True)
            if qi > 0:
                s_past = _dg(q, k_aug[hh][:lo], 1, 1)
                m = jnp.maximum(m, jnp.max(s_past, -1, keepdims=True))
            p = jnp.exp2(s_diag - m)
            num = _dg(p.astype(BF16), v[lo:hi], 1, 0)
            den = jnp.sum(p, -1, keepdims=True)
            if qi > 0:
                p = jnp.exp2(s_past - m)
                num = num + _dg(p.astype(BF16), v[:lo], 1, 0)
                den = den + jnp.sum(p, -1, keepdims=True)
            heads.append(num / den)
        o_ref[lo:hi, :] = jnp.where(own[0], heads[0], heads[1])


def _fox_prompt(q, kb, vb, q_bias, k_bias, bsz, t, tq):
    n, d = q.shape
    seq = pl.BlockSpec((t, LANES), lambda b, p: (b, p))
    return pl.pallas_call(
        functools.partial(_fox_prompt_kernel, tq),
        grid=(bsz, d // LANES),
        in_specs=[seq] * 5,
        out_specs=seq,
        out_shape=jax.ShapeDtypeStruct((n, d), F32),
        compiler_params=_params(("parallel", "parallel")),
        name="fox_prompt",
    )(q, kb, vb, q_bias, k_bias)


def _fox_sample_kernel(q_ref, kn_ref, vn_ref, fn_ref, fnt_ref, ck_ref, cv_ref, clf_ref, upper_ref,
                       o_ref, m_ref, l_ref, acc_ref, carry_ref):
    step = pl.program_id(1)
    t, d = q_ref.shape
    nh = d // HEAD_DIM

    @pl.when(step == 0)
    def _():
        m_ref[...] = jnp.full_like(m_ref, NEG_BIG)
        l_ref[...] = jnp.zeros_like(l_ref)
        acc_ref[...] = jnp.zeros_like(acc_ref)
        carry_ref[...] = jnp.zeros_like(carry_ref)

    heads = range(nh)
    slabs = [slice(pr * LANES, (pr + 1) * LANES) for pr in range(nh // 2)]
    lane = lax.broadcasted_iota(jnp.int32, (1, LANES), 1)
    q_own = [jnp.where((lane < HEAD_DIM) == (h % 2 == 0), q_ref[:, slabs[h // 2]], jnp.zeros((), BF16))
             for h in heads]

    def attend(keys, values, bias, mask):
        s = [_dg(q_own[h], keys[h // 2], 1, 1) + bias[h] for h in heads]
        if mask is not None:
            s = [jnp.where(mask, x, NEG_BIG) for x in s]
        m_old = m_ref[...]
        m_new = jnp.maximum(m_old, jnp.stack([jnp.max(x, -1, keepdims=True) for x in s]))
        alpha = jnp.exp(m_old - m_new)
        p = [jnp.exp(x - m_new[h]) for h, x in enumerate(s)]
        l_ref[...] = alpha * l_ref[...] + jnp.stack([jnp.sum(x, -1, keepdims=True) for x in p])
        acc_ref[...] = alpha * acc_ref[...] + jnp.stack(
            [_dg(p[h].astype(BF16), values[h // 2], 1, 0) for h in heads])
        m_ref[...] = m_new

    clf = clf_ref[...]
    suf = _dot_exact_rhs(clf, upper_ref[...]) + carry_ref[...]
    carry_ref[...] = carry_ref[...] + jnp.sum(clf, -1, keepdims=True)
    fn = fn_ref[...]
    attend([ck_ref[:, s].astype(BF16) for s in slabs], [cv_ref[:, s].astype(BF16) for s in slabs],
           [fn[:, h:h + 1] + suf[h:h + 1, :] for h in heads], None)

    @pl.when(step == pl.num_programs(1) - 1)
    def _():
        rr = lax.broadcasted_iota(jnp.int32, (t, t), 0)
        cc = lax.broadcasted_iota(jnp.int32, (t, t), 1)
        fnt = fnt_ref[...]
        attend([kn_ref[:, s] for s in slabs], [vn_ref[:, s] for s in slabs],
               [fn[:, h:h + 1] - fnt[h:h + 1, :] for h in heads], cc <= rr)
        out = acc_ref[...] / l_ref[...]
        o_ref[...] = jnp.concatenate(
            [jnp.where(lane < HEAD_DIM, out[2 * pr], out[2 * pr + 1]) for pr in range(nh // 2)], 1)


def _fox_sample(q, kb, vb, fcum, cache_k, cache_v, cache_logf, bsz, t, tk):
    n, d = q.shape
    nh = d // HEAD_DIM
    plen = cache_k.shape[1]
    nk = plen // tk
    ck = cache_k.reshape(bsz * plen, d)
    cv = cache_v.reshape(bsz * plen, d)
    clf_t = cache_logf.astype(F32).transpose(0, 2, 1)
    fn_t = fcum[:, :nh].reshape(bsz, t, nh).transpose(0, 2, 1)
    upper = (jnp.arange(tk)[:, None] > jnp.arange(tk)[None, :]).astype(BF16)
    tok = pl.BlockSpec((t, d), lambda b, j: (b, 0))
    past = pl.BlockSpec((tk, d), lambda b, j: (b * nk + nk - 1 - j, 0))
    return pl.pallas_call(
        _fox_sample_kernel,
        grid=(bsz, nk),
        in_specs=[tok, tok, tok,
                  pl.BlockSpec((t, LANES), lambda b, j: (b, 0)),
                  pl.BlockSpec((None, nh, t), lambda b, j: (b, 0, 0)),
                  past, past,
                  pl.BlockSpec((None, nh, tk), lambda b, j: (b, 0, nk - 1 - j)),
                  pl.BlockSpec((tk, tk), lambda b, j: (0, 0))],
        out_specs=tok,
        out_shape=jax.ShapeDtypeStruct((n, d), F32),
        scratch_shapes=[pltpu.VMEM((nh, t, 1), F32), pltpu.VMEM((nh, t, 1), F32),
                        pltpu.VMEM((nh, t, LANES), F32), pltpu.VMEM((nh, 1), F32)],
        compiler_params=_params(("parallel", "arbitrary")),
        name="fox_sample",
    )(q, kb, vb, fcum, fn_t, ck, cv, clf_t, upper)


def _tile(t, cap):
    return min(t, cap)


def _trunk(x, mods, shift_in, wkv_in, cache, w):
    bsz, t, d = x.shape
    depth = mods.shape[0]
    n_a = w["n_a"]
    nh = d // HEAD_DIM
    x2d = x.reshape(bsz * t, d)
    shifts, states = [], []
    kv = None
    for l in range(depth):
        mods_l = mods[l]
        if l < n_a:
            wl = w["rwkv"][l]
            proj, last = _rwkv_proj(x2d, shift_in[l], mods_l, wl, bsz, t, _tile(t, 256))
            y, s_new = _wkv(proj[:6], wkv_in[l], wl, bsz, t)
            shifts.append(last)
            states.append(s_new.astype(wkv_in.dtype))
            mixer_out, gate, w_o = y, proj[6], wl["w_o"]
        else:
            if kv is None:
                kv = _kv_proj(x2d, w["kv_w"], w["f_w"], w["f_b"], bsz, t, _tile(t, 512))
            k_sh, v_sh, kb, vb, lf, fcum, q_bias, k_bias = kv
            j = l - n_a
            q_scale = HEAD_DIM ** -0.5 * (LOG2E if cache is None else 1.0)
            q = _q_proj(x2d, mods_l, w["b_wq"][j], q_scale, bsz, t, _tile(t, 512))
            if cache is None:
                mixer_out = _fox_prompt(q, kb, vb, q_bias, k_bias, bsz, t, _tile(t, 512))
            else:
                mixer_out = _fox_sample(q, kb, vb, fcum, cache[0], cache[1], cache[2], bsz, t,
                                        _tile(cache[0].shape[1], 512))
            gate, w_o = None, w["b_wo"][j]
        x1, comb = _post_mixer(x2d, mixer_out, gate, mods_l, w_o, w["ln"][l][0], w["router_w"][l],
                               w["router_b"][l], w["alpha"], bsz, t, _tile(t, 512))
        x2d = _moe(x1, comb, mods_l, w["ln"][l][1], w["exp_wg"][l], w["exp_wu"][l], w["exp_wd"][l],
                   w["alpha"], bsz, t, _tile(t, 1024))
    k_sh, v_sh, lf = kv[0], kv[1], kv[4]
    return (x2d.reshape(bsz, t, d), jnp.stack(shifts), jnp.stack(states),
            k_sh.reshape(bsz, t, nh, HEAD_DIM), v_sh.reshape(bsz, t, nh, HEAD_DIM),
            lf[:, :nh].reshape(bsz, t, nh).astype(x.dtype))


def _prepare_weights(ln_g, ln_b, a_mu, a_w_rkv, a_w0, a_w1, a_w2, a_a0, a_a1, a_a2, a_g1, a_g2, a_k_k,
                     a_k_a, a_r_k, a_lnx_g, a_lnx_b, a_w_o, kv_w, f_w, f_b, b_wq, b_wo, rg_w, rg_b,
                     re_w, re_b, exp_wg, exp_wu, exp_wd):
    depth, _, d = ln_g.shape
    n_a = a_mu.shape[0]
    nh = d // HEAD_DIM
    alpha = (2.0 * depth) ** 0.25
    zrow = jnp.zeros((d,), F32)
    head_of = jnp.arange(d) // HEAD_DIM
    head_down = (head_of[:, None] == jnp.arange(LANES)[None, :]).astype(BF16)
    rwkv = []
    for l in range(n_a):
        rwkv.append(dict(
            mu=a_mu[l],
            vec=jnp.stack([a_w0[l], a_a0[l], a_k_k[l], a_k_a[l], zrow, zrow, zrow, zrow]),
            scan_vec=jnp.stack([a_r_k[l].reshape(d), a_lnx_g[l], a_lnx_b[l], zrow, zrow, zrow, zrow, zrow]),
            w_rkv=a_w_rkv[l].astype(BF16), w1=a_w1[l].astype(BF16), w2=a_w2[l].astype(BF16),
            a1=a_a1[l].astype(BF16), a2=a_a2[l].astype(BF16), g1=a_g1[l].astype(BF16),
            g2=a_g2[l].astype(BF16), w_o=a_w_o[l].astype(BF16),
            head_down=head_down, head_up=head_down.T))
    ln = [[jnp.stack([ln_g[l, s], ln_b[l, s], zrow, zrow, zrow, zrow, zrow, zrow])
           for s in range(2)] for l in range(depth)]
    pad = LANES - N_GROUPS - N_EXPERTS
    router_w = [jnp.concatenate([rg_w[l], re_w[l], jnp.zeros((d, pad), F32)], 1) for l in range(depth)]
    router_b = [jnp.concatenate([rg_b[l], re_b[l], jnp.zeros((pad,), F32)])[None, :] for l in range(depth)]
    return dict(
        n_a=n_a, alpha=alpha, rwkv=rwkv, ln=ln, router_w=router_w, router_b=router_b,
        kv_w=kv_w.astype(BF16),
        f_w=jnp.concatenate([f_w, jnp.zeros((d, LANES - nh), F32)], 1),
        f_b=jnp.concatenate([f_b, jnp.zeros((LANES - nh,), F32)])[None, :],
        b_wq=b_wq.astype(BF16), b_wo=b_wo.astype(BF16),
        exp_wg=exp_wg, exp_wu=exp_wu, exp_wd=exp_wd)


def kernel(x_prompt, x_sample, state_shift, state_wkv, cache_k, cache_v, cache_logf, c_prompt, c_sample,
           ada_w, ada_b, ln_g, ln_b, a_mu, a_w_rkv, a_w0, a_w1, a_w2, a_a0, a_a1, a_a2, a_g1, a_g2, a_k_k,
           a_k_a, a_r_k, a_lnx_g, a_lnx_b, a_w_o, kv_w, f_w, f_b, b_wq, b_wo, rg_w, rg_b, re_w, re_b,
           exp_wg, exp_wu, exp_wd):
    w = _prepare_weights(ln_g, ln_b, a_mu, a_w_rkv, a_w0, a_w1, a_w2, a_a0, a_a1, a_a2, a_g1, a_g2,
                         a_k_k, a_k_a, a_r_k, a_lnx_g, a_lnx_b, a_w_o, kv_w, f_w, f_b, b_wq, b_wo,
                         rg_w, rg_b, re_w, re_b, exp_wg, exp_wu, exp_wd)
    bp, _, d = x_prompt.shape
    n_a = a_mu.shape[0]
    nh = d // HEAD_DIM
    mods = _ada_mods(jnp.concatenate([c_prompt, c_sample], 0), ada_w, ada_b)
    zero_shift = jnp.zeros((n_a, bp, d), x_prompt.dtype)
    zero_wkv = jnp.zeros((n_a, bp, nh, HEAD_DIM, HEAD_DIM), x_prompt.dtype)
    outs_p = _trunk(x_prompt, mods[:, :bp], zero_shift, zero_wkv, None, w)
    outs_s = _trunk(x_sample, mods[:, bp:], state_shift, state_wkv, (cache_k, cache_v, cache_logf), w)
    y_p, p_shift, p_wkv, p_k, p_v, p_logf = outs_p
    y_s, s_shift, s_wkv, s_k, s_v, s_logf = outs_s
    return (y_p, y_s, p_shift, p_wkv, p_k, p_v, p_logf, s_shift, s_wkv, s_k, s_v, s_logf)
```

```python
import functools

import jax
import jax.numpy as jnp
import numpy as np
from jax import lax
from jax.experimental import pallas as pl
from jax.experimental.pallas import tpu as pltpu

F32 = jnp.float32
BF16 = jnp.bfloat16

HEAD_DIM = 64
LANES = 128
N_GROUPS = 4
EXP_PER_GROUP = 4
N_EXPERTS = N_GROUPS * EXP_PER_GROUP
LN_EPS = 1e-5
GN_EPS = 64e-5
WKV_CHUNK = 64
WKV_CHUNKS_PER_STEP = 2
MOE_EXPERTS_PER_STEP = 2
MOE_TILE = 1024
LOG2E = 1.4426950408889634
VMEM_LIMIT = 56 * 1024 * 1024
NEG_BIG = -1e30


def _params(sem):
    return pltpu.CompilerParams(dimension_semantics=sem, vmem_limit_bytes=VMEM_LIMIT)


def _dg(a, b, ca, cb):
    return lax.dot_general(a, b, (((ca,), (cb,)), ((), ())), preferred_element_type=F32)


def _bdot(a, b, ca=1, cb=0):
    return _dg(a.astype(BF16), b.astype(BF16), ca, cb)


def _split3(x):
    hi = x.astype(BF16)
    r1 = x - hi.astype(F32)
    mid = r1.astype(BF16)
    lo = (r1 - mid.astype(F32)).astype(BF16)
    return hi, mid, lo


def _dot_exact_rhs(a, b_exact, ca=1, cb=0):
    hi, mid, lo = _split3(a)
    bb = b_exact.astype(BF16)
    return _dg(hi, bb, ca, cb) + _dg(mid, bb, ca, cb) + _dg(lo, bb, ca, cb)


def _dot_exact_lhs(a_exact, b, ca=1, cb=0):
    hi, mid, lo = _split3(b)
    aa = a_exact.astype(BF16)
    return _dg(aa, hi, ca, cb) + _dg(aa, mid, ca, cb) + _dg(aa, lo, ca, cb)


def _dot3(a, b, ca=1, cb=0):
    ah = a.astype(BF16)
    al = (a - ah.astype(F32)).astype(BF16)
    bh = b.astype(BF16)
    bl = (b - bh.astype(F32)).astype(BF16)
    return _dg(ah, bh, ca, cb) + _dg(ah, bl, ca, cb) + _dg(al, bh, ca, cb)


_inv_dot = _bdot


def _layer_norm(z, g, b):
    mu = jnp.mean(z, -1, keepdims=True)
    d = z - mu
    var = jnp.mean(d * d, -1, keepdims=True)
    return d * lax.rsqrt(var + LN_EPS) * g + b


def _softplus(z):
    return jnp.maximum(z, 0.0) + jnp.log(1.0 + jnp.exp(-jnp.abs(z)))


def _sigmoid(z):
    return 1.0 / (1.0 + jnp.exp(-z))


def _silu(z):
    return z * _sigmoid(z)


def _ada_kernel(c_ref, w_ref, b_ref, o_ref):
    o_ref[...] = _dot3(_silu(c_ref[...]), w_ref[...]) + b_ref[...]


def _ada_mods(c_all, ada_w, ada_b):
    depth, d, d6 = ada_w.shape
    bsz = c_all.shape[0]
    tn = d
    out = pl.pallas_call(
        _ada_kernel,
        grid=(depth, d6 // tn),
        in_specs=[
            pl.BlockSpec((bsz, d), lambda l, j: (0, 0)),
            pl.BlockSpec((None, d, tn), lambda l, j: (l, 0, j)),
            pl.BlockSpec((None, 1, tn), lambda l, j: (l, 0, j)),
        ],
        out_specs=pl.BlockSpec((None, bsz, tn), lambda l, j: (l, 0, j)),
        out_shape=jax.ShapeDtypeStruct((depth, bsz, d6), F32),
        compiler_params=_params(("parallel", "parallel")),
        name="ada_mods",
    )(c_all, ada_w, ada_b.reshape(depth, 1, d6))
    return out.reshape(depth, bsz, 6, d)


def _rwkv_proj_kernel(x_ref, xp_ref, shift_ref, mods_ref, mu_ref, vec_ref, wrkv_ref, w1_ref, w2_ref,
                      a1_ref, a2_ref, g1_ref, g2_ref, hd_ref, hu_ref,
                      r_ref, lw_ref, k_ref, v_ref, kk_ref, b_ref, g_ref, last_ref):
    tm = x_ref.shape[0]
    sh1 = mods_ref[0:1, :]
    sc1 = mods_ref[1:2, :]
    h = x_ref[...] * (1.0 + sc1) + sh1
    h_prev_tile = xp_ref[7:8, :] * (1.0 + sc1) + sh1
    prev_row = jnp.where(pl.program_id(1) == 0, shift_ref[...], h_prev_tile)
    row = lax.broadcasted_iota(jnp.int32, (tm, 1), 0)
    xx = jnp.where(row == 0, prev_row, pltpu.roll(h, 1, axis=0)) - h

    def mix(i):
        return h + xx * mu_ref[i:i + 1, :]

    w0, a0 = vec_ref[0:1, :], vec_ref[1:2, :]
    k_k, k_a = vec_ref[2:3, :], vec_ref[3:4, :]
    r = _bdot(mix(0), wrkv_ref[0])
    k = _bdot(mix(1), wrkv_ref[1])
    v = _bdot(mix(2), wrkv_ref[2])
    ww = w0 + _bdot(jnp.tanh(_bdot(mix(3), w1_ref[...])), w2_ref[...])
    w_log = -_softplus(-ww) - 0.5
    a = _sigmoid(a0 + _bdot(_bdot(mix(4), a1_ref[...]), a2_ref[...]))
    g = _bdot(_sigmoid(_bdot(mix(5), g1_ref[...])), g2_ref[...])
    kk = k * k_k
    ss = _dot_exact_rhs(_dot_exact_rhs(kk * kk, hd_ref[...]), hu_ref[...])
    kk = kk * lax.rsqrt(jnp.maximum(ss, 1e-24))
    r_ref[...] = r.astype(BF16)
    lw_ref[...] = -jnp.exp(w_log)
    k_ref[...] = (k * (1.0 + (a - 1.0) * k_a)).astype(BF16)
    v_ref[...] = v.astype(BF16)
    kk_ref[...] = kk.astype(BF16)
    b_ref[...] = (kk * a).astype(BF16)
    g_ref[...] = g.astype(BF16)
    last_ref[...] = h[tm - 1:tm, :]


def _rwkv_proj(x2d, shift_prev, mods_l, wl, bsz, t, tm):
    n, d = x2d.shape
    nt = t // tm
    tok = pl.BlockSpec((tm, d), lambda b, i: (b * nt + i, 0))
    full = lambda a: pl.BlockSpec(a.shape, lambda b, i: (0,) * a.ndim)
    weights = [wl["mu"], wl["vec"], wl["w_rkv"], wl["w1"], wl["w2"], wl["a1"], wl["a2"], wl["g1"],
               wl["g2"], wl["head_down"], wl["head_up"]]
    outs = pl.pallas_call(
        _rwkv_proj_kernel,
        grid=(bsz, nt),
        in_specs=[
            tok,
            pl.BlockSpec((8, d), lambda b, i: (jnp.maximum((b * nt + i) * (tm // 8) - 1, 0), 0)),
            pl.BlockSpec((None, 1, d), lambda b, i: (b, 0, 0)),
            pl.BlockSpec((None, 6, d), lambda b, i: (b, 0, 0)),
        ] + [full(a) for a in weights],
        out_specs=[tok] * 7 + [pl.BlockSpec((None, 1, d), lambda b, i: (b, 0, 0))],
        out_shape=[jax.ShapeDtypeStruct((n, d), F32 if i == 1 else BF16) for i in range(7)]
        + [jax.ShapeDtypeStruct((bsz, 1, d), F32)],
        compiler_params=_params(("parallel", "arbitrary")),
        name="rwkv_proj",
    )(x2d, x2d, shift_prev.reshape(bsz, 1, d), mods_l, *weights)
    return outs[:7], outs[7].reshape(bsz, d)


def _wkv_kernel(c, r_ref, lw_ref, k_ref, v_ref, kk_ref, b_ref, s0_ref, vec_ref, tri_ref, ones_ref,
                y_ref, s_out_ref, state_ref):
    n_chunks = r_ref.shape[0] // c
    n_pairs = r_ref.shape[1] // LANES
    first = pl.program_id(1) == 0

    @pl.when(first)
    def _():
        state_ref[...] = s0_ref[...]

    lane = lax.broadcasted_iota(jnp.int32, (c, LANES), 1)
    low = lane < HEAD_DIM
    ri = lax.broadcasted_iota(jnp.int32, (2 * c, 2 * c), 0)
    ci = lax.broadcasted_iota(jnp.int32, (2 * c, 2 * c), 1)
    same = (ri >= c) == (ci >= c)
    strict = same & (ci < ri)
    incl = same & (ci <= ri)
    eye = (ri == ci).astype(F32)
    tri = tri_ref[...]
    ones_bd = ones_ref[...]

    def stack(z):
        return jnp.concatenate([jnp.where(low, z, 0.0), jnp.where(low, 0.0, z)], axis=0)

    pairs = range(n_pairs)
    slabs = [slice(p * LANES, (p + 1) * LANES) for p in pairs]

    def rows(z):
        return jnp.concatenate([z[:, s] for s in slabs], 0)

    def lanes(z):
        return jnp.concatenate([z[p * c:(p + 1) * c] for p in pairs], 1)

    s_prev = [state_ref[p] for p in pairs]
    for ch in range(n_chunks):
        rs = slice(ch * c, (ch + 1) * c)
        s_prev, y_ref[rs, :] = _wkv_chunk(
            c, s_prev, lw_ref[rs, :], r_ref[rs, :], k_ref[rs, :], v_ref[rs, :], kk_ref[rs, :], b_ref[rs, :],
            vec_ref, tri, ones_bd, stack, rows, lanes, slabs, strict, incl, eye)
    state_ref[...] = jnp.stack(s_prev)

    @pl.when(pl.program_id(1) == pl.num_programs(1) - 1)
    def _():
        s_out_ref[...] = state_ref[...]


def _wkv_chunk(c, s_prev, lw_all, r_bf, k_bf, v_bf, kk_bf, b_bf, vec_ref, tri, ones_bd, stack, rows, lanes,
               slabs, strict, incl, eye):
    cum_all = _dot_exact_lhs(tri, lw_all)
    prev_all = cum_all - lw_all
    mid_all = cum_all[c // 2 - 1:c // 2, :]
    end_all = cum_all[c - 1:c, :]
    e_in = jnp.exp(mid_all - cum_all)
    e_out = jnp.exp(end_all - cum_all)
    e_end = jnp.exp(end_all)
    r_all, k_all, v_all = r_bf.astype(F32), k_bf.astype(F32), v_bf.astype(F32)
    a_all, b_all = -kk_bf.astype(F32), b_bf.astype(F32)
    aq, rq = a_all * jnp.exp(prev_all - mid_all), r_all * jnp.exp(cum_all - mid_all)
    bi, ki = b_all * e_in, k_all * e_in
    a0, r0 = a_all * jnp.exp(prev_all), r_all * jnp.exp(cum_all)
    bo, ko = b_all * e_out, k_all * e_out

    gram = [_bdot(jnp.concatenate([stack(aq[:, s]), stack(rq[:, s])], 0),
                  jnp.concatenate([stack(bi[:, s]), stack(ki[:, s])], 0), 1, 1) for s in slabs]
    a_ab = [jnp.where(strict, g[:2 * c, :2 * c], 0.0) for g in gram]
    a_ak = [jnp.where(strict, g[:2 * c, 2 * c:], 0.0) for g in gram]
    a_r = [jnp.concatenate([jnp.where(incl, g[2 * c:, :2 * c], 0.0),
                            jnp.where(incl, g[2 * c:, 2 * c:], 0.0)], 1) for g in gram]
    tinv = [eye + m for m in a_ab]
    pw = a_ab
    for _ in range(max(c.bit_length() - 2, 0)):
        pw = [_inv_dot(m, m) for m in pw]
        tinv = [t_ + _inv_dot(t_, m) for t_, m in zip(tinv, pw)]
    from_state = [_bdot(jnp.concatenate([stack(a0[:, s]), stack(r0[:, s])], 0), sp, 1, 1)
                  for s, sp in zip(slabs, s_prev)]
    v_st = [stack(v_all[:, s]) for s in slabs]
    rhs = [fs[:2 * c] + _bdot(m, vs) for fs, m, vs in zip(from_state, a_ak, v_st)]
    u = [_inv_dot(t_, x) for t_, x in zip(tinv, rhs)]
    uv = [jnp.concatenate([u_, vs], 0) for u_, vs in zip(u, v_st)]
    o_st = [fs[2 * c:] + _bdot(m, x) for fs, m, x in zip(from_state, a_r, uv)]
    o = [x[:c] + x[c:] for x in o_st]
    new_state = [sp * e_end[:, s] + _bdot(x, jnp.concatenate([stack(bo[:, s]), stack(ko[:, s])], 0), 0, 0)
                 for s, sp, x in zip(slabs, s_prev, uv)]
    o_rows = jnp.concatenate(o, 0)
    dev = o_rows - _dot_exact_rhs(o_rows, ones_bd) * (1.0 / HEAD_DIM)
    var = _dot_exact_rhs(dev * dev, ones_bd) * (1.0 / HEAD_DIM)
    bonus = _dot_exact_rhs(rows(r_all * k_all * vec_ref[0:1, :]), ones_bd)
    y = (lanes(dev * lax.rsqrt(var + GN_EPS)) * vec_ref[1:2, :] + vec_ref[2:3, :] + lanes(bonus) * v_all)
    return new_state, y


def _pair_states(s):
    bsz, nh, n, _ = s.shape
    s = s.reshape(bsz, nh // 2, 2, n, n)
    z = jnp.zeros_like(s[:, :, 0])
    top = jnp.concatenate([s[:, :, 0], z], -1)
    bot = jnp.concatenate([z, s[:, :, 1]], -1)
    return jnp.concatenate([top, bot], -2)


def _unpair_states(sp):
    bsz, npair, _, _ = sp.shape
    n = HEAD_DIM
    return jnp.stack([sp[:, :, :n, :n], sp[:, :, n:, n:]], 2).reshape(bsz, 2 * npair, n, n)


def _wkv(proj, s0, wl, bsz, t):
    r, lw, k, v, kk, b = proj
    n, d = r.shape
    c = min(WKV_CHUNK, t)
    rows_per_step = min(WKV_CHUNKS_PER_STEP * c, t)
    nc = t // rows_per_step
    npair = d // LANES
    tok = pl.BlockSpec((rows_per_step, d), lambda bb, i: (bb * nc + i, 0))
    st = pl.BlockSpec((None, npair, LANES, LANES), lambda bb, i: (bb, 0, 0, 0))
    tri = jnp.tril(jnp.ones((c, c), BF16))
    hid = jnp.arange(LANES) // HEAD_DIM
    ones_bd = (hid[:, None] == hid[None, :]).astype(BF16)
    y, s_out = pl.pallas_call(
        functools.partial(_wkv_kernel, c),
        grid=(bsz, nc),
        in_specs=[tok] * 6 + [st,
                              pl.BlockSpec((8, d), lambda bb, i: (0, 0)),
                              pl.BlockSpec((c, c), lambda bb, i: (0, 0)),
                              pl.BlockSpec((LANES, LANES), lambda bb, i: (0, 0))],
        out_specs=[tok, st],
        out_shape=[jax.ShapeDtypeStruct((n, d), F32),
                   jax.ShapeDtypeStruct((bsz, npair, LANES, LANES), F32)],
        scratch_shapes=[pltpu.VMEM((npair, LANES, LANES), F32)],
        compiler_params=_params(("parallel", "arbitrary")),
        name="wkv_scan",
    )(r, lw, k, v, kk, b, _pair_states(s0.astype(F32)), wl["scan_vec"], tri, ones_bd)
    return y, _unpair_states(s_out)


def _router(logits):
    lane_i = lax.broadcasted_iota(jnp.int32, logits.shape, 1)
    lane = lane_i.astype(F32)
    far = 1e9
    is_g = lane_i < N_GROUPS
    gl = jnp.where(is_g, logits, NEG_BIG)
    gmax = jnp.max(gl, -1, keepdims=True)
    gsel = jnp.min(jnp.where(gl == gmax, lane, far), -1, keepdims=True)
    gprob = 1.0 / jnp.sum(jnp.where(is_g, jnp.exp(gl - gmax), 0.0), -1, keepdims=True)
    group_of = lax.shift_right_arithmetic(lane_i - N_GROUPS, 2).astype(F32)
    in_group = (lane_i >= N_GROUPS) & (lane_i < N_GROUPS + N_EXPERTS) & (group_of == gsel)
    el = jnp.where(in_group, logits, NEG_BIG)
    v1 = jnp.max(el, -1, keepdims=True)
    i1 = jnp.min(jnp.where(el == v1, lane, far), -1, keepdims=True)
    el2 = jnp.where(lane == i1, NEG_BIG, el)
    v2 = jnp.max(el2, -1, keepdims=True)
    i2 = jnp.min(jnp.where(el2 == v2, lane, far), -1, keepdims=True)
    e2 = jnp.exp(v2 - v1)
    w1 = gprob / (1.0 + e2)
    w2 = gprob * e2 / (1.0 + e2)
    return jnp.where(lane == i1, w1, 0.0) + jnp.where(lane == i2, w2, 0.0)


def _post_kernel(gated, alpha, *refs):
    if gated:
        x_ref, y_ref, g_ref, mods_ref, wo_ref, ln_ref, rw_ref, rb_ref, x1_ref, comb_ref = refs
        y = y_ref[...] * g_ref[...].astype(F32)
    else:
        x_ref, y_ref, mods_ref, wo_ref, ln_ref, rw_ref, rb_ref, x1_ref, comb_ref = refs
        y = y_ref[...]
    gt1, sh2, sc2 = mods_ref[2:3, :], mods_ref[3:4, :], mods_ref[4:5, :]
    x1 = _layer_norm(alpha * x_ref[...] + gt1 * _bdot(y, wo_ref[...]), ln_ref[0:1, :], ln_ref[1:2, :])
    x1_ref[...] = x1
    h2 = x1 * (1.0 + sc2) + sh2
    comb_ref[...] = _router(_dot3(h2, rw_ref[...]) + rb_ref[...])


def _post_mixer(x2d, y2d, g2d, mods_l, w_o, ln_pack, router_w, router_b, alpha, bsz, t, tm):
    n, d = x2d.shape
    nt = t // tm
    tok = pl.BlockSpec((tm, d), lambda b, i: (b * nt + i, 0))
    full = lambda a: pl.BlockSpec(a.shape, lambda b, i: (0,) * a.ndim)
    gated = g2d is not None
    acts = [x2d, y2d] + ([g2d] if gated else [])
    consts = [w_o, ln_pack, router_w, router_b]
    return pl.pallas_call(
        functools.partial(_post_kernel, gated, alpha),
        grid=(bsz, nt),
        in_specs=[tok] * len(acts) + [pl.BlockSpec((None, 6, d), lambda b, i: (b, 0, 0))]
        + [full(a) for a in consts],
        out_specs=[tok, pl.BlockSpec((tm, LANES), lambda b, i: (b * nt + i, 0))],
        out_shape=[jax.ShapeDtypeStruct((n, d), F32), jax.ShapeDtypeStruct((n, LANES), F32)],
        compiler_params=_params(("parallel", "parallel")),
        name="post_mixer",
    )(*acts, mods_l, *consts)


def _moe_kernel(alpha, x1_ref, comb_ref, mods_ref, ln_ref, wg_ref, wu_ref, wd_ref, x2_ref, h2_ref, acc_ref):
    e = pl.program_id(1)

    @pl.when(e == 0)
    def _():
        sh2, sc2 = mods_ref[3], mods_ref[4]
        h2_ref[...] = (x1_ref[...] * (1.0 + sc2) + sh2).astype(BF16)
        acc_ref[...] = jnp.zeros_like(acc_ref)

    h2 = h2_ref[...]
    per_step, f, d = wd_ref.shape
    lane = lax.broadcasted_iota(jnp.int32, comb_ref.shape, 1)
    comb = comb_ref[...]
    hidden = []
    for i in range(per_step):
        hg = _dg(h2, wg_ref[i].astype(BF16), 1, 0)
        hu = _dg(h2, wu_ref[i].astype(BF16), 1, 0)
        ce = jnp.sum(jnp.where(lane == e * per_step + (i + N_GROUPS), comb, 0.0), -1, keepdims=True)
        hidden.append((_silu(hg) * hu * ce).astype(BF16))
    acc_ref[...] += _dg(jnp.concatenate(hidden, 1), wd_ref[...].reshape(per_step * f, d).astype(BF16), 1, 0)

    @pl.when(e == pl.num_programs(1) - 1)
    def _():
        gt2 = mods_ref[5]
        x2_ref[...] = _layer_norm(alpha * x1_ref[...] + gt2 * acc_ref[...], ln_ref[0:1, :], ln_ref[1:2, :])


def _moe(x1, comb, mods_l, ln_pack, wg, wu, wd, alpha, bsz, t):
    n, d = x1.shape
    ne, _, f = wg.shape
    per_step = MOE_EXPERTS_PER_STEP
    if t >= MOE_TILE:
        tm = MOE_TILE
        mods = mods_l.reshape(bsz, 6, 1, d)
        mods_spec = pl.BlockSpec((None, 6, 1, d), lambda i, e: (i // (t // tm), 0, 0, 0))
    else:
        tm = min(n, MOE_TILE)
        mods = jnp.repeat(mods_l.transpose(1, 0, 2), t, axis=1)
        mods_spec = pl.BlockSpec((6, tm, d), lambda i, e: (0, i, 0))
    return pl.pallas_call(
        functools.partial(_moe_kernel, alpha),
        grid=(n // tm, ne // per_step),
        in_specs=[
            pl.BlockSpec((tm, d), lambda i, e: (i, 0)),
            pl.BlockSpec((tm, LANES), lambda i, e: (i, 0)),
            mods_spec,
            pl.BlockSpec(ln_pack.shape, lambda i, e: (0, 0)),
            pl.BlockSpec((per_step, d, f), lambda i, e: (e, 0, 0)),
            pl.BlockSpec((per_step, d, f), lambda i, e: (e, 0, 0)),
            pl.BlockSpec((per_step, f, d), lambda i, e: (e, 0, 0)),
        ],
        out_specs=pl.BlockSpec((tm, d), lambda i, e: (i, 0)),
        out_shape=jax.ShapeDtypeStruct((n, d), F32),
        scratch_shapes=[pltpu.VMEM((tm, d), BF16), pltpu.VMEM((tm, d), F32)],
        compiler_params=_params(("parallel", "arbitrary")),
        name="hmoe",
    )(x1, comb, mods, ln_pack, wg, wu, wd)


def _fox_bias_placement(d):
    place_q = np.zeros((3 * LANES, d), np.float32)
    place_k = np.zeros((3 * LANES, d), np.float32)
    const_q = np.zeros((1, d), np.float32)
    const_k = np.zeros((1, d), np.float32)
    for h in range(d // HEAD_DIM):
        base = (h // 2) * LANES + (HEAD_DIM if h % 2 == 0 else 0)
        for term in range(3):
            place_q[term * LANES + h, base + term] = 1.0
            place_k[term * LANES + h, base + 3 + term] = -1.0
            const_q[0, base + 3 + term] = 1.0
            const_k[0, base + term] = 1.0
    return (jnp.asarray(place_q, BF16), jnp.asarray(place_k, BF16), jnp.asarray(const_q),
            jnp.asarray(const_k))


def _kv_kernel(x_ref, kvw_ref, fw_ref, fb_ref, tri_ref, pq_ref, pk_ref, cq_ref, ck_ref,
               k_ref, v_ref, kb_ref, vb_ref, lf_ref, fc_ref, qbias_ref, kbias_ref, carry_ref):
    d = x_ref.shape[1]

    @pl.when(pl.program_id(1) == 0)
    def _():
        carry_ref[...] = jnp.zeros_like(carry_ref)

    x = x_ref[...]
    kv = _bdot(x, kvw_ref[...])
    k, v = kv[:, :d], kv[:, d:]
    k_ref[...] = k
    v_ref[...] = v
    kb_ref[...] = k.astype(BF16)
    vb_ref[...] = v.astype(BF16)
    z = _dot3(x, fw_ref[...]) + fb_ref[...]
    lf = -_softplus(-z)
    lf_ref[...] = lf
    fc = _dot_exact_lhs(tri_ref[...], lf) + carry_ref[...]
    fc_ref[...] = fc
    carry_ref[...] = fc[fc.shape[0] - 1:, :]
    terms = jnp.concatenate(_split3(fc * LOG2E), 1)
    qbias_ref[...] = (_dg(terms, pq_ref[...], 1, 0) + cq_ref[...]).astype(BF16)
    kbias_ref[...] = (_dg(terms, pk_ref[...], 1, 0) + ck_ref[...]).astype(BF16)


def _kv_proj(x2d, kv_w, f_w, f_b, bsz, t, tm):
    n, d = x2d.shape
    nt = t // tm
    tok = pl.BlockSpec((tm, d), lambda b, i: (b * nt + i, 0))
    nar = pl.BlockSpec((tm, LANES), lambda b, i: (b * nt + i, 0))
    tri = jnp.tril(jnp.ones((tm, tm), BF16))
    full = lambda a: pl.BlockSpec(a.shape, lambda b, i: (0,) * a.ndim)
    consts = [kv_w, f_w, f_b, tri, *_fox_bias_placement(d)]
    return pl.pallas_call(
        _kv_kernel,
        grid=(bsz, nt),
        in_specs=[tok] + [full(a) for a in consts],
        out_specs=[tok, tok, tok, tok, nar, nar, tok, tok],
        out_shape=[jax.ShapeDtypeStruct((n, d), F32), jax.ShapeDtypeStruct((n, d), F32),
                   jax.ShapeDtypeStruct((n, d), BF16), jax.ShapeDtypeStruct((n, d), BF16),
                   jax.ShapeDtypeStruct((n, LANES), F32), jax.ShapeDtypeStruct((n, LANES), F32),
                   jax.ShapeDtypeStruct((n, d), BF16), jax.ShapeDtypeStruct((n, d), BF16)],
        scratch_shapes=[pltpu.VMEM((1, LANES), F32)],
        compiler_params=_params(("parallel", "arbitrary")),
        name="kv_proj",
    )(x2d, *consts)


def _q_kernel(scale, x_ref, mods_ref, wq_ref, q_ref):
    h = x_ref[...] * (1.0 + mods_ref[1:2, :]) + mods_ref[0:1, :]
    q_ref[...] = (_bdot(h, wq_ref[...]) * scale).astype(BF16)


def _q_proj(x2d, mods_l, wq, scale, bsz, t, tm):
    n, d = x2d.shape
    nt = t // tm
    tok = pl.BlockSpec((tm, d), lambda b, i: (b * nt + i, 0))
    return pl.pallas_call(
        functools.partial(_q_kernel, scale),
        grid=(bsz, nt),
        in_specs=[tok, pl.BlockSpec((None, 6, d), lambda b, i: (b, 0, 0)),
                  pl.BlockSpec(wq.shape, lambda b, i: (0, 0))],
        out_specs=tok,
        out_shape=jax.ShapeDtypeStruct((n, d), BF16),
        compiler_params=_params(("parallel", "parallel")),
        name="q_proj",
    )(x2d, mods_l, wq)


def _fox_prompt_kernel(tq, q_ref, k_ref, v_ref, qb_ref, kb_ref, o_ref):
    t = q_ref.shape[0]
    lane = lax.broadcasted_iota(jnp.int32, (1, LANES), 1)
    rr = lax.broadcasted_iota(jnp.int32, (tq, tq), 0)
    cc = lax.broadcasted_iota(jnp.int32, (tq, tq), 1)
    own = [lane < HEAD_DIM, lane >= HEAD_DIM]
    k_aug = [jnp.where(own[hh], k_ref[...], kb_ref[...]) for hh in range(2)]
    v = v_ref[...]
    for qi in range(t // tq):
        lo, hi = qi * tq, (qi + 1) * tq
        heads = []
        for hh in range(2):
            q = jnp.where(own[hh], q_ref[lo:hi, :], qb_ref[lo:hi, :])
            s_diag = jnp.where(cc <= rr, _dg(q, k_aug[hh][lo:hi], 1, 1), NEG_BIG)
            m = jnp.max(s_diag, -1, keepdims=True)
            if qi > 0:
                s_past = _dg(q, k_aug[hh][:lo], 1, 1)
                m = jnp.maximum(m, jnp.max(s_past, -1, keepdims=True))
            p = jnp.exp2(s_diag - m)
            num = _dg(p.astype(BF16), v[lo:hi], 1, 0)
            den = jnp.sum(p, -1, keepdims=True)
            if qi > 0:
                p = jnp.exp2(s_past - m)
                num = num + _dg(p.astype(BF16), v[:lo], 1, 0)
                den = den + jnp.sum(p, -1, keepdims=True)
            heads.append(num / den)
        o_ref[lo:hi, :] = jnp.where(own[0], heads[0], heads[1])


def _fox_prompt(q, kb, vb, q_bias, k_bias, bsz, t, tq):
    n, d = q.shape
    seq = pl.BlockSpec((t, LANES), lambda b, p: (b, p))
    return pl.pallas_call(
        functools.partial(_fox_prompt_kernel, tq),
        grid=(bsz, d // LANES),
        in_specs=[seq] * 5,
        out_specs=seq,
        out_shape=jax.ShapeDtypeStruct((n, d), F32),
        compiler_params=_params(("parallel", "parallel")),
        name="fox_prompt",
    )(q, kb, vb, q_bias, k_bias)


def _fox_sample_kernel(q_ref, kn_ref, vn_ref, fn_ref, fnt_ref, ck_ref, cv_ref, clf_ref, upper_ref,
                       o_ref, m_ref, l_ref, acc_ref, carry_ref):
    step = pl.program_id(1)
    t, d = q_ref.shape
    nh = d // HEAD_DIM

    @pl.when(step == 0)
    def _():
        m_ref[...] = jnp.full_like(m_ref, NEG_BIG)
        l_ref[...] = jnp.zeros_like(l_ref)
        acc_ref[...] = jnp.zeros_like(acc_ref)
        carry_ref[...] = jnp.zeros_like(carry_ref)

    heads = range(nh)
    slabs = [slice(pr * LANES, (pr + 1) * LANES) for pr in range(nh // 2)]
    lane = lax.broadcasted_iota(jnp.int32, (1, LANES), 1)
    q_own = [jnp.where((lane < HEAD_DIM) == (h % 2 == 0), q_ref[:, slabs[h // 2]], jnp.zeros((), BF16))
             for h in heads]

    def attend(keys, values, bias, mask):
        s = [_dg(q_own[h], keys[h // 2], 1, 1) + bias[h] for h in heads]
        if mask is not None:
            s = [jnp.where(mask, x, NEG_BIG) for x in s]
        m_old = m_ref[...]
        m_new = jnp.maximum(m_old, jnp.stack([jnp.max(x, -1, keepdims=True) for x in s]))
        alpha = jnp.exp(m_old - m_new)
        p = [jnp.exp(x - m_new[h]) for h, x in enumerate(s)]
        l_ref[...] = alpha * l_ref[...] + jnp.stack([jnp.sum(x, -1, keepdims=True) for x in p])
        acc_ref[...] = alpha * acc_ref[...] + jnp.stack(
            [_dg(p[h].astype(BF16), values[h // 2], 1, 0) for h in heads])
        m_ref[...] = m_new

    clf = clf_ref[...]
    suf = _dot_exact_rhs(clf, upper_ref[...]) + carry_ref[...]
    carry_ref[...] = carry_ref[...] + jnp.sum(clf, -1, keepdims=True)
    fn = fn_ref[...]
    attend([ck_ref[:, s].astype(BF16) for s in slabs], [cv_ref[:, s].astype(BF16) for s in slabs],
           [fn[:, h:h + 1] + suf[h:h + 1, :] for h in heads], None)

    @pl.when(step == pl.num_programs(1) - 1)
    def _():
        rr = lax.broadcasted_iota(jnp.int32, (t, t), 0)
        cc = lax.broadcasted_iota(jnp.int32, (t, t), 1)
        fnt = fnt_ref[...]
        attend([kn_ref[:, s] for s in slabs], [vn_ref[:, s] for s in slabs],
               [fn[:, h:h + 1] - fnt[h:h + 1, :] for h in heads], cc <= rr)
        out = acc_ref[...] / l_ref[...]
        o_ref[...] = jnp.concatenate(
            [jnp.where(lane < HEAD_DIM, out[2 * pr], out[2 * pr + 1]) for pr in range(nh // 2)], 1)


def _fox_sample(q, kb, vb, fcum, cache_k, cache_v, cache_logf, bsz, t, tk):
    n, d = q.shape
    nh = d // HEAD_DIM
    plen = cache_k.shape[1]
    nk = plen // tk
    ck = cache_k.reshape(bsz * plen, d)
    cv = cache_v.reshape(bsz * plen, d)
    clf_t = cache_logf.astype(F32).transpose(0, 2, 1)
    fn_t = fcum[:, :nh].reshape(bsz, t, nh).transpose(0, 2, 1)
    upper = (jnp.arange(tk)[:, None] > jnp.arange(tk)[None, :]).astype(BF16)
    tok = pl.BlockSpec((t, d), lambda b, j: (b, 0))
    past = pl.BlockSpec((tk, d), lambda b, j: (b * nk + nk - 1 - j, 0))
    return pl.pallas_call(
        _fox_sample_kernel,
        grid=(bsz, nk),
        in_specs=[tok, tok, tok,
                  pl.BlockSpec((t, LANES), lambda b, j: (b, 0)),
                  pl.BlockSpec((None, nh, t), lambda b, j: (b, 0, 0)),
                  past, past,
                  pl.BlockSpec((None, nh, tk), lambda b, j: (b, 0, nk - 1 - j)),
                  pl.BlockSpec((tk, tk), lambda b, j: (0, 0))],
        out_specs=tok,
        out_shape=jax.ShapeDtypeStruct((n, d), F32),
        scratch_shapes=[pltpu.VMEM((nh, t, 1), F32), pltpu.VMEM((nh, t, 1), F32),
                        pltpu.VMEM((nh, t, LANES), F32), pltpu.VMEM((nh, 1), F32)],
        compiler_params=_params(("parallel", "arbitrary")),
        name="fox_sample",
    )(q, kb, vb, fcum, fn_t, ck, cv, clf_t, upper)


def _tile(t, cap):
    return min(t, cap)


def _trunk(x, mods, shift_in, wkv_in, cache, w):
    bsz, t, d = x.shape
    depth = mods.shape[0]
    n_a = w["n_a"]
    nh = d // HEAD_DIM
    x2d = x.reshape(bsz * t, d)
    shifts, states = [], []
    kv = None
    for l in range(depth):
        mods_l = mods[l]
        if l < n_a:
            wl = w["rwkv"][l]
            proj, last = _rwkv_proj(x2d, shift_in[l], mods_l, wl, bsz, t, _tile(t, 256))
            y, s_new = _wkv(proj[:6], wkv_in[l], wl, bsz, t)
            shifts.append(last)
            states.append(s_new.astype(wkv_in.dtype))
            mixer_out, gate, w_o = y, proj[6], wl["w_o"]
        else:
            if kv is None:
                kv = _kv_proj(x2d, w["kv_w"], w["f_w"], w["f_b"], bsz, t, _tile(t, 512))
            k_sh, v_sh, kb, vb, lf, fcum, q_bias, k_bias = kv
            j = l - n_a
            q_scale = HEAD_DIM ** -0.5 * (LOG2E if cache is None else 1.0)
            q = _q_proj(x2d, mods_l, w["b_wq"][j], q_scale, bsz, t, _tile(t, 512))
            if cache is None:
                mixer_out = _fox_prompt(q, kb, vb, q_bias, k_bias, bsz, t, _tile(t, 512))
            else:
                mixer_out = _fox_sample(q, kb, vb, fcum, cache[0], cache[1], cache[2], bsz, t,
                                        _tile(cache[0].shape[1], 512))
            gate, w_o = None, w["b_wo"][j]
        x1, comb = _post_mixer(x2d, mixer_out, gate, mods_l, w_o, w["ln"][l][0], w["router_w"][l],
                               w["router_b"][l], w["alpha"], bsz, t, _tile(t, 512))
        x2d = _moe(x1, comb, mods_l, w["ln"][l][1], w["exp_wg"][l], w["exp_wu"][l], w["exp_wd"][l],
                   w["alpha"], bsz, t)
    k_sh, v_sh, lf = kv[0], kv[1], kv[4]
    return (x2d.reshape(bsz, t, d), jnp.stack(shifts), jnp.stack(states),
            k_sh.reshape(bsz, t, nh, HEAD_DIM), v_sh.reshape(bsz, t, nh, HEAD_DIM),
            lf[:, :nh].reshape(bsz, t, nh).astype(x.dtype))


def _prepare_weights(ln_g, ln_b, a_mu, a_w_rkv, a_w0, a_w1, a_w2, a_a0, a_a1, a_a2, a_g1, a_g2, a_k_k,
                     a_k_a, a_r_k, a_lnx_g, a_lnx_b, a_w_o, kv_w, f_w, f_b, b_wq, b_wo, rg_w, rg_b,
                     re_w, re_b, exp_wg, exp_wu, exp_wd):
    depth, _, d = ln_g.shape
    n_a = a_mu.shape[0]
    nh = d // HEAD_DIM
    alpha = (2.0 * depth) ** 0.25
    zrow = jnp.zeros((d,), F32)
    head_of = jnp.arange(d) // HEAD_DIM
    head_down = (head_of[:, None] == jnp.arange(LANES)[None, :]).astype(BF16)
    rwkv = []
    for l in range(n_a):
        rwkv.append(dict(
            mu=a_mu[l],
            vec=jnp.stack([a_w0[l], a_a0[l], a_k_k[l], a_k_a[l], zrow, zrow, zrow, zrow]),
            scan_vec=jnp.stack([a_r_k[l].reshape(d), a_lnx_g[l], a_lnx_b[l], zrow, zrow, zrow, zrow, zrow]),
            w_rkv=a_w_rkv[l].astype(BF16), w1=a_w1[l].astype(BF16), w2=a_w2[l].astype(BF16),
            a1=a_a1[l].astype(BF16), a2=a_a2[l].astype(BF16), g1=a_g1[l].astype(BF16),
            g2=a_g2[l].astype(BF16), w_o=a_w_o[l].astype(BF16),
            head_down=head_down, head_up=head_down.T))
    ln = [[jnp.stack([ln_g[l, s], ln_b[l, s], zrow, zrow, zrow, zrow, zrow, zrow])
           for s in range(2)] for l in range(depth)]
    pad = LANES - N_GROUPS - N_EXPERTS
    router_w = [jnp.concatenate([rg_w[l], re_w[l], jnp.zeros((d, pad), F32)], 1) for l in range(depth)]
    router_b = [jnp.concatenate([rg_b[l], re_b[l], jnp.zeros((pad,), F32)])[None, :] for l in range(depth)]
    return dict(
        n_a=n_a, alpha=alpha, rwkv=rwkv, ln=ln, router_w=router_w, router_b=router_b,
        kv_w=kv_w.astype(BF16),
        f_w=jnp.concatenate([f_w, jnp.zeros((d, LANES - nh), F32)], 1),
        f_b=jnp.concatenate([f_b, jnp.zeros((LANES - nh,), F32)])[None, :],
        b_wq=b_wq.astype(BF16), b_wo=b_wo.astype(BF16),
        exp_wg=exp_wg, exp_wu=exp_wu, exp_wd=exp_wd)


def kernel(x_prompt, x_sample, state_shift, state_wkv, cache_k, cache_v, cache_logf, c_prompt, c_sample,
           ada_w, ada_b, ln_g, ln_b, a_mu, a_w_rkv, a_w0, a_w1, a_w2, a_a0, a_a1, a_a2, a_g1, a_g2, a_k_k,
           a_k_a, a_r_k, a_lnx_g, a_lnx_b, a_w_o, kv_w, f_w, f_b, b_wq, b_wo, rg_w, rg_b, re_w, re_b,
           exp_wg, exp_wu, exp_wd):
    w = _prepare_weights(ln_g, ln_b, a_mu, a_w_rkv, a_w0, a_w1, a_w2, a_a0, a_a1, a_a2, a_g1, a_g2,
                         a_k_k, a_k_a, a_r_k, a_lnx_g, a_lnx_b, a_w_o, kv_w, f_w, f_b, b_wq, b_wo,
                         rg_w, rg_b, re_w, re_b, exp_wg, exp_wu, exp_wd)
    bp, _, d = x_prompt.shape
    n_a = a_mu.shape[0]
    nh = d // HEAD_DIM
    mods = _ada_mods(jnp.concatenate([c_prompt, c_sample], 0), ada_w, ada_b)
    zero_shift = jnp.zeros((n_a, bp, d), x_prompt.dtype)
    zero_wkv = jnp.zeros((n_a, bp, nh, HEAD_DIM, HEAD_DIM), x_prompt.dtype)
    outs_p = _trunk(x_prompt, mods[:, :bp], zero_shift, zero_wkv, None, w)
    outs_s = _trunk(x_sample, mods[:, bp:], state_shift, state_wkv, (cache_k, cache_v, cache_logf), w)
    y_p, p_shift, p_wkv, p_k, p_v, p_logf = outs_p
    y_s, s_shift, s_wkv, s_k, s_v, s_logf = outs_s
    return (y_p, y_s, p_shift, p_wkv, p_k, p_v, p_logf, s_shift, s_wkv, s_k, s_v, s_logf)
```

```python
import functools

import jax
import jax.numpy as jnp
import numpy as np
from jax import lax
from jax.experimental import pallas as pl
from jax.experimental.pallas import tpu as pltpu

F32 = jnp.float32
BF16 = jnp.bfloat16

HEAD_DIM = 64
LANES = 128
N_GROUPS = 4
EXP_PER_GROUP = 4
N_EXPERTS = N_GROUPS * EXP_PER_GROUP
LN_EPS = 1e-5
GN_EPS = 64e-5
WKV_CHUNK = 64
WKV_CHUNKS_PER_STEP = 4
MOE_EXPERTS_PER_STEP = 2
MOE_TILE = 1024
LOG2E = 1.4426950408889634
VMEM_LIMIT = 56 * 1024 * 1024
NEG_BIG = -1e30


def _params(sem):
    return pltpu.CompilerParams(dimension_semantics=sem, vmem_limit_bytes=VMEM_LIMIT)


def _dg(a, b, ca, cb):
    return lax.dot_general(a, b, (((ca,), (cb,)), ((), ())), preferred_element_type=F32)


def _bdot(a, b, ca=1, cb=0):
    return _dg(a.astype(BF16), b.astype(BF16), ca, cb)


def _split3(x):
    hi = x.astype(BF16)
    r1 = x - hi.astype(F32)
    mid = r1.astype(BF16)
    lo = (r1 - mid.astype(F32)).astype(BF16)
    return hi, mid, lo


def _dot_exact_rhs(a, b_exact, ca=1, cb=0):
    hi, mid, lo = _split3(a)
    bb = b_exact.astype(BF16)
    return _dg(hi, bb, ca, cb) + _dg(mid, bb, ca, cb) + _dg(lo, bb, ca, cb)


def _dot_exact_lhs(a_exact, b, ca=1, cb=0):
    hi, mid, lo = _split3(b)
    aa = a_exact.astype(BF16)
    return _dg(aa, hi, ca, cb) + _dg(aa, mid, ca, cb) + _dg(aa, lo, ca, cb)


def _dot3(a, b, ca=1, cb=0):
    ah = a.astype(BF16)
    al = (a - ah.astype(F32)).astype(BF16)
    bh = b.astype(BF16)
    bl = (b - bh.astype(F32)).astype(BF16)
    return _dg(ah, bh, ca, cb) + _dg(ah, bl, ca, cb) + _dg(al, bh, ca, cb)


_inv_dot = _bdot


def _layer_norm(z, g, b):
    mu = jnp.mean(z, -1, keepdims=True)
    d = z - mu
    var = jnp.mean(d * d, -1, keepdims=True)
    return d * lax.rsqrt(var + LN_EPS) * g + b


def _softplus(z):
    return jnp.maximum(z, 0.0) + jnp.log(1.0 + jnp.exp(-jnp.abs(z)))


def _sigmoid(z):
    return 1.0 / (1.0 + jnp.exp(-z))


def _silu(z):
    return z * _sigmoid(z)


def _ada_kernel(c_ref, w_ref, b_ref, o_ref):
    o_ref[...] = _dot3(_silu(c_ref[...]), w_ref[...]) + b_ref[...]


def _ada_mods(c_all, ada_w, ada_b):
    depth, d, d6 = ada_w.shape
    bsz = c_all.shape[0]
    tn = d
    out = pl.pallas_call(
        _ada_kernel,
        grid=(depth, d6 // tn),
        in_specs=[
            pl.BlockSpec((bsz, d), lambda l, j: (0, 0)),
            pl.BlockSpec((None, d, tn), lambda l, j: (l, 0, j)),
            pl.BlockSpec((None, 1, tn), lambda l, j: (l, 0, j)),
        ],
        out_specs=pl.BlockSpec((None, bsz, tn), lambda l, j: (l, 0, j)),
        out_shape=jax.ShapeDtypeStruct((depth, bsz, d6), F32),
        compiler_params=_params(("parallel", "parallel")),
        name="ada_mods",
    )(c_all, ada_w, ada_b.reshape(depth, 1, d6))
    return out.reshape(depth, bsz, 6, d)


def _rwkv_proj_kernel(x_ref, xp_ref, shift_ref, mods_ref, mu_ref, vec_ref, wrkv_ref, w1_ref, w2_ref,
                      a1_ref, a2_ref, g1_ref, g2_ref, hd_ref, hu_ref,
                      r_ref, lw_ref, k_ref, v_ref, kk_ref, b_ref, g_ref, last_ref):
    tm = x_ref.shape[0]
    sh1 = mods_ref[0:1, :]
    sc1 = mods_ref[1:2, :]
    h = x_ref[...] * (1.0 + sc1) + sh1
    h_prev_tile = xp_ref[7:8, :] * (1.0 + sc1) + sh1
    prev_row = jnp.where(pl.program_id(1) == 0, shift_ref[...], h_prev_tile)
    row = lax.broadcasted_iota(jnp.int32, (tm, 1), 0)
    xx = jnp.where(row == 0, prev_row, pltpu.roll(h, 1, axis=0)) - h

    def mix(i):
        return h + xx * mu_ref[i:i + 1, :]

    w0, a0 = vec_ref[0:1, :], vec_ref[1:2, :]
    k_k, k_a = vec_ref[2:3, :], vec_ref[3:4, :]
    r = _bdot(mix(0), wrkv_ref[0])
    k = _bdot(mix(1), wrkv_ref[1])
    v = _bdot(mix(2), wrkv_ref[2])
    ww = w0 + _bdot(jnp.tanh(_bdot(mix(3), w1_ref[...])), w2_ref[...])
    w_log = -_softplus(-ww) - 0.5
    a = _sigmoid(a0 + _bdot(_bdot(mix(4), a1_ref[...]), a2_ref[...]))
    g = _bdot(_sigmoid(_bdot(mix(5), g1_ref[...])), g2_ref[...])
    kk = k * k_k
    head_sq = _bdot(kk * kk, hd_ref[...])
    hi = head_sq.astype(BF16)
    lo = (head_sq - hi.astype(F32)).astype(BF16)
    ss = _dg(hi, hu_ref[...], 1, 0) + _dg(lo, hu_ref[...], 1, 0)
    kk = kk * lax.rsqrt(jnp.maximum(ss, 1e-24))
    r_ref[...] = r.astype(BF16)
    lw_ref[...] = -jnp.exp(w_log)
    k_ref[...] = (k * (1.0 + (a - 1.0) * k_a)).astype(BF16)
    v_ref[...] = v.astype(BF16)
    kk_ref[...] = kk.astype(BF16)
    b_ref[...] = (kk * a).astype(BF16)
    g_ref[...] = g.astype(BF16)
    last_ref[...] = h[tm - 1:tm, :]


def _rwkv_proj(x2d, shift_prev, mods_l, wl, bsz, t, tm):
    n, d = x2d.shape
    nt = t // tm
    tok = pl.BlockSpec((tm, d), lambda b, i: (b * nt + i, 0))
    full = lambda a: pl.BlockSpec(a.shape, lambda b, i: (0,) * a.ndim)
    weights = [wl["mu"], wl["vec"], wl["w_rkv"], wl["w1"], wl["w2"], wl["a1"], wl["a2"], wl["g1"],
               wl["g2"], wl["head_down"], wl["head_up"]]
    outs = pl.pallas_call(
        _rwkv_proj_kernel,
        grid=(bsz, nt),
        in_specs=[
            tok,
            pl.BlockSpec((8, d), lambda b, i: (jnp.maximum((b * nt + i) * (tm // 8) - 1, 0), 0)),
            pl.BlockSpec((None, 1, d), lambda b, i: (b, 0, 0)),
            pl.BlockSpec((None, 6, d), lambda b, i: (b, 0, 0)),
        ] + [full(a) for a in weights],
        out_specs=[tok] * 7 + [pl.BlockSpec((None, 1, d), lambda b, i: (b, 0, 0))],
        out_shape=[jax.ShapeDtypeStruct((n, d), F32 if i == 1 else BF16) for i in range(7)]
        + [jax.ShapeDtypeStruct((bsz, 1, d), F32)],
        compiler_params=_params(("parallel", "arbitrary")),
        name="rwkv_proj",
    )(x2d, x2d, shift_prev.reshape(bsz, 1, d), mods_l, *weights)
    return outs[:7], outs[7].reshape(bsz, d)


def _wkv_kernel(c, r_ref, lw_ref, k_ref, v_ref, kk_ref, b_ref, s0_ref, vec_ref, tri_ref, ones_ref,
                y_ref, s_out_ref, state_ref):
    n_chunks = r_ref.shape[0] // c
    n_pairs = r_ref.shape[1] // LANES
    first = pl.program_id(1) == 0

    @pl.when(first)
    def _():
        state_ref[...] = s0_ref[...]

    lane = lax.broadcasted_iota(jnp.int32, (c, LANES), 1)
    low = lane < HEAD_DIM
    ri = lax.broadcasted_iota(jnp.int32, (2 * c, 2 * c), 0)
    ci = lax.broadcasted_iota(jnp.int32, (2 * c, 2 * c), 1)
    same = (ri >= c) == (ci >= c)
    strict = same & (ci < ri)
    incl = same & (ci <= ri)
    eye = (ri == ci).astype(F32)
    tri = tri_ref[...]
    ones_bd = ones_ref[...]

    def stack(z):
        return jnp.concatenate([jnp.where(low, z, 0.0), jnp.where(low, 0.0, z)], axis=0)

    pairs = range(n_pairs)
    slabs = [slice(p * LANES, (p + 1) * LANES) for p in pairs]

    def rows(z):
        return jnp.concatenate([z[:, s] for s in slabs], 0)

    def lanes(z):
        return jnp.concatenate([z[p * c:(p + 1) * c] for p in pairs], 1)

    def prepare(ch):
        rs = slice(ch * c, (ch + 1) * c)
        return _wkv_prepare(c, lw_ref[rs, :], r_ref[rs, :], k_ref[rs, :], v_ref[rs, :], kk_ref[rs, :],
                            b_ref[rs, :], vec_ref, tri, ones_bd, stack, rows, lanes, slabs, strict, incl, eye)

    s_prev = [state_ref[p] for p in pairs]
    ready = prepare(0)
    for ch in range(n_chunks):
        upcoming = prepare(ch + 1) if ch + 1 < n_chunks else None
        s_prev, y_ref[ch * c:(ch + 1) * c, :] = _wkv_apply(c, ready, s_prev, vec_ref, ones_bd, lanes, slabs)
        ready = upcoming
    state_ref[...] = jnp.stack(s_prev)

    @pl.when(pl.program_id(1) == pl.num_programs(1) - 1)
    def _():
        s_out_ref[...] = state_ref[...]


def _wkv_prepare(c, lw_all, r_bf, k_bf, v_bf, kk_bf, b_bf, vec_ref, tri, ones_bd, stack, rows, lanes,
                 slabs, strict, incl, eye):
    cum_all = _dot_exact_lhs(tri, lw_all)
    prev_all = cum_all - lw_all
    mid_all = cum_all[c // 2 - 1:c // 2, :]
    end_all = cum_all[c - 1:c, :]
    e_in = jnp.exp(mid_all - cum_all)
    e_out = jnp.exp(end_all - cum_all)
    e_end = jnp.exp(end_all)
    r_all, k_all, v_all = r_bf.astype(F32), k_bf.astype(F32), v_bf.astype(F32)
    a_all, b_all = -kk_bf.astype(F32), b_bf.astype(F32)
    aq, rq = a_all * jnp.exp(prev_all - mid_all), r_all * jnp.exp(cum_all - mid_all)
    bi, ki = b_all * e_in, k_all * e_in
    a0, r0 = a_all * jnp.exp(prev_all), r_all * jnp.exp(cum_all)
    bo, ko = b_all * e_out, k_all * e_out

    gram = [_bdot(jnp.concatenate([stack(aq[:, s]), stack(rq[:, s])], 0),
                  jnp.concatenate([stack(bi[:, s]), stack(ki[:, s])], 0), 1, 1) for s in slabs]
    a_ab = [jnp.where(strict, g[:2 * c, :2 * c], 0.0) for g in gram]
    a_ak = [jnp.where(strict, g[:2 * c, 2 * c:], 0.0) for g in gram]
    a_r = [jnp.concatenate([jnp.where(incl, g[2 * c:, :2 * c], 0.0),
                            jnp.where(incl, g[2 * c:, 2 * c:], 0.0)], 1) for g in gram]
    tinv = [eye + m for m in a_ab]
    pw = a_ab
    for _ in range(max(c.bit_length() - 2, 0)):
        pw = [_inv_dot(m, m) for m in pw]
        tinv = [t_ + _inv_dot(t_, m) for t_, m in zip(tinv, pw)]
    v_st = [stack(v_all[:, s]) for s in slabs]
    return dict(
        state_lhs=[jnp.concatenate([_inv_dot(t_, stack(a0[:, s])), stack(r0[:, s])], 0).astype(BF16)
                   for t_, s in zip(tinv, slabs)],
        u_free=[_inv_dot(t_, _bdot(m, vs)) for t_, m, vs in zip(tinv, a_ak, v_st)],
        a_r=[m.astype(BF16) for m in a_r],
        v_st=[vs.astype(BF16) for vs in v_st],
        decay_rows=[jnp.concatenate([stack(bo[:, s]), stack(ko[:, s])], 0).astype(BF16) for s in slabs],
        e_end=e_end,
        bonus=lanes(_bdot(rows(r_all * k_all * vec_ref[0:1, :]), ones_bd)) * v_all)


def _wkv_apply(c, prep, s_prev, vec_ref, ones_bd, lanes, slabs):
    from_state = [_bdot(lhs, sp, 1, 1) for lhs, sp in zip(prep["state_lhs"], s_prev)]
    uv = [jnp.concatenate([(fs[:2 * c] + uf).astype(BF16), vs], 0)
          for fs, uf, vs in zip(from_state, prep["u_free"], prep["v_st"])]
    o_st = [fs[2 * c:] + _dg(m, x, 1, 0) for fs, m, x in zip(from_state, prep["a_r"], uv)]
    o = [x[:c] + x[c:] for x in o_st]
    new_state = [sp * prep["e_end"][:, s] + _dg(x, rows_e, 0, 0)
                 for s, sp, x, rows_e in zip(slabs, s_prev, uv, prep["decay_rows"])]
    o_rows = jnp.concatenate(o, 0)
    dev = o_rows - _bdot(o_rows, ones_bd) * (1.0 / HEAD_DIM)
    var = _bdot(dev * dev, ones_bd) * (1.0 / HEAD_DIM)
    y = lanes(dev * lax.rsqrt(var + GN_EPS)) * vec_ref[1:2, :] + vec_ref[2:3, :] + prep["bonus"]
    return new_state, y


def _pair_states(s):
    bsz, nh, n, _ = s.shape
    s = s.reshape(bsz, nh // 2, 2, n, n)
    z = jnp.zeros_like(s[:, :, 0])
    top = jnp.concatenate([s[:, :, 0], z], -1)
    bot = jnp.concatenate([z, s[:, :, 1]], -1)
    return jnp.concatenate([top, bot], -2)


def _unpair_states(sp):
    bsz, npair, _, _ = sp.shape
    n = HEAD_DIM
    return jnp.stack([sp[:, :, :n, :n], sp[:, :, n:, n:]], 2).reshape(bsz, 2 * npair, n, n)


def _wkv(proj, s0, wl, bsz, t):
    r, lw, k, v, kk, b = proj
    n, d = r.shape
    c = min(WKV_CHUNK, t)
    rows_per_step = min(WKV_CHUNKS_PER_STEP * c, t)
    nc = t // rows_per_step
    npair = d // LANES
    tok = pl.BlockSpec((rows_per_step, d), lambda bb, i: (bb * nc + i, 0))
    st = pl.BlockSpec((None, npair, LANES, LANES), lambda bb, i: (bb, 0, 0, 0))
    tri = jnp.tril(jnp.ones((c, c), BF16))
    hid = jnp.arange(LANES) // HEAD_DIM
    ones_bd = (hid[:, None] == hid[None, :]).astype(BF16)
    y, s_out = pl.pallas_call(
        functools.partial(_wkv_kernel, c),
        grid=(bsz, nc),
        in_specs=[tok] * 6 + [st,
                              pl.BlockSpec((8, d), lambda bb, i: (0, 0)),
                              pl.BlockSpec((c, c), lambda bb, i: (0, 0)),
                              pl.BlockSpec((LANES, LANES), lambda bb, i: (0, 0))],
        out_specs=[tok, st],
        out_shape=[jax.ShapeDtypeStruct((n, d), F32),
                   jax.ShapeDtypeStruct((bsz, npair, LANES, LANES), F32)],
        scratch_shapes=[pltpu.VMEM((npair, LANES, LANES), F32)],
        compiler_params=_params(("parallel", "arbitrary")),
        name="wkv_scan",
    )(r, lw, k, v, kk, b, _pair_states(s0.astype(F32)), wl["scan_vec"], tri, ones_bd)
    return y, _unpair_states(s_out)


def _router(logits):
    lane_i = lax.broadcasted_iota(jnp.int32, logits.shape, 1)
    lane = lane_i.astype(F32)
    far = 1e9
    is_g = lane_i < N_GROUPS
    gl = jnp.where(is_g, logits, NEG_BIG)
    gmax = jnp.max(gl, -1, keepdims=True)
    gsel = jnp.min(jnp.where(gl == gmax, lane, far), -1, keepdims=True)
    gprob = 1.0 / jnp.sum(jnp.where(is_g, jnp.exp(gl - gmax), 0.0), -1, keepdims=True)
    group_of = lax.shift_right_arithmetic(lane_i - N_GROUPS, 2).astype(F32)
    in_group = (lane_i >= N_GROUPS) & (lane_i < N_GROUPS + N_EXPERTS) & (group_of == gsel)
    el = jnp.where(in_group, logits, NEG_BIG)
    v1 = jnp.max(el, -1, keepdims=True)
    i1 = jnp.min(jnp.where(el == v1, lane, far), -1, keepdims=True)
    el2 = jnp.where(lane == i1, NEG_BIG, el)
    v2 = jnp.max(el2, -1, keepdims=True)
    i2 = jnp.min(jnp.where(el2 == v2, lane, far), -1, keepdims=True)
    e2 = jnp.exp(v2 - v1)
    w1 = gprob / (1.0 + e2)
    w2 = gprob * e2 / (1.0 + e2)
    return jnp.where(lane == i1, w1, 0.0) + jnp.where(lane == i2, w2, 0.0)


def _post_kernel(gated, alpha, *refs):
    if gated:
        x_ref, y_ref, g_ref, mods_ref, wo_ref, ln_ref, rw_ref, rb_ref, x1_ref, comb_ref = refs
        y = y_ref[...] * g_ref[...].astype(F32)
    else:
        x_ref, y_ref, mods_ref, wo_ref, ln_ref, rw_ref, rb_ref, x1_ref, comb_ref = refs
        y = y_ref[...]
    gt1, sh2, sc2 = mods_ref[2:3, :], mods_ref[3:4, :], mods_ref[4:5, :]
    x1 = _layer_norm(alpha * x_ref[...] + gt1 * _bdot(y, wo_ref[...]), ln_ref[0:1, :], ln_ref[1:2, :])
    x1_ref[...] = x1
    h2 = x1 * (1.0 + sc2) + sh2
    comb_ref[...] = _router(_dot3(h2, rw_ref[...]) + rb_ref[...])


def _post_mixer(x2d, y2d, g2d, mods_l, w_o, ln_pack, router_w, router_b, alpha, bsz, t, tm):
    n, d = x2d.shape
    nt = t // tm
    tok = pl.BlockSpec((tm, d), lambda b, i: (b * nt + i, 0))
    full = lambda a: pl.BlockSpec(a.shape, lambda b, i: (0,) * a.ndim)
    gated = g2d is not None
    acts = [x2d, y2d] + ([g2d] if gated else [])
    consts = [w_o, ln_pack, router_w, router_b]
    return pl.pallas_call(
        functools.partial(_post_kernel, gated, alpha),
        grid=(bsz, nt),
        in_specs=[tok] * len(acts) + [pl.BlockSpec((None, 6, d), lambda b, i: (b, 0, 0))]
        + [full(a) for a in consts],
        out_specs=[tok, pl.BlockSpec((tm, LANES), lambda b, i: (b * nt + i, 0))],
        out_shape=[jax.ShapeDtypeStruct((n, d), F32), jax.ShapeDtypeStruct((n, LANES), F32)],
        compiler_params=_params(("parallel", "parallel")),
        name="post_mixer",
    )(*acts, mods_l, *consts)


def _moe_kernel(alpha, x1_ref, comb_ref, mods_ref, ln_ref, wg_ref, wu_ref, wd_ref, x2_ref, h2_ref, acc_ref):
    e = pl.program_id(1)

    @pl.when(e == 0)
    def _():
        sh2, sc2 = mods_ref[3], mods_ref[4]
        h2_ref[...] = (x1_ref[...] * (1.0 + sc2) + sh2).astype(BF16)
        acc_ref[...] = jnp.zeros_like(acc_ref)

    h2 = h2_ref[...]
    per_step, f, d = wd_ref.shape
    lane = lax.broadcasted_iota(jnp.int32, comb_ref.shape, 1)
    comb = comb_ref[...]
    hidden = []
    for i in range(per_step):
        hg = _dg(h2, wg_ref[i].astype(BF16), 1, 0)
        hu = _dg(h2, wu_ref[i].astype(BF16), 1, 0)
        ce = jnp.sum(jnp.where(lane == e * per_step + (i + N_GROUPS), comb, 0.0), -1, keepdims=True)
        hidden.append((_silu(hg) * hu * ce).astype(BF16))
    acc_ref[...] += _dg(jnp.concatenate(hidden, 1), wd_ref[...].reshape(per_step * f, d).astype(BF16), 1, 0)

    @pl.when(e == pl.num_programs(1) - 1)
    def _():
        gt2 = mods_ref[5]
        x2_ref[...] = _layer_norm(alpha * x1_ref[...] + gt2 * acc_ref[...], ln_ref[0:1, :], ln_ref[1:2, :])


def _moe(x1, comb, mods_l, ln_pack, wg, wu, wd, alpha, bsz, t):
    n, d = x1.shape
    ne, _, f = wg.shape
    per_step = MOE_EXPERTS_PER_STEP
    if t >= MOE_TILE:
        tm = MOE_TILE
        mods = mods_l.reshape(bsz, 6, 1, d)
        mods_spec = pl.BlockSpec((None, 6, 1, d), lambda i, e: (i // (t // tm), 0, 0, 0))
    else:
        tm = min(n, MOE_TILE)
        mods = jnp.repeat(mods_l.transpose(1, 0, 2), t, axis=1)
        mods_spec = pl.BlockSpec((6, tm, d), lambda i, e: (0, i, 0))
    return pl.pallas_call(
        functools.partial(_moe_kernel, alpha),
        grid=(n // tm, ne // per_step),
        in_specs=[
            pl.BlockSpec((tm, d), lambda i, e: (i, 0)),
            pl.BlockSpec((tm, LANES), lambda i, e: (i, 0)),
            mods_spec,
            pl.BlockSpec(ln_pack.shape, lambda i, e: (0, 0)),
            pl.BlockSpec((per_step, d, f), lambda i, e: (e, 0, 0)),
            pl.BlockSpec((per_step, d, f), lambda i, e: (e, 0, 0)),
            pl.BlockSpec((per_step, f, d), lambda i, e: (e, 0, 0)),
        ],
        out_specs=pl.BlockSpec((tm, d), lambda i, e: (i, 0)),
        out_shape=jax.ShapeDtypeStruct((n, d), F32),
        scratch_shapes=[pltpu.VMEM((tm, d), BF16), pltpu.VMEM((tm, d), F32)],
        compiler_params=_params(("parallel", "arbitrary")),
        name="hmoe",
    )(x1, comb, mods, ln_pack, wg, wu, wd)


def _fox_bias_placement(d):
    place_q = np.zeros((3 * LANES, d), np.float32)
    place_k = np.zeros((3 * LANES, d), np.float32)
    const_q = np.zeros((1, d), np.float32)
    const_k = np.zeros((1, d), np.float32)
    for h in range(d // HEAD_DIM):
        base = (h // 2) * LANES + (HEAD_DIM if h % 2 == 0 else 0)
        for term in range(3):
            place_q[term * LANES + h, base + term] = 1.0
            place_k[term * LANES + h, base + 3 + term] = -1.0
            const_q[0, base + 3 + term] = 1.0
            const_k[0, base + term] = 1.0
    return (jnp.asarray(place_q, BF16), jnp.asarray(place_k, BF16), jnp.asarray(const_q),
            jnp.asarray(const_k))


def _kv_kernel(x_ref, kvw_ref, fw_ref, fb_ref, tri_ref, pq_ref, pk_ref, cq_ref, ck_ref,
               k_ref, v_ref, kb_ref, vb_ref, lf_ref, fc_ref, qbias_ref, kbias_ref, carry_ref):
    d = x_ref.shape[1]

    @pl.when(pl.program_id(1) == 0)
    def _():
        carry_ref[...] = jnp.zeros_like(carry_ref)

    x = x_ref[...]
    kv = _bdot(x, kvw_ref[...])
    k, v = kv[:, :d], kv[:, d:]
    k_ref[...] = k
    v_ref[...] = v
    kb_ref[...] = k.astype(BF16)
    vb_ref[...] = v.astype(BF16)
    z = _dot3(x, fw_ref[...]) + fb_ref[...]
    lf = -_softplus(-z)
    lf_ref[...] = lf
    fc = _dot_exact_lhs(tri_ref[...], lf) + carry_ref[...]
    fc_ref[...] = fc
    carry_ref[...] = fc[fc.shape[0] - 1:, :]
    terms = jnp.concatenate(_split3(fc * LOG2E), 1)
    qbias_ref[...] = (_dg(terms, pq_ref[...], 1, 0) + cq_ref[...]).astype(BF16)
    kbias_ref[...] = (_dg(terms, pk_ref[...], 1, 0) + ck_ref[...]).astype(BF16)


def _kv_proj(x2d, kv_w, f_w, f_b, bsz, t, tm):
    n, d = x2d.shape
    nt = t // tm
    tok = pl.BlockSpec((tm, d), lambda b, i: (b * nt + i, 0))
    nar = pl.BlockSpec((tm, LANES), lambda b, i: (b * nt + i, 0))
    tri = jnp.tril(jnp.ones((tm, tm), BF16))
    full = lambda a: pl.BlockSpec(a.shape, lambda b, i: (0,) * a.ndim)
    consts = [kv_w, f_w, f_b, tri, *_fox_bias_placement(d)]
    return pl.pallas_call(
        _kv_kernel,
        grid=(bsz, nt),
        in_specs=[tok] + [full(a) for a in consts],
        out_specs=[tok, tok, tok, tok, nar, nar, tok, tok],
        out_shape=[jax.ShapeDtypeStruct((n, d), F32), jax.ShapeDtypeStruct((n, d), F32),
                   jax.ShapeDtypeStruct((n, d), BF16), jax.ShapeDtypeStruct((n, d), BF16),
                   jax.ShapeDtypeStruct((n, LANES), F32), jax.ShapeDtypeStruct((n, LANES), F32),
                   jax.ShapeDtypeStruct((n, d), BF16), jax.ShapeDtypeStruct((n, d), BF16)],
        scratch_shapes=[pltpu.VMEM((1, LANES), F32)],
        compiler_params=_params(("parallel", "arbitrary")),
        name="kv_proj",
    )(x2d, *consts)


def _q_kernel(scale, x_ref, mods_ref, wq_ref, q_ref):
    h = x_ref[...] * (1.0 + mods_ref[1:2, :]) + mods_ref[0:1, :]
    q_ref[...] = (_bdot(h, wq_ref[...]) * scale).astype(BF16)


def _q_proj(x2d, mods_l, wq, scale, bsz, t, tm):
    n, d = x2d.shape
    nt = t // tm
    tok = pl.BlockSpec((tm, d), lambda b, i: (b * nt + i, 0))
    return pl.pallas_call(
        functools.partial(_q_kernel, scale),
        grid=(bsz, nt),
        in_specs=[tok, pl.BlockSpec((None, 6, d), lambda b, i: (b, 0, 0)),
                  pl.BlockSpec(wq.shape, lambda b, i: (0, 0))],
        out_specs=tok,
        out_shape=jax.ShapeDtypeStruct((n, d), BF16),
        compiler_params=_params(("parallel", "parallel")),
        name="q_proj",
    )(x2d, mods_l, wq)


def _fox_prompt_kernel(tq, q_ref, k_ref, v_ref, qb_ref, kb_ref, o_ref):
    t = q_ref.shape[0]
    lane = lax.broadcasted_iota(jnp.int32, (1, LANES), 1)
    rr = lax.broadcasted_iota(jnp.int32, (tq, tq), 0)
    cc = lax.broadcasted_iota(jnp.int32, (tq, tq), 1)
    own = [lane < HEAD_DIM, lane >= HEAD_DIM]
    k_aug = [jnp.where(own[hh], k_ref[...], kb_ref[...]) for hh in range(2)]
    v = v_ref[...]
    for qi in range(t // tq):
        lo, hi = qi * tq, (qi + 1) * tq
        heads = []
        for hh in range(2):
            q = jnp.where(own[hh], q_ref[lo:hi, :], qb_ref[lo:hi, :])
            s_diag = jnp.where(cc <= rr, _dg(q, k_aug[hh][lo:hi], 1, 1), NEG_BIG)
            m = jnp.max(s_diag, -1, keepdims=True)
            if qi > 0:
                s_past = _dg(q, k_aug[hh][:lo], 1, 1)
                m = jnp.maximum(m, jnp.max(s_past, -1, keepdims=True))
            p = jnp.exp2(s_diag - m)
            num = _dg(p.astype(BF16), v[lo:hi], 1, 0)
            den = jnp.sum(p, -1, keepdims=True)
            if qi > 0:
                p = jnp.exp2(s_past - m)
                num = num + _dg(p.astype(BF16), v[:lo], 1, 0)
                den = den + jnp.sum(p, -1, keepdims=True)
            heads.append(num / den)
        o_ref[lo:hi, :] = jnp.where(own[0], heads[0], heads[1])


def _fox_prompt(q, kb, vb, q_bias, k_bias, bsz, t, tq):
    n, d = q.shape
    seq = pl.BlockSpec((t, LANES), lambda b, p: (b, p))
    return pl.pallas_call(
        functools.partial(_fox_prompt_kernel, tq),
        grid=(bsz, d // LANES),
        in_specs=[seq] * 5,
        out_specs=seq,
        out_shape=jax.ShapeDtypeStruct((n, d), F32),
        compiler_params=_params(("parallel", "parallel")),
        name="fox_prompt",
    )(q, kb, vb, q_bias, k_bias)


def _fox_sample_kernel(q_ref, kn_ref, vn_ref, fn_ref, fnt_ref, ck_ref, cv_ref, clf_ref, upper_ref,
                       o_ref, m_ref, l_ref, acc_ref, carry_ref):
    step = pl.program_id(1)
    t, d = q_ref.shape
    nh = d // HEAD_DIM

    @pl.when(step == 0)
    def _():
        m_ref[...] = jnp.full_like(m_ref, NEG_BIG)
        l_ref[...] = jnp.zeros_like(l_ref)
        acc_ref[...] = jnp.zeros_like(acc_ref)
        carry_ref[...] = jnp.zeros_like(carry_ref)

    heads = range(nh)
    slabs = [slice(pr * LANES, (pr + 1) * LANES) for pr in range(nh // 2)]
    lane = lax.broadcasted_iota(jnp.int32, (1, LANES), 1)
    q_own = [jnp.where((lane < HEAD_DIM) == (h % 2 == 0), q_ref[:, slabs[h // 2]], jnp.zeros((), BF16))
             for h in heads]

    def attend(keys, values, bias, mask):
        s = [_dg(q_own[h], keys[h // 2], 1, 1) + bias[h] for h in heads]
        if mask is not None:
            s = [jnp.where(mask, x, NEG_BIG) for x in s]
        m_old = m_ref[...]
        m_new = jnp.maximum(m_old, jnp.stack([jnp.max(x, -1, keepdims=True) for x in s]))
        alpha = jnp.exp(m_old - m_new)
        p = [jnp.exp(x - m_new[h]) for h, x in enumerate(s)]
        l_ref[...] = alpha * l_ref[...] + jnp.stack([jnp.sum(x, -1, keepdims=True) for x in p])
        acc_ref[...] = alpha * acc_ref[...] + jnp.stack(
            [_dg(p[h].astype(BF16), values[h // 2], 1, 0) for h in heads])
        m_ref[...] = m_new

    clf = clf_ref[...]
    suf = _dot_exact_rhs(clf, upper_ref[...]) + carry_ref[...]
    carry_ref[...] = carry_ref[...] + jnp.sum(clf, -1, keepdims=True)
    fn = fn_ref[...]
    attend([ck_ref[:, s].astype(BF16) for s in slabs], [cv_ref[:, s].astype(BF16) for s in slabs],
           [fn[:, h:h + 1] + suf[h:h + 1, :] for h in heads], None)

    @pl.when(step == pl.num_programs(1) - 1)
    def _():
        rr = lax.broadcasted_iota(jnp.int32, (t, t), 0)
        cc = lax.broadcasted_iota(jnp.int32, (t, t), 1)
        fnt = fnt_ref[...]
        attend([kn_ref[:, s] for s in slabs], [vn_ref[:, s] for s in slabs],
               [fn[:, h:h + 1] - fnt[h:h + 1, :] for h in heads], cc <= rr)
        out = acc_ref[...] / l_ref[...]
        o_ref[...] = jnp.concatenate(
            [jnp.where(lane < HEAD_DIM, out[2 * pr], out[2 * pr + 1]) for pr in range(nh // 2)], 1)


def _fox_sample(q, kb, vb, fcum, cache_k, cache_v, cache_logf, bsz, t, tk):
    n, d = q.shape
    nh = d // HEAD_DIM
    plen = cache_k.shape[1]
    nk = plen // tk
    ck = cache_k.reshape(bsz * plen, d)
    cv = cache_v.reshape(bsz * plen, d)
    clf_t = cache_logf.astype(F32).transpose(0, 2, 1)
    fn_t = fcum[:, :nh].reshape(bsz, t, nh).transpose(0, 2, 1)
    upper = (jnp.arange(tk)[:, None] > jnp.arange(tk)[None, :]).astype(BF16)
    tok = pl.BlockSpec((t, d), lambda b, j: (b, 0))
    past = pl.BlockSpec((tk, d), lambda b, j: (b * nk + nk - 1 - j, 0))
    return pl.pallas_call(
        _fox_sample_kernel,
        grid=(bsz, nk),
        in_specs=[tok, tok, tok,
                  pl.BlockSpec((t, LANES), lambda b, j: (b, 0)),
                  pl.BlockSpec((None, nh, t), lambda b, j: (b, 0, 0)),
                  past, past,
                  pl.BlockSpec((None, nh, tk), lambda b, j: (b, 0, nk - 1 - j)),
                  pl.BlockSpec((tk, tk), lambda b, j: (0, 0))],
        out_specs=tok,
        out_shape=jax.ShapeDtypeStruct((n, d), F32),
        scratch_shapes=[pltpu.VMEM((nh, t, 1), F32), pltpu.VMEM((nh, t, 1), F32),
                        pltpu.VMEM((nh, t, LANES), F32), pltpu.VMEM((nh, 1), F32)],
        compiler_params=_params(("parallel", "arbitrary")),
        name="fox_sample",
    )(q, kb, vb, fcum, fn_t, ck, cv, clf_t, upper)


def _tile(t, cap):
    return min(t, cap)


def _trunk(x, mods, shift_in, wkv_in, cache, w):
    bsz, t, d = x.shape
    depth = mods.shape[0]
    n_a = w["n_a"]
    nh = d // HEAD_DIM
    x2d = x.reshape(bsz * t, d)
    shifts, states = [], []
    kv = None
    for l in range(depth):
        mods_l = mods[l]
        if l < n_a:
            wl = w["rwkv"][l]
            proj, last = _rwkv_proj(x2d, shift_in[l], mods_l, wl, bsz, t, _tile(t, 256))
            y, s_new = _wkv(proj[:6], wkv_in[l], wl, bsz, t)
            shifts.append(last)
            states.append(s_new.astype(wkv_in.dtype))
            mixer_out, gate, w_o = y, proj[6], wl["w_o"]
        else:
            if kv is None:
                kv = _kv_proj(x2d, w["kv_w"], w["f_w"], w["f_b"], bsz, t, _tile(t, 512))
            k_sh, v_sh, kb, vb, lf, fcum, q_bias, k_bias = kv
            j = l - n_a
            q_scale = HEAD_DIM ** -0.5 * (LOG2E if cache is None else 1.0)
            q = _q_proj(x2d, mods_l, w["b_wq"][j], q_scale, bsz, t, _tile(t, 512))
            if cache is None:
                mixer_out = _fox_prompt(q, kb, vb, q_bias, k_bias, bsz, t, _tile(t, 512))
            else:
                mixer_out = _fox_sample(q, kb, vb, fcum, cache[0], cache[1], cache[2], bsz, t,
                                        _tile(cache[0].shape[1], 512))
            gate, w_o = None, w["b_wo"][j]
        x1, comb = _post_mixer(x2d, mixer_out, gate, mods_l, w_o, w["ln"][l][0], w["router_w"][l],
                               w["router_b"][l], w["alpha"], bsz, t, _tile(t, 512))
        x2d = _moe(x1, comb, mods_l, w["ln"][l][1], w["exp_wg"][l], w["exp_wu"][l], w["exp_wd"][l],
                   w["alpha"], bsz, t)
    k_sh, v_sh, lf = kv[0], kv[1], kv[4]
    return (x2d.reshape(bsz, t, d), jnp.stack(shifts), jnp.stack(states),
            k_sh.reshape(bsz, t, nh, HEAD_DIM), v_sh.reshape(bsz, t, nh, HEAD_DIM),
            lf[:, :nh].reshape(bsz, t, nh).astype(x.dtype))


def _prepare_weights(ln_g, ln_b, a_mu, a_w_rkv, a_w0, a_w1, a_w2, a_a0, a_a1, a_a2, a_g1, a_g2, a_k_k,
                     a_k_a, a_r_k, a_lnx_g, a_lnx_b, a_w_o, kv_w, f_w, f_b, b_wq, b_wo, rg_w, rg_b,
                     re_w, re_b, exp_wg, exp_wu, exp_wd):
    depth, _, d = ln_g.shape
    n_a = a_mu.shape[0]
    nh = d // HEAD_DIM
    alpha = (2.0 * depth) ** 0.25
    zrow = jnp.zeros((d,), F32)
    head_of = jnp.arange(d) // HEAD_DIM
    head_down = (head_of[:, None] == jnp.arange(LANES)[None, :]).astype(BF16)
    rwkv = []
    for l in range(n_a):
        rwkv.append(dict(
            mu=a_mu[l],
            vec=jnp.stack([a_w0[l], a_a0[l], a_k_k[l], a_k_a[l], zrow, zrow, zrow, zrow]),
            scan_vec=jnp.stack([a_r_k[l].reshape(d), a_lnx_g[l], a_lnx_b[l], zrow, zrow, zrow, zrow, zrow]),
            w_rkv=a_w_rkv[l].astype(BF16), w1=a_w1[l].astype(BF16), w2=a_w2[l].astype(BF16),
            a1=a_a1[l].astype(BF16), a2=a_a2[l].astype(BF16), g1=a_g1[l].astype(BF16),
            g2=a_g2[l].astype(BF16), w_o=a_w_o[l].astype(BF16),
            head_down=head_down, head_up=head_down.T))
    ln = [[jnp.stack([ln_g[l, s], ln_b[l, s], zrow, zrow, zrow, zrow, zrow, zrow])
           for s in range(2)] for l in range(depth)]
    pad = LANES - N_GROUPS - N_EXPERTS
    router_w = [jnp.concatenate([rg_w[l], re_w[l], jnp.zeros((d, pad), F32)], 1) for l in range(depth)]
    router_b = [jnp.concatenate([rg_b[l], re_b[l], jnp.zeros((pad,), F32)])[None, :] for l in range(depth)]
    return dict(
        n_a=n_a, alpha=alpha, rwkv=rwkv, ln=ln, router_w=router_w, router_b=router_b,
        kv_w=kv_w.astype(BF16),
        f_w=jnp.concatenate([f_w, jnp.zeros((d, LANES - nh), F32)], 1),
        f_b=jnp.concatenate([f_b, jnp.zeros((LANES - nh,), F32)])[None, :],
        b_wq=b_wq.astype(BF16), b_wo=b_wo.astype(BF16),
        exp_wg=exp_wg, exp_wu=exp_wu, exp_wd=exp_wd)


def kernel(x_prompt, x_sample, state_shift, state_wkv, cache_k, cache_v, cache_logf, c_prompt, c_sample,
           ada_w, ada_b, ln_g, ln_b, a_mu, a_w_rkv, a_w0, a_w1, a_w2, a_a0, a_a1, a_a2, a_g1, a_g2, a_k_k,
           a_k_a, a_r_k, a_lnx_g, a_lnx_b, a_w_o, kv_w, f_w, f_b, b_wq, b_wo, rg_w, rg_b, re_w, re_b,
           exp_wg, exp_wu, exp_wd):
    w = _prepare_weights(ln_g, ln_b, a_mu, a_w_rkv, a_w0, a_w1, a_w2, a_a0, a_a1, a_a2, a_g1, a_g2,
                         a_k_k, a_k_a, a_r_k, a_lnx_g, a_lnx_b, a_w_o, kv_w, f_w, f_b, b_wq, b_wo,
                         rg_w, rg_b, re_w, re_b, exp_wg, exp_wu, exp_wd)
    bp, _, d = x_prompt.shape
    n_a = a_mu.shape[0]
    nh = d // HEAD_DIM
    mods = _ada_mods(jnp.concatenate([c_prompt, c_sample], 0), ada_w, ada_b)
    zero_shift = jnp.zeros((n_a, bp, d), x_prompt.dtype)
    zero_wkv = jnp.zeros((n_a, bp, nh, HEAD_DIM, HEAD_DIM), x_prompt.dtype)
    outs_p = _trunk(x_prompt, mods[:, :bp], zero_shift, zero_wkv, None, w)
    outs_s = _trunk(x_sample, mods[:, bp:], state_shift, state_wkv, (cache_k, cache_v, cache_logf), w)
    y_p, p_shift, p_wkv, p_k, p_v, p_logf = outs_p
    y_s, s_shift, s_wkv, s_k, s_v, s_logf = outs_s
    return (y_p, y_s, p_shift, p_wkv, p_k, p_v, p_logf, s_shift, s_wkv, s_k, s_v, s_logf)
```

```python
import functools

import jax
import jax.numpy as jnp
import numpy as np
from jax import lax
from jax.experimental import pallas as pl
from jax.experimental.pallas import tpu as pltpu

F32 = jnp.float32
BF16 = jnp.bfloat16

HEAD_DIM = 64
LANES = 128
N_GROUPS = 4
EXP_PER_GROUP = 4
N_EXPERTS = N_GROUPS * EXP_PER_GROUP
LN_EPS = 1e-5
GN_EPS = 64e-5
WKV_CHUNK = 64
WKV_CHUNKS_PER_STEP = 4
MOE_EXPERTS_PER_STEP = 2
MOE_TILE = 1024
LOG2E = 1.4426950408889634
VMEM_LIMIT = 56 * 1024 * 1024
NEG_BIG = -1e30


def _params(sem):
    return pltpu.CompilerParams(dimension_semantics=sem, vmem_limit_bytes=VMEM_LIMIT)


def _dg(a, b, ca, cb):
    return lax.dot_general(a, b, (((ca,), (cb,)), ((), ())), preferred_element_type=F32)


def _bdot(a, b, ca=1, cb=0):
    return _dg(a.astype(BF16), b.astype(BF16), ca, cb)


def _split3(x):
    hi = x.astype(BF16)
    r1 = x - hi.astype(F32)
    mid = r1.astype(BF16)
    lo = (r1 - mid.astype(F32)).astype(BF16)
    return hi, mid, lo


def _dot_exact_rhs(a, b_exact, ca=1, cb=0):
    hi, mid, lo = _split3(a)
    bb = b_exact.astype(BF16)
    return _dg(hi, bb, ca, cb) + _dg(mid, bb, ca, cb) + _dg(lo, bb, ca, cb)


def _dot_exact_lhs(a_exact, b, ca=1, cb=0):
    hi, mid, lo = _split3(b)
    aa = a_exact.astype(BF16)
    return _dg(aa, hi, ca, cb) + _dg(aa, mid, ca, cb) + _dg(aa, lo, ca, cb)


def _dot3(a, b, ca=1, cb=0):
    ah = a.astype(BF16)
    al = (a - ah.astype(F32)).astype(BF16)
    bh = b.astype(BF16)
    bl = (b - bh.astype(F32)).astype(BF16)
    return _dg(ah, bh, ca, cb) + _dg(ah, bl, ca, cb) + _dg(al, bh, ca, cb)


_inv_dot = _bdot


def _layer_norm(z, g, b):
    mu = jnp.mean(z, -1, keepdims=True)
    d = z - mu
    var = jnp.mean(d * d, -1, keepdims=True)
    return d * lax.rsqrt(var + LN_EPS) * g + b


def _softplus(z):
    return jnp.maximum(z, 0.0) + jnp.log(1.0 + jnp.exp(-jnp.abs(z)))


def _sigmoid(z):
    return 1.0 / (1.0 + jnp.exp(-z))


def _silu(z):
    return z * _sigmoid(z)


def _ada_kernel(c_ref, w_ref, b_ref, o_ref):
    o_ref[...] = _dot3(_silu(c_ref[...]), w_ref[...]) + b_ref[...]


def _ada_mods(c_all, ada_w, ada_b):
    depth, d, d6 = ada_w.shape
    bsz = c_all.shape[0]
    tn = d
    out = pl.pallas_call(
        _ada_kernel,
        grid=(depth, d6 // tn),
        in_specs=[
            pl.BlockSpec((bsz, d), lambda l, j: (0, 0)),
            pl.BlockSpec((None, d, tn), lambda l, j: (l, 0, j)),
            pl.BlockSpec((None, 1, tn), lambda l, j: (l, 0, j)),
        ],
        out_specs=pl.BlockSpec((None, bsz, tn), lambda l, j: (l, 0, j)),
        out_shape=jax.ShapeDtypeStruct((depth, bsz, d6), F32),
        compiler_params=_params(("parallel", "parallel")),
        name="ada_mods",
    )(c_all, ada_w, ada_b.reshape(depth, 1, d6))
    return out.reshape(depth, bsz, 6, d)


def _rwkv_proj_kernel(x_ref, xp_ref, shift_ref, mods_ref, mu_ref, vec_ref, wrkv_ref, w1_ref, w2_ref,
                      a1_ref, a2_ref, g1_ref, g2_ref, hd_ref, hu_ref,
                      r_ref, lw_ref, k_ref, v_ref, kk_ref, b_ref, g_ref, last_ref):
    tm = x_ref.shape[0]
    sh1 = mods_ref[0:1, :]
    sc1 = mods_ref[1:2, :]
    h = x_ref[...] * (1.0 + sc1) + sh1
    h_prev_tile = xp_ref[7:8, :] * (1.0 + sc1) + sh1
    prev_row = jnp.where(pl.program_id(1) == 0, shift_ref[...], h_prev_tile)
    row = lax.broadcasted_iota(jnp.int32, (tm, 1), 0)
    xx = jnp.where(row == 0, prev_row, pltpu.roll(h, 1, axis=0)) - h

    def mix(i):
        return h + xx * mu_ref[i:i + 1, :]

    w0, a0 = vec_ref[0:1, :], vec_ref[1:2, :]
    k_k, k_a = vec_ref[2:3, :], vec_ref[3:4, :]
    r = _bdot(mix(0), wrkv_ref[0])
    k = _bdot(mix(1), wrkv_ref[1])
    v = _bdot(mix(2), wrkv_ref[2])
    ww = w0 + _bdot(jnp.tanh(_bdot(mix(3), w1_ref[...])), w2_ref[...])
    w_log = -_softplus(-ww) - 0.5
    a = _sigmoid(a0 + _bdot(_bdot(mix(4), a1_ref[...]), a2_ref[...]))
    g = _bdot(_sigmoid(_bdot(mix(5), g1_ref[...])), g2_ref[...])
    kk = k * k_k
    head_sq = _bdot(kk * kk, hd_ref[...])
    hi = head_sq.astype(BF16)
    lo = (head_sq - hi.astype(F32)).astype(BF16)
    ss = _dg(hi, hu_ref[...], 1, 0) + _dg(lo, hu_ref[...], 1, 0)
    kk = kk * lax.rsqrt(jnp.maximum(ss, 1e-24))
    r_ref[...] = r.astype(BF16)
    lw_ref[...] = -jnp.exp(w_log)
    k_ref[...] = (k * (1.0 + (a - 1.0) * k_a)).astype(BF16)
    v_ref[...] = v.astype(BF16)
    kk_ref[...] = kk.astype(BF16)
    b_ref[...] = (kk * a).astype(BF16)
    g_ref[...] = g.astype(BF16)
    last_ref[...] = h[tm - 1:tm, :]


def _rwkv_proj(x2d, shift_prev, mods_l, wl, bsz, t, tm):
    n, d = x2d.shape
    nt = t // tm
    tok = pl.BlockSpec((tm, d), lambda b, i: (b * nt + i, 0))
    full = lambda a: pl.BlockSpec(a.shape, lambda b, i: (0,) * a.ndim)
    weights = [wl["mu"], wl["vec"], wl["w_rkv"], wl["w1"], wl["w2"], wl["a1"], wl["a2"], wl["g1"],
               wl["g2"], wl["head_down"], wl["head_up"]]
    outs = pl.pallas_call(
        _rwkv_proj_kernel,
        grid=(bsz, nt),
        in_specs=[
            tok,
            pl.BlockSpec((8, d), lambda b, i: (jnp.maximum((b * nt + i) * (tm // 8) - 1, 0), 0)),
            pl.BlockSpec((None, 1, d), lambda b, i: (b, 0, 0)),
            pl.BlockSpec((None, 6, d), lambda b, i: (b, 0, 0)),
        ] + [full(a) for a in weights],
        out_specs=[tok] * 7 + [pl.BlockSpec((None, 1, d), lambda b, i: (b, 0, 0))],
        out_shape=[jax.ShapeDtypeStruct((n, d), F32 if i == 1 else BF16) for i in range(7)]
        + [jax.ShapeDtypeStruct((bsz, 1, d), F32)],
        compiler_params=_params(("parallel", "arbitrary")),
        name="rwkv_proj",
    )(x2d, x2d, shift_prev.reshape(bsz, 1, d), mods_l, *weights)
    return outs[:7], outs[7].reshape(bsz, d)


def _wkv_kernel(c, r_ref, lw_ref, k_ref, v_ref, kk_ref, b_ref, s0_ref, vec_ref, tri_ref, ones_ref,
                y_ref, s_out_ref, state_ref):
    n_chunks = r_ref.shape[0] // c
    n_pairs = r_ref.shape[1] // LANES
    first = pl.program_id(1) == 0

    @pl.when(first)
    def _():
        state_ref[...] = s0_ref[...]

    lane = lax.broadcasted_iota(jnp.int32, (c, LANES), 1)
    low = lane < HEAD_DIM
    ri = lax.broadcasted_iota(jnp.int32, (2 * c, 2 * c), 0)
    ci = lax.broadcasted_iota(jnp.int32, (2 * c, 2 * c), 1)
    same = (ri >= c) == (ci >= c)
    strict = same & (ci < ri)
    incl = same & (ci <= ri)
    eye = (ri == ci).astype(F32)
    tri = tri_ref[...]
    ones_bd = ones_ref[...]

    def stack(z):
        return jnp.concatenate([jnp.where(low, z, 0.0), jnp.where(low, 0.0, z)], axis=0)

    pairs = range(n_pairs)
    slabs = [slice(p * LANES, (p + 1) * LANES) for p in pairs]

    def rows(z):
        return jnp.concatenate([z[:, s] for s in slabs], 0)

    def lanes(z):
        return jnp.concatenate([z[p * c:(p + 1) * c] for p in pairs], 1)

    def prepare(ch):
        rs = slice(ch * c, (ch + 1) * c)
        return _wkv_prepare(c, lw_ref[rs, :], r_ref[rs, :], k_ref[rs, :], v_ref[rs, :], kk_ref[rs, :],
                            b_ref[rs, :], vec_ref, tri, ones_bd, stack, rows, lanes, slabs, strict, incl, eye)

    s_prev = [state_ref[p] for p in pairs]
    ready = prepare(0)
    for ch in range(n_chunks):
        upcoming = prepare(ch + 1) if ch + 1 < n_chunks else None
        s_prev, y_ref[ch * c:(ch + 1) * c, :] = _wkv_apply(c, ready, s_prev, vec_ref, ones_bd, lanes, slabs)
        ready = upcoming
    state_ref[...] = jnp.stack(s_prev)

    @pl.when(pl.program_id(1) == pl.num_programs(1) - 1)
    def _():
        s_out_ref[...] = state_ref[...]


def _wkv_prepare(c, lw_all, r_bf, k_bf, v_bf, kk_bf, b_bf, vec_ref, tri, ones_bd, stack, rows, lanes,
                 slabs, strict, incl, eye):
    cum_all = _dot_exact_lhs(tri, lw_all)
    prev_all = cum_all - lw_all
    mid_all = cum_all[c // 2 - 1:c // 2, :]
    end_all = cum_all[c - 1:c, :]
    e_in = jnp.exp(mid_all - cum_all)
    e_out = jnp.exp(end_all - cum_all)
    e_end = jnp.exp(end_all)
    r_all, k_all, v_all = r_bf.astype(F32), k_bf.astype(F32), v_bf.astype(F32)
    a_all, b_all = -kk_bf.astype(F32), b_bf.astype(F32)
    aq, rq = a_all * jnp.exp(prev_all - mid_all), r_all * jnp.exp(cum_all - mid_all)
    bi, ki = b_all * e_in, k_all * e_in
    a0, r0 = a_all * jnp.exp(prev_all), r_all * jnp.exp(cum_all)
    bo, ko = b_all * e_out, k_all * e_out

    gram = [_bdot(jnp.concatenate([stack(aq[:, s]), stack(rq[:, s])], 0),
                  jnp.concatenate([stack(bi[:, s]), stack(ki[:, s])], 0), 1, 1) for s in slabs]
    a_ab = [jnp.where(strict, g[:2 * c, :2 * c], 0.0) for g in gram]
    a_ak = [jnp.where(strict, g[:2 * c, 2 * c:], 0.0) for g in gram]
    a_r = [jnp.concatenate([jnp.where(incl, g[2 * c:, :2 * c], 0.0),
                            jnp.where(incl, g[2 * c:, 2 * c:], 0.0)], 1) for g in gram]
    tinv = [eye + m for m in a_ab]
    pw = a_ab
    for _ in range(max(c.bit_length() - 2, 0)):
        pw = [_inv_dot(m, m) for m in pw]
        tinv = [t_ + _inv_dot(t_, m) for t_, m in zip(tinv, pw)]
    v_st = [stack(v_all[:, s]) for s in slabs]
    return dict(
        state_lhs=[jnp.concatenate([_inv_dot(t_, stack(a0[:, s])), stack(r0[:, s])], 0).astype(BF16)
                   for t_, s in zip(tinv, slabs)],
        u_free=[_inv_dot(t_, _bdot(m, vs)) for t_, m, vs in zip(tinv, a_ak, v_st)],
        a_r=[m.astype(BF16) for m in a_r],
        v_st=[vs.astype(BF16) for vs in v_st],
        decay_rows=[jnp.concatenate([stack(bo[:, s]), stack(ko[:, s])], 0).astype(BF16) for s in slabs],
        e_end=e_end,
        bonus=lanes(_bdot(rows(r_all * k_all * vec_ref[0:1, :]), ones_bd)) * v_all)


def _wkv_apply(c, prep, s_prev, vec_ref, ones_bd, lanes, slabs):
    from_state = [_bdot(lhs, sp, 1, 1) for lhs, sp in zip(prep["state_lhs"], s_prev)]
    uv = [jnp.concatenate([(fs[:2 * c] + uf).astype(BF16), vs], 0)
          for fs, uf, vs in zip(from_state, prep["u_free"], prep["v_st"])]
    o_st = [fs[2 * c:] + _dg(m, x, 1, 0) for fs, m, x in zip(from_state, prep["a_r"], uv)]
    o = [x[:c] + x[c:] for x in o_st]
    new_state = [sp * prep["e_end"][:, s] + _dg(x, rows_e, 0, 0)
                 for s, sp, x, rows_e in zip(slabs, s_prev, uv, prep["decay_rows"])]
    o_rows = jnp.concatenate(o, 0)
    dev = o_rows - _bdot(o_rows, ones_bd) * (1.0 / HEAD_DIM)
    var = _bdot(dev * dev, ones_bd) * (1.0 / HEAD_DIM)
    y = lanes(dev * lax.rsqrt(var + GN_EPS)) * vec_ref[1:2, :] + vec_ref[2:3, :] + prep["bonus"]
    return new_state, y


def _pair_states(s):
    bsz, nh, n, _ = s.shape
    s = s.reshape(bsz, nh // 2, 2, n, n)
    z = jnp.zeros_like(s[:, :, 0])
    top = jnp.concatenate([s[:, :, 0], z], -1)
    bot = jnp.concatenate([z, s[:, :, 1]], -1)
    return jnp.concatenate([top, bot], -2)


def _unpair_states(sp):
    bsz, npair, _, _ = sp.shape
    n = HEAD_DIM
    return jnp.stack([sp[:, :, :n, :n], sp[:, :, n:, n:]], 2).reshape(bsz, 2 * npair, n, n)


def _wkv(proj, s0, wl, bsz, t):
    r, lw, k, v, kk, b = proj
    n, d = r.shape
    c = min(WKV_CHUNK, t)
    rows_per_step = min(WKV_CHUNKS_PER_STEP * c, t)
    nc = t // rows_per_step
    npair = d // LANES
    tok = pl.BlockSpec((rows_per_step, d), lambda bb, i: (bb * nc + i, 0))
    st = pl.BlockSpec((None, npair, LANES, LANES), lambda bb, i: (bb, 0, 0, 0))
    tri = jnp.tril(jnp.ones((c, c), BF16))
    hid = jnp.arange(LANES) // HEAD_DIM
    ones_bd = (hid[:, None] == hid[None, :]).astype(BF16)
    y, s_out = pl.pallas_call(
        functools.partial(_wkv_kernel, c),
        grid=(bsz, nc),
        in_specs=[tok] * 6 + [st,
                              pl.BlockSpec((8, d), lambda bb, i: (0, 0)),
                              pl.BlockSpec((c, c), lambda bb, i: (0, 0)),
                              pl.BlockSpec((LANES, LANES), lambda bb, i: (0, 0))],
        out_specs=[tok, st],
        out_shape=[jax.ShapeDtypeStruct((n, d), F32),
                   jax.ShapeDtypeStruct((bsz, npair, LANES, LANES), F32)],
        scratch_shapes=[pltpu.VMEM((npair, LANES, LANES), F32)],
        compiler_params=_params(("parallel", "arbitrary")),
        name="wkv_scan",
    )(r, lw, k, v, kk, b, _pair_states(s0.astype(F32)), wl["scan_vec"], tri, ones_bd)
    return y, _unpair_states(s_out)


def _router(logits):
    lane_i = lax.broadcasted_iota(jnp.int32, logits.shape, 1)
    lane = lane_i.astype(F32)
    far = 1e9
    is_g = lane_i < N_GROUPS
    gl = jnp.where(is_g, logits, NEG_BIG)
    gmax = jnp.max(gl, -1, keepdims=True)
    gsel = jnp.min(jnp.where(gl == gmax, lane, far), -1, keepdims=True)
    gprob = 1.0 / jnp.sum(jnp.where(is_g, jnp.exp(gl - gmax), 0.0), -1, keepdims=True)
    group_of = lax.shift_right_arithmetic(lane_i - N_GROUPS, 2).astype(F32)
    in_group = (lane_i >= N_GROUPS) & (lane_i < N_GROUPS + N_EXPERTS) & (group_of == gsel)
    el = jnp.where(in_group, logits, NEG_BIG)
    v1 = jnp.max(el, -1, keepdims=True)
    i1 = jnp.min(jnp.where(el == v1, lane, far), -1, keepdims=True)
    el2 = jnp.where(lane == i1, NEG_BIG, el)
    v2 = jnp.max(el2, -1, keepdims=True)
    i2 = jnp.min(jnp.where(el2 == v2, lane, far), -1, keepdims=True)
    e2 = jnp.exp(v2 - v1)
    w1 = gprob / (1.0 + e2)
    w2 = gprob * e2 / (1.0 + e2)
    return jnp.where(lane == i1, w1, 0.0) + jnp.where(lane == i2, w2, 0.0)


def _post_kernel(gated, alpha, *refs):
    if gated:
        x_ref, y_ref, g_ref, mods_ref, wo_ref, ln_ref, rw_ref, rb_ref, x1_ref, comb_ref = refs
        y = y_ref[...] * g_ref[...].astype(F32)
    else:
        x_ref, y_ref, mods_ref, wo_ref, ln_ref, rw_ref, rb_ref, x1_ref, comb_ref = refs
        y = y_ref[...]
    gt1, sh2, sc2 = mods_ref[2:3, :], mods_ref[3:4, :], mods_ref[4:5, :]
    x1 = _layer_norm(alpha * x_ref[...] + gt1 * _bdot(y, wo_ref[...]), ln_ref[0:1, :], ln_ref[1:2, :])
    x1_ref[...] = x1
    h2 = x1 * (1.0 + sc2) + sh2
    comb_ref[...] = _router(_dot3(h2, rw_ref[...]) + rb_ref[...])


def _post_mixer(x2d, y2d, g2d, mods_l, w_o, ln_pack, router_w, router_b, alpha, bsz, t, tm):
    n, d = x2d.shape
    nt = t // tm
    tok = pl.BlockSpec((tm, d), lambda b, i: (b * nt + i, 0))
    full = lambda a: pl.BlockSpec(a.shape, lambda b, i: (0,) * a.ndim)
    gated = g2d is not None
    acts = [x2d, y2d] + ([g2d] if gated else [])
    consts = [w_o, ln_pack, router_w, router_b]
    return pl.pallas_call(
        functools.partial(_post_kernel, gated, alpha),
        grid=(bsz, nt),
        in_specs=[tok] * len(acts) + [pl.BlockSpec((None, 6, d), lambda b, i: (b, 0, 0))]
        + [full(a) for a in consts],
        out_specs=[tok, pl.BlockSpec((tm, LANES), lambda b, i: (b * nt + i, 0))],
        out_shape=[jax.ShapeDtypeStruct((n, d), F32), jax.ShapeDtypeStruct((n, LANES), F32)],
        compiler_params=_params(("parallel", "parallel")),
        name="post_mixer",
    )(*acts, mods_l, *consts)


def _moe_kernel(alpha, x1_ref, comb_ref, mods_ref, ln_ref, wg_ref, wu_ref, wd_ref, x2_ref, h2_ref, acc_ref):
    e = pl.program_id(1)

    @pl.when(e == 0)
    def _():
        sh2, sc2 = mods_ref[3], mods_ref[4]
        h2_ref[...] = (x1_ref[...] * (1.0 + sc2) + sh2).astype(BF16)
        acc_ref[...] = jnp.zeros_like(acc_ref)

    h2 = h2_ref[...]
    per_step, f, d = wd_ref.shape
    lane = lax.broadcasted_iota(jnp.int32, comb_ref.shape, 1)
    comb = comb_ref[...]
    hidden = []
    for i in range(per_step):
        hg = _dg(h2, wg_ref[i].astype(BF16), 1, 0)
        hu = _dg(h2, wu_ref[i].astype(BF16), 1, 0)
        ce = jnp.sum(jnp.where(lane == e * per_step + (i + N_GROUPS), comb, 0.0), -1, keepdims=True)
        hidden.append((_silu(hg) * hu * ce).astype(BF16))
    acc_ref[...] += _dg(jnp.concatenate(hidden, 1), wd_ref[...].reshape(per_step * f, d).astype(BF16), 1, 0)

    @pl.when(e == pl.num_programs(1) - 1)
    def _():
        gt2 = mods_ref[5]
        x2_ref[...] = _layer_norm(alpha * x1_ref[...] + gt2 * acc_ref[...], ln_ref[0:1, :], ln_ref[1:2, :])


def _moe(x1, comb, mods_l, ln_pack, wg, wu, wd, alpha, bsz, t):
    n, d = x1.shape
    ne, _, f = wg.shape
    per_step = MOE_EXPERTS_PER_STEP
    if t >= MOE_TILE:
        tm = MOE_TILE
        mods = mods_l.reshape(bsz, 6, 1, d)
        mods_spec = pl.BlockSpec((None, 6, 1, d), lambda i, e: (i // (t // tm), 0, 0, 0))
    else:
        tm = min(n, MOE_TILE)
        mods = jnp.repeat(mods_l.transpose(1, 0, 2), t, axis=1)
        mods_spec = pl.BlockSpec((6, tm, d), lambda i, e: (0, i, 0))
    return pl.pallas_call(
        functools.partial(_moe_kernel, alpha),
        grid=(n // tm, ne // per_step),
        in_specs=[
            pl.BlockSpec((tm, d), lambda i, e: (i, 0)),
            pl.BlockSpec((tm, LANES), lambda i, e: (i, 0)),
            mods_spec,
            pl.BlockSpec(ln_pack.shape, lambda i, e: (0, 0)),
            pl.BlockSpec((per_step, d, f), lambda i, e: (e, 0, 0)),
            pl.BlockSpec((per_step, d, f), lambda i, e: (e, 0, 0)),
            pl.BlockSpec((per_step, f, d), lambda i, e: (e, 0, 0)),
        ],
        out_specs=pl.BlockSpec((tm, d), lambda i, e: (i, 0)),
        out_shape=jax.ShapeDtypeStruct((n, d), F32),
        scratch_shapes=[pltpu.VMEM((tm, d), BF16), pltpu.VMEM((tm, d), F32)],
        compiler_params=_params(("parallel", "arbitrary")),
        name="hmoe",
    )(x1, comb, mods, ln_pack, wg, wu, wd)


def _fox_bias_placement(d):
    place_q = np.zeros((3 * LANES, d), np.float32)
    place_k = np.zeros((3 * LANES, d), np.float32)
    const_q = np.zeros((1, d), np.float32)
    const_k = np.zeros((1, d), np.float32)
    for h in range(d // HEAD_DIM):
        base = (h // 2) * LANES + (HEAD_DIM if h % 2 == 0 else 0)
        for term in range(3):
            place_q[term * LANES + h, base + term] = 1.0
            place_k[term * LANES + h, base + 3 + term] = -1.0
            const_q[0, base + 3 + term] = 1.0
            const_k[0, base + term] = 1.0
    return (jnp.asarray(place_q, BF16), jnp.asarray(place_k, BF16), jnp.asarray(const_q),
            jnp.asarray(const_k))


def _kv_kernel(x_ref, kvw_ref, fw_ref, fb_ref, tri_ref, pq_ref, pk_ref, cq_ref, ck_ref,
               k_ref, v_ref, kb_ref, vb_ref, lf_ref, fc_ref, qbias_ref, kbias_ref, carry_ref):
    d = x_ref.shape[1]

    @pl.when(pl.program_id(1) == 0)
    def _():
        carry_ref[...] = jnp.zeros_like(carry_ref)

    x = x_ref[...]
    kv = _bdot(x, kvw_ref[...])
    k, v = kv[:, :d], kv[:, d:]
    tm = x.shape[0]
    nh = d // HEAD_DIM
    for h in range(nh):
        k_ref[pl.ds(h, tm, stride=nh), :] = k[:, h * HEAD_DIM:(h + 1) * HEAD_DIM]
        v_ref[pl.ds(h, tm, stride=nh), :] = v[:, h * HEAD_DIM:(h + 1) * HEAD_DIM]
    kb_ref[...] = k.astype(BF16)
    vb_ref[...] = v.astype(BF16)
    z = _dot3(x, fw_ref[...]) + fb_ref[...]
    lf = -_softplus(-z)
    lf_ref[...] = lf
    fc = _dot_exact_lhs(tri_ref[...], lf) + carry_ref[...]
    fc_ref[...] = fc
    carry_ref[...] = fc[fc.shape[0] - 1:, :]
    terms = jnp.concatenate(_split3(fc * LOG2E), 1)
    qbias_ref[...] = (_dg(terms, pq_ref[...], 1, 0) + cq_ref[...]).astype(BF16)
    kbias_ref[...] = (_dg(terms, pk_ref[...], 1, 0) + ck_ref[...]).astype(BF16)


def _kv_proj(x2d, kv_w, f_w, f_b, bsz, t, tm):
    n, d = x2d.shape
    nt = t // tm
    tok = pl.BlockSpec((tm, d), lambda b, i: (b * nt + i, 0))
    nar = pl.BlockSpec((tm, LANES), lambda b, i: (b * nt + i, 0))
    nh = d // HEAD_DIM
    by_head = pl.BlockSpec((tm * nh, HEAD_DIM), lambda b, i: (b * nt + i, 0))
    tri = jnp.tril(jnp.ones((tm, tm), BF16))
    full = lambda a: pl.BlockSpec(a.shape, lambda b, i: (0,) * a.ndim)
    consts = [kv_w, f_w, f_b, tri, *_fox_bias_placement(d)]
    return pl.pallas_call(
        _kv_kernel,
        grid=(bsz, nt),
        in_specs=[tok] + [full(a) for a in consts],
        out_specs=[by_head, by_head, tok, tok, nar, nar, tok, tok],
        out_shape=[jax.ShapeDtypeStruct((n * nh, HEAD_DIM), F32), jax.ShapeDtypeStruct((n * nh, HEAD_DIM), F32),
                   jax.ShapeDtypeStruct((n, d), BF16), jax.ShapeDtypeStruct((n, d), BF16),
                   jax.ShapeDtypeStruct((n, LANES), F32), jax.ShapeDtypeStruct((n, LANES), F32),
                   jax.ShapeDtypeStruct((n, d), BF16), jax.ShapeDtypeStruct((n, d), BF16)],
        scratch_shapes=[pltpu.VMEM((1, LANES), F32)],
        compiler_params=_params(("parallel", "arbitrary")),
        name="kv_proj",
    )(x2d, *consts)


def _q_kernel(scale, x_ref, mods_ref, wq_ref, q_ref):
    h = x_ref[...] * (1.0 + mods_ref[1:2, :]) + mods_ref[0:1, :]
    q_ref[...] = (_bdot(h, wq_ref[...]) * scale).astype(BF16)


def _q_proj(x2d, mods_l, wq, scale, bsz, t, tm):
    n, d = x2d.shape
    nt = t // tm
    tok = pl.BlockSpec((tm, d), lambda b, i: (b * nt + i, 0))
    return pl.pallas_call(
        functools.partial(_q_kernel, scale),
        grid=(bsz, nt),
        in_specs=[tok, pl.BlockSpec((None, 6, d), lambda b, i: (b, 0, 0)),
                  pl.BlockSpec(wq.shape, lambda b, i: (0, 0))],
        out_specs=tok,
        out_shape=jax.ShapeDtypeStruct((n, d), BF16),
        compiler_params=_params(("parallel", "parallel")),
        name="q_proj",
    )(x2d, mods_l, wq)


def _fox_prompt_kernel(tq, q_ref, k_ref, v_ref, qb_ref, kb_ref, o_ref):
    t = q_ref.shape[0]
    lane = lax.broadcasted_iota(jnp.int32, (1, LANES), 1)
    rr = lax.broadcasted_iota(jnp.int32, (tq, tq), 0)
    cc = lax.broadcasted_iota(jnp.int32, (tq, tq), 1)
    own = [lane < HEAD_DIM, lane >= HEAD_DIM]
    k_aug = [jnp.where(own[hh], k_ref[...], kb_ref[...]) for hh in range(2)]
    v = v_ref[...]
    for qi in range(t // tq):
        lo, hi = qi * tq, (qi + 1) * tq
        heads = []
        for hh in range(2):
            q = jnp.where(own[hh], q_ref[lo:hi, :], qb_ref[lo:hi, :])
            s_diag = jnp.where(cc <= rr, _dg(q, k_aug[hh][lo:hi], 1, 1), NEG_BIG)
            m = jnp.max(s_diag, -1, keepdims=True)
            if qi > 0:
                s_past = _dg(q, k_aug[hh][:lo], 1, 1)
                m = jnp.maximum(m, jnp.max(s_past, -1, keepdims=True))
            p = jnp.exp2(s_diag - m)
            num = _dg(p.astype(BF16), v[lo:hi], 1, 0)
            den = jnp.sum(p, -1, keepdims=True)
            if qi > 0:
                p = jnp.exp2(s_past - m)
                num = num + _dg(p.astype(BF16), v[:lo], 1, 0)
                den = den + jnp.sum(p, -1, keepdims=True)
            heads.append(num / den)
        o_ref[lo:hi, :] = jnp.where(own[0], heads[0], heads[1])


def _fox_prompt(q, kb, vb, q_bias, k_bias, bsz, t, tq):
    n, d = q.shape
    seq = pl.BlockSpec((t, LANES), lambda b, p: (b, p))
    return pl.pallas_call(
        functools.partial(_fox_prompt_kernel, tq),
        grid=(bsz, d // LANES),
        in_specs=[seq] * 5,
        out_specs=seq,
        out_shape=jax.ShapeDtypeStruct((n, d), F32),
        compiler_params=_params(("parallel", "parallel")),
        name="fox_prompt",
    )(q, kb, vb, q_bias, k_bias)


def _cache_pack_kernel(k_ref, v_ref, ko_ref, vo_ref):
    tk, d = ko_ref.shape
    nh = d // HEAD_DIM
    for src, dst in ((k_ref, ko_ref), (v_ref, vo_ref)):
        dst[...] = jnp.concatenate(
            [src[pl.ds(h, tk, stride=nh), :].astype(BF16) for h in range(nh)], 1)


def _cache_pack(cache_k, cache_v, tk):
    bsz, plen, nh, hd = cache_k.shape
    rows = bsz * plen
    by_head = pl.BlockSpec((tk * nh, hd), lambda i: (i, 0))
    packed = pl.BlockSpec((tk, nh * hd), lambda i: (i, 0))
    return pl.pallas_call(
        _cache_pack_kernel,
        grid=(rows // tk,),
        in_specs=[by_head, by_head],
        out_specs=[packed, packed],
        out_shape=[jax.ShapeDtypeStruct((rows, nh * hd), BF16)] * 2,
        compiler_params=_params(("parallel",)),
        name="cache_pack",
    )(cache_k.reshape(rows * nh, hd), cache_v.reshape(rows * nh, hd))


def _fox_sample_kernel(q_ref, kn_ref, vn_ref, fn_ref, fnt_ref, ck_ref, cv_ref, clf_ref, upper_ref,
                       o_ref, m_ref, l_ref, acc_ref, carry_ref):
    step = pl.program_id(1)
    t, d = q_ref.shape
    nh = d // HEAD_DIM

    @pl.when(step == 0)
    def _():
        m_ref[...] = jnp.full_like(m_ref, NEG_BIG)
        l_ref[...] = jnp.zeros_like(l_ref)
        acc_ref[...] = jnp.zeros_like(acc_ref)
        carry_ref[...] = jnp.zeros_like(carry_ref)

    heads = range(nh)
    slabs = [slice(pr * LANES, (pr + 1) * LANES) for pr in range(nh // 2)]
    lane = lax.broadcasted_iota(jnp.int32, (1, LANES), 1)
    q_own = [jnp.where((lane < HEAD_DIM) == (h % 2 == 0), q_ref[:, slabs[h // 2]], jnp.zeros((), BF16))
             for h in heads]

    def attend(keys, values, bias, mask):
        s = [_dg(q_own[h], keys[h // 2], 1, 1) + bias[h] for h in heads]
        if mask is not None:
            s = [jnp.where(mask, x, NEG_BIG) for x in s]
        m_old = m_ref[...]
        m_new = jnp.maximum(m_old, jnp.stack([jnp.max(x, -1, keepdims=True) for x in s]))
        alpha = jnp.exp(m_old - m_new)
        p = [jnp.exp(x - m_new[h]) for h, x in enumerate(s)]
        l_ref[...] = alpha * l_ref[...] + jnp.stack([jnp.sum(x, -1, keepdims=True) for x in p])
        acc_ref[...] = alpha * acc_ref[...] + jnp.stack(
            [_dg(p[h].astype(BF16), values[h // 2], 1, 0) for h in heads])
        m_ref[...] = m_new

    clf = clf_ref[...]
    suf = _dot_exact_rhs(clf, upper_ref[...]) + carry_ref[...]
    carry_ref[...] = carry_ref[...] + jnp.sum(clf, -1, keepdims=True)
    fn = fn_ref[...]
    attend([ck_ref[:, s] for s in slabs], [cv_ref[:, s] for s in slabs],
           [fn[:, h:h + 1] + suf[h:h + 1, :] for h in heads], None)

    @pl.when(step == pl.num_programs(1) - 1)
    def _():
        rr = lax.broadcasted_iota(jnp.int32, (t, t), 0)
        cc = lax.broadcasted_iota(jnp.int32, (t, t), 1)
        fnt = fnt_ref[...]
        attend([kn_ref[:, s] for s in slabs], [vn_ref[:, s] for s in slabs],
               [fn[:, h:h + 1] - fnt[h:h + 1, :] for h in heads], cc <= rr)
        out = acc_ref[...] / l_ref[...]
        o_ref[...] = jnp.concatenate(
            [jnp.where(lane < HEAD_DIM, out[2 * pr], out[2 * pr + 1]) for pr in range(nh // 2)], 1)


def _fox_sample(q, kb, vb, fcum, ck, cv, cache_logf, bsz, t, tk):
    n, d = q.shape
    nh = d // HEAD_DIM
    plen = cache_logf.shape[1]
    nk = plen // tk
    clf_t = cache_logf.astype(F32).transpose(0, 2, 1)
    fn_t = fcum[:, :nh].reshape(bsz, t, nh).transpose(0, 2, 1)
    upper = (jnp.arange(tk)[:, None] > jnp.arange(tk)[None, :]).astype(BF16)
    tok = pl.BlockSpec((t, d), lambda b, j: (b, 0))
    past = pl.BlockSpec((tk, d), lambda b, j: (b * nk + nk - 1 - j, 0))
    return pl.pallas_call(
        _fox_sample_kernel,
        grid=(bsz, nk),
        in_specs=[tok, tok, tok,
                  pl.BlockSpec((t, LANES), lambda b, j: (b, 0)),
                  pl.BlockSpec((None, nh, t), lambda b, j: (b, 0, 0)),
                  past, past,
                  pl.BlockSpec((None, nh, tk), lambda b, j: (b, 0, nk - 1 - j)),
                  pl.BlockSpec((tk, tk), lambda b, j: (0, 0))],
        out_specs=tok,
        out_shape=jax.ShapeDtypeStruct((n, d), F32),
        scratch_shapes=[pltpu.VMEM((nh, t, 1), F32), pltpu.VMEM((nh, t, 1), F32),
                        pltpu.VMEM((nh, t, LANES), F32), pltpu.VMEM((nh, 1), F32)],
        compiler_params=_params(("parallel", "arbitrary")),
        name="fox_sample",
    )(q, kb, vb, fcum, fn_t, ck, cv, clf_t, upper)


def _tile(t, cap):
    return min(t, cap)


def _trunk(x, mods, shift_in, wkv_in, cache, w):
    bsz, t, d = x.shape
    depth = mods.shape[0]
    n_a = w["n_a"]
    nh = d // HEAD_DIM
    x2d = x.reshape(bsz * t, d)
    shifts, states = [], []
    kv = None
    if cache is not None:
        past_tile = _tile(cache[0].shape[1], 512)
        packed = _cache_pack(cache[0], cache[1], past_tile)
    for l in range(depth):
        mods_l = mods[l]
        if l < n_a:
            wl = w["rwkv"][l]
            proj, last = _rwkv_proj(x2d, shift_in[l], mods_l, wl, bsz, t, _tile(t, 256))
            y, s_new = _wkv(proj[:6], wkv_in[l], wl, bsz, t)
            shifts.append(last)
            states.append(s_new.astype(wkv_in.dtype))
            mixer_out, gate, w_o = y, proj[6], wl["w_o"]
        else:
            if kv is None:
                kv = _kv_proj(x2d, w["kv_w"], w["f_w"], w["f_b"], bsz, t, _tile(t, 256))
            k_sh, v_sh, kb, vb, lf, fcum, q_bias, k_bias = kv
            j = l - n_a
            q_scale = HEAD_DIM ** -0.5 * (LOG2E if cache is None else 1.0)
            q = _q_proj(x2d, mods_l, w["b_wq"][j], q_scale, bsz, t, _tile(t, 512))
            if cache is None:
                mixer_out = _fox_prompt(q, kb, vb, q_bias, k_bias, bsz, t, _tile(t, 512))
            else:
                mixer_out = _fox_sample(q, kb, vb, fcum, packed[0], packed[1], cache[2], bsz, t, past_tile)
            gate, w_o = None, w["b_wo"][j]
        x1, comb = _post_mixer(x2d, mixer_out, gate, mods_l, w_o, w["ln"][l][0], w["router_w"][l],
                               w["router_b"][l], w["alpha"], bsz, t, _tile(t, 512))
        x2d = _moe(x1, comb, mods_l, w["ln"][l][1], w["exp_wg"][l], w["exp_wu"][l], w["exp_wd"][l],
                   w["alpha"], bsz, t)
    k_sh, v_sh, lf = kv[0], kv[1], kv[4]
    return (x2d.reshape(bsz, t, d), jnp.stack(shifts), jnp.stack(states),
            k_sh.reshape(bsz, t, nh, HEAD_DIM), v_sh.reshape(bsz, t, nh, HEAD_DIM),
            lf[:, :nh].reshape(bsz, t, nh).astype(x.dtype))


def _prepare_weights(ln_g, ln_b, a_mu, a_w_rkv, a_w0, a_w1, a_w2, a_a0, a_a1, a_a2, a_g1, a_g2, a_k_k,
                     a_k_a, a_r_k, a_lnx_g, a_lnx_b, a_w_o, kv_w, f_w, f_b, b_wq, b_wo, rg_w, rg_b,
                     re_w, re_b, exp_wg, exp_wu, exp_wd):
    depth, _, d = ln_g.shape
    n_a = a_mu.shape[0]
    nh = d // HEAD_DIM
    alpha = (2.0 * depth) ** 0.25
    zrow = jnp.zeros((d,), F32)
    head_of = jnp.arange(d) // HEAD_DIM
    head_down = (head_of[:, None] == jnp.arange(LANES)[None, :]).astype(BF16)
    rwkv = []
    for l in range(n_a):
        rwkv.append(dict(
            mu=a_mu[l],
            vec=jnp.stack([a_w0[l], a_a0[l], a_k_k[l], a_k_a[l], zrow, zrow, zrow, zrow]),
            scan_vec=jnp.stack([a_r_k[l].reshape(d), a_lnx_g[l], a_lnx_b[l], zrow, zrow, zrow, zrow, zrow]),
            w_rkv=a_w_rkv[l].astype(BF16), w1=a_w1[l].astype(BF16), w2=a_w2[l].astype(BF16),
            a1=a_a1[l].astype(BF16), a2=a_a2[l].astype(BF16), g1=a_g1[l].astype(BF16),
            g2=a_g2[l].astype(BF16), w_o=a_w_o[l].astype(BF16),
            head_down=head_down, head_up=head_down.T))
    ln = [[jnp.stack([ln_g[l, s], ln_b[l, s], zrow, zrow, zrow, zrow, zrow, zrow])
           for s in range(2)] for l in range(depth)]
    pad = LANES - N_GROUPS - N_EXPERTS
    router_w = [jnp.concatenate([rg_w[l], re_w[l], jnp.zeros((d, pad), F32)], 1) for l in range(depth)]
    router_b = [jnp.concatenate([rg_b[l], re_b[l], jnp.zeros((pad,), F32)])[None, :] for l in range(depth)]
    return dict(
        n_a=n_a, alpha=alpha, rwkv=rwkv, ln=ln, router_w=router_w, router_b=router_b,
        kv_w=kv_w.astype(BF16),
        f_w=jnp.concatenate([f_w, jnp.zeros((d, LANES - nh), F32)], 1),
        f_b=jnp.concatenate([f_b, jnp.zeros((LANES - nh,), F32)])[None, :],
        b_wq=b_wq.astype(BF16), b_wo=b_wo.astype(BF16),
        exp_wg=exp_wg, exp_wu=exp_wu, exp_wd=exp_wd)


def kernel(x_prompt, x_sample, state_shift, state_wkv, cache_k, cache_v, cache_logf, c_prompt, c_sample,
           ada_w, ada_b, ln_g, ln_b, a_mu, a_w_rkv, a_w0, a_w1, a_w2, a_a0, a_a1, a_a2, a_g1, a_g2, a_k_k,
           a_k_a, a_r_k, a_lnx_g, a_lnx_b, a_w_o, kv_w, f_w, f_b, b_wq, b_wo, rg_w, rg_b, re_w, re_b,
           exp_wg, exp_wu, exp_wd):
    w = _prepare_weights(ln_g, ln_b, a_mu, a_w_rkv, a_w0, a_w1, a_w2, a_a0, a_a1, a_a2, a_g1, a_g2,
                         a_k_k, a_k_a, a_r_k, a_lnx_g, a_lnx_b, a_w_o, kv_w, f_w, f_b, b_wq, b_wo,
                         rg_w, rg_b, re_w, re_b, exp_wg, exp_wu, exp_wd)
    bp, _, d = x_prompt.shape
    n_a = a_mu.shape[0]
    nh = d // HEAD_DIM
    mods = _ada_mods(jnp.concatenate([c_prompt, c_sample], 0), ada_w, ada_b)
    zero_shift = jnp.zeros((n_a, bp, d), x_prompt.dtype)
    zero_wkv = jnp.zeros((n_a, bp, nh, HEAD_DIM, HEAD_DIM), x_prompt.dtype)
    outs_p = _trunk(x_prompt, mods[:, :bp], zero_shift, zero_wkv, None, w)
    outs_s = _trunk(x_sample, mods[:, bp:], state_shift, state_wkv, (cache_k, cache_v, cache_logf), w)
    y_p, p_shift, p_wkv, p_k, p_v, p_logf = outs_p
    y_s, s_shift, s_wkv, s_k, s_v, s_logf = outs_s
    return (y_p, y_s, p_shift, p_wkv, p_k, p_v, p_logf, s_shift, s_wkv, s_k, s_v, s_logf)
```

```python
import functools

import jax
import jax.numpy as jnp
import numpy as np
from jax import lax
from jax.experimental import pallas as pl
from jax.experimental.pallas import tpu as pltpu

F32 = jnp.float32
BF16 = jnp.bfloat16

HEAD_DIM = 64
LANES = 128
N_GROUPS = 4
EXP_PER_GROUP = 4
N_EXPERTS = N_GROUPS * EXP_PER_GROUP
LN_EPS = 1e-5
GN_EPS = 64e-5
WKV_CHUNK = 64
WKV_CHUNKS_PER_STEP = 4
MOE_EXPERTS_PER_STEP = 2
MOE_TILE = 1024
LOG2E = 1.4426950408889634
VMEM_LIMIT = 56 * 1024 * 1024
NEG_BIG = -1e30


def _params(sem):
    return pltpu.CompilerParams(dimension_semantics=sem, vmem_limit_bytes=VMEM_LIMIT)


def _dg(a, b, ca, cb):
    return lax.dot_general(a, b, (((ca,), (cb,)), ((), ())), preferred_element_type=F32)


def _bdot(a, b, ca=1, cb=0):
    return _dg(a.astype(BF16), b.astype(BF16), ca, cb)


def _split3(x):
    hi = x.astype(BF16)
    r1 = x - hi.astype(F32)
    mid = r1.astype(BF16)
    lo = (r1 - mid.astype(F32)).astype(BF16)
    return hi, mid, lo


def _dot_exact_rhs(a, b_exact, ca=1, cb=0):
    hi, mid, lo = _split3(a)
    bb = b_exact.astype(BF16)
    return _dg(hi, bb, ca, cb) + _dg(mid, bb, ca, cb) + _dg(lo, bb, ca, cb)


def _dot_exact_lhs(a_exact, b, ca=1, cb=0):
    hi, mid, lo = _split3(b)
    aa = a_exact.astype(BF16)
    return _dg(aa, hi, ca, cb) + _dg(aa, mid, ca, cb) + _dg(aa, lo, ca, cb)


def _dot3(a, b, ca=1, cb=0):
    ah = a.astype(BF16)
    al = (a - ah.astype(F32)).astype(BF16)
    bh = b.astype(BF16)
    bl = (b - bh.astype(F32)).astype(BF16)
    return _dg(ah, bh, ca, cb) + _dg(ah, bl, ca, cb) + _dg(al, bh, ca, cb)


_inv_dot = _bdot


def _layer_norm(z, g, b):
    mu = jnp.mean(z, -1, keepdims=True)
    d = z - mu
    var = jnp.mean(d * d, -1, keepdims=True)
    return d * lax.rsqrt(var + LN_EPS) * g + b


def _softplus(z):
    return jnp.maximum(z, 0.0) + jnp.log(1.0 + jnp.exp(-jnp.abs(z)))


def _sigmoid(z):
    return 1.0 / (1.0 + jnp.exp(-z))


def _silu(z):
    return z * _sigmoid(z)


def _ada_kernel(c_ref, w_ref, b_ref, o_ref):
    o_ref[...] = _dot3(_silu(c_ref[...]), w_ref[...]) + b_ref[...]


def _ada_mods(c_all, ada_w, ada_b):
    depth, d, d6 = ada_w.shape
    bsz = c_all.shape[0]
    tn = d
    out = pl.pallas_call(
        _ada_kernel,
        grid=(depth, d6 // tn),
        in_specs=[
            pl.BlockSpec((bsz, d), lambda l, j: (0, 0)),
            pl.BlockSpec((None, d, tn), lambda l, j: (l, 0, j)),
            pl.BlockSpec((None, 1, tn), lambda l, j: (l, 0, j)),
        ],
        out_specs=pl.BlockSpec((None, bsz, tn), lambda l, j: (l, 0, j)),
        out_shape=jax.ShapeDtypeStruct((depth, bsz, d6), F32),
        compiler_params=_params(("parallel", "parallel")),
        name="ada_mods",
    )(c_all, ada_w, ada_b.reshape(depth, 1, d6))
    return out.reshape(depth, bsz, 6, d)


def _rwkv_proj_kernel(x_ref, xp_ref, shift_ref, mods_ref, mu_ref, vec_ref, wrkv_ref, w1_ref, w2_ref,
                      a1_ref, a2_ref, g1_ref, g2_ref, hd_ref, hu_ref,
                      r_ref, lw_ref, k_ref, v_ref, kk_ref, b_ref, g_ref, last_ref):
    tm = x_ref.shape[0]
    sh1 = mods_ref[0:1, :]
    sc1 = mods_ref[1:2, :]
    h = x_ref[...] * (1.0 + sc1) + sh1
    h_prev_tile = xp_ref[7:8, :] * (1.0 + sc1) + sh1
    prev_row = jnp.where(pl.program_id(1) == 0, shift_ref[...], h_prev_tile)
    row = lax.broadcasted_iota(jnp.int32, (tm, 1), 0)
    xx = jnp.where(row == 0, prev_row, pltpu.roll(h, 1, axis=0)) - h

    def mix(i):
        return h + xx * mu_ref[i:i + 1, :]

    w0, a0 = vec_ref[0:1, :], vec_ref[1:2, :]
    k_k, k_a = vec_ref[2:3, :], vec_ref[3:4, :]
    r = _bdot(mix(0), wrkv_ref[0])
    k = _bdot(mix(1), wrkv_ref[1])
    v = _bdot(mix(2), wrkv_ref[2])
    ww = w0 + _bdot(jnp.tanh(_bdot(mix(3), w1_ref[...])), w2_ref[...])
    w_log = -_softplus(-ww) - 0.5
    a = _sigmoid(a0 + _bdot(_bdot(mix(4), a1_ref[...]), a2_ref[...]))
    g = _bdot(_sigmoid(_bdot(mix(5), g1_ref[...])), g2_ref[...])
    kk = k * k_k
    head_sq = _bdot(kk * kk, hd_ref[...])
    hi = head_sq.astype(BF16)
    lo = (head_sq - hi.astype(F32)).astype(BF16)
    ss = _dg(hi, hu_ref[...], 1, 0) + _dg(lo, hu_ref[...], 1, 0)
    kk = kk * lax.rsqrt(jnp.maximum(ss, 1e-24))
    r_ref[...] = r.astype(BF16)
    lw_ref[...] = -jnp.exp(w_log)
    k_ref[...] = (k * (1.0 + (a - 1.0) * k_a)).astype(BF16)
    v_ref[...] = v.astype(BF16)
    kk_ref[...] = kk.astype(BF16)
    b_ref[...] = (kk * a).astype(BF16)
    g_ref[...] = g.astype(BF16)
    last_ref[...] = h[tm - 1:tm, :]


def _rwkv_proj(x2d, shift_prev, mods_l, wl, bsz, t, tm):
    n, d = x2d.shape
    nt = t // tm
    tok = pl.BlockSpec((tm, d), lambda b, i: (b * nt + i, 0))
    full = lambda a: pl.BlockSpec(a.shape, lambda b, i: (0,) * a.ndim)
    weights = [wl["mu"], wl["vec"], wl["w_rkv"], wl["w1"], wl["w2"], wl["a1"], wl["a2"], wl["g1"],
               wl["g2"], wl["head_down"], wl["head_up"]]
    outs = pl.pallas_call(
        _rwkv_proj_kernel,
        grid=(bsz, nt),
        in_specs=[
            tok,
            pl.BlockSpec((8, d), lambda b, i: (jnp.maximum((b * nt + i) * (tm // 8) - 1, 0), 0)),
            pl.BlockSpec((None, 1, d), lambda b, i: (b, 0, 0)),
            pl.BlockSpec((None, 6, d), lambda b, i: (b, 0, 0)),
        ] + [full(a) for a in weights],
        out_specs=[tok] * 7 + [pl.BlockSpec((None, 1, d), lambda b, i: (b, 0, 0))],
        out_shape=[jax.ShapeDtypeStruct((n, d), F32 if i == 1 else BF16) for i in range(7)]
        + [jax.ShapeDtypeStruct((bsz, 1, d), F32)],
        compiler_params=_params(("parallel", "arbitrary")),
        name="rwkv_proj",
    )(x2d, x2d, shift_prev.reshape(bsz, 1, d), mods_l, *weights)
    return outs[:7], outs[7].reshape(bsz, d)


def _wkv_kernel(c, r_ref, lw_ref, k_ref, v_ref, kk_ref, b_ref, s0_ref, vec_ref, tri_ref, ones_ref,
                y_ref, s_out_ref, state_ref):
    n_chunks = r_ref.shape[0] // c
    n_pairs = r_ref.shape[1] // LANES
    first = pl.program_id(1) == 0

    @pl.when(first)
    def _():
        state_ref[...] = s0_ref[...]

    lane = lax.broadcasted_iota(jnp.int32, (c, LANES), 1)
    low = lane < HEAD_DIM
    ri = lax.broadcasted_iota(jnp.int32, (2 * c, 2 * c), 0)
    ci = lax.broadcasted_iota(jnp.int32, (2 * c, 2 * c), 1)
    same = (ri >= c) == (ci >= c)
    strict = same & (ci < ri)
    incl = same & (ci <= ri)
    eye = (ri == ci).astype(F32)
    tri = tri_ref[...]
    ones_bd = ones_ref[...]

    def stack(z):
        return jnp.concatenate([jnp.where(low, z, 0.0), jnp.where(low, 0.0, z)], axis=0)

    pairs = range(n_pairs)
    slabs = [slice(p * LANES, (p + 1) * LANES) for p in pairs]

    def rows(z):
        return jnp.concatenate([z[:, s] for s in slabs], 0)

    def lanes(z):
        return jnp.concatenate([z[p * c:(p + 1) * c] for p in pairs], 1)

    def prepare(ch):
        rs = slice(ch * c, (ch + 1) * c)
        return _wkv_prepare(c, lw_ref[rs, :], r_ref[rs, :], k_ref[rs, :], v_ref[rs, :], kk_ref[rs, :],
                            b_ref[rs, :], vec_ref, tri, ones_bd, stack, rows, lanes, slabs, strict, incl, eye)

    s_prev = [state_ref[p] for p in pairs]
    ready = prepare(0)
    for ch in range(n_chunks):
        upcoming = prepare(ch + 1) if ch + 1 < n_chunks else None
        s_prev, y_ref[ch * c:(ch + 1) * c, :] = _wkv_apply(c, ready, s_prev, vec_ref, ones_bd, lanes, slabs)
        ready = upcoming
    state_ref[...] = jnp.stack(s_prev)

    @pl.when(pl.program_id(1) == pl.num_programs(1) - 1)
    def _():
        s_out_ref[...] = state_ref[...]


def _wkv_prepare(c, lw_all, r_bf, k_bf, v_bf, kk_bf, b_bf, vec_ref, tri, ones_bd, stack, rows, lanes,
                 slabs, strict, incl, eye):
    cum_all = _dot_exact_lhs(tri, lw_all)
    prev_all = cum_all - lw_all
    mid_all = cum_all[c // 2 - 1:c // 2, :]
    end_all = cum_all[c - 1:c, :]
    e_in = jnp.exp(mid_all - cum_all)
    e_out = jnp.exp(end_all - cum_all)
    e_end = jnp.exp(end_all)
    r_all, k_all, v_all = r_bf.astype(F32), k_bf.astype(F32), v_bf.astype(F32)
    a_all, b_all = -kk_bf.astype(F32), b_bf.astype(F32)
    aq, rq = a_all * jnp.exp(prev_all - mid_all), r_all * jnp.exp(cum_all - mid_all)
    bi, ki = b_all * e_in, k_all * e_in
    a0, r0 = a_all * jnp.exp(prev_all), r_all * jnp.exp(cum_all)
    bo, ko = b_all * e_out, k_all * e_out

    gram = [_bdot(jnp.concatenate([stack(aq[:, s]), stack(rq[:, s])], 0),
                  jnp.concatenate([stack(bi[:, s]), stack(ki[:, s])], 0), 1, 1) for s in slabs]
    a_ab = [jnp.where(strict, g[:2 * c, :2 * c], 0.0) for g in gram]
    a_ak = [jnp.where(strict, g[:2 * c, 2 * c:], 0.0) for g in gram]
    a_r = [jnp.concatenate([jnp.where(incl, g[2 * c:, :2 * c], 0.0),
                            jnp.where(incl, g[2 * c:, 2 * c:], 0.0)], 1) for g in gram]
    tinv = [eye + m for m in a_ab]
    pw = a_ab
    for _ in range(max(c.bit_length() - 2, 0)):
        pw = [_inv_dot(m, m) for m in pw]
        tinv = [t_ + _inv_dot(t_, m) for t_, m in zip(tinv, pw)]
    v_st = [stack(v_all[:, s]) for s in slabs]
    return dict(
        state_lhs=[jnp.concatenate([_inv_dot(t_, stack(a0[:, s])), stack(r0[:, s])], 0).astype(BF16)
                   for t_, s in zip(tinv, slabs)],
        u_free=[_inv_dot(t_, _bdot(m, vs)) for t_, m, vs in zip(tinv, a_ak, v_st)],
        a_r=[m.astype(BF16) for m in a_r],
        v_st=[vs.astype(BF16) for vs in v_st],
        decay_rows=[jnp.concatenate([stack(bo[:, s]), stack(ko[:, s])], 0).astype(BF16) for s in slabs],
        e_end=e_end,
        bonus=lanes(_bdot(rows(r_all * k_all * vec_ref[0:1, :]), ones_bd)) * v_all)


def _wkv_apply(c, prep, s_prev, vec_ref, ones_bd, lanes, slabs):
    from_state = [_bdot(lhs, sp, 1, 1) for lhs, sp in zip(prep["state_lhs"], s_prev)]
    uv = [jnp.concatenate([(fs[:2 * c] + uf).astype(BF16), vs], 0)
          for fs, uf, vs in zip(from_state, prep["u_free"], prep["v_st"])]
    o_st = [fs[2 * c:] + _dg(m, x, 1, 0) for fs, m, x in zip(from_state, prep["a_r"], uv)]
    o = [x[:c] + x[c:] for x in o_st]
    new_state = [sp * prep["e_end"][:, s] + _dg(x, rows_e, 0, 0)
                 for s, sp, x, rows_e in zip(slabs, s_prev, uv, prep["decay_rows"])]
    o_rows = jnp.concatenate(o, 0)
    dev = o_rows - _bdot(o_rows, ones_bd) * (1.0 / HEAD_DIM)
    var = _bdot(dev * dev, ones_bd) * (1.0 / HEAD_DIM)
    y = lanes(dev * lax.rsqrt(var + GN_EPS)) * vec_ref[1:2, :] + vec_ref[2:3, :] + prep["bonus"]
    return new_state, y


def _pair_states(s):
    bsz, nh, n, _ = s.shape
    s = s.reshape(bsz, nh // 2, 2, n, n)
    z = jnp.zeros_like(s[:, :, 0])
    top = jnp.concatenate([s[:, :, 0], z], -1)
    bot = jnp.concatenate([z, s[:, :, 1]], -1)
    return jnp.concatenate([top, bot], -2)


def _unpair_states(sp):
    bsz, npair, _, _ = sp.shape
    n = HEAD_DIM
    return jnp.stack([sp[:, :, :n, :n], sp[:, :, n:, n:]], 2).reshape(bsz, 2 * npair, n, n)


def _wkv(proj, s0, wl, bsz, t):
    r, lw, k, v, kk, b = proj
    n, d = r.shape
    c = min(WKV_CHUNK, t)
    rows_per_step = min(WKV_CHUNKS_PER_STEP * c, t)
    nc = t // rows_per_step
    npair = d // LANES
    tok = pl.BlockSpec((rows_per_step, d), lambda bb, i: (bb * nc + i, 0))
    st = pl.BlockSpec((None, npair, LANES, LANES), lambda bb, i: (bb, 0, 0, 0))
    tri = jnp.tril(jnp.ones((c, c), BF16))
    hid = jnp.arange(LANES) // HEAD_DIM
    ones_bd = (hid[:, None] == hid[None, :]).astype(BF16)
    y, s_out = pl.pallas_call(
        functools.partial(_wkv_kernel, c),
        grid=(bsz, nc),
        in_specs=[tok] * 6 + [st,
                              pl.BlockSpec((8, d), lambda bb, i: (0, 0)),
                              pl.BlockSpec((c, c), lambda bb, i: (0, 0)),
                              pl.BlockSpec((LANES, LANES), lambda bb, i: (0, 0))],
        out_specs=[tok, st],
        out_shape=[jax.ShapeDtypeStruct((n, d), F32),
                   jax.ShapeDtypeStruct((bsz, npair, LANES, LANES), F32)],
        scratch_shapes=[pltpu.VMEM((npair, LANES, LANES), F32)],
        compiler_params=_params(("parallel", "arbitrary")),
        name="wkv_scan",
    )(r, lw, k, v, kk, b, _pair_states(s0.astype(F32)), wl["scan_vec"], tri, ones_bd)
    return y, _unpair_states(s_out)


def _router(logits):
    lane_i = lax.broadcasted_iota(jnp.int32, logits.shape, 1)
    lane = lane_i.astype(F32)
    far = 1e9
    is_g = lane_i < N_GROUPS
    gl = jnp.where(is_g, logits, NEG_BIG)
    gmax = jnp.max(gl, -1, keepdims=True)
    gsel = jnp.min(jnp.where(gl == gmax, lane, far), -1, keepdims=True)
    gprob = 1.0 / jnp.sum(jnp.where(is_g, jnp.exp(gl - gmax), 0.0), -1, keepdims=True)
    group_of = lax.shift_right_arithmetic(lane_i - N_GROUPS, 2).astype(F32)
    in_group = (lane_i >= N_GROUPS) & (lane_i < N_GROUPS + N_EXPERTS) & (group_of == gsel)
    el = jnp.where(in_group, logits, NEG_BIG)
    v1 = jnp.max(el, -1, keepdims=True)
    i1 = jnp.min(jnp.where(el == v1, lane, far), -1, keepdims=True)
    el2 = jnp.where(lane == i1, NEG_BIG, el)
    v2 = jnp.max(el2, -1, keepdims=True)
    i2 = jnp.min(jnp.where(el2 == v2, lane, far), -1, keepdims=True)
    e2 = jnp.exp(v2 - v1)
    w1 = gprob / (1.0 + e2)
    w2 = gprob * e2 / (1.0 + e2)
    return jnp.where(lane == i1, w1, 0.0) + jnp.where(lane == i2, w2, 0.0)


def _post_kernel(gated, alpha, *refs):
    if gated:
        x_ref, y_ref, g_ref, mods_ref, wo_ref, ln_ref, rw_ref, rb_ref, x1_ref, comb_ref = refs
        y = y_ref[...] * g_ref[...].astype(F32)
    else:
        x_ref, y_ref, mods_ref, wo_ref, ln_ref, rw_ref, rb_ref, x1_ref, comb_ref = refs
        y = y_ref[...]
    gt1, sh2, sc2 = mods_ref[2:3, :], mods_ref[3:4, :], mods_ref[4:5, :]
    x1 = _layer_norm(alpha * x_ref[...] + gt1 * _bdot(y, wo_ref[...]), ln_ref[0:1, :], ln_ref[1:2, :])
    x1_ref[...] = x1
    h2 = x1 * (1.0 + sc2) + sh2
    comb_ref[...] = _router(_dot3(h2, rw_ref[...]) + rb_ref[...])


def _post_mixer(x2d, y2d, g2d, mods_l, w_o, ln_pack, router_w, router_b, alpha, bsz, t, tm):
    n, d = x2d.shape
    nt = t // tm
    tok = pl.BlockSpec((tm, d), lambda b, i: (b * nt + i, 0))
    full = lambda a: pl.BlockSpec(a.shape, lambda b, i: (0,) * a.ndim)
    gated = g2d is not None
    acts = [x2d, y2d] + ([g2d] if gated else [])
    consts = [w_o, ln_pack, router_w, router_b]
    return pl.pallas_call(
        functools.partial(_post_kernel, gated, alpha),
        grid=(bsz, nt),
        in_specs=[tok] * len(acts) + [pl.BlockSpec((None, 6, d), lambda b, i: (b, 0, 0))]
        + [full(a) for a in consts],
        out_specs=[tok, pl.BlockSpec((tm, LANES), lambda b, i: (b * nt + i, 0))],
        out_shape=[jax.ShapeDtypeStruct((n, d), F32), jax.ShapeDtypeStruct((n, LANES), F32)],
        compiler_params=_params(("parallel", "parallel")),
        name="post_mixer",
    )(*acts, mods_l, *consts)


def _moe_kernel(alpha, x1_ref, comb_ref, mods_ref, ln_ref, wg_ref, wu_ref, wd_ref, x2_ref, h2_ref, acc_ref):
    e = pl.program_id(1)

    @pl.when(e == 0)
    def _():
        sh2, sc2 = mods_ref[3], mods_ref[4]
        h2_ref[...] = (x1_ref[...] * (1.0 + sc2) + sh2).astype(BF16)
        acc_ref[...] = jnp.zeros_like(acc_ref)

    h2 = h2_ref[...]
    per_step, f, d = wd_ref.shape
    lane = lax.broadcasted_iota(jnp.int32, comb_ref.shape, 1)
    comb = comb_ref[...]
    hidden = []
    for i in range(per_step):
        hg = _dg(h2, wg_ref[i].astype(BF16), 1, 0)
        hu = _dg(h2, wu_ref[i].astype(BF16), 1, 0)
        ce = jnp.sum(jnp.where(lane == e * per_step + (i + N_GROUPS), comb, 0.0), -1, keepdims=True)
        hidden.append((_silu(hg) * hu * ce).astype(BF16))
    acc_ref[...] += _dg(jnp.concatenate(hidden, 1), wd_ref[...].reshape(per_step * f, d).astype(BF16), 1, 0)

    @pl.when(e == pl.num_programs(1) - 1)
    def _():
        gt2 = mods_ref[5]
        x2_ref[...] = _layer_norm(alpha * x1_ref[...] + gt2 * acc_ref[...], ln_ref[0:1, :], ln_ref[1:2, :])


def _moe(x1, comb, mods_l, ln_pack, wg, wu, wd, alpha, bsz, t):
    n, d = x1.shape
    ne, _, f = wg.shape
    per_step = MOE_EXPERTS_PER_STEP
    if t >= MOE_TILE:
        tm = MOE_TILE
        mods = mods_l.reshape(bsz, 6, 1, d)
        mods_spec = pl.BlockSpec((None, 6, 1, d), lambda i, e: (i // (t // tm), 0, 0, 0))
    else:
        tm = min(n, MOE_TILE)
        mods = jnp.repeat(mods_l.transpose(1, 0, 2), t, axis=1)
        mods_spec = pl.BlockSpec((6, tm, d), lambda i, e: (0, i, 0))
    return pl.pallas_call(
        functools.partial(_moe_kernel, alpha),
        grid=(n // tm, ne // per_step),
        in_specs=[
            pl.BlockSpec((tm, d), lambda i, e: (i, 0)),
            pl.BlockSpec((tm, LANES), lambda i, e: (i, 0)),
            mods_spec,
            pl.BlockSpec(ln_pack.shape, lambda i, e: (0, 0)),
            pl.BlockSpec((per_step, d, f), lambda i, e: (e, 0, 0)),
            pl.BlockSpec((per_step, d, f), lambda i, e: (e, 0, 0)),
            pl.BlockSpec((per_step, f, d), lambda i, e: (e, 0, 0)),
        ],
        out_specs=pl.BlockSpec((tm, d), lambda i, e: (i, 0)),
        out_shape=jax.ShapeDtypeStruct((n, d), F32),
        scratch_shapes=[pltpu.VMEM((tm, d), BF16), pltpu.VMEM((tm, d), F32)],
        compiler_params=_params(("parallel", "arbitrary")),
        name="hmoe",
    )(x1, comb, mods, ln_pack, wg, wu, wd)


def _fox_bias_placement(d):
    place_q = np.zeros((3 * LANES, d), np.float32)
    place_k = np.zeros((3 * LANES, d), np.float32)
    const_q = np.zeros((1, d), np.float32)
    const_k = np.zeros((1, d), np.float32)
    for h in range(d // HEAD_DIM):
        base = (h // 2) * LANES + (HEAD_DIM if h % 2 == 0 else 0)
        for term in range(3):
            place_q[term * LANES + h, base + term] = 1.0
            place_k[term * LANES + h, base + 3 + term] = -1.0
            const_q[0, base + 3 + term] = 1.0
            const_k[0, base + term] = 1.0
    return (jnp.asarray(place_q, BF16), jnp.asarray(place_k, BF16), jnp.asarray(const_q),
            jnp.asarray(const_k))


def _kv_kernel(x_ref, kvw_ref, fw_ref, fb_ref, tri_ref, pq_ref, pk_ref, cq_ref, ck_ref,
               k_ref, v_ref, kb_ref, vb_ref, lf_ref, fc_ref, qbias_ref, kbias_ref, carry_ref):
    d = x_ref.shape[1]

    @pl.when(pl.program_id(1) == 0)
    def _():
        carry_ref[...] = jnp.zeros_like(carry_ref)

    x = x_ref[...]
    kv = _bdot(x, kvw_ref[...])
    k, v = kv[:, :d], kv[:, d:]
    k_ref[...] = k.T
    v_ref[...] = v.T
    kb_ref[...] = k.astype(BF16)
    vb_ref[...] = v.astype(BF16)
    z = _dot3(x, fw_ref[...]) + fb_ref[...]
    lf = -_softplus(-z)
    lf_ref[...] = lf
    fc = _dot_exact_lhs(tri_ref[...], lf) + carry_ref[...]
    fc_ref[...] = fc
    carry_ref[...] = fc[fc.shape[0] - 1:, :]
    terms = jnp.concatenate(_split3(fc * LOG2E), 1)
    qbias_ref[...] = (_dg(terms, pq_ref[...], 1, 0) + cq_ref[...]).astype(BF16)
    kbias_ref[...] = (_dg(terms, pk_ref[...], 1, 0) + ck_ref[...]).astype(BF16)


def _kv_proj(x2d, kv_w, f_w, f_b, bsz, t, tm):
    n, d = x2d.shape
    nt = t // tm
    tok = pl.BlockSpec((tm, d), lambda b, i: (b * nt + i, 0))
    nar = pl.BlockSpec((tm, LANES), lambda b, i: (b * nt + i, 0))
    feature_major = pl.BlockSpec((None, d, tm), lambda b, i: (b, 0, i))
    tri = jnp.tril(jnp.ones((tm, tm), BF16))
    full = lambda a: pl.BlockSpec(a.shape, lambda b, i: (0,) * a.ndim)
    consts = [kv_w, f_w, f_b, tri, *_fox_bias_placement(d)]
    return pl.pallas_call(
        _kv_kernel,
        grid=(bsz, nt),
        in_specs=[tok] + [full(a) for a in consts],
        out_specs=[feature_major, feature_major, tok, tok, nar, nar, tok, tok],
        out_shape=[jax.ShapeDtypeStruct((bsz, d, t), F32), jax.ShapeDtypeStruct((bsz, d, t), F32),
                   jax.ShapeDtypeStruct((n, d), BF16), jax.ShapeDtypeStruct((n, d), BF16),
                   jax.ShapeDtypeStruct((n, LANES), F32), jax.ShapeDtypeStruct((n, LANES), F32),
                   jax.ShapeDtypeStruct((n, d), BF16), jax.ShapeDtypeStruct((n, d), BF16)],
        scratch_shapes=[pltpu.VMEM((1, LANES), F32)],
        compiler_params=_params(("parallel", "arbitrary")),
        name="kv_proj",
    )(x2d, *consts)


def _q_kernel(scale, x_ref, mods_ref, wq_ref, q_ref):
    h = x_ref[...] * (1.0 + mods_ref[1:2, :]) + mods_ref[0:1, :]
    q_ref[...] = (_bdot(h, wq_ref[...]) * scale).astype(BF16)


def _q_proj(x2d, mods_l, wq, scale, bsz, t, tm):
    n, d = x2d.shape
    nt = t // tm
    tok = pl.BlockSpec((tm, d), lambda b, i: (b * nt + i, 0))
    return pl.pallas_call(
        functools.partial(_q_kernel, scale),
        grid=(bsz, nt),
        in_specs=[tok, pl.BlockSpec((None, 6, d), lambda b, i: (b, 0, 0)),
                  pl.BlockSpec(wq.shape, lambda b, i: (0, 0))],
        out_specs=tok,
        out_shape=jax.ShapeDtypeStruct((n, d), BF16),
        compiler_params=_params(("parallel", "parallel")),
        name="q_proj",
    )(x2d, mods_l, wq)


def _fox_prompt_kernel(tq, q_ref, k_ref, v_ref, qb_ref, kb_ref, o_ref):
    t = q_ref.shape[0]
    lane = lax.broadcasted_iota(jnp.int32, (1, LANES), 1)
    rr = lax.broadcasted_iota(jnp.int32, (tq, tq), 0)
    cc = lax.broadcasted_iota(jnp.int32, (tq, tq), 1)
    own = [lane < HEAD_DIM, lane >= HEAD_DIM]
    k_aug = [jnp.where(own[hh], k_ref[...], kb_ref[...]) for hh in range(2)]
    v = v_ref[...]
    for qi in range(t // tq):
        lo, hi = qi * tq, (qi + 1) * tq
        heads = []
        for hh in range(2):
            q = jnp.where(own[hh], q_ref[lo:hi, :], qb_ref[lo:hi, :])
            s_diag = jnp.where(cc <= rr, _dg(q, k_aug[hh][lo:hi], 1, 1), NEG_BIG)
            m = jnp.max(s_diag, -1, keepdims=True)
            if qi > 0:
                s_past = _dg(q, k_aug[hh][:lo], 1, 1)
                m = jnp.maximum(m, jnp.max(s_past, -1, keepdims=True))
            p = jnp.exp2(s_diag - m)
            num = _dg(p.astype(BF16), v[lo:hi], 1, 0)
            den = jnp.sum(p, -1, keepdims=True)
            if qi > 0:
                p = jnp.exp2(s_past - m)
                num = num + _dg(p.astype(BF16), v[:lo], 1, 0)
                den = den + jnp.sum(p, -1, keepdims=True)
            heads.append(num / den)
        o_ref[lo:hi, :] = jnp.where(own[0], heads[0], heads[1])


def _fox_prompt(q, kb, vb, q_bias, k_bias, bsz, t, tq):
    n, d = q.shape
    seq = pl.BlockSpec((t, LANES), lambda b, p: (b, p))
    return pl.pallas_call(
        functools.partial(_fox_prompt_kernel, tq),
        grid=(bsz, d // LANES),
        in_specs=[seq] * 5,
        out_specs=seq,
        out_shape=jax.ShapeDtypeStruct((n, d), F32),
        compiler_params=_params(("parallel", "parallel")),
        name="fox_prompt",
    )(q, kb, vb, q_bias, k_bias)


def _fox_sample_kernel(q_ref, kn_ref, vn_ref, fn_ref, fnt_ref, ck_ref, cv_ref, clf_ref, upper_ref,
                       o_ref, m_ref, l_ref, acc_ref, carry_ref):
    step = pl.program_id(1)
    t, d = q_ref.shape
    nh = d // HEAD_DIM

    @pl.when(step == 0)
    def _():
        m_ref[...] = jnp.full_like(m_ref, NEG_BIG)
        l_ref[...] = jnp.zeros_like(l_ref)
        acc_ref[...] = jnp.zeros_like(acc_ref)
        carry_ref[...] = jnp.zeros_like(carry_ref)

    heads = range(nh)
    cols = [slice(h * HEAD_DIM, (h + 1) * HEAD_DIM) for h in heads]
    q = [q_ref[:, c] for c in cols]

    def attend(keys, values, feature_major, bias, mask):
        s = [_dg(q[h], keys[h], 1, 0 if feature_major else 1) + bias[h] for h in heads]
        if mask is not None:
            s = [jnp.where(mask, x, NEG_BIG) for x in s]
        m_old = m_ref[...]
        m_new = jnp.maximum(m_old, jnp.stack([jnp.max(x, -1, keepdims=True) for x in s]))
        alpha = jnp.exp(m_old - m_new)
        p = [jnp.exp(x - m_new[h]) for h, x in enumerate(s)]
        l_ref[...] = alpha * l_ref[...] + jnp.stack([jnp.sum(x, -1, keepdims=True) for x in p])
        acc_ref[...] = alpha * acc_ref[...] + jnp.stack(
            [_dg(p[h].astype(BF16), values[h], 1, 1 if feature_major else 0) for h in heads])
        m_ref[...] = m_new

    clf = clf_ref[...]
    suf = _dot_exact_rhs(clf, upper_ref[...]) + carry_ref[...]
    carry_ref[...] = carry_ref[...] + jnp.sum(clf, -1, keepdims=True)
    fn = fn_ref[...]
    attend([ck_ref[h].astype(BF16) for h in heads], [cv_ref[h].astype(BF16) for h in heads], True,
           [fn[:, h:h + 1] + suf[h:h + 1, :] for h in heads], None)

    @pl.when(step == pl.num_programs(1) - 1)
    def _():
        rr = lax.broadcasted_iota(jnp.int32, (t, t), 0)
        cc = lax.broadcasted_iota(jnp.int32, (t, t), 1)
        fnt = fnt_ref[...]
        attend([kn_ref[:, c] for c in cols], [vn_ref[:, c] for c in cols], False,
               [fn[:, h:h + 1] - fnt[h:h + 1, :] for h in heads], cc <= rr)
        out = acc_ref[...] / l_ref[...]
        o_ref[...] = jnp.concatenate([out[h] for h in heads], 1)


def _fox_sample(q, kb, vb, fcum, cache_k, cache_v, cache_logf, bsz, t, tk):
    n, d = q.shape
    nh = d // HEAD_DIM
    plen = cache_logf.shape[1]
    nk = plen // tk
    ck = cache_k.transpose(0, 2, 3, 1)
    cv = cache_v.transpose(0, 2, 3, 1)
    clf_t = cache_logf.astype(F32).transpose(0, 2, 1)
    fn_t = fcum[:, :nh].reshape(bsz, t, nh).transpose(0, 2, 1)
    upper = (jnp.arange(tk)[:, None] > jnp.arange(tk)[None, :]).astype(BF16)
    tok = pl.BlockSpec((t, d), lambda b, j: (b, 0))
    past = pl.BlockSpec((None, nh, HEAD_DIM, tk), lambda b, j: (b, 0, 0, nk - 1 - j))
    return pl.pallas_call(
        _fox_sample_kernel,
        grid=(bsz, nk),
        in_specs=[tok, tok, tok,
                  pl.BlockSpec((t, LANES), lambda b, j: (b, 0)),
                  pl.BlockSpec((None, nh, t), lambda b, j: (b, 0, 0)),
                  past, past,
                  pl.BlockSpec((None, nh, tk), lambda b, j: (b, 0, nk - 1 - j)),
                  pl.BlockSpec((tk, tk), lambda b, j: (0, 0))],
        out_specs=tok,
        out_shape=jax.ShapeDtypeStruct((n, d), F32),
        scratch_shapes=[pltpu.VMEM((nh, t, 1), F32), pltpu.VMEM((nh, t, 1), F32),
                        pltpu.VMEM((nh, t, HEAD_DIM), F32), pltpu.VMEM((nh, 1), F32)],
        compiler_params=_params(("parallel", "arbitrary")),
        name="fox_sample",
    )(q, kb, vb, fcum, fn_t, ck, cv, clf_t, upper)


def _tile(t, cap):
    return min(t, cap)


def _trunk(x, mods, shift_in, wkv_in, cache, w):
    bsz, t, d = x.shape
    depth = mods.shape[0]
    n_a = w["n_a"]
    nh = d // HEAD_DIM
    x2d = x.reshape(bsz * t, d)
    shifts, states = [], []
    kv = None
    for l in range(depth):
        mods_l = mods[l]
        if l < n_a:
            wl = w["rwkv"][l]
            proj, last = _rwkv_proj(x2d, shift_in[l], mods_l, wl, bsz, t, _tile(t, 256))
            y, s_new = _wkv(proj[:6], wkv_in[l], wl, bsz, t)
            shifts.append(last)
            states.append(s_new.astype(wkv_in.dtype))
            mixer_out, gate, w_o = y, proj[6], wl["w_o"]
        else:
            if kv is None:
                kv = _kv_proj(x2d, w["kv_w"], w["f_w"], w["f_b"], bsz, t, _tile(t, 256))
            k_sh, v_sh, kb, vb, lf, fcum, q_bias, k_bias = kv
            j = l - n_a
            q_scale = HEAD_DIM ** -0.5 * (LOG2E if cache is None else 1.0)
            q = _q_proj(x2d, mods_l, w["b_wq"][j], q_scale, bsz, t, _tile(t, 512))
            if cache is None:
                mixer_out = _fox_prompt(q, kb, vb, q_bias, k_bias, bsz, t, _tile(t, 512))
            else:
                mixer_out = _fox_sample(q, kb, vb, fcum, cache[0], cache[1], cache[2], bsz, t,
                                        _tile(cache[0].shape[1], 512))
            gate, w_o = None, w["b_wo"][j]
        x1, comb = _post_mixer(x2d, mixer_out, gate, mods_l, w_o, w["ln"][l][0], w["router_w"][l],
                               w["router_b"][l], w["alpha"], bsz, t, _tile(t, 512))
        x2d = _moe(x1, comb, mods_l, w["ln"][l][1], w["exp_wg"][l], w["exp_wu"][l], w["exp_wd"][l],
                   w["alpha"], bsz, t)
    k_sh, v_sh, lf = kv[0], kv[1], kv[4]
    by_head = lambda z: z.reshape(bsz, nh, HEAD_DIM, t).transpose(0, 3, 1, 2)
    return (x2d.reshape(bsz, t, d), jnp.stack(shifts), jnp.stack(states), by_head(k_sh), by_head(v_sh),
            lf[:, :nh].reshape(bsz, t, nh).astype(x.dtype))


def _prepare_weights(ln_g, ln_b, a_mu, a_w_rkv, a_w0, a_w1, a_w2, a_a0, a_a1, a_a2, a_g1, a_g2, a_k_k,
                     a_k_a, a_r_k, a_lnx_g, a_lnx_b, a_w_o, kv_w, f_w, f_b, b_wq, b_wo, rg_w, rg_b,
                     re_w, re_b, exp_wg, exp_wu, exp_wd):
    depth, _, d = ln_g.shape
    n_a = a_mu.shape[0]
    nh = d // HEAD_DIM
    alpha = (2.0 * depth) ** 0.25
    zrow = jnp.zeros((d,), F32)
    head_of = jnp.arange(d) // HEAD_DIM
    head_down = (head_of[:, None] == jnp.arange(LANES)[None, :]).astype(BF16)
    rwkv = []
    for l in range(n_a):
        rwkv.append(dict(
            mu=a_mu[l],
            vec=jnp.stack([a_w0[l], a_a0[l], a_k_k[l], a_k_a[l], zrow, zrow, zrow, zrow]),
            scan_vec=jnp.stack([a_r_k[l].reshape(d), a_lnx_g[l], a_lnx_b[l], zrow, zrow, zrow, zrow, zrow]),
            w_rkv=a_w_rkv[l].astype(BF16), w1=a_w1[l].astype(BF16), w2=a_w2[l].astype(BF16),
            a1=a_a1[l].astype(BF16), a2=a_a2[l].astype(BF16), g1=a_g1[l].astype(BF16),
            g2=a_g2[l].astype(BF16), w_o=a_w_o[l].astype(BF16),
            head_down=head_down, head_up=head_down.T))
    ln = [[jnp.stack([ln_g[l, s], ln_b[l, s], zrow, zrow, zrow, zrow, zrow, zrow])
           for s in range(2)] for l in range(depth)]
    pad = LANES - N_GROUPS - N_EXPERTS
    router_w = [jnp.concatenate([rg_w[l], re_w[l], jnp.zeros((d, pad), F32)], 1) for l in range(depth)]
    router_b = [jnp.concatenate([rg_b[l], re_b[l], jnp.zeros((pad,), F32)])[None, :] for l in range(depth)]
    return dict(
        n_a=n_a, alpha=alpha, rwkv=rwkv, ln=ln, router_w=router_w, router_b=router_b,
        kv_w=kv_w.astype(BF16),
        f_w=jnp.concatenate([f_w, jnp.zeros((d, LANES - nh), F32)], 1),
        f_b=jnp.concatenate([f_b, jnp.zeros((LANES - nh,), F32)])[None, :],
        b_wq=b_wq.astype(BF16), b_wo=b_wo.astype(BF16),
        exp_wg=exp_wg, exp_wu=exp_wu, exp_wd=exp_wd)


def kernel(x_prompt, x_sample, state_shift, state_wkv, cache_k, cache_v, cache_logf, c_prompt, c_sample,
           ada_w, ada_b, ln_g, ln_b, a_mu, a_w_rkv, a_w0, a_w1, a_w2, a_a0, a_a1, a_a2, a_g1, a_g2, a_k_k,
           a_k_a, a_r_k, a_lnx_g, a_lnx_b, a_w_o, kv_w, f_w, f_b, b_wq, b_wo, rg_w, rg_b, re_w, re_b,
           exp_wg, exp_wu, exp_wd):
    w = _prepare_weights(ln_g, ln_b, a_mu, a_w_rkv, a_w0, a_w1, a_w2, a_a0, a_a1, a_a2, a_g1, a_g2,
                         a_k_k, a_k_a, a_r_k, a_lnx_g, a_lnx_b, a_w_o, kv_w, f_w, f_b, b_wq, b_wo,
                         rg_w, rg_b, re_w, re_b, exp_wg, exp_wu, exp_wd)
    bp, _, d = x_prompt.shape
    n_a = a_mu.shape[0]
    nh = d // HEAD_DIM
    mods = _ada_mods(jnp.concatenate([c_prompt, c_sample], 0), ada_w, ada_b)
    zero_shift = jnp.zeros((n_a, bp, d), x_prompt.dtype)
    zero_wkv = jnp.zeros((n_a, bp, nh, HEAD_DIM, HEAD_DIM), x_prompt.dtype)
    outs_p = _trunk(x_prompt, mods[:, :bp], zero_shift, zero_wkv, None, w)
    outs_s = _trunk(x_sample, mods[:, bp:], state_shift, state_wkv, (cache_k, cache_v, cache_logf), w)
    y_p, p_shift, p_wkv, p_k, p_v, p_logf = outs_p
    y_s, s_shift, s_wkv, s_k, s_v, s_logf = outs_s
    return (y_p, y_s, p_shift, p_wkv, p_k, p_v, p_logf, s_shift, s_wkv, s_k, s_v, s_logf)
```

```python
import functools

import jax
import jax.numpy as jnp
import numpy as np
from jax import lax
from jax.experimental import pallas as pl
from jax.experimental.pallas import tpu as pltpu

F32 = jnp.float32
BF16 = jnp.bfloat16

HEAD_DIM = 64
LANES = 128
N_GROUPS = 4
EXP_PER_GROUP = 4
N_EXPERTS = N_GROUPS * EXP_PER_GROUP
LN_EPS = 1e-5
GN_EPS = 64e-5
WKV_CHUNK = 64
WKV_CHUNKS_PER_STEP = 4
MOE_EXPERTS_PER_STEP = 2
MOE_TILE = 1024
LOG2E = 1.4426950408889634
VMEM_LIMIT = 56 * 1024 * 1024
NEG_BIG = -1e30


def _params(sem):
    return pltpu.CompilerParams(dimension_semantics=sem, vmem_limit_bytes=VMEM_LIMIT)


def _dg(a, b, ca, cb):
    return lax.dot_general(a, b, (((ca,), (cb,)), ((), ())), preferred_element_type=F32)


def _bdot(a, b, ca=1, cb=0):
    return _dg(a.astype(BF16), b.astype(BF16), ca, cb)


def _split3(x):
    hi = x.astype(BF16)
    r1 = x - hi.astype(F32)
    mid = r1.astype(BF16)
    lo = (r1 - mid.astype(F32)).astype(BF16)
    return hi, mid, lo


def _dot_exact_rhs(a, b_exact, ca=1, cb=0):
    hi, mid, lo = _split3(a)
    bb = b_exact.astype(BF16)
    return _dg(hi, bb, ca, cb) + _dg(mid, bb, ca, cb) + _dg(lo, bb, ca, cb)


def _dot_exact_lhs(a_exact, b, ca=1, cb=0):
    hi, mid, lo = _split3(b)
    aa = a_exact.astype(BF16)
    return _dg(aa, hi, ca, cb) + _dg(aa, mid, ca, cb) + _dg(aa, lo, ca, cb)


def _dot3(a, b, ca=1, cb=0):
    ah = a.astype(BF16)
    al = (a - ah.astype(F32)).astype(BF16)
    bh = b.astype(BF16)
    bl = (b - bh.astype(F32)).astype(BF16)
    return _dg(ah, bh, ca, cb) + _dg(ah, bl, ca, cb) + _dg(al, bh, ca, cb)


_inv_dot = _bdot


def _layer_norm(z, g, b):
    mu = jnp.mean(z, -1, keepdims=True)
    d = z - mu
    var = jnp.mean(d * d, -1, keepdims=True)
    return d * lax.rsqrt(var + LN_EPS) * g + b


def _softplus(z):
    return jnp.maximum(z, 0.0) + jnp.log(1.0 + jnp.exp(-jnp.abs(z)))


def _sigmoid(z):
    return 1.0 / (1.0 + jnp.exp(-z))


def _silu(z):
    return z * _sigmoid(z)


def _ada_kernel(c_ref, w_ref, b_ref, o_ref):
    o_ref[...] = _dot3(_silu(c_ref[...]), w_ref[...]) + b_ref[...]


def _ada_mods(c_all, ada_w, ada_b):
    depth, d, d6 = ada_w.shape
    bsz = c_all.shape[0]
    tn = d
    out = pl.pallas_call(
        _ada_kernel,
        grid=(depth, d6 // tn),
        in_specs=[
            pl.BlockSpec((bsz, d), lambda l, j: (0, 0)),
            pl.BlockSpec((None, d, tn), lambda l, j: (l, 0, j)),
            pl.BlockSpec((None, 1, tn), lambda l, j: (l, 0, j)),
        ],
        out_specs=pl.BlockSpec((None, bsz, tn), lambda l, j: (l, 0, j)),
        out_shape=jax.ShapeDtypeStruct((depth, bsz, d6), F32),
        compiler_params=_params(("parallel", "parallel")),
        name="ada_mods",
    )(c_all, ada_w, ada_b.reshape(depth, 1, d6))
    return out.reshape(depth, bsz, 6, d)


def _rwkv_proj_kernel(x_ref, xp_ref, shift_ref, mods_ref, mu_ref, vec_ref, wrkv_ref, w1_ref, w2_ref,
                      a1_ref, a2_ref, g1_ref, g2_ref, hd_ref, hu_ref,
                      r_ref, lw_ref, k_ref, v_ref, kk_ref, b_ref, g_ref, last_ref):
    tm = x_ref.shape[0]
    sh1 = mods_ref[0:1, :]
    sc1 = mods_ref[1:2, :]
    h = x_ref[...] * (1.0 + sc1) + sh1
    h_prev_tile = xp_ref[7:8, :] * (1.0 + sc1) + sh1
    prev_row = jnp.where(pl.program_id(1) == 0, shift_ref[...], h_prev_tile)
    row = lax.broadcasted_iota(jnp.int32, (tm, 1), 0)
    xx = jnp.where(row == 0, prev_row, pltpu.roll(h, 1, axis=0)) - h

    def mix(i):
        return h + xx * mu_ref[i:i + 1, :]

    w0, a0 = vec_ref[0:1, :], vec_ref[1:2, :]
    k_k, k_a = vec_ref[2:3, :], vec_ref[3:4, :]
    r = _bdot(mix(0), wrkv_ref[0])
    k = _bdot(mix(1), wrkv_ref[1])
    v = _bdot(mix(2), wrkv_ref[2])
    ww = w0 + _bdot(jnp.tanh(_bdot(mix(3), w1_ref[...])), w2_ref[...])
    w_log = -_softplus(-ww) - 0.5
    a = _sigmoid(a0 + _bdot(_bdot(mix(4), a1_ref[...]), a2_ref[...]))
    g = _bdot(_sigmoid(_bdot(mix(5), g1_ref[...])), g2_ref[...])
    kk = k * k_k
    head_sq = _bdot(kk * kk, hd_ref[...])
    hi = head_sq.astype(BF16)
    lo = (head_sq - hi.astype(F32)).astype(BF16)
    ss = _dg(hi, hu_ref[...], 1, 0) + _dg(lo, hu_ref[...], 1, 0)
    kk = kk * lax.rsqrt(jnp.maximum(ss, 1e-24))
    r_ref[...] = r.astype(BF16)
    lw_ref[...] = -jnp.exp(w_log)
    k_ref[...] = (k * (1.0 + (a - 1.0) * k_a)).astype(BF16)
    v_ref[...] = v.astype(BF16)
    kk_ref[...] = kk.astype(BF16)
    b_ref[...] = (kk * a).astype(BF16)
    g_ref[...] = g.astype(BF16)
    last_ref[...] = h[tm - 1:tm, :]


def _rwkv_proj(x2d, shift_prev, mods_l, wl, bsz, t, tm):
    n, d = x2d.shape
    nt = t // tm
    tok = pl.BlockSpec((tm, d), lambda b, i: (b * nt + i, 0))
    full = lambda a: pl.BlockSpec(a.shape, lambda b, i: (0,) * a.ndim)
    weights = [wl["mu"], wl["vec"], wl["w_rkv"], wl["w1"], wl["w2"], wl["a1"], wl["a2"], wl["g1"],
               wl["g2"], wl["head_down"], wl["head_up"]]
    outs = pl.pallas_call(
        _rwkv_proj_kernel,
        grid=(bsz, nt),
        in_specs=[
            tok,
            pl.BlockSpec((8, d), lambda b, i: (jnp.maximum((b * nt + i) * (tm // 8) - 1, 0), 0)),
            pl.BlockSpec((None, 1, d), lambda b, i: (b, 0, 0)),
            pl.BlockSpec((None, 6, d), lambda b, i: (b, 0, 0)),
        ] + [full(a) for a in weights],
        out_specs=[tok] * 7 + [pl.BlockSpec((None, 1, d), lambda b, i: (b, 0, 0))],
        out_shape=[jax.ShapeDtypeStruct((n, d), F32 if i == 1 else BF16) for i in range(7)]
        + [jax.ShapeDtypeStruct((bsz, 1, d), F32)],
        compiler_params=_params(("parallel", "arbitrary")),
        name="rwkv_proj",
    )(x2d, x2d, shift_prev.reshape(bsz, 1, d), mods_l, *weights)
    return outs[:7], outs[7].reshape(bsz, d)


def _wkv_kernel(c, r_ref, lw_ref, k_ref, v_ref, kk_ref, b_ref, s0_ref, vec_ref, tri_ref, ones_ref,
                y_ref, s_out_ref, state_ref):
    n_chunks = r_ref.shape[0] // c
    n_pairs = r_ref.shape[1] // LANES
    first = pl.program_id(1) == 0

    @pl.when(first)
    def _():
        state_ref[...] = s0_ref[...]

    lane = lax.broadcasted_iota(jnp.int32, (c, LANES), 1)
    low = lane < HEAD_DIM
    ri = lax.broadcasted_iota(jnp.int32, (2 * c, 2 * c), 0)
    ci = lax.broadcasted_iota(jnp.int32, (2 * c, 2 * c), 1)
    same = (ri >= c) == (ci >= c)
    strict = same & (ci < ri)
    incl = same & (ci <= ri)
    eye = (ri == ci).astype(F32)
    tri = tri_ref[...]
    ones_bd = ones_ref[...]

    def stack(z):
        return jnp.concatenate([jnp.where(low, z, 0.0), jnp.where(low, 0.0, z)], axis=0)

    pairs = range(n_pairs)
    slabs = [slice(p * LANES, (p + 1) * LANES) for p in pairs]

    def rows(z):
        return jnp.concatenate([z[:, s] for s in slabs], 0)

    def lanes(z):
        return jnp.concatenate([z[p * c:(p + 1) * c] for p in pairs], 1)

    def prepare(ch):
        rs = slice(ch * c, (ch + 1) * c)
        return _wkv_prepare(c, lw_ref[rs, :], r_ref[rs, :], k_ref[rs, :], v_ref[rs, :], kk_ref[rs, :],
                            b_ref[rs, :], vec_ref, tri, ones_bd, stack, rows, lanes, slabs, strict, incl, eye)

    s_prev = [state_ref[p] for p in pairs]
    ready = prepare(0)
    for ch in range(n_chunks):
        upcoming = prepare(ch + 1) if ch + 1 < n_chunks else None
        s_prev, y_ref[ch * c:(ch + 1) * c, :] = _wkv_apply(c, ready, s_prev, vec_ref, ones_bd, lanes, slabs)
        ready = upcoming
    state_ref[...] = jnp.stack(s_prev)

    @pl.when(pl.program_id(1) == pl.num_programs(1) - 1)
    def _():
        s_out_ref[...] = state_ref[...]


def _wkv_prepare(c, lw_all, r_bf, k_bf, v_bf, kk_bf, b_bf, vec_ref, tri, ones_bd, stack, rows, lanes,
                 slabs, strict, incl, eye):
    cum_all = _dot_exact_lhs(tri, lw_all)
    prev_all = cum_all - lw_all
    mid_all = cum_all[c // 2 - 1:c // 2, :]
    end_all = cum_all[c - 1:c, :]
    e_in = jnp.exp(mid_all - cum_all)
    e_out = jnp.exp(end_all - cum_all)
    e_end = jnp.exp(end_all)
    r_all, k_all, v_all = r_bf.astype(F32), k_bf.astype(F32), v_bf.astype(F32)
    a_all, b_all = -kk_bf.astype(F32), b_bf.astype(F32)
    aq, rq = a_all * jnp.exp(prev_all - mid_all), r_all * jnp.exp(cum_all - mid_all)
    bi, ki = b_all * e_in, k_all * e_in
    a0, r0 = a_all * jnp.exp(prev_all), r_all * jnp.exp(cum_all)
    bo, ko = b_all * e_out, k_all * e_out

    gram = [_bdot(jnp.concatenate([stack(aq[:, s]), stack(rq[:, s])], 0),
                  jnp.concatenate([stack(bi[:, s]), stack(ki[:, s])], 0), 1, 1) for s in slabs]
    a_ab = [jnp.where(strict, g[:2 * c, :2 * c], 0.0) for g in gram]
    a_ak = [jnp.where(strict, g[:2 * c, 2 * c:], 0.0) for g in gram]
    a_r = [jnp.concatenate([jnp.where(incl, g[2 * c:, :2 * c], 0.0),
                            jnp.where(incl, g[2 * c:, 2 * c:], 0.0)], 1) for g in gram]
    tinv = [eye + m for m in a_ab]
    pw = a_ab
    for _ in range(max(c.bit_length() - 2, 0)):
        pw = [_inv_dot(m, m) for m in pw]
        tinv = [t_ + _inv_dot(t_, m) for t_, m in zip(tinv, pw)]
    v_st = [stack(v_all[:, s]) for s in slabs]
    return dict(
        state_lhs=[jnp.concatenate([_inv_dot(t_, stack(a0[:, s])), stack(r0[:, s])], 0).astype(BF16)
                   for t_, s in zip(tinv, slabs)],
        u_free=[_inv_dot(t_, _bdot(m, vs)) for t_, m, vs in zip(tinv, a_ak, v_st)],
        a_r=[m.astype(BF16) for m in a_r],
        v_st=[vs.astype(BF16) for vs in v_st],
        decay_rows=[jnp.concatenate([stack(bo[:, s]), stack(ko[:, s])], 0).astype(BF16) for s in slabs],
        e_end=e_end,
        bonus=lanes(_bdot(rows(r_all * k_all * vec_ref[0:1, :]), ones_bd)) * v_all)


def _wkv_apply(c, prep, s_prev, vec_ref, ones_bd, lanes, slabs):
    from_state = [_bdot(lhs, sp, 1, 1) for lhs, sp in zip(prep["state_lhs"], s_prev)]
    uv = [jnp.concatenate([(fs[:2 * c] + uf).astype(BF16), vs], 0)
          for fs, uf, vs in zip(from_state, prep["u_free"], prep["v_st"])]
    o_st = [fs[2 * c:] + _dg(m, x, 1, 0) for fs, m, x in zip(from_state, prep["a_r"], uv)]
    o = [x[:c] + x[c:] for x in o_st]
    new_state = [sp * prep["e_end"][:, s] + _dg(x, rows_e, 0, 0)
                 for s, sp, x, rows_e in zip(slabs, s_prev, uv, prep["decay_rows"])]
    o_rows = jnp.concatenate(o, 0)
    dev = o_rows - _bdot(o_rows, ones_bd) * (1.0 / HEAD_DIM)
    var = _bdot(dev * dev, ones_bd) * (1.0 / HEAD_DIM)
    y = lanes(dev * lax.rsqrt(var + GN_EPS)) * vec_ref[1:2, :] + vec_ref[2:3, :] + prep["bonus"]
    return new_state, y


def _pair_states(s):
    bsz, nh, n, _ = s.shape
    s = s.reshape(bsz, nh // 2, 2, n, n)
    z = jnp.zeros_like(s[:, :, 0])
    top = jnp.concatenate([s[:, :, 0], z], -1)
    bot = jnp.concatenate([z, s[:, :, 1]], -1)
    return jnp.concatenate([top, bot], -2)


def _unpair_states(sp):
    bsz, npair, _, _ = sp.shape
    n = HEAD_DIM
    return jnp.stack([sp[:, :, :n, :n], sp[:, :, n:, n:]], 2).reshape(bsz, 2 * npair, n, n)


def _wkv(proj, s0, wl, bsz, t):
    r, lw, k, v, kk, b = proj
    n, d = r.shape
    c = min(WKV_CHUNK, t)
    rows_per_step = min(WKV_CHUNKS_PER_STEP * c, t)
    nc = t // rows_per_step
    npair = d // LANES
    tok = pl.BlockSpec((rows_per_step, d), lambda bb, i: (bb * nc + i, 0))
    st = pl.BlockSpec((None, npair, LANES, LANES), lambda bb, i: (bb, 0, 0, 0))
    tri = jnp.tril(jnp.ones((c, c), BF16))
    hid = jnp.arange(LANES) // HEAD_DIM
    ones_bd = (hid[:, None] == hid[None, :]).astype(BF16)
    y, s_out = pl.pallas_call(
        functools.partial(_wkv_kernel, c),
        grid=(bsz, nc),
        in_specs=[tok] * 6 + [st,
                              pl.BlockSpec((8, d), lambda bb, i: (0, 0)),
                              pl.BlockSpec((c, c), lambda bb, i: (0, 0)),
                              pl.BlockSpec((LANES, LANES), lambda bb, i: (0, 0))],
        out_specs=[tok, st],
        out_shape=[jax.ShapeDtypeStruct((n, d), F32),
                   jax.ShapeDtypeStruct((bsz, npair, LANES, LANES), F32)],
        scratch_shapes=[pltpu.VMEM((npair, LANES, LANES), F32)],
        compiler_params=_params(("parallel", "arbitrary")),
        name="wkv_scan",
    )(r, lw, k, v, kk, b, _pair_states(s0.astype(F32)), wl["scan_vec"], tri, ones_bd)
    return y, _unpair_states(s_out)


def _router(logits):
    lane_i = lax.broadcasted_iota(jnp.int32, logits.shape, 1)
    lane = lane_i.astype(F32)
    far = 1e9
    is_g = lane_i < N_GROUPS
    gl = jnp.where(is_g, logits, NEG_BIG)
    gmax = jnp.max(gl, -1, keepdims=True)
    gsel = jnp.min(jnp.where(gl == gmax, lane, far), -1, keepdims=True)
    gprob = 1.0 / jnp.sum(jnp.where(is_g, jnp.exp(gl - gmax), 0.0), -1, keepdims=True)
    group_of = lax.shift_right_arithmetic(lane_i - N_GROUPS, 2).astype(F32)
    in_group = (lane_i >= N_GROUPS) & (lane_i < N_GROUPS + N_EXPERTS) & (group_of == gsel)
    el = jnp.where(in_group, logits, NEG_BIG)
    v1 = jnp.max(el, -1, keepdims=True)
    i1 = jnp.min(jnp.where(el == v1, lane, far), -1, keepdims=True)
    el2 = jnp.where(lane == i1, NEG_BIG, el)
    v2 = jnp.max(el2, -1, keepdims=True)
    i2 = jnp.min(jnp.where(el2 == v2, lane, far), -1, keepdims=True)
    e2 = jnp.exp(v2 - v1)
    w1 = gprob / (1.0 + e2)
    w2 = gprob * e2 / (1.0 + e2)
    return jnp.where(lane == i1, w1, 0.0) + jnp.where(lane == i2, w2, 0.0)


def _post_kernel(gated, alpha, *refs):
    if gated:
        x_ref, y_ref, g_ref, mods_ref, wo_ref, ln_ref, rw_ref, rb_ref, x1_ref, comb_ref = refs
        y = y_ref[...] * g_ref[...].astype(F32)
    else:
        x_ref, y_ref, mods_ref, wo_ref, ln_ref, rw_ref, rb_ref, x1_ref, comb_ref = refs
        y = y_ref[...]
    gt1, sh2, sc2 = mods_ref[2:3, :], mods_ref[3:4, :], mods_ref[4:5, :]
    x1 = _layer_norm(alpha * x_ref[...] + gt1 * _bdot(y, wo_ref[...]), ln_ref[0:1, :], ln_ref[1:2, :])
    x1_ref[...] = x1
    h2 = x1 * (1.0 + sc2) + sh2
    comb_ref[...] = _router(_dot3(h2, rw_ref[...]) + rb_ref[...])


def _post_mixer(x2d, y2d, g2d, mods_l, w_o, ln_pack, router_w, router_b, alpha, bsz, t, tm):
    n, d = x2d.shape
    nt = t // tm
    tok = pl.BlockSpec((tm, d), lambda b, i: (b * nt + i, 0))
    full = lambda a: pl.BlockSpec(a.shape, lambda b, i: (0,) * a.ndim)
    gated = g2d is not None
    acts = [x2d, y2d] + ([g2d] if gated else [])
    consts = [w_o, ln_pack, router_w, router_b]
    return pl.pallas_call(
        functools.partial(_post_kernel, gated, alpha),
        grid=(bsz, nt),
        in_specs=[tok] * len(acts) + [pl.BlockSpec((None, 6, d), lambda b, i: (b, 0, 0))]
        + [full(a) for a in consts],
        out_specs=[tok, pl.BlockSpec((tm, LANES), lambda b, i: (b * nt + i, 0))],
        out_shape=[jax.ShapeDtypeStruct((n, d), F32), jax.ShapeDtypeStruct((n, LANES), F32)],
        compiler_params=_params(("parallel", "parallel")),
        name="post_mixer",
    )(*acts, mods_l, *consts)


def _moe_kernel(alpha, x1_ref, comb_ref, mods_ref, ln_ref, wg_ref, wu_ref, wd_ref, x2_ref, h2_ref, acc_ref):
    e = pl.program_id(1)

    @pl.when(e == 0)
    def _():
        sh2, sc2 = mods_ref[3], mods_ref[4]
        h2_ref[...] = (x1_ref[...] * (1.0 + sc2) + sh2).astype(BF16)
        acc_ref[...] = jnp.zeros_like(acc_ref)

    h2 = h2_ref[...]
    per_step, f, d = wd_ref.shape
    lane = lax.broadcasted_iota(jnp.int32, comb_ref.shape, 1)
    comb = comb_ref[...]
    hidden = []
    for i in range(per_step):
        hg = _dg(h2, wg_ref[i].astype(BF16), 1, 0)
        hu = _dg(h2, wu_ref[i].astype(BF16), 1, 0)
        ce = jnp.sum(jnp.where(lane == e * per_step + (i + N_GROUPS), comb, 0.0), -1, keepdims=True)
        hidden.append((_silu(hg) * hu * ce).astype(BF16))
    acc_ref[...] += _dg(jnp.concatenate(hidden, 1), wd_ref[...].reshape(per_step * f, d).astype(BF16), 1, 0)

    @pl.when(e == pl.num_programs(1) - 1)
    def _():
        gt2 = mods_ref[5]
        x2_ref[...] = _layer_norm(alpha * x1_ref[...] + gt2 * acc_ref[...], ln_ref[0:1, :], ln_ref[1:2, :])


def _moe(x1, comb, mods_l, ln_pack, wg, wu, wd, layer, alpha, bsz, t):
    n, d = x1.shape
    _, ne, _, f = wg.shape
    per_step = MOE_EXPERTS_PER_STEP
    if t >= MOE_TILE:
        tm = MOE_TILE
        mods = mods_l.reshape(bsz, 6, 1, d)
        mods_spec = pl.BlockSpec((None, 6, 1, d), lambda i, e: (i // (t // tm), 0, 0, 0))
    else:
        tm = min(n, MOE_TILE)
        mods = jnp.repeat(mods_l.transpose(1, 0, 2), t, axis=1)
        mods_spec = pl.BlockSpec((6, tm, d), lambda i, e: (0, i, 0))
    return pl.pallas_call(
        functools.partial(_moe_kernel, alpha),
        grid=(n // tm, ne // per_step),
        in_specs=[
            pl.BlockSpec((tm, d), lambda i, e: (i, 0)),
            pl.BlockSpec((tm, LANES), lambda i, e: (i, 0)),
            mods_spec,
            pl.BlockSpec(ln_pack.shape, lambda i, e: (0, 0)),
            pl.BlockSpec((None, per_step, d, f), lambda i, e: (layer, e, 0, 0)),
            pl.BlockSpec((None, per_step, d, f), lambda i, e: (layer, e, 0, 0)),
            pl.BlockSpec((None, per_step, f, d), lambda i, e: (layer, e, 0, 0)),
        ],
        out_specs=pl.BlockSpec((tm, d), lambda i, e: (i, 0)),
        out_shape=jax.ShapeDtypeStruct((n, d), F32),
        scratch_shapes=[pltpu.VMEM((tm, d), BF16), pltpu.VMEM((tm, d), F32)],
        compiler_params=_params(("parallel", "arbitrary")),
        name="hmoe",
    )(x1, comb, mods, ln_pack, wg, wu, wd)


def _fox_bias_placement(d):
    place_q = np.zeros((3 * LANES, d), np.float32)
    place_k = np.zeros((3 * LANES, d), np.float32)
    const_q = np.zeros((1, d), np.float32)
    const_k = np.zeros((1, d), np.float32)
    for h in range(d // HEAD_DIM):
        base = (h // 2) * LANES + (HEAD_DIM if h % 2 == 0 else 0)
        for term in range(3):
            place_q[term * LANES + h, base + term] = 1.0
            place_k[term * LANES + h, base + 3 + term] = -1.0
            const_q[0, base + 3 + term] = 1.0
            const_k[0, base + term] = 1.0
    return (jnp.asarray(place_q, BF16), jnp.asarray(place_k, BF16), jnp.asarray(const_q),
            jnp.asarray(const_k))


def _kv_kernel(x_ref, kvw_ref, fw_ref, fb_ref, tri_ref, pq_ref, pk_ref, cq_ref, ck_ref,
               k_ref, v_ref, kb_ref, vb_ref, lf_ref, fc_ref, qbias_ref, kbias_ref, vbt_ref, carry_ref):
    d = x_ref.shape[1]

    @pl.when(pl.program_id(1) == 0)
    def _():
        carry_ref[...] = jnp.zeros_like(carry_ref)

    x = x_ref[...]
    kv = _bdot(x, kvw_ref[...])
    k, v = kv[:, :d], kv[:, d:]
    k_ref[...] = k.T
    v_t = v.T
    v_ref[...] = v_t
    vbt_ref[...] = v_t.astype(BF16)
    kb_ref[...] = k.astype(BF16)
    vb_ref[...] = v.astype(BF16)
    z = _dot3(x, fw_ref[...]) + fb_ref[...]
    lf = -_softplus(-z)
    lf_ref[...] = lf
    fc = _dot_exact_lhs(tri_ref[...], lf) + carry_ref[...]
    fc_ref[...] = fc
    carry_ref[...] = fc[fc.shape[0] - 1:, :]
    terms = jnp.concatenate(_split3(fc * LOG2E), 1)
    qbias_ref[...] = (_dg(terms, pq_ref[...], 1, 0) + cq_ref[...]).astype(BF16)
    kbias_ref[...] = (_dg(terms, pk_ref[...], 1, 0) + ck_ref[...]).astype(BF16)


def _kv_proj(x2d, kv_w, f_w, f_b, bsz, t, tm):
    n, d = x2d.shape
    nt = t // tm
    tok = pl.BlockSpec((tm, d), lambda b, i: (b * nt + i, 0))
    nar = pl.BlockSpec((tm, LANES), lambda b, i: (b * nt + i, 0))
    feature_major = pl.BlockSpec((None, d, tm), lambda b, i: (b, 0, i))
    tri = jnp.tril(jnp.ones((tm, tm), BF16))
    full = lambda a: pl.BlockSpec(a.shape, lambda b, i: (0,) * a.ndim)
    consts = [kv_w, f_w, f_b, tri, *_fox_bias_placement(d)]
    return pl.pallas_call(
        _kv_kernel,
        grid=(bsz, nt),
        in_specs=[tok] + [full(a) for a in consts],
        out_specs=[feature_major, feature_major, tok, tok, nar, nar, tok, tok, feature_major],
        out_shape=[jax.ShapeDtypeStruct((bsz, d, t), F32), jax.ShapeDtypeStruct((bsz, d, t), F32),
                   jax.ShapeDtypeStruct((n, d), BF16), jax.ShapeDtypeStruct((n, d), BF16),
                   jax.ShapeDtypeStruct((n, LANES), F32), jax.ShapeDtypeStruct((n, LANES), F32),
                   jax.ShapeDtypeStruct((n, d), BF16), jax.ShapeDtypeStruct((n, d), BF16),
                   jax.ShapeDtypeStruct((bsz, d, t), BF16)],
        scratch_shapes=[pltpu.VMEM((1, LANES), F32)],
        compiler_params=_params(("parallel", "arbitrary")),
        name="kv_proj",
    )(x2d, *consts)


def _q_kernel(scale, x_ref, mods_ref, wq_ref, q_ref):
    h = x_ref[...] * (1.0 + mods_ref[1:2, :]) + mods_ref[0:1, :]
    q_ref[...] = (_bdot(h, wq_ref[...]) * scale).astype(BF16)


def _q_proj(x2d, mods_l, wq, scale, bsz, t, tm):
    n, d = x2d.shape
    nt = t // tm
    tok = pl.BlockSpec((tm, d), lambda b, i: (b * nt + i, 0))
    return pl.pallas_call(
        functools.partial(_q_kernel, scale),
        grid=(bsz, nt),
        in_specs=[tok, pl.BlockSpec((None, 6, d), lambda b, i: (b, 0, 0)),
                  pl.BlockSpec(wq.shape, lambda b, i: (0, 0))],
        out_specs=tok,
        out_shape=jax.ShapeDtypeStruct((n, d), BF16),
        compiler_params=_params(("parallel", "parallel")),
        name="q_proj",
    )(x2d, mods_l, wq)


def _fox_prompt_kernel(tq, q_ref, k_ref, vt_ref, qb_ref, kb_ref, o_ref):
    t = q_ref.shape[0]
    lane = lax.broadcasted_iota(jnp.int32, (1, LANES), 1)
    key = lax.broadcasted_iota(jnp.int32, (tq, tq), 0)
    qry = lax.broadcasted_iota(jnp.int32, (tq, tq), 1)
    own = [lane < HEAD_DIM, lane >= HEAD_DIM]
    k_aug = [jnp.where(own[hh], k_ref[...], kb_ref[...]) for hh in range(2)]
    v_t = vt_ref[...]
    for qi in range(t // tq):
        lo, hi = qi * tq, (qi + 1) * tq
        heads = []
        for hh in range(2):
            q = jnp.where(own[hh], q_ref[lo:hi, :], qb_ref[lo:hi, :])
            v_h = v_t[hh * HEAD_DIM:(hh + 1) * HEAD_DIM]
            s_diag = jnp.where(key <= qry, _dg(k_aug[hh][lo:hi], q, 1, 1), NEG_BIG)
            m = jnp.max(s_diag, 0, keepdims=True)
            if qi > 0:
                s_past = _dg(k_aug[hh][:lo], q, 1, 1)
                m = jnp.maximum(m, jnp.max(s_past, 0, keepdims=True))
            p = jnp.exp2(s_diag - m)
            num = _dg(v_h[:, lo:hi], p.astype(BF16), 1, 0)
            den = jnp.sum(p, 0, keepdims=True)
            if qi > 0:
                p = jnp.exp2(s_past - m)
                num = num + _dg(v_h[:, :lo], p.astype(BF16), 1, 0)
                den = den + jnp.sum(p, 0, keepdims=True)
            heads.append(num / den)
        o_ref[lo:hi, :] = jnp.concatenate(heads, 0).T


def _fox_prompt(q, kb, vb_t, q_bias, k_bias, bsz, t, tq):
    n, d = q.shape
    seq = pl.BlockSpec((t, LANES), lambda b, p: (b, p))
    return pl.pallas_call(
        functools.partial(_fox_prompt_kernel, tq),
        grid=(bsz, d // LANES),
        in_specs=[seq, seq, pl.BlockSpec((None, LANES, t), lambda b, p: (b, p, 0)), seq, seq],
        out_specs=seq,
        out_shape=jax.ShapeDtypeStruct((n, d), F32),
        compiler_params=_params(("parallel", "parallel")),
        name="fox_prompt",
    )(q, kb, vb_t, q_bias, k_bias)


def _fox_sample_kernel(q_ref, kn_ref, vn_ref, fn_ref, fnt_ref, ck_ref, cv_ref, clf_ref, upper_ref,
                       o_ref, m_ref, l_ref, acc_ref, carry_ref):
    step = pl.program_id(1)
    t, d = q_ref.shape
    nh = d // HEAD_DIM

    @pl.when(step == 0)
    def _():
        m_ref[...] = jnp.full_like(m_ref, NEG_BIG)
        l_ref[...] = jnp.zeros_like(l_ref)
        acc_ref[...] = jnp.zeros_like(acc_ref)
        carry_ref[...] = jnp.zeros_like(carry_ref)

    heads = range(nh)
    cols = [slice(h * HEAD_DIM, (h + 1) * HEAD_DIM) for h in heads]
    q = [q_ref[:, c] for c in cols]

    def attend(keys, values, feature_major, bias, mask):
        s = [_dg(q[h], keys[h], 1, 0 if feature_major else 1) + bias[h] for h in heads]
        if mask is not None:
            s = [jnp.where(mask, x, NEG_BIG) for x in s]
        m_old = m_ref[...]
        m_new = jnp.maximum(m_old, jnp.stack([jnp.max(x, -1, keepdims=True) for x in s]))
        alpha = jnp.exp(m_old - m_new)
        p = [jnp.exp(x - m_new[h]) for h, x in enumerate(s)]
        l_ref[...] = alpha * l_ref[...] + jnp.stack([jnp.sum(x, -1, keepdims=True) for x in p])
        acc_ref[...] = alpha * acc_ref[...] + jnp.stack(
            [_dg(p[h].astype(BF16), values[h], 1, 1 if feature_major else 0) for h in heads])
        m_ref[...] = m_new

    clf = clf_ref[...]
    suf = _dot_exact_rhs(clf, upper_ref[...]) + carry_ref[...]
    carry_ref[...] = carry_ref[...] + jnp.sum(clf, -1, keepdims=True)
    fn = fn_ref[...]
    attend([ck_ref[h].astype(BF16) for h in heads], [cv_ref[h].astype(BF16) for h in heads], True,
           [fn[:, h:h + 1] + suf[h:h + 1, :] for h in heads], None)

    @pl.when(step == pl.num_programs(1) - 1)
    def _():
        rr = lax.broadcasted_iota(jnp.int32, (t, t), 0)
        cc = lax.broadcasted_iota(jnp.int32, (t, t), 1)
        fnt = fnt_ref[...]
        attend([kn_ref[:, c] for c in cols], [vn_ref[:, c] for c in cols], False,
               [fn[:, h:h + 1] - fnt[h:h + 1, :] for h in heads], cc <= rr)
        out = acc_ref[...] / l_ref[...]
        o_ref[...] = jnp.concatenate([out[h] for h in heads], 1)


def _fox_sample(q, kb, vb, fcum, cache_k, cache_v, cache_logf, bsz, t, tk):
    n, d = q.shape
    nh = d // HEAD_DIM
    plen = cache_logf.shape[1]
    nk = plen // tk
    ck = cache_k.transpose(0, 2, 3, 1)
    cv = cache_v.transpose(0, 2, 3, 1)
    clf_t = cache_logf.astype(F32).transpose(0, 2, 1)
    fn_t = fcum[:, :nh].reshape(bsz, t, nh).transpose(0, 2, 1)
    upper = (jnp.arange(tk)[:, None] > jnp.arange(tk)[None, :]).astype(BF16)
    tok = pl.BlockSpec((t, d), lambda b, j: (b, 0))
    past = pl.BlockSpec((None, nh, HEAD_DIM, tk), lambda b, j: (b, 0, 0, nk - 1 - j))
    return pl.pallas_call(
        _fox_sample_kernel,
        grid=(bsz, nk),
        in_specs=[tok, tok, tok,
                  pl.BlockSpec((t, LANES), lambda b, j: (b, 0)),
                  pl.BlockSpec((None, nh, t), lambda b, j: (b, 0, 0)),
                  past, past,
                  pl.BlockSpec((None, nh, tk), lambda b, j: (b, 0, nk - 1 - j)),
                  pl.BlockSpec((tk, tk), lambda b, j: (0, 0))],
        out_specs=tok,
        out_shape=jax.ShapeDtypeStruct((n, d), F32),
        scratch_shapes=[pltpu.VMEM((nh, t, 1), F32), pltpu.VMEM((nh, t, 1), F32),
                        pltpu.VMEM((nh, t, HEAD_DIM), F32), pltpu.VMEM((nh, 1), F32)],
        compiler_params=_params(("parallel", "arbitrary")),
        name="fox_sample",
    )(q, kb, vb, fcum, fn_t, ck, cv, clf_t, upper)


def _tile(t, cap):
    return min(t, cap)


def _trunk(x, mods, shift_in, wkv_in, cache, w):
    bsz, t, d = x.shape
    depth = mods.shape[0]
    n_a = w["n_a"]
    nh = d // HEAD_DIM
    x2d = x.reshape(bsz * t, d)
    shifts, states = [], []
    kv = None
    for l in range(depth):
        mods_l = mods[l]
        if l < n_a:
            wl = w["rwkv"][l]
            proj, last = _rwkv_proj(x2d, shift_in[l], mods_l, wl, bsz, t, _tile(t, 256))
            y, s_new = _wkv(proj[:6], wkv_in[l], wl, bsz, t)
            shifts.append(last)
            states.append(s_new.astype(wkv_in.dtype))
            mixer_out, gate, w_o = y, proj[6], wl["w_o"]
        else:
            if kv is None:
                kv = _kv_proj(x2d, w["kv_w"], w["f_w"], w["f_b"], bsz, t, _tile(t, 256))
            k_sh, v_sh, kb, vb, lf, fcum, q_bias, k_bias, vb_t = kv
            j = l - n_a
            q_scale = HEAD_DIM ** -0.5 * (LOG2E if cache is None else 1.0)
            q = _q_proj(x2d, mods_l, w["b_wq"][j], q_scale, bsz, t, _tile(t, 512))
            if cache is None:
                mixer_out = _fox_prompt(q, kb, vb_t, q_bias, k_bias, bsz, t, _tile(t, 512))
            else:
                mixer_out = _fox_sample(q, kb, vb, fcum, cache[0], cache[1], cache[2], bsz, t,
                                        _tile(cache[0].shape[1], 512))
            gate, w_o = None, w["b_wo"][j]
        x1, comb = _post_mixer(x2d, mixer_out, gate, mods_l, w_o, w["ln"][l][0], w["router_w"][l],
                               w["router_b"][l], w["alpha"], bsz, t, _tile(t, 512))
        x2d = _moe(x1, comb, mods_l, w["ln"][l][1], w["exp_wg"], w["exp_wu"], w["exp_wd"], l,
                   w["alpha"], bsz, t)
    k_sh, v_sh, lf = kv[0], kv[1], kv[4]
    by_head = lambda z: z.reshape(bsz, nh, HEAD_DIM, t).transpose(0, 3, 1, 2)
    return (x2d.reshape(bsz, t, d), jnp.stack(shifts), jnp.stack(states), by_head(k_sh), by_head(v_sh),
            lf[:, :nh].reshape(bsz, t, nh).astype(x.dtype))


def _prepare_weights(ln_g, ln_b, a_mu, a_w_rkv, a_w0, a_w1, a_w2, a_a0, a_a1, a_a2, a_g1, a_g2, a_k_k,
                     a_k_a, a_r_k, a_lnx_g, a_lnx_b, a_w_o, kv_w, f_w, f_b, b_wq, b_wo, rg_w, rg_b,
                     re_w, re_b, exp_wg, exp_wu, exp_wd):
    depth, _, d = ln_g.shape
    n_a = a_mu.shape[0]
    nh = d // HEAD_DIM
    alpha = (2.0 * depth) ** 0.25
    zrow = jnp.zeros((d,), F32)
    head_of = jnp.arange(d) // HEAD_DIM
    head_down = (head_of[:, None] == jnp.arange(LANES)[None, :]).astype(BF16)
    rwkv = []
    for l in range(n_a):
        rwkv.append(dict(
            mu=a_mu[l],
            vec=jnp.stack([a_w0[l], a_a0[l], a_k_k[l], a_k_a[l], zrow, zrow, zrow, zrow]),
            scan_vec=jnp.stack([a_r_k[l].reshape(d), a_lnx_g[l], a_lnx_b[l], zrow, zrow, zrow, zrow, zrow]),
            w_rkv=a_w_rkv[l].astype(BF16), w1=a_w1[l].astype(BF16), w2=a_w2[l].astype(BF16),
            a1=a_a1[l].astype(BF16), a2=a_a2[l].astype(BF16), g1=a_g1[l].astype(BF16),
            g2=a_g2[l].astype(BF16), w_o=a_w_o[l].astype(BF16),
            head_down=head_down, head_up=head_down.T))
    ln = [[jnp.stack([ln_g[l, s], ln_b[l, s], zrow, zrow, zrow, zrow, zrow, zrow])
           for s in range(2)] for l in range(depth)]
    pad = LANES - N_GROUPS - N_EXPERTS
    router_w = [jnp.concatenate([rg_w[l], re_w[l], jnp.zeros((d, pad), F32)], 1) for l in range(depth)]
    router_b = [jnp.concatenate([rg_b[l], re_b[l], jnp.zeros((pad,), F32)])[None, :] for l in range(depth)]
    return dict(
        n_a=n_a, alpha=alpha, rwkv=rwkv, ln=ln, router_w=router_w, router_b=router_b,
        kv_w=kv_w.astype(BF16),
        f_w=jnp.concatenate([f_w, jnp.zeros((d, LANES - nh), F32)], 1),
        f_b=jnp.concatenate([f_b, jnp.zeros((LANES - nh,), F32)])[None, :],
        b_wq=b_wq.astype(BF16), b_wo=b_wo.astype(BF16),
        exp_wg=exp_wg, exp_wu=exp_wu, exp_wd=exp_wd)


def kernel(x_prompt, x_sample, state_shift, state_wkv, cache_k, cache_v, cache_logf, c_prompt, c_sample,
           ada_w, ada_b, ln_g, ln_b, a_mu, a_w_rkv, a_w0, a_w1, a_w2, a_a0, a_a1, a_a2, a_g1, a_g2, a_k_k,
           a_k_a, a_r_k, a_lnx_g, a_lnx_b, a_w_o, kv_w, f_w, f_b, b_wq, b_wo, rg_w, rg_b, re_w, re_b,
           exp_wg, exp_wu, exp_wd):
    w = _prepare_weights(ln_g, ln_b, a_mu, a_w_rkv, a_w0, a_w1, a_w2, a_a0, a_a1, a_a2, a_g1, a_g2,
                         a_k_k, a_k_a, a_r_k, a_lnx_g, a_lnx_b, a_w_o, kv_w, f_w, f_b, b_wq, b_wo,
                         rg_w, rg_b, re_w, re_b, exp_wg, exp_wu, exp_wd)
    bp, _, d = x_prompt.shape
    n_a = a_mu.shape[0]
    nh = d // HEAD_DIM
    mods = _ada_mods(jnp.concatenate([c_prompt, c_sample], 0), ada_w, ada_b)
    zero_shift = jnp.zeros((n_a, bp, d), x_prompt.dtype)
    zero_wkv = jnp.zeros((n_a, bp, nh, HEAD_DIM, HEAD_DIM), x_prompt.dtype)
    outs_p = _trunk(x_prompt, mods[:, :bp], zero_shift, zero_wkv, None, w)
    outs_s = _trunk(x_sample, mods[:, bp:], state_shift, state_wkv, (cache_k, cache_v, cache_logf), w)
    y_p, p_shift, p_wkv, p_k, p_v, p_logf = outs_p
    y_s, s_shift, s_wkv, s_k, s_v, s_logf = outs_s
    return (y_p, y_s, p_shift, p_wkv, p_k, p_v, p_logf, s_shift, s_wkv, s_k, s_v, s_logf)
```

```python
import functools

import jax
import jax.numpy as jnp
import numpy as np
from jax import lax
from jax.experimental import pallas as pl
from jax.experimental.pallas import tpu as pltpu

F32 = jnp.float32
BF16 = jnp.bfloat16

HEAD_DIM = 64
LANES = 128
N_GROUPS = 4
EXP_PER_GROUP = 4
N_EXPERTS = N_GROUPS * EXP_PER_GROUP
LN_EPS = 1e-5
GN_EPS = 64e-5
WKV_CHUNK = 64
WKV_CHUNKS_PER_STEP = 8
MOE_EXPERTS_PER_STEP = 2
MOE_TILE = 1024
LOG2E = 1.4426950408889634
VMEM_LIMIT = 56 * 1024 * 1024
NEG_BIG = -1e30


def _params(sem):
    return pltpu.CompilerParams(dimension_semantics=sem, vmem_limit_bytes=VMEM_LIMIT)


def _dg(a, b, ca, cb):
    return lax.dot_general(a, b, (((ca,), (cb,)), ((), ())), preferred_element_type=F32)


def _bdot(a, b, ca=1, cb=0):
    return _dg(a.astype(BF16), b.astype(BF16), ca, cb)


def _split3(x):
    hi = x.astype(BF16)
    r1 = x - hi.astype(F32)
    mid = r1.astype(BF16)
    lo = (r1 - mid.astype(F32)).astype(BF16)
    return hi, mid, lo


def _dot_exact_rhs(a, b_exact, ca=1, cb=0):
    hi, mid, lo = _split3(a)
    bb = b_exact.astype(BF16)
    return _dg(hi, bb, ca, cb) + _dg(mid, bb, ca, cb) + _dg(lo, bb, ca, cb)


def _dot_exact_lhs(a_exact, b, ca=1, cb=0):
    hi, mid, lo = _split3(b)
    aa = a_exact.astype(BF16)
    return _dg(aa, hi, ca, cb) + _dg(aa, mid, ca, cb) + _dg(aa, lo, ca, cb)


def _dot3(a, b, ca=1, cb=0):
    ah = a.astype(BF16)
    al = (a - ah.astype(F32)).astype(BF16)
    bh = b.astype(BF16)
    bl = (b - bh.astype(F32)).astype(BF16)
    return _dg(ah, bh, ca, cb) + _dg(ah, bl, ca, cb) + _dg(al, bh, ca, cb)


_inv_dot = _bdot


def _layer_norm(z, g, b):
    mu = jnp.mean(z, -1, keepdims=True)
    d = z - mu
    var = jnp.mean(d * d, -1, keepdims=True)
    return d * lax.rsqrt(var + LN_EPS) * g + b


def _softplus(z):
    return jnp.maximum(z, 0.0) + jnp.log(1.0 + jnp.exp(-jnp.abs(z)))


def _sigmoid(z):
    return 1.0 / (1.0 + jnp.exp(-z))


def _silu(z):
    return z * _sigmoid(z)


def _ada_kernel(c_ref, w_ref, b_ref, o_ref):
    o_ref[...] = _dot3(_silu(c_ref[...]), w_ref[...]) + b_ref[...]


def _ada_mods(c_all, ada_w, ada_b):
    depth, d, d6 = ada_w.shape
    bsz = c_all.shape[0]
    tn = d
    out = pl.pallas_call(
        _ada_kernel,
        grid=(depth, d6 // tn),
        in_specs=[
            pl.BlockSpec((bsz, d), lambda l, j: (0, 0)),
            pl.BlockSpec((None, d, tn), lambda l, j: (l, 0, j)),
            pl.BlockSpec((None, 1, tn), lambda l, j: (l, 0, j)),
        ],
        out_specs=pl.BlockSpec((None, bsz, tn), lambda l, j: (l, 0, j)),
        out_shape=jax.ShapeDtypeStruct((depth, bsz, d6), F32),
        compiler_params=_params(("parallel", "parallel")),
        name="ada_mods",
    )(c_all, ada_w, ada_b.reshape(depth, 1, d6))
    return out.reshape(depth, bsz, 6, d)


def _rwkv_proj_kernel(x_ref, xp_ref, shift_ref, mods_ref, mu_ref, vec_ref, wrkv_ref, w1_ref, w2_ref,
                      a1_ref, a2_ref, g1_ref, g2_ref, hd_ref, hu_ref,
                      r_ref, lw_ref, k_ref, v_ref, kk_ref, b_ref, g_ref, last_ref):
    tm = x_ref.shape[0]
    sh1 = mods_ref[0:1, :]
    sc1 = mods_ref[1:2, :]
    h = x_ref[...] * (1.0 + sc1) + sh1
    h_prev_tile = xp_ref[7:8, :] * (1.0 + sc1) + sh1
    prev_row = jnp.where(pl.program_id(1) == 0, shift_ref[...], h_prev_tile)
    row = lax.broadcasted_iota(jnp.int32, (tm, 1), 0)
    xx = jnp.where(row == 0, prev_row, pltpu.roll(h, 1, axis=0)) - h

    def mix(i):
        return h + xx * mu_ref[i:i + 1, :]

    w0, a0 = vec_ref[0:1, :], vec_ref[1:2, :]
    k_k, k_a = vec_ref[2:3, :], vec_ref[3:4, :]
    r = _bdot(mix(0), wrkv_ref[0])
    k = _bdot(mix(1), wrkv_ref[1])
    v = _bdot(mix(2), wrkv_ref[2])
    ww = w0 + _bdot(jnp.tanh(_bdot(mix(3), w1_ref[...])), w2_ref[...])
    w_log = -_softplus(-ww) - 0.5
    a = _sigmoid(a0 + _bdot(_bdot(mix(4), a1_ref[...]), a2_ref[...]))
    g = _bdot(_sigmoid(_bdot(mix(5), g1_ref[...])), g2_ref[...])
    kk = k * k_k
    head_sq = _bdot(kk * kk, hd_ref[...])
    hi = head_sq.astype(BF16)
    lo = (head_sq - hi.astype(F32)).astype(BF16)
    ss = _dg(hi, hu_ref[...], 1, 0) + _dg(lo, hu_ref[...], 1, 0)
    kk = kk * lax.rsqrt(jnp.maximum(ss, 1e-24))
    r_ref[...] = r.astype(BF16)
    lw_ref[...] = -jnp.exp(w_log)
    k_ref[...] = (k * (1.0 + (a - 1.0) * k_a)).astype(BF16)
    v_ref[...] = v.astype(BF16)
    kk_ref[...] = kk.astype(BF16)
    b_ref[...] = (kk * a).astype(BF16)
    g_ref[...] = g.astype(BF16)
    last_ref[...] = h[tm - 1:tm, :]


def _rwkv_proj(x2d, shift_prev, mods_l, wl, bsz, t, tm):
    n, d = x2d.shape
    nt = t // tm
    tok = pl.BlockSpec((tm, d), lambda b, i: (b * nt + i, 0))
    full = lambda a: pl.BlockSpec(a.shape, lambda b, i: (0,) * a.ndim)
    weights = [wl["mu"], wl["vec"], wl["w_rkv"], wl["w1"], wl["w2"], wl["a1"], wl["a2"], wl["g1"],
               wl["g2"], wl["head_down"], wl["head_up"]]
    outs = pl.pallas_call(
        _rwkv_proj_kernel,
        grid=(bsz, nt),
        in_specs=[
            tok,
            pl.BlockSpec((8, d), lambda b, i: (jnp.maximum((b * nt + i) * (tm // 8) - 1, 0), 0)),
            pl.BlockSpec((None, 1, d), lambda b, i: (b, 0, 0)),
            pl.BlockSpec((None, 6, d), lambda b, i: (b, 0, 0)),
        ] + [full(a) for a in weights],
        out_specs=[tok] * 7 + [pl.BlockSpec((None, 1, d), lambda b, i: (b, 0, 0))],
        out_shape=[jax.ShapeDtypeStruct((n, d), F32 if i == 1 else BF16) for i in range(7)]
        + [jax.ShapeDtypeStruct((bsz, 1, d), F32)],
        compiler_params=_params(("parallel", "arbitrary")),
        name="rwkv_proj",
    )(x2d, x2d, shift_prev.reshape(bsz, 1, d), mods_l, *weights)
    return outs[:7], outs[7].reshape(bsz, d)


def _wkv_kernel(c, r_ref, lw_ref, k_ref, v_ref, kk_ref, b_ref, s0_ref, vec_ref, tri_ref, ones_ref,
                y_ref, s_out_ref, state_ref):
    n_chunks = r_ref.shape[0] // c
    n_pairs = r_ref.shape[1] // LANES
    first = pl.program_id(1) == 0

    @pl.when(first)
    def _():
        state_ref[...] = s0_ref[...]

    lane = lax.broadcasted_iota(jnp.int32, (c, LANES), 1)
    low = lane < HEAD_DIM
    ri = lax.broadcasted_iota(jnp.int32, (2 * c, 2 * c), 0)
    ci = lax.broadcasted_iota(jnp.int32, (2 * c, 2 * c), 1)
    same = (ri >= c) == (ci >= c)
    strict = same & (ci < ri)
    incl = same & (ci <= ri)
    eye = (ri == ci).astype(F32)
    tri = tri_ref[...]
    ones_bd = ones_ref[...]

    def stack(z):
        return jnp.concatenate([jnp.where(low, z, 0.0), jnp.where(low, 0.0, z)], axis=0)

    pairs = range(n_pairs)
    slabs = [slice(p * LANES, (p + 1) * LANES) for p in pairs]

    def rows(z):
        return jnp.concatenate([z[:, s] for s in slabs], 0)

    def lanes(z):
        return jnp.concatenate([z[p * c:(p + 1) * c] for p in pairs], 1)

    def prepare(ch):
        rs = slice(ch * c, (ch + 1) * c)
        return _wkv_prepare(c, lw_ref[rs, :], r_ref[rs, :], k_ref[rs, :], v_ref[rs, :], kk_ref[rs, :],
                            b_ref[rs, :], vec_ref, tri, ones_bd, stack, rows, lanes, slabs, strict, incl, eye)

    s_prev = [state_ref[p] for p in pairs]
    ready = prepare(0)
    for ch in range(n_chunks):
        upcoming = prepare(ch + 1) if ch + 1 < n_chunks else None
        s_prev, y_ref[ch * c:(ch + 1) * c, :] = _wkv_apply(c, ready, s_prev, vec_ref, ones_bd, lanes, slabs)
        ready = upcoming
    state_ref[...] = jnp.stack(s_prev)

    @pl.when(pl.program_id(1) == pl.num_programs(1) - 1)
    def _():
        s_out_ref[...] = state_ref[...]


def _wkv_prepare(c, lw_all, r_bf, k_bf, v_bf, kk_bf, b_bf, vec_ref, tri, ones_bd, stack, rows, lanes,
                 slabs, strict, incl, eye):
    cum_all = _dot_exact_lhs(tri, lw_all)
    prev_all = cum_all - lw_all
    mid_all = cum_all[c // 2 - 1:c // 2, :]
    end_all = cum_all[c - 1:c, :]
    e_in = jnp.exp(mid_all - cum_all)
    e_out = jnp.exp(end_all - cum_all)
    e_end = jnp.exp(end_all)
    r_all, k_all, v_all = r_bf.astype(F32), k_bf.astype(F32), v_bf.astype(F32)
    a_all, b_all = -kk_bf.astype(F32), b_bf.astype(F32)
    aq, rq = a_all * jnp.exp(prev_all - mid_all), r_all * jnp.exp(cum_all - mid_all)
    bi, ki = b_all * e_in, k_all * e_in
    a0, r0 = a_all * jnp.exp(prev_all), r_all * jnp.exp(cum_all)
    bo, ko = b_all * e_out, k_all * e_out

    gram = [_bdot(jnp.concatenate([stack(aq[:, s]), stack(rq[:, s])], 0),
                  jnp.concatenate([stack(bi[:, s]), stack(ki[:, s])], 0), 1, 1) for s in slabs]
    a_ab = [jnp.where(strict, g[:2 * c, :2 * c], 0.0) for g in gram]
    a_ak = [jnp.where(strict, g[:2 * c, 2 * c:], 0.0) for g in gram]
    a_r = [jnp.concatenate([jnp.where(incl, g[2 * c:, :2 * c], 0.0),
                            jnp.where(incl, g[2 * c:, 2 * c:], 0.0)], 1) for g in gram]
    tinv = [eye + m for m in a_ab]
    pw = a_ab
    for _ in range(max(c.bit_length() - 2, 0)):
        pw = [_inv_dot(m, m) for m in pw]
        tinv = [t_ + _inv_dot(t_, m) for t_, m in zip(tinv, pw)]
    v_st = [stack(v_all[:, s]) for s in slabs]
    return dict(
        state_lhs=[jnp.concatenate([_inv_dot(t_, stack(a0[:, s])), stack(r0[:, s])], 0).astype(BF16)
                   for t_, s in zip(tinv, slabs)],
        u_free=[_inv_dot(t_, _bdot(m, vs)) for t_, m, vs in zip(tinv, a_ak, v_st)],
        a_r=[m.astype(BF16) for m in a_r],
        v_st=[vs.astype(BF16) for vs in v_st],
        decay_rows=[jnp.concatenate([stack(bo[:, s]), stack(ko[:, s])], 0).astype(BF16) for s in slabs],
        e_end=e_end,
        bonus=lanes(_bdot(rows(r_all * k_all * vec_ref[0:1, :]), ones_bd)) * v_all)


def _wkv_apply(c, prep, s_prev, vec_ref, ones_bd, lanes, slabs):
    from_state = [_bdot(lhs, sp, 1, 1) for lhs, sp in zip(prep["state_lhs"], s_prev)]
    uv = [jnp.concatenate([(fs[:2 * c] + uf).astype(BF16), vs], 0)
          for fs, uf, vs in zip(from_state, prep["u_free"], prep["v_st"])]
    o_st = [fs[2 * c:] + _dg(m, x, 1, 0) for fs, m, x in zip(from_state, prep["a_r"], uv)]
    o = [x[:c] + x[c:] for x in o_st]
    new_state = [sp * prep["e_end"][:, s] + _dg(x, rows_e, 0, 0)
                 for s, sp, x, rows_e in zip(slabs, s_prev, uv, prep["decay_rows"])]
    o_rows = jnp.concatenate(o, 0)
    dev = o_rows - _bdot(o_rows, ones_bd) * (1.0 / HEAD_DIM)
    var = _bdot(dev * dev, ones_bd) * (1.0 / HEAD_DIM)
    y = lanes(dev * lax.rsqrt(var + GN_EPS)) * vec_ref[1:2, :] + vec_ref[2:3, :] + prep["bonus"]
    return new_state, y


def _pair_states(s):
    bsz, nh, n, _ = s.shape
    s = s.reshape(bsz, nh // 2, 2, n, n)
    z = jnp.zeros_like(s[:, :, 0])
    top = jnp.concatenate([s[:, :, 0], z], -1)
    bot = jnp.concatenate([z, s[:, :, 1]], -1)
    return jnp.concatenate([top, bot], -2)


def _unpair_states(sp):
    bsz, npair, _, _ = sp.shape
    n = HEAD_DIM
    return jnp.stack([sp[:, :, :n, :n], sp[:, :, n:, n:]], 2).reshape(bsz, 2 * npair, n, n)


def _wkv(proj, s0, wl, bsz, t):
    r, lw, k, v, kk, b = proj
    n, d = r.shape
    c = min(WKV_CHUNK, t)
    rows_per_step = min(WKV_CHUNKS_PER_STEP * c, t)
    nc = t // rows_per_step
    npair = d // LANES
    tok = pl.BlockSpec((rows_per_step, d), lambda bb, i: (bb * nc + i, 0))
    st = pl.BlockSpec((None, npair, LANES, LANES), lambda bb, i: (bb, 0, 0, 0))
    tri = jnp.tril(jnp.ones((c, c), BF16))
    hid = jnp.arange(LANES) // HEAD_DIM
    ones_bd = (hid[:, None] == hid[None, :]).astype(BF16)
    y, s_out = pl.pallas_call(
        functools.partial(_wkv_kernel, c),
        grid=(bsz, nc),
        in_specs=[tok] * 6 + [st,
                              pl.BlockSpec((8, d), lambda bb, i: (0, 0)),
                              pl.BlockSpec((c, c), lambda bb, i: (0, 0)),
                              pl.BlockSpec((LANES, LANES), lambda bb, i: (0, 0))],
        out_specs=[tok, st],
        out_shape=[jax.ShapeDtypeStruct((n, d), F32),
                   jax.ShapeDtypeStruct((bsz, npair, LANES, LANES), F32)],
        scratch_shapes=[pltpu.VMEM((npair, LANES, LANES), F32)],
        compiler_params=_params(("parallel", "arbitrary")),
        name="wkv_scan",
    )(r, lw, k, v, kk, b, _pair_states(s0.astype(F32)), wl["scan_vec"], tri, ones_bd)
    return y, _unpair_states(s_out)


def _router(logits):
    lane_i = lax.broadcasted_iota(jnp.int32, logits.shape, 1)
    lane = lane_i.astype(F32)
    far = 1e9
    is_g = lane_i < N_GROUPS
    gl = jnp.where(is_g, logits, NEG_BIG)
    gmax = jnp.max(gl, -1, keepdims=True)
    gsel = jnp.min(jnp.where(gl == gmax, lane, far), -1, keepdims=True)
    gprob = 1.0 / jnp.sum(jnp.where(is_g, jnp.exp(gl - gmax), 0.0), -1, keepdims=True)
    group_of = lax.shift_right_arithmetic(lane_i - N_GROUPS, 2).astype(F32)
    in_group = (lane_i >= N_GROUPS) & (lane_i < N_GROUPS + N_EXPERTS) & (group_of == gsel)
    el = jnp.where(in_group, logits, NEG_BIG)
    v1 = jnp.max(el, -1, keepdims=True)
    i1 = jnp.min(jnp.where(el == v1, lane, far), -1, keepdims=True)
    el2 = jnp.where(lane == i1, NEG_BIG, el)
    v2 = jnp.max(el2, -1, keepdims=True)
    i2 = jnp.min(jnp.where(el2 == v2, lane, far), -1, keepdims=True)
    e2 = jnp.exp(v2 - v1)
    w1 = gprob / (1.0 + e2)
    w2 = gprob * e2 / (1.0 + e2)
    return jnp.where(lane == i1, w1, 0.0) + jnp.where(lane == i2, w2, 0.0)


def _post_kernel(gated, alpha, *refs):
    if gated:
        x_ref, y_ref, g_ref, mods_ref, wo_ref, ln_ref, rw_ref, rb_ref, x1_ref, comb_ref = refs
        y = y_ref[...] * g_ref[...].astype(F32)
    else:
        x_ref, y_ref, mods_ref, wo_ref, ln_ref, rw_ref, rb_ref, x1_ref, comb_ref = refs
        y = y_ref[...]
    gt1, sh2, sc2 = mods_ref[2:3, :], mods_ref[3:4, :], mods_ref[4:5, :]
    x1 = _layer_norm(alpha * x_ref[...] + gt1 * _bdot(y, wo_ref[...]), ln_ref[0:1, :], ln_ref[1:2, :])
    x1_ref[...] = x1
    h2 = x1 * (1.0 + sc2) + sh2
    comb_ref[...] = _router(_dot3(h2, rw_ref[...]) + rb_ref[...])


def _post_mixer(x2d, y2d, g2d, mods_l, w_o, ln_pack, router_w, router_b, alpha, bsz, t, tm):
    n, d = x2d.shape
    nt = t // tm
    tok = pl.BlockSpec((tm, d), lambda b, i: (b * nt + i, 0))
    full = lambda a: pl.BlockSpec(a.shape, lambda b, i: (0,) * a.ndim)
    gated = g2d is not None
    acts = [x2d, y2d] + ([g2d] if gated else [])
    consts = [w_o, ln_pack, router_w, router_b]
    return pl.pallas_call(
        functools.partial(_post_kernel, gated, alpha),
        grid=(bsz, nt),
        in_specs=[tok] * len(acts) + [pl.BlockSpec((None, 6, d), lambda b, i: (b, 0, 0))]
        + [full(a) for a in consts],
        out_specs=[tok, pl.BlockSpec((tm, LANES), lambda b, i: (b * nt + i, 0))],
        out_shape=[jax.ShapeDtypeStruct((n, d), F32), jax.ShapeDtypeStruct((n, LANES), F32)],
        compiler_params=_params(("parallel", "parallel")),
        name="post_mixer",
    )(*acts, mods_l, *consts)


def _moe_kernel(alpha, x1_ref, comb_ref, mods_ref, ln_ref, wg_ref, wu_ref, wd_ref, x2_ref, h2_ref, acc_ref):
    e = pl.program_id(1)

    @pl.when(e == 0)
    def _():
        sh2, sc2 = mods_ref[3], mods_ref[4]
        h2_ref[...] = (x1_ref[...] * (1.0 + sc2) + sh2).astype(BF16)
        acc_ref[...] = jnp.zeros_like(acc_ref)

    h2 = h2_ref[...]
    per_step, f, d = wd_ref.shape
    lane = lax.broadcasted_iota(jnp.int32, comb_ref.shape, 1)
    comb = comb_ref[...]
    hidden = []
    for i in range(per_step):
        hg = _dg(h2, wg_ref[i].astype(BF16), 1, 0)
        hu = _dg(h2, wu_ref[i].astype(BF16), 1, 0)
        ce = jnp.sum(jnp.where(lane == e * per_step + (i + N_GROUPS), comb, 0.0), -1, keepdims=True)
        hidden.append((_silu(hg) * hu * ce).astype(BF16))
    acc_ref[...] += _dg(jnp.concatenate(hidden, 1), wd_ref[...].reshape(per_step * f, d).astype(BF16), 1, 0)

    @pl.when(e == pl.num_programs(1) - 1)
    def _():
        gt2 = mods_ref[5]
        x2_ref[...] = _layer_norm(alpha * x1_ref[...] + gt2 * acc_ref[...], ln_ref[0:1, :], ln_ref[1:2, :])


def _moe(x1, comb, mods_l, ln_pack, wg, wu, wd, layer, alpha, bsz, t):
    n, d = x1.shape
    _, ne, _, f = wg.shape
    per_step = MOE_EXPERTS_PER_STEP
    if t >= MOE_TILE:
        tm = MOE_TILE
        mods = mods_l.reshape(bsz, 6, 1, d)
        mods_spec = pl.BlockSpec((None, 6, 1, d), lambda i, e: (i // (t // tm), 0, 0, 0))
    else:
        tm = min(n, MOE_TILE)
        mods = jnp.repeat(mods_l.transpose(1, 0, 2), t, axis=1)
        mods_spec = pl.BlockSpec((6, tm, d), lambda i, e: (0, i, 0))
    return pl.pallas_call(
        functools.partial(_moe_kernel, alpha),
        grid=(n // tm, ne // per_step),
        in_specs=[
            pl.BlockSpec((tm, d), lambda i, e: (i, 0)),
            pl.BlockSpec((tm, LANES), lambda i, e: (i, 0)),
            mods_spec,
            pl.BlockSpec(ln_pack.shape, lambda i, e: (0, 0)),
            pl.BlockSpec((None, per_step, d, f), lambda i, e: (layer, e, 0, 0)),
            pl.BlockSpec((None, per_step, d, f), lambda i, e: (layer, e, 0, 0)),
            pl.BlockSpec((None, per_step, f, d), lambda i, e: (layer, e, 0, 0)),
        ],
        out_specs=pl.BlockSpec((tm, d), lambda i, e: (i, 0)),
        out_shape=jax.ShapeDtypeStruct((n, d), F32),
        scratch_shapes=[pltpu.VMEM((tm, d), BF16), pltpu.VMEM((tm, d), F32)],
        compiler_params=_params(("parallel", "arbitrary")),
        name="hmoe",
    )(x1, comb, mods, ln_pack, wg, wu, wd)


def _fox_bias_placement(d):
    place_q = np.zeros((3 * LANES, d), np.float32)
    place_k = np.zeros((3 * LANES, d), np.float32)
    const_q = np.zeros((1, d), np.float32)
    const_k = np.zeros((1, d), np.float32)
    for h in range(d // HEAD_DIM):
        base = (h // 2) * LANES + (HEAD_DIM if h % 2 == 0 else 0)
        for term in range(3):
            place_q[term * LANES + h, base + term] = 1.0
            place_k[term * LANES + h, base + 3 + term] = -1.0
            const_q[0, base + 3 + term] = 1.0
            const_k[0, base + term] = 1.0
    return (jnp.asarray(place_q, BF16), jnp.asarray(place_k, BF16), jnp.asarray(const_q),
            jnp.asarray(const_k))


def _kv_kernel(x_ref, kvw_ref, fw_ref, fb_ref, tri_ref, pq_ref, pk_ref, cq_ref, ck_ref,
               k_ref, v_ref, kb_ref, vb_ref, lf_ref, fc_ref, qbias_ref, kbias_ref, vbt_ref, carry_ref):
    d = x_ref.shape[1]

    @pl.when(pl.program_id(1) == 0)
    def _():
        carry_ref[...] = jnp.zeros_like(carry_ref)

    x = x_ref[...]
    kv = _bdot(x, kvw_ref[...])
    k, v = kv[:, :d], kv[:, d:]
    k_ref[...] = k.T
    v_t = v.T
    v_ref[...] = v_t
    vbt_ref[...] = v_t.astype(BF16)
    kb_ref[...] = k.astype(BF16)
    vb_ref[...] = v.astype(BF16)
    z = _dot3(x, fw_ref[...]) + fb_ref[...]
    lf = -_softplus(-z)
    lf_ref[...] = lf
    fc = _dot_exact_lhs(tri_ref[...], lf) + carry_ref[...]
    fc_ref[...] = fc
    carry_ref[...] = fc[fc.shape[0] - 1:, :]
    terms = jnp.concatenate(_split3(fc * LOG2E), 1)
    qbias_ref[...] = (_dg(terms, pq_ref[...], 1, 0) + cq_ref[...]).astype(BF16)
    kbias_ref[...] = (_dg(terms, pk_ref[...], 1, 0) + ck_ref[...]).astype(BF16)


def _kv_proj(x2d, kv_w, f_w, f_b, bsz, t, tm):
    n, d = x2d.shape
    nt = t // tm
    tok = pl.BlockSpec((tm, d), lambda b, i: (b * nt + i, 0))
    nar = pl.BlockSpec((tm, LANES), lambda b, i: (b * nt + i, 0))
    feature_major = pl.BlockSpec((None, d, tm), lambda b, i: (b, 0, i))
    tri = jnp.tril(jnp.ones((tm, tm), BF16))
    full = lambda a: pl.BlockSpec(a.shape, lambda b, i: (0,) * a.ndim)
    consts = [kv_w, f_w, f_b, tri, *_fox_bias_placement(d)]
    return pl.pallas_call(
        _kv_kernel,
        grid=(bsz, nt),
        in_specs=[tok] + [full(a) for a in consts],
        out_specs=[feature_major, feature_major, tok, tok, nar, nar, tok, tok, feature_major],
        out_shape=[jax.ShapeDtypeStruct((bsz, d, t), F32), jax.ShapeDtypeStruct((bsz, d, t), F32),
                   jax.ShapeDtypeStruct((n, d), BF16), jax.ShapeDtypeStruct((n, d), BF16),
                   jax.ShapeDtypeStruct((n, LANES), F32), jax.ShapeDtypeStruct((n, LANES), F32),
                   jax.ShapeDtypeStruct((n, d), BF16), jax.ShapeDtypeStruct((n, d), BF16),
                   jax.ShapeDtypeStruct((bsz, d, t), BF16)],
        scratch_shapes=[pltpu.VMEM((1, LANES), F32)],
        compiler_params=_params(("parallel", "arbitrary")),
        name="kv_proj",
    )(x2d, *consts)


def _q_kernel(scale, x_ref, mods_ref, wq_ref, q_ref):
    h = x_ref[...] * (1.0 + mods_ref[1:2, :]) + mods_ref[0:1, :]
    q_ref[...] = (_bdot(h, wq_ref[...]) * scale).astype(BF16)


def _q_proj(x2d, mods_l, wq, scale, bsz, t, tm):
    n, d = x2d.shape
    nt = t // tm
    tok = pl.BlockSpec((tm, d), lambda b, i: (b * nt + i, 0))
    return pl.pallas_call(
        functools.partial(_q_kernel, scale),
        grid=(bsz, nt),
        in_specs=[tok, pl.BlockSpec((None, 6, d), lambda b, i: (b, 0, 0)),
                  pl.BlockSpec(wq.shape, lambda b, i: (0, 0))],
        out_specs=tok,
        out_shape=jax.ShapeDtypeStruct((n, d), BF16),
        compiler_params=_params(("parallel", "parallel")),
        name="q_proj",
    )(x2d, mods_l, wq)


def _fox_prompt_kernel(tq, q_ref, k_ref, vt_ref, qb_ref, kb_ref, o_ref):
    t = q_ref.shape[0]
    lane = lax.broadcasted_iota(jnp.int32, (1, LANES), 1)
    key = lax.broadcasted_iota(jnp.int32, (tq, tq), 0)
    qry = lax.broadcasted_iota(jnp.int32, (tq, tq), 1)
    own = [lane < HEAD_DIM, lane >= HEAD_DIM]
    k_aug = [jnp.where(own[hh], k_ref[...], kb_ref[...]) for hh in range(2)]
    v_t = vt_ref[...]
    for qi in range(t // tq):
        lo, hi = qi * tq, (qi + 1) * tq
        heads = []
        for hh in range(2):
            q = jnp.where(own[hh], q_ref[lo:hi, :], qb_ref[lo:hi, :])
            v_h = v_t[hh * HEAD_DIM:(hh + 1) * HEAD_DIM]
            s_diag = jnp.where(key <= qry, _dg(k_aug[hh][lo:hi], q, 1, 1), NEG_BIG)
            m = jnp.max(s_diag, 0, keepdims=True)
            if qi > 0:
                s_past = _dg(k_aug[hh][:lo], q, 1, 1)
                m = jnp.maximum(m, jnp.max(s_past, 0, keepdims=True))
            p = jnp.exp2(s_diag - m)
            num = _dg(v_h[:, lo:hi], p.astype(BF16), 1, 0)
            den = jnp.sum(p, 0, keepdims=True)
            if qi > 0:
                p = jnp.exp2(s_past - m)
                num = num + _dg(v_h[:, :lo], p.astype(BF16), 1, 0)
                den = den + jnp.sum(p, 0, keepdims=True)
            heads.append(num / den)
        o_ref[lo:hi, :] = jnp.concatenate(heads, 0).T


def _fox_prompt(q, kb, vb_t, q_bias, k_bias, bsz, t, tq):
    n, d = q.shape
    seq = pl.BlockSpec((t, LANES), lambda b, p: (b, p))
    return pl.pallas_call(
        functools.partial(_fox_prompt_kernel, tq),
        grid=(bsz, d // LANES),
        in_specs=[seq, seq, pl.BlockSpec((None, LANES, t), lambda b, p: (b, p, 0)), seq, seq],
        out_specs=seq,
        out_shape=jax.ShapeDtypeStruct((n, d), F32),
        compiler_params=_params(("parallel", "parallel")),
        name="fox_prompt",
    )(q, kb, vb_t, q_bias, k_bias)


def _fox_sample_kernel(q_ref, kn_ref, vn_ref, fn_ref, fnt_ref, ck_ref, cv_ref, clf_ref, upper_ref,
                       o_ref, m_ref, l_ref, acc_ref, carry_ref):
    step = pl.program_id(1)
    t, d = q_ref.shape
    nh = d // HEAD_DIM

    @pl.when(step == 0)
    def _():
        m_ref[...] = jnp.full_like(m_ref, NEG_BIG)
        l_ref[...] = jnp.zeros_like(l_ref)
        acc_ref[...] = jnp.zeros_like(acc_ref)
        carry_ref[...] = jnp.zeros_like(carry_ref)

    heads = range(nh)
    cols = [slice(h * HEAD_DIM, (h + 1) * HEAD_DIM) for h in heads]
    q = [q_ref[:, c] for c in cols]

    def attend(keys, values, feature_major, bias, mask):
        s = [_dg(q[h], keys[h], 1, 0 if feature_major else 1) + bias[h] for h in heads]
        if mask is not None:
            s = [jnp.where(mask, x, NEG_BIG) for x in s]
        m_old = m_ref[...]
        m_new = jnp.maximum(m_old, jnp.stack([jnp.max(x, -1, keepdims=True) for x in s]))
        alpha = jnp.exp(m_old - m_new)
        p = [jnp.exp(x - m_new[h]) for h, x in enumerate(s)]
        l_ref[...] = alpha * l_ref[...] + jnp.stack([jnp.sum(x, -1, keepdims=True) for x in p])
        acc_ref[...] = alpha * acc_ref[...] + jnp.stack(
            [_dg(p[h].astype(BF16), values[h], 1, 1 if feature_major else 0) for h in heads])
        m_ref[...] = m_new

    clf = clf_ref[...]
    suf = _dot_exact_rhs(clf, upper_ref[...]) + carry_ref[...]
    carry_ref[...] = carry_ref[...] + jnp.sum(clf, -1, keepdims=True)
    fn = fn_ref[...]
    attend([ck_ref[h].astype(BF16) for h in heads], [cv_ref[h].astype(BF16) for h in heads], True,
           [fn[:, h:h + 1] + suf[h:h + 1, :] for h in heads], None)

    @pl.when(step == pl.num_programs(1) - 1)
    def _():
        rr = lax.broadcasted_iota(jnp.int32, (t, t), 0)
        cc = lax.broadcasted_iota(jnp.int32, (t, t), 1)
        fnt = fnt_ref[...]
        attend([kn_ref[:, c] for c in cols], [vn_ref[:, c] for c in cols], False,
               [fn[:, h:h + 1] - fnt[h:h + 1, :] for h in heads], cc <= rr)
        out = acc_ref[...] / l_ref[...]
        o_ref[...] = jnp.concatenate([out[h] for h in heads], 1)


def _fox_sample(q, kb, vb, fcum, cache_k, cache_v, cache_logf, bsz, t, tk):
    n, d = q.shape
    nh = d // HEAD_DIM
    plen = cache_logf.shape[1]
    nk = plen // tk
    ck = cache_k.transpose(0, 2, 3, 1)
    cv = cache_v.transpose(0, 2, 3, 1)
    clf_t = cache_logf.astype(F32).transpose(0, 2, 1)
    fn_t = fcum[:, :nh].reshape(bsz, t, nh).transpose(0, 2, 1)
    upper = (jnp.arange(tk)[:, None] > jnp.arange(tk)[None, :]).astype(BF16)
    tok = pl.BlockSpec((t, d), lambda b, j: (b, 0))
    past = pl.BlockSpec((None, nh, HEAD_DIM, tk), lambda b, j: (b, 0, 0, nk - 1 - j))
    return pl.pallas_call(
        _fox_sample_kernel,
        grid=(bsz, nk),
        in_specs=[tok, tok, tok,
                  pl.BlockSpec((t, LANES), lambda b, j: (b, 0)),
                  pl.BlockSpec((None, nh, t), lambda b, j: (b, 0, 0)),
                  past, past,
                  pl.BlockSpec((None, nh, tk), lambda b, j: (b, 0, nk - 1 - j)),
                  pl.BlockSpec((tk, tk), lambda b, j: (0, 0))],
        out_specs=tok,
        out_shape=jax.ShapeDtypeStruct((n, d), F32),
        scratch_shapes=[pltpu.VMEM((nh, t, 1), F32), pltpu.VMEM((nh, t, 1), F32),
                        pltpu.VMEM((nh, t, HEAD_DIM), F32), pltpu.VMEM((nh, 1), F32)],
        compiler_params=_params(("parallel", "arbitrary")),
        name="fox_sample",
    )(q, kb, vb, fcum, fn_t, ck, cv, clf_t, upper)


def _tile(t, cap):
    return min(t, cap)


def _trunk(x, mods, shift_in, wkv_in, cache, w):
    bsz, t, d = x.shape
    depth = mods.shape[0]
    n_a = w["n_a"]
    nh = d // HEAD_DIM
    x2d = x.reshape(bsz * t, d)
    shifts, states = [], []
    kv = None
    for l in range(depth):
        mods_l = mods[l]
        if l < n_a:
            wl = w["rwkv"][l]
            proj, last = _rwkv_proj(x2d, shift_in[l], mods_l, wl, bsz, t, _tile(t, 512))
            y, s_new = _wkv(proj[:6], wkv_in[l], wl, bsz, t)
            shifts.append(last)
            states.append(s_new.astype(wkv_in.dtype))
            mixer_out, gate, w_o = y, proj[6], wl["w_o"]
        else:
            if kv is None:
                kv = _kv_proj(x2d, w["kv_w"], w["f_w"], w["f_b"], bsz, t, _tile(t, 512))
            k_sh, v_sh, kb, vb, lf, fcum, q_bias, k_bias, vb_t = kv
            j = l - n_a
            q_scale = HEAD_DIM ** -0.5 * (LOG2E if cache is None else 1.0)
            q = _q_proj(x2d, mods_l, w["b_wq"][j], q_scale, bsz, t, _tile(t, 1024))
            if cache is None:
                mixer_out = _fox_prompt(q, kb, vb_t, q_bias, k_bias, bsz, t, _tile(t, 512))
            else:
                mixer_out = _fox_sample(q, kb, vb, fcum, cache[0], cache[1], cache[2], bsz, t,
                                        _tile(cache[0].shape[1], 1024))
            gate, w_o = None, w["b_wo"][j]
        x1, comb = _post_mixer(x2d, mixer_out, gate, mods_l, w_o, w["ln"][l][0], w["router_w"][l],
                               w["router_b"][l], w["alpha"], bsz, t, _tile(t, 1024))
        x2d = _moe(x1, comb, mods_l, w["ln"][l][1], w["exp_wg"], w["exp_wu"], w["exp_wd"], l,
                   w["alpha"], bsz, t)
    k_sh, v_sh, lf = kv[0], kv[1], kv[4]
    by_head = lambda z: z.reshape(bsz, nh, HEAD_DIM, t).transpose(0, 3, 1, 2)
    return (x2d.reshape(bsz, t, d), jnp.stack(shifts), jnp.stack(states), by_head(k_sh), by_head(v_sh),
            lf[:, :nh].reshape(bsz, t, nh).astype(x.dtype))


def _prepare_weights(ln_g, ln_b, a_mu, a_w_rkv, a_w0, a_w1, a_w2, a_a0, a_a1, a_a2, a_g1, a_g2, a_k_k,
                     a_k_a, a_r_k, a_lnx_g, a_lnx_b, a_w_o, kv_w, f_w, f_b, b_wq, b_wo, rg_w, rg_b,
                     re_w, re_b, exp_wg, exp_wu, exp_wd):
    depth, _, d = ln_g.shape
    n_a = a_mu.shape[0]
    nh = d // HEAD_DIM
    alpha = (2.0 * depth) ** 0.25
    zrow = jnp.zeros((d,), F32)
    head_of = jnp.arange(d) // HEAD_DIM
    head_down = (head_of[:, None] == jnp.arange(LANES)[None, :]).astype(BF16)
    rwkv = []
    for l in range(n_a):
        rwkv.append(dict(
            mu=a_mu[l],
            vec=jnp.stack([a_w0[l], a_a0[l], a_k_k[l], a_k_a[l], zrow, zrow, zrow, zrow]),
            scan_vec=jnp.stack([a_r_k[l].reshape(d), a_lnx_g[l], a_lnx_b[l], zrow, zrow, zrow, zrow, zrow]),
            w_rkv=a_w_rkv[l].astype(BF16), w1=a_w1[l].astype(BF16), w2=a_w2[l].astype(BF16),
            a1=a_a1[l].astype(BF16), a2=a_a2[l].astype(BF16), g1=a_g1[l].astype(BF16),
            g2=a_g2[l].astype(BF16), w_o=a_w_o[l].astype(BF16),
            head_down=head_down, head_up=head_down.T))
    ln = [[jnp.stack([ln_g[l, s], ln_b[l, s], zrow, zrow, zrow, zrow, zrow, zrow])
           for s in range(2)] for l in range(depth)]
    pad = LANES - N_GROUPS - N_EXPERTS
    router_w = [jnp.concatenate([rg_w[l], re_w[l], jnp.zeros((d, pad), F32)], 1) for l in range(depth)]
    router_b = [jnp.concatenate([rg_b[l], re_b[l], jnp.zeros((pad,), F32)])[None, :] for l in range(depth)]
    return dict(
        n_a=n_a, alpha=alpha, rwkv=rwkv, ln=ln, router_w=router_w, router_b=router_b,
        kv_w=kv_w.astype(BF16),
        f_w=jnp.concatenate([f_w, jnp.zeros((d, LANES - nh), F32)], 1),
        f_b=jnp.concatenate([f_b, jnp.zeros((LANES - nh,), F32)])[None, :],
        b_wq=b_wq.astype(BF16), b_wo=b_wo.astype(BF16),
        exp_wg=exp_wg, exp_wu=exp_wu, exp_wd=exp_wd)


def kernel(x_prompt, x_sample, state_shift, state_wkv, cache_k, cache_v, cache_logf, c_prompt, c_sample,
           ada_w, ada_b, ln_g, ln_b, a_mu, a_w_rkv, a_w0, a_w1, a_w2, a_a0, a_a1, a_a2, a_g1, a_g2, a_k_k,
           a_k_a, a_r_k, a_lnx_g, a_lnx_b, a_w_o, kv_w, f_w, f_b, b_wq, b_wo, rg_w, rg_b, re_w, re_b,
           exp_wg, exp_wu, exp_wd):
    w = _prepare_weights(ln_g, ln_b, a_mu, a_w_rkv, a_w0, a_w1, a_w2, a_a0, a_a1, a_a2, a_g1, a_g2,
                         a_k_k, a_k_a, a_r_k, a_lnx_g, a_lnx_b, a_w_o, kv_w, f_w, f_b, b_wq, b_wo,
                         rg_w, rg_b, re_w, re_b, exp_wg, exp_wu, exp_wd)
    bp, _, d = x_prompt.shape
    n_a = a_mu.shape[0]
    nh = d // HEAD_DIM
    mods = _ada_mods(jnp.concatenate([c_prompt, c_sample], 0), ada_w, ada_b)
    zero_shift = jnp.zeros((n_a, bp, d), x_prompt.dtype)
    zero_wkv = jnp.zeros((n_a, bp, nh, HEAD_DIM, HEAD_DIM), x_prompt.dtype)
    outs_p = _trunk(x_prompt, mods[:, :bp], zero_shift, zero_wkv, None, w)
    outs_s = _trunk(x_sample, mods[:, bp:], state_shift, state_wkv, (cache_k, cache_v, cache_logf), w)
    y_p, p_shift, p_wkv, p_k, p_v, p_logf = outs_p
    y_s, s_shift, s_wkv, s_k, s_v, s_logf = outs_s
    return (y_p, y_s, p_shift, p_wkv, p_k, p_v, p_logf, s_shift, s_wkv, s_k, s_v, s_logf)
```

```python
import functools

import jax
import jax.numpy as jnp
import numpy as np
from jax import lax
from jax.experimental import pallas as pl
from jax.experimental.pallas import tpu as pltpu

F32 = jnp.float32
BF16 = jnp.bfloat16

HEAD_DIM = 64
LANES = 128
N_GROUPS = 4
EXP_PER_GROUP = 4
N_EXPERTS = N_GROUPS * EXP_PER_GROUP
LN_EPS = 1e-5
GN_EPS = 64e-5
WKV_CHUNK = 64
WKV_CHUNKS_PER_STEP = 8
MOE_EXPERTS_PER_STEP = 2
MOE_TILE = 1024
ROUTER_ROWS = 32
LOG2E = 1.4426950408889634
TERM_STRIDE = 32
VMEM_LIMIT = 56 * 1024 * 1024
NEG_BIG = -1e30


def _params(sem):
    return pltpu.CompilerParams(dimension_semantics=sem, vmem_limit_bytes=VMEM_LIMIT)


def _dg(a, b, ca, cb):
    return lax.dot_general(a, b, (((ca,), (cb,)), ((), ())), preferred_element_type=F32)


def _bdot(a, b, ca=1, cb=0):
    return _dg(a.astype(BF16), b.astype(BF16), ca, cb)


def _split3(x):
    hi = x.astype(BF16)
    r1 = x - hi.astype(F32)
    mid = r1.astype(BF16)
    lo = (r1 - mid.astype(F32)).astype(BF16)
    return hi, mid, lo


def _dot_exact_rhs(a, b_exact, ca=1, cb=0):
    hi, mid, lo = _split3(a)
    bb = b_exact.astype(BF16)
    return _dg(hi, bb, ca, cb) + _dg(mid, bb, ca, cb) + _dg(lo, bb, ca, cb)


def _dot_exact_lhs(a_exact, b, ca=1, cb=0):
    hi, mid, lo = _split3(b)
    aa = a_exact.astype(BF16)
    return _dg(aa, hi, ca, cb) + _dg(aa, mid, ca, cb) + _dg(aa, lo, ca, cb)


def _dot3(a, b, ca=1, cb=0):
    ah = a.astype(BF16)
    al = (a - ah.astype(F32)).astype(BF16)
    bh = b.astype(BF16)
    bl = (b - bh.astype(F32)).astype(BF16)
    return _dg(ah, bh, ca, cb) + _dg(ah, bl, ca, cb) + _dg(al, bh, ca, cb)


_inv_dot = _bdot


def _layer_norm(z, g, b):
    mu = jnp.mean(z, -1, keepdims=True)
    d = z - mu
    var = jnp.mean(d * d, -1, keepdims=True)
    return d * lax.rsqrt(var + LN_EPS) * g + b


def _softplus(z):
    return jnp.maximum(z, 0.0) + jnp.log(1.0 + jnp.exp(-jnp.abs(z)))


def _sigmoid(z):
    return 1.0 / (1.0 + jnp.exp(-z))


def _silu(z):
    return z * _sigmoid(z)


def _ada_kernel(c_ref, w_ref, b_ref, o_ref):
    o_ref[...] = _dot3(_silu(c_ref[...]), w_ref[...]) + b_ref[...]


def _ada_mods(c_all, ada_w, ada_b):
    depth, d, d6 = ada_w.shape
    bsz = c_all.shape[0]
    tn = d
    out = pl.pallas_call(
        _ada_kernel,
        grid=(depth, d6 // tn),
        in_specs=[
            pl.BlockSpec((bsz, d), lambda l, j: (0, 0)),
            pl.BlockSpec((None, d, tn), lambda l, j: (l, 0, j)),
            pl.BlockSpec((None, 1, tn), lambda l, j: (l, 0, j)),
        ],
        out_specs=pl.BlockSpec((None, bsz, tn), lambda l, j: (l, 0, j)),
        out_shape=jax.ShapeDtypeStruct((depth, bsz, d6), F32),
        compiler_params=_params(("parallel", "parallel")),
        name="ada_mods",
    )(c_all, ada_w, ada_b.reshape(depth, 1, d6))
    return out.reshape(depth, bsz, 6, d)


def _rwkv_proj_kernel(x_ref, xp_ref, shift_ref, mods_ref, mu_ref, vec_ref, wrkv_ref, w1_ref, w2_ref,
                      a1_ref, a2_ref, g1_ref, g2_ref, hd_ref, hu_ref,
                      r_ref, lw_ref, k_ref, v_ref, kk_ref, b_ref, g_ref, last_ref):
    tm = x_ref.shape[0]
    sh1 = mods_ref[0:1, :]
    sc1 = mods_ref[1:2, :]
    h = x_ref[...] * (1.0 + sc1) + sh1
    h_prev_tile = xp_ref[7:8, :] * (1.0 + sc1) + sh1
    prev_row = jnp.where(pl.program_id(1) == 0, shift_ref[...], h_prev_tile)
    row = lax.broadcasted_iota(jnp.int32, (tm, 1), 0)
    xx = jnp.where(row == 0, prev_row, pltpu.roll(h, 1, axis=0)) - h

    def mix(i):
        return h + xx * mu_ref[i:i + 1, :]

    w0, a0 = vec_ref[0:1, :], vec_ref[1:2, :]
    k_k, k_a = vec_ref[2:3, :], vec_ref[3:4, :]
    r = _bdot(mix(0), wrkv_ref[0])
    k = _bdot(mix(1), wrkv_ref[1])
    v = _bdot(mix(2), wrkv_ref[2])
    ww = w0 + _bdot(jnp.tanh(_bdot(mix(3), w1_ref[...])), w2_ref[...])
    w_log = -_softplus(-ww) - 0.5
    a = _sigmoid(a0 + _bdot(_bdot(mix(4), a1_ref[...]), a2_ref[...]))
    g = _bdot(_sigmoid(_bdot(mix(5), g1_ref[...])), g2_ref[...])
    kk = k * k_k
    head_sq = _bdot(kk * kk, hd_ref[...])
    hi = head_sq.astype(BF16)
    lo = (head_sq - hi.astype(F32)).astype(BF16)
    ss = _dg(hi, hu_ref[...], 1, 0) + _dg(lo, hu_ref[...], 1, 0)
    kk = kk * lax.rsqrt(jnp.maximum(ss, 1e-24))
    r_ref[...] = r.astype(BF16)
    lw_ref[...] = -jnp.exp(w_log)
    k_ref[...] = (k * (1.0 + (a - 1.0) * k_a)).astype(BF16)
    v_ref[...] = v.astype(BF16)
    kk_ref[...] = kk.astype(BF16)
    b_ref[...] = (kk * a).astype(BF16)
    g_ref[...] = g.astype(BF16)
    last_ref[...] = h[tm - 1:tm, :]


def _rwkv_proj(x2d, shift_prev, mods_l, wl, bsz, t, tm):
    n, d = x2d.shape
    nt = t // tm
    tok = pl.BlockSpec((tm, d), lambda b, i: (b * nt + i, 0))
    full = lambda a: pl.BlockSpec(a.shape, lambda b, i: (0,) * a.ndim)
    weights = [wl["mu"], wl["vec"], wl["w_rkv"], wl["w1"], wl["w2"], wl["a1"], wl["a2"], wl["g1"],
               wl["g2"], wl["head_down"], wl["head_up"]]
    outs = pl.pallas_call(
        _rwkv_proj_kernel,
        grid=(bsz, nt),
        in_specs=[
            tok,
            pl.BlockSpec((8, d), lambda b, i: (jnp.maximum((b * nt + i) * (tm // 8) - 1, 0), 0)),
            pl.BlockSpec((None, 1, d), lambda b, i: (b, 0, 0)),
            pl.BlockSpec((None, 6, d), lambda b, i: (b, 0, 0)),
        ] + [full(a) for a in weights],
        out_specs=[tok] * 7 + [pl.BlockSpec((None, 1, d), lambda b, i: (b, 0, 0))],
        out_shape=[jax.ShapeDtypeStruct((n, d), F32 if i == 1 else BF16) for i in range(7)]
        + [jax.ShapeDtypeStruct((bsz, 1, d), F32)],
        compiler_params=_params(("parallel", "arbitrary")),
        name="rwkv_proj",
    )(x2d, x2d, shift_prev.reshape(bsz, 1, d), mods_l, *weights)
    return outs[:7], outs[7].reshape(bsz, d)


def _wkv_kernel(c, r_ref, lw_ref, k_ref, v_ref, kk_ref, b_ref, s0_ref, vec_ref, tri_ref, ones_ref,
                y_ref, s_out_ref, state_ref):
    n_chunks = r_ref.shape[0] // c
    n_pairs = r_ref.shape[1] // LANES
    first = pl.program_id(1) == 0

    @pl.when(first)
    def _():
        state_ref[...] = s0_ref[...]

    lane = lax.broadcasted_iota(jnp.int32, (c, LANES), 1)
    low = lane < HEAD_DIM
    ri = lax.broadcasted_iota(jnp.int32, (2 * c, 2 * c), 0)
    ci = lax.broadcasted_iota(jnp.int32, (2 * c, 2 * c), 1)
    same = (ri >= c) == (ci >= c)
    strict = same & (ci < ri)
    incl = same & (ci <= ri)
    eye = (ri == ci).astype(F32)
    tri = tri_ref[...]
    ones_bd = ones_ref[...]

    def stack(z):
        return jnp.concatenate([jnp.where(low, z, 0.0), jnp.where(low, 0.0, z)], axis=0)

    pairs = range(n_pairs)
    slabs = [slice(p * LANES, (p + 1) * LANES) for p in pairs]

    def rows(z):
        return jnp.concatenate([z[:, s] for s in slabs], 0)

    def lanes(z):
        return jnp.concatenate([z[p * c:(p + 1) * c] for p in pairs], 1)

    def prepare(ch):
        rs = slice(ch * c, (ch + 1) * c)
        return _wkv_prepare(c, lw_ref[rs, :], r_ref[rs, :], k_ref[rs, :], v_ref[rs, :], kk_ref[rs, :],
                            b_ref[rs, :], vec_ref, tri, ones_bd, stack, rows, lanes, slabs, strict, incl, eye)

    s_prev = [state_ref[p] for p in pairs]
    ready = prepare(0)
    for ch in range(n_chunks):
        upcoming = prepare(ch + 1) if ch + 1 < n_chunks else None
        s_prev, y_ref[ch * c:(ch + 1) * c, :] = _wkv_apply(c, ready, s_prev, vec_ref, ones_bd, lanes, slabs)
        ready = upcoming
    state_ref[...] = jnp.stack(s_prev)

    @pl.when(pl.program_id(1) == pl.num_programs(1) - 1)
    def _():
        s_out_ref[...] = state_ref[...]


def _wkv_prepare(c, lw_all, r_bf, k_bf, v_bf, kk_bf, b_bf, vec_ref, tri, ones_bd, stack, rows, lanes,
                 slabs, strict, incl, eye):
    cum_all = _dot_exact_lhs(tri, lw_all)
    prev_all = cum_all - lw_all
    mid_all = cum_all[c // 2 - 1:c // 2, :]
    end_all = cum_all[c - 1:c, :]
    e_in = jnp.exp(mid_all - cum_all)
    e_out = jnp.exp(end_all - cum_all)
    e_end = jnp.exp(end_all)
    r_all, k_all, v_all = r_bf.astype(F32), k_bf.astype(F32), v_bf.astype(F32)
    a_all, b_all = -kk_bf.astype(F32), b_bf.astype(F32)
    aq, rq = a_all * jnp.exp(prev_all - mid_all), r_all * jnp.exp(cum_all - mid_all)
    bi, ki = b_all * e_in, k_all * e_in
    a0, r0 = a_all * jnp.exp(prev_all), r_all * jnp.exp(cum_all)
    bo, ko = b_all * e_out, k_all * e_out

    gram = [_bdot(jnp.concatenate([stack(aq[:, s]), stack(rq[:, s])], 0),
                  jnp.concatenate([stack(bi[:, s]), stack(ki[:, s])], 0), 1, 1) for s in slabs]
    a_ab = [jnp.where(strict, g[:2 * c, :2 * c], 0.0) for g in gram]
    a_ak = [jnp.where(strict, g[:2 * c, 2 * c:], 0.0) for g in gram]
    a_r = [jnp.concatenate([jnp.where(incl, g[2 * c:, :2 * c], 0.0),
                            jnp.where(incl, g[2 * c:, 2 * c:], 0.0)], 1) for g in gram]
    tinv = [eye + m for m in a_ab]
    pw = a_ab
    for _ in range(max(c.bit_length() - 2, 0)):
        pw = [_inv_dot(m, m) for m in pw]
        tinv = [t_ + _inv_dot(t_, m) for t_, m in zip(tinv, pw)]
    v_st = [stack(v_all[:, s]) for s in slabs]
    return dict(
        state_lhs=[jnp.concatenate([_inv_dot(t_, stack(a0[:, s])), stack(r0[:, s])], 0).astype(BF16)
                   for t_, s in zip(tinv, slabs)],
        u_free=[_inv_dot(t_, _bdot(m, vs)) for t_, m, vs in zip(tinv, a_ak, v_st)],
        a_r=[m.astype(BF16) for m in a_r],
        v_st=[vs.astype(BF16) for vs in v_st],
        decay_rows=[jnp.concatenate([stack(bo[:, s]), stack(ko[:, s])], 0).astype(BF16) for s in slabs],
        e_end=e_end,
        bonus=lanes(_bdot(rows(r_all * k_all * vec_ref[0:1, :]), ones_bd)) * v_all)


def _wkv_apply(c, prep, s_prev, vec_ref, ones_bd, lanes, slabs):
    from_state = [_bdot(lhs, sp, 1, 1) for lhs, sp in zip(prep["state_lhs"], s_prev)]
    uv = [jnp.concatenate([(fs[:2 * c] + uf).astype(BF16), vs], 0)
          for fs, uf, vs in zip(from_state, prep["u_free"], prep["v_st"])]
    o_st = [fs[2 * c:] + _dg(m, x, 1, 0) for fs, m, x in zip(from_state, prep["a_r"], uv)]
    o = [x[:c] + x[c:] for x in o_st]
    new_state = [sp * prep["e_end"][:, s] + _dg(x, rows_e, 0, 0)
                 for s, sp, x, rows_e in zip(slabs, s_prev, uv, prep["decay_rows"])]
    o_rows = jnp.concatenate(o, 0)
    dev = o_rows - _bdot(o_rows, ones_bd) * (1.0 / HEAD_DIM)
    var = _bdot(dev * dev, ones_bd) * (1.0 / HEAD_DIM)
    y = lanes(dev * lax.rsqrt(var + GN_EPS)) * vec_ref[1:2, :] + vec_ref[2:3, :] + prep["bonus"]
    return new_state, y


def _pair_states(s):
    bsz, nh, n, _ = s.shape
    s = s.reshape(bsz, nh // 2, 2, n, n)
    z = jnp.zeros_like(s[:, :, 0])
    top = jnp.concatenate([s[:, :, 0], z], -1)
    bot = jnp.concatenate([z, s[:, :, 1]], -1)
    return jnp.concatenate([top, bot], -2)


def _unpair_states(sp):
    bsz, npair, _, _ = sp.shape
    n = HEAD_DIM
    return jnp.stack([sp[:, :, :n, :n], sp[:, :, n:, n:]], 2).reshape(bsz, 2 * npair, n, n)


def _wkv(proj, s0, wl, bsz, t):
    r, lw, k, v, kk, b = proj
    n, d = r.shape
    c = min(WKV_CHUNK, t)
    rows_per_step = min(WKV_CHUNKS_PER_STEP * c, t)
    nc = t // rows_per_step
    npair = d // LANES
    tok = pl.BlockSpec((rows_per_step, d), lambda bb, i: (bb * nc + i, 0))
    st = pl.BlockSpec((None, npair, LANES, LANES), lambda bb, i: (bb, 0, 0, 0))
    tri = jnp.tril(jnp.ones((c, c), BF16))
    hid = jnp.arange(LANES) // HEAD_DIM
    ones_bd = (hid[:, None] == hid[None, :]).astype(BF16)
    y, s_out = pl.pallas_call(
        functools.partial(_wkv_kernel, c),
        grid=(bsz, nc),
        in_specs=[tok] * 6 + [st,
                              pl.BlockSpec((8, d), lambda bb, i: (0, 0)),
                              pl.BlockSpec((c, c), lambda bb, i: (0, 0)),
                              pl.BlockSpec((LANES, LANES), lambda bb, i: (0, 0))],
        out_specs=[tok, st],
        out_shape=[jax.ShapeDtypeStruct((n, d), F32),
                   jax.ShapeDtypeStruct((bsz, npair, LANES, LANES), F32)],
        scratch_shapes=[pltpu.VMEM((npair, LANES, LANES), F32)],
        compiler_params=_params(("parallel", "arbitrary")),
        name="wkv_scan",
    )(r, lw, k, v, kk, b, _pair_states(s0.astype(F32)), wl["scan_vec"], tri, ones_bd)
    return y, _unpair_states(s_out)


def _router(logits):
    row_i = lax.broadcasted_iota(jnp.int32, logits.shape, 0)
    row = row_i.astype(F32)
    far = 1e9
    is_g = row_i < N_GROUPS
    gl = jnp.where(is_g, logits, NEG_BIG)
    gmax = jnp.max(gl, 0, keepdims=True)
    gsel = jnp.min(jnp.where(gl == gmax, row, far), 0, keepdims=True)
    gprob = 1.0 / jnp.sum(jnp.where(is_g, jnp.exp(gl - gmax), 0.0), 0, keepdims=True)
    group_of = lax.shift_right_arithmetic(row_i - N_GROUPS, 2).astype(F32)
    in_group = (row_i >= N_GROUPS) & (row_i < N_GROUPS + N_EXPERTS) & (group_of == gsel)
    el = jnp.where(in_group, logits, NEG_BIG)
    v1 = jnp.max(el, 0, keepdims=True)
    i1 = jnp.min(jnp.where(el == v1, row, far), 0, keepdims=True)
    el2 = jnp.where(row == i1, NEG_BIG, el)
    v2 = jnp.max(el2, 0, keepdims=True)
    i2 = jnp.min(jnp.where(el2 == v2, row, far), 0, keepdims=True)
    e2 = jnp.exp(v2 - v1)
    w1 = gprob / (1.0 + e2)
    w2 = gprob * e2 / (1.0 + e2)
    return jnp.where(row == i1, w1, 0.0) + jnp.where(row == i2, w2, 0.0)


def _post_kernel(gated, alpha, *refs):
    if gated:
        x_ref, y_ref, g_ref, mods_ref, wo_ref, ln_ref, rw_ref, rb_ref, x1_ref, comb_ref = refs
        y = y_ref[...] * g_ref[...].astype(F32)
    else:
        x_ref, y_ref, mods_ref, wo_ref, ln_ref, rw_ref, rb_ref, x1_ref, comb_ref = refs
        y = y_ref[...]
    gt1, sh2, sc2 = mods_ref[2:3, :], mods_ref[3:4, :], mods_ref[4:5, :]
    x1 = _layer_norm(alpha * x_ref[...] + gt1 * _bdot(y, wo_ref[...]), ln_ref[0:1, :], ln_ref[1:2, :])
    x1_ref[...] = x1
    h2 = x1 * (1.0 + sc2) + sh2
    comb_t = _router(_dot3(rw_ref[...], h2, 1, 1) + rb_ref[...])
    pad_rows = jnp.zeros((LANES - comb_t.shape[0], comb_t.shape[1]), F32)
    comb_ref[...] = jnp.concatenate([comb_t, pad_rows], 0).T


def _post_mixer(x2d, y2d, g2d, mods_l, w_o, ln_pack, router_w, router_b, alpha, bsz, t, tm):
    n, d = x2d.shape
    nt = t // tm
    tok = pl.BlockSpec((tm, d), lambda b, i: (b * nt + i, 0))
    full = lambda a: pl.BlockSpec(a.shape, lambda b, i: (0,) * a.ndim)
    gated = g2d is not None
    acts = [x2d, y2d] + ([g2d] if gated else [])
    consts = [w_o, ln_pack, router_w, router_b]
    return pl.pallas_call(
        functools.partial(_post_kernel, gated, alpha),
        grid=(bsz, nt),
        in_specs=[tok] * len(acts) + [pl.BlockSpec((None, 6, d), lambda b, i: (b, 0, 0))]
        + [full(a) for a in consts],
        out_specs=[tok, pl.BlockSpec((tm, LANES), lambda b, i: (b * nt + i, 0))],
        out_shape=[jax.ShapeDtypeStruct((n, d), F32), jax.ShapeDtypeStruct((n, LANES), F32)],
        compiler_params=_params(("parallel", "parallel")),
        name="post_mixer",
    )(*acts, mods_l, *consts)


def _moe_kernel(alpha, x1_ref, comb_ref, mods_ref, ln_ref, wg_ref, wu_ref, wd_ref, x2_ref, h2_ref, acc_ref):
    e = pl.program_id(1)

    @pl.when(e == 0)
    def _():
        sh2, sc2 = mods_ref[3], mods_ref[4]
        h2_ref[...] = (x1_ref[...] * (1.0 + sc2) + sh2).astype(BF16)
        acc_ref[...] = jnp.zeros_like(acc_ref)

    h2 = h2_ref[...]
    per_step, f, d = wd_ref.shape
    lane = lax.broadcasted_iota(jnp.int32, comb_ref.shape, 1)
    comb = comb_ref[...]
    hidden = []
    for i in range(per_step):
        hg = _dg(h2, wg_ref[i].astype(BF16), 1, 0)
        hu = _dg(h2, wu_ref[i].astype(BF16), 1, 0)
        ce = jnp.sum(jnp.where(lane == e * per_step + (i + N_GROUPS), comb, 0.0), -1, keepdims=True)
        hidden.append((_silu(hg) * hu * ce).astype(BF16))
    acc_ref[...] += _dg(jnp.concatenate(hidden, 1), wd_ref[...].reshape(per_step * f, d).astype(BF16), 1, 0)

    @pl.when(e == pl.num_programs(1) - 1)
    def _():
        gt2 = mods_ref[5]
        x2_ref[...] = _layer_norm(alpha * x1_ref[...] + gt2 * acc_ref[...], ln_ref[0:1, :], ln_ref[1:2, :])


def _moe(x1, comb, mods_l, ln_pack, wg, wu, wd, layer, alpha, bsz, t):
    n, d = x1.shape
    _, ne, _, f = wg.shape
    per_step = MOE_EXPERTS_PER_STEP
    if t >= MOE_TILE:
        tm = MOE_TILE
        mods = mods_l.reshape(bsz, 6, 1, d)
        mods_spec = pl.BlockSpec((None, 6, 1, d), lambda i, e: (i // (t // tm), 0, 0, 0))
    else:
        tm = min(n, MOE_TILE)
        mods = jnp.repeat(mods_l.transpose(1, 0, 2), t, axis=1)
        mods_spec = pl.BlockSpec((6, tm, d), lambda i, e: (0, i, 0))
    return pl.pallas_call(
        functools.partial(_moe_kernel, alpha),
        grid=(n // tm, ne // per_step),
        in_specs=[
            pl.BlockSpec((tm, d), lambda i, e: (i, 0)),
            pl.BlockSpec((tm, LANES), lambda i, e: (i, 0)),
            mods_spec,
            pl.BlockSpec(ln_pack.shape, lambda i, e: (0, 0)),
            pl.BlockSpec((None, per_step, d, f), lambda i, e: (layer, e, 0, 0)),
            pl.BlockSpec((None, per_step, d, f), lambda i, e: (layer, e, 0, 0)),
            pl.BlockSpec((None, per_step, f, d), lambda i, e: (layer, e, 0, 0)),
        ],
        out_specs=pl.BlockSpec((tm, d), lambda i, e: (i, 0)),
        out_shape=jax.ShapeDtypeStruct((n, d), F32),
        scratch_shapes=[pltpu.VMEM((tm, d), BF16), pltpu.VMEM((tm, d), F32)],
        compiler_params=_params(("parallel", "arbitrary")),
        name="hmoe",
    )(x1, comb, mods, ln_pack, wg, wu, wd)


def _fox_bias_placement(d):
    place_q = np.zeros((LANES, d), np.float32)
    place_k = np.zeros((LANES, d), np.float32)
    const_q = np.zeros((1, d), np.float32)
    const_k = np.zeros((1, d), np.float32)
    for h in range(d // HEAD_DIM):
        base = (h // 2) * LANES + (HEAD_DIM if h % 2 == 0 else 0)
        for term in range(3):
            place_q[term * TERM_STRIDE + h, base + term] = 1.0
            place_k[term * TERM_STRIDE + h, base + 3 + term] = -1.0
            const_q[0, base + 3 + term] = 1.0
            const_k[0, base + term] = 1.0
    return (jnp.asarray(place_q, BF16), jnp.asarray(place_k, BF16), jnp.asarray(const_q),
            jnp.asarray(const_k))


def _kv_kernel(x_ref, kvw_ref, fw_ref, fb_ref, tri_ref, pq_ref, pk_ref, cq_ref, ck_ref,
               k_ref, v_ref, kb_ref, vb_ref, lf_ref, fc_ref, qbias_ref, kbias_ref, vbt_ref, carry_ref):
    d = x_ref.shape[1]

    @pl.when(pl.program_id(1) == 0)
    def _():
        carry_ref[...] = jnp.zeros_like(carry_ref)

    x = x_ref[...]
    kv = _bdot(x, kvw_ref[...])
    k, v = kv[:, :d], kv[:, d:]
    k_ref[...] = k.T
    v_t = v.T
    v_ref[...] = v_t
    vbt_ref[...] = v_t.astype(BF16)
    kb_ref[...] = k.astype(BF16)
    vb_ref[...] = v.astype(BF16)
    z = _dot3(x, fw_ref[...]) + fb_ref[...]
    lf = -_softplus(-z)
    lf_ref[...] = lf
    fc = _dot_exact_lhs(tri_ref[...], lf) + carry_ref[...]
    fc_ref[...] = fc
    carry_ref[...] = fc[fc.shape[0] - 1:, :]
    lane = lax.broadcasted_iota(jnp.int32, fc.shape, 1)
    hi, mid, lo = _split3(jnp.where(lane < d // HEAD_DIM, fc * LOG2E, 0.0))
    terms = (hi.astype(F32) + pltpu.roll(mid.astype(F32), TERM_STRIDE, axis=1)
             + pltpu.roll(lo.astype(F32), 2 * TERM_STRIDE, axis=1)).astype(BF16)
    qbias_ref[...] = (_dg(terms, pq_ref[...], 1, 0) + cq_ref[...]).astype(BF16)
    kbias_ref[...] = (_dg(terms, pk_ref[...], 1, 0) + ck_ref[...]).astype(BF16)


def _kv_proj(x2d, kv_w, f_w, f_b, bsz, t, tm):
    n, d = x2d.shape
    nt = t // tm
    tok = pl.BlockSpec((tm, d), lambda b, i: (b * nt + i, 0))
    nar = pl.BlockSpec((tm, LANES), lambda b, i: (b * nt + i, 0))
    feature_major = pl.BlockSpec((None, d, tm), lambda b, i: (b, 0, i))
    tri = jnp.tril(jnp.ones((tm, tm), BF16))
    full = lambda a: pl.BlockSpec(a.shape, lambda b, i: (0,) * a.ndim)
    consts = [kv_w, f_w, f_b, tri, *_fox_bias_placement(d)]
    return pl.pallas_call(
        _kv_kernel,
        grid=(bsz, nt),
        in_specs=[tok] + [full(a) for a in consts],
        out_specs=[feature_major, feature_major, tok, tok, nar, nar, tok, tok, feature_major],
        out_shape=[jax.ShapeDtypeStruct((bsz, d, t), F32), jax.ShapeDtypeStruct((bsz, d, t), F32),
                   jax.ShapeDtypeStruct((n, d), BF16), jax.ShapeDtypeStruct((n, d), BF16),
                   jax.ShapeDtypeStruct((n, LANES), F32), jax.ShapeDtypeStruct((n, LANES), F32),
                   jax.ShapeDtypeStruct((n, d), BF16), jax.ShapeDtypeStruct((n, d), BF16),
                   jax.ShapeDtypeStruct((bsz, d, t), BF16)],
        scratch_shapes=[pltpu.VMEM((1, LANES), F32)],
        compiler_params=_params(("parallel", "arbitrary")),
        name="kv_proj",
    )(x2d, *consts)


def _q_kernel(scale, x_ref, mods_ref, wq_ref, q_ref):
    h = x_ref[...] * (1.0 + mods_ref[1:2, :]) + mods_ref[0:1, :]
    q_ref[...] = (_bdot(h, wq_ref[...]) * scale).astype(BF16)


def _q_proj(x2d, mods_l, wq, scale, bsz, t, tm):
    n, d = x2d.shape
    nt = t // tm
    tok = pl.BlockSpec((tm, d), lambda b, i: (b * nt + i, 0))
    return pl.pallas_call(
        functools.partial(_q_kernel, scale),
        grid=(bsz, nt),
        in_specs=[tok, pl.BlockSpec((None, 6, d), lambda b, i: (b, 0, 0)),
                  pl.BlockSpec(wq.shape, lambda b, i: (0, 0))],
        out_specs=tok,
        out_shape=jax.ShapeDtypeStruct((n, d), BF16),
        compiler_params=_params(("parallel", "parallel")),
        name="q_proj",
    )(x2d, mods_l, wq)


def _fox_prompt_kernel(tq, q_ref, k_ref, vt_ref, qb_ref, kb_ref, o_ref):
    t = q_ref.shape[0]
    lane = lax.broadcasted_iota(jnp.int32, (1, LANES), 1)
    key = lax.broadcasted_iota(jnp.int32, (tq, tq), 0)
    qry = lax.broadcasted_iota(jnp.int32, (tq, tq), 1)
    own = [lane < HEAD_DIM, lane >= HEAD_DIM]
    k_aug = [jnp.where(own[hh], k_ref[...], kb_ref[...]) for hh in range(2)]
    v_t = vt_ref[...]
    for qi in range(t // tq):
        lo, hi = qi * tq, (qi + 1) * tq
        heads = []
        for hh in range(2):
            q = jnp.where(own[hh], q_ref[lo:hi, :], qb_ref[lo:hi, :])
            v_h = v_t[hh * HEAD_DIM:(hh + 1) * HEAD_DIM]
            s_diag = jnp.where(key <= qry, _dg(k_aug[hh][lo:hi], q, 1, 1), NEG_BIG)
            m = jnp.max(s_diag, 0, keepdims=True)
            if qi > 0:
                s_past = _dg(k_aug[hh][:lo], q, 1, 1)
                m = jnp.maximum(m, jnp.max(s_past, 0, keepdims=True))
            p = jnp.exp2(s_diag - m)
            num = _dg(v_h[:, lo:hi], p.astype(BF16), 1, 0)
            den = jnp.sum(p, 0, keepdims=True)
            if qi > 0:
                p = jnp.exp2(s_past - m)
                num = num + _dg(v_h[:, :lo], p.astype(BF16), 1, 0)
                den = den + jnp.sum(p, 0, keepdims=True)
            heads.append(num / den)
        o_ref[lo:hi, :] = jnp.concatenate(heads, 0).T


def _fox_prompt(q, kb, vb_t, q_bias, k_bias, bsz, t, tq):
    n, d = q.shape
    seq = pl.BlockSpec((t, LANES), lambda b, p: (b, p))
    return pl.pallas_call(
        functools.partial(_fox_prompt_kernel, tq),
        grid=(bsz, d // LANES),
        in_specs=[seq, seq, pl.BlockSpec((None, LANES, t), lambda b, p: (b, p, 0)), seq, seq],
        out_specs=seq,
        out_shape=jax.ShapeDtypeStruct((n, d), F32),
        compiler_params=_params(("parallel", "parallel")),
        name="fox_prompt",
    )(q, kb, vb_t, q_bias, k_bias)


def _fox_sample_kernel(q_ref, kn_ref, vn_ref, fn_ref, fnt_ref, ck_ref, cv_ref, clf_ref, upper_ref,
                       o_ref, m_ref, l_ref, acc_ref, carry_ref):
    step = pl.program_id(1)
    t, d = q_ref.shape
    nh = d // HEAD_DIM

    @pl.when(step == 0)
    def _():
        m_ref[...] = jnp.full_like(m_ref, NEG_BIG)
        l_ref[...] = jnp.zeros_like(l_ref)
        acc_ref[...] = jnp.zeros_like(acc_ref)
        carry_ref[...] = jnp.zeros_like(carry_ref)

    heads = range(nh)
    cols = [slice(h * HEAD_DIM, (h + 1) * HEAD_DIM) for h in heads]
    q = [q_ref[:, c] for c in cols]

    def attend(keys, values, feature_major, bias, mask):
        s = [_dg(q[h], keys[h], 1, 0 if feature_major else 1) + bias[h] for h in heads]
        if mask is not None:
            s = [jnp.where(mask, x, NEG_BIG) for x in s]
        m_old = m_ref[...]
        m_new = jnp.maximum(m_old, jnp.stack([jnp.max(x, -1, keepdims=True) for x in s]))
        alpha = jnp.exp(m_old - m_new)
        p = [jnp.exp(x - m_new[h]) for h, x in enumerate(s)]
        l_ref[...] = alpha * l_ref[...] + jnp.stack([jnp.sum(x, -1, keepdims=True) for x in p])
        acc_ref[...] = alpha * acc_ref[...] + jnp.stack(
            [_dg(p[h].astype(BF16), values[h], 1, 1 if feature_major else 0) for h in heads])
        m_ref[...] = m_new

    clf = clf_ref[...]
    suf = _dot_exact_rhs(clf, upper_ref[...]) + carry_ref[...]
    carry_ref[...] = carry_ref[...] + jnp.sum(clf, -1, keepdims=True)
    fn = fn_ref[...]
    attend([ck_ref[h].astype(BF16) for h in heads], [cv_ref[h].astype(BF16) for h in heads], True,
           [fn[:, h:h + 1] + suf[h:h + 1, :] for h in heads], None)

    @pl.when(step == pl.num_programs(1) - 1)
    def _():
        rr = lax.broadcasted_iota(jnp.int32, (t, t), 0)
        cc = lax.broadcasted_iota(jnp.int32, (t, t), 1)
        fnt = fnt_ref[...]
        attend([kn_ref[:, c] for c in cols], [vn_ref[:, c] for c in cols], False,
               [fn[:, h:h + 1] - fnt[h:h + 1, :] for h in heads], cc <= rr)
        out = acc_ref[...] / l_ref[...]
        o_ref[...] = jnp.concatenate([out[h] for h in heads], 1)


def _fox_sample(q, kb, vb, fcum, cache_k, cache_v, cache_logf, bsz, t, tk):
    n, d = q.shape
    nh = d // HEAD_DIM
    plen = cache_logf.shape[1]
    nk = plen // tk
    ck = cache_k.transpose(0, 2, 3, 1)
    cv = cache_v.transpose(0, 2, 3, 1)
    clf_t = cache_logf.astype(F32).transpose(0, 2, 1)
    fn_t = fcum[:, :nh].reshape(bsz, t, nh).transpose(0, 2, 1)
    upper = (jnp.arange(tk)[:, None] > jnp.arange(tk)[None, :]).astype(BF16)
    tok = pl.BlockSpec((t, d), lambda b, j: (b, 0))
    past = pl.BlockSpec((None, nh, HEAD_DIM, tk), lambda b, j: (b, 0, 0, nk - 1 - j))
    return pl.pallas_call(
        _fox_sample_kernel,
        grid=(bsz, nk),
        in_specs=[tok, tok, tok,
                  pl.BlockSpec((t, LANES), lambda b, j: (b, 0)),
                  pl.BlockSpec((None, nh, t), lambda b, j: (b, 0, 0)),
                  past, past,
                  pl.BlockSpec((None, nh, tk), lambda b, j: (b, 0, nk - 1 - j)),
                  pl.BlockSpec((tk, tk), lambda b, j: (0, 0))],
        out_specs=tok,
        out_shape=jax.ShapeDtypeStruct((n, d), F32),
        scratch_shapes=[pltpu.VMEM((nh, t, 1), F32), pltpu.VMEM((nh, t, 1), F32),
                        pltpu.VMEM((nh, t, HEAD_DIM), F32), pltpu.VMEM((nh, 1), F32)],
        compiler_params=_params(("parallel", "arbitrary")),
        name="fox_sample",
    )(q, kb, vb, fcum, fn_t, ck, cv, clf_t, upper)


def _tile(t, cap):
    return min(t, cap)


def _trunk(x, mods, shift_in, wkv_in, cache, w):
    bsz, t, d = x.shape
    depth = mods.shape[0]
    n_a = w["n_a"]
    nh = d // HEAD_DIM
    x2d = x.reshape(bsz * t, d)
    shifts, states = [], []
    kv = None
    for l in range(depth):
        mods_l = mods[l]
        if l < n_a:
            wl = w["rwkv"][l]
            proj, last = _rwkv_proj(x2d, shift_in[l], mods_l, wl, bsz, t, _tile(t, 512))
            y, s_new = _wkv(proj[:6], wkv_in[l], wl, bsz, t)
            shifts.append(last)
            states.append(s_new.astype(wkv_in.dtype))
            mixer_out, gate, w_o = y, proj[6], wl["w_o"]
        else:
            if kv is None:
                kv = _kv_proj(x2d, w["kv_w"], w["f_w"], w["f_b"], bsz, t, _tile(t, 512))
            k_sh, v_sh, kb, vb, lf, fcum, q_bias, k_bias, vb_t = kv
            j = l - n_a
            q_scale = HEAD_DIM ** -0.5 * (LOG2E if cache is None else 1.0)
            q = _q_proj(x2d, mods_l, w["b_wq"][j], q_scale, bsz, t, _tile(t, 1024))
            if cache is None:
                mixer_out = _fox_prompt(q, kb, vb_t, q_bias, k_bias, bsz, t, _tile(t, 512))
            else:
                mixer_out = _fox_sample(q, kb, vb, fcum, cache[0], cache[1], cache[2], bsz, t,
                                        _tile(cache[0].shape[1], 1024))
            gate, w_o = None, w["b_wo"][j]
        x1, comb = _post_mixer(x2d, mixer_out, gate, mods_l, w_o, w["ln"][l][0], w["router_w"][l],
                               w["router_b"][l], w["alpha"], bsz, t, _tile(t, 1024))
        x2d = _moe(x1, comb, mods_l, w["ln"][l][1], w["exp_wg"], w["exp_wu"], w["exp_wd"], l,
                   w["alpha"], bsz, t)
    k_sh, v_sh, lf = kv[0], kv[1], kv[4]
    by_head = lambda z: z.reshape(bsz, nh, HEAD_DIM, t).transpose(0, 3, 1, 2)
    return (x2d.reshape(bsz, t, d), jnp.stack(shifts), jnp.stack(states), by_head(k_sh), by_head(v_sh),
            lf[:, :nh].reshape(bsz, t, nh).astype(x.dtype))


def _prepare_weights(ln_g, ln_b, a_mu, a_w_rkv, a_w0, a_w1, a_w2, a_a0, a_a1, a_a2, a_g1, a_g2, a_k_k,
                     a_k_a, a_r_k, a_lnx_g, a_lnx_b, a_w_o, kv_w, f_w, f_b, b_wq, b_wo, rg_w, rg_b,
                     re_w, re_b, exp_wg, exp_wu, exp_wd):
    depth, _, d = ln_g.shape
    n_a = a_mu.shape[0]
    nh = d // HEAD_DIM
    alpha = (2.0 * depth) ** 0.25
    zrow = jnp.zeros((d,), F32)
    head_of = jnp.arange(d) // HEAD_DIM
    head_down = (head_of[:, None] == jnp.arange(LANES)[None, :]).astype(BF16)
    rwkv = []
    for l in range(n_a):
        rwkv.append(dict(
            mu=a_mu[l],
            vec=jnp.stack([a_w0[l], a_a0[l], a_k_k[l], a_k_a[l], zrow, zrow, zrow, zrow]),
            scan_vec=jnp.stack([a_r_k[l].reshape(d), a_lnx_g[l], a_lnx_b[l], zrow, zrow, zrow, zrow, zrow]),
            w_rkv=a_w_rkv[l].astype(BF16), w1=a_w1[l].astype(BF16), w2=a_w2[l].astype(BF16),
            a1=a_a1[l].astype(BF16), a2=a_a2[l].astype(BF16), g1=a_g1[l].astype(BF16),
            g2=a_g2[l].astype(BF16), w_o=a_w_o[l].astype(BF16),
            head_down=head_down, head_up=head_down.T))
    ln = [[jnp.stack([ln_g[l, s], ln_b[l, s], zrow, zrow, zrow, zrow, zrow, zrow])
           for s in range(2)] for l in range(depth)]
    pad = ROUTER_ROWS - N_GROUPS - N_EXPERTS
    router_w = [jnp.concatenate([rg_w[l], re_w[l], jnp.zeros((d, pad), F32)], 1).T for l in range(depth)]
    router_b = [jnp.concatenate([rg_b[l], re_b[l], jnp.zeros((pad,), F32)])[:, None] for l in range(depth)]
    return dict(
        n_a=n_a, alpha=alpha, rwkv=rwkv, ln=ln, router_w=router_w, router_b=router_b,
        kv_w=kv_w.astype(BF16),
        f_w=jnp.concatenate([f_w, jnp.zeros((d, LANES - nh), F32)], 1),
        f_b=jnp.concatenate([f_b, jnp.zeros((LANES - nh,), F32)])[None, :],
        b_wq=b_wq.astype(BF16), b_wo=b_wo.astype(BF16),
        exp_wg=exp_wg, exp_wu=exp_wu, exp_wd=exp_wd)


def kernel(x_prompt, x_sample, state_shift, state_wkv, cache_k, cache_v, cache_logf, c_prompt, c_sample,
           ada_w, ada_b, ln_g, ln_b, a_mu, a_w_rkv, a_w0, a_w1, a_w2, a_a0, a_a1, a_a2, a_g1, a_g2, a_k_k,
           a_k_a, a_r_k, a_lnx_g, a_lnx_b, a_w_o, kv_w, f_w, f_b, b_wq, b_wo, rg_w, rg_b, re_w, re_b,
           exp_wg, exp_wu, exp_wd):
    w = _prepare_weights(ln_g, ln_b, a_mu, a_w_rkv, a_w0, a_w1, a_w2, a_a0, a_a1, a_a2, a_g1, a_g2,
                         a_k_k, a_k_a, a_r_k, a_lnx_g, a_lnx_b, a_w_o, kv_w, f_w, f_b, b_wq, b_wo,
                         rg_w, rg_b, re_w, re_b, exp_wg, exp_wu, exp_wd)
    bp, _, d = x_prompt.shape
    n_a = a_mu.shape[0]
    nh = d // HEAD_DIM
    mods = _ada_mods(jnp.concatenate([c_prompt, c_sample], 0), ada_w, ada_b)
    zero_shift = jnp.zeros((n_a, bp, d), x_prompt.dtype)
    zero_wkv = jnp.zeros((n_a, bp, nh, HEAD_DIM, HEAD_DIM), x_prompt.dtype)
    outs_p = _trunk(x_prompt, mods[:, :bp], zero_shift, zero_wkv, None, w)
    outs_s = _trunk(x_sample, mods[:, bp:], state_shift, state_wkv, (cache_k, cache_v, cache_logf), w)
    y_p, p_shift, p_wkv, p_k, p_v, p_logf = outs_p
    y_s, s_shift, s_wkv, s_k, s_v, s_logf = outs_s
    return (y_p, y_s, p_shift, p_wkv, p_k, p_v, p_logf, s_shift, s_wkv, s_k, s_v, s_logf)
```

```python
import functools

import jax
import jax.numpy as jnp
import numpy as np
from jax import lax
from jax.experimental import pallas as pl
from jax.experimental.pallas import tpu as pltpu

F32 = jnp.float32
BF16 = jnp.bfloat16

HEAD_DIM = 64
LANES = 128
N_GROUPS = 4
EXP_PER_GROUP = 4
N_EXPERTS = N_GROUPS * EXP_PER_GROUP
LN_EPS = 1e-5
GN_EPS = 64e-5
WKV_CHUNK = 64
WKV_CHUNKS_PER_STEP = 8
MOE_EXPERTS_PER_STEP = 2
MOE_TILE = 1024
ROUTER_ROWS = 32
LOG2E = 1.4426950408889634
TERM_STRIDE = 32
VMEM_LIMIT = 56 * 1024 * 1024
NEG_BIG = -1e30


def _params(sem):
    return pltpu.CompilerParams(dimension_semantics=sem, vmem_limit_bytes=VMEM_LIMIT)


def _dg(a, b, ca, cb):
    return lax.dot_general(a, b, (((ca,), (cb,)), ((), ())), preferred_element_type=F32)


def _bdot(a, b, ca=1, cb=0):
    return _dg(a.astype(BF16), b.astype(BF16), ca, cb)


def _split3(x):
    hi = x.astype(BF16)
    r1 = x - hi.astype(F32)
    mid = r1.astype(BF16)
    lo = (r1 - mid.astype(F32)).astype(BF16)
    return hi, mid, lo


def _dot_exact_rhs(a, b_exact, ca=1, cb=0):
    hi, mid, lo = _split3(a)
    bb = b_exact.astype(BF16)
    return _dg(hi, bb, ca, cb) + _dg(mid, bb, ca, cb) + _dg(lo, bb, ca, cb)


def _dot_exact_lhs(a_exact, b, ca=1, cb=0):
    hi, mid, lo = _split3(b)
    aa = a_exact.astype(BF16)
    return _dg(aa, hi, ca, cb) + _dg(aa, mid, ca, cb) + _dg(aa, lo, ca, cb)


def _dot3(a, b, ca=1, cb=0):
    ah = a.astype(BF16)
    al = (a - ah.astype(F32)).astype(BF16)
    bh = b.astype(BF16)
    bl = (b - bh.astype(F32)).astype(BF16)
    return _dg(ah, bh, ca, cb) + _dg(ah, bl, ca, cb) + _dg(al, bh, ca, cb)


_inv_dot = _bdot


def _layer_norm(z, g, b):
    mu = jnp.mean(z, -1, keepdims=True)
    d = z - mu
    var = jnp.mean(d * d, -1, keepdims=True)
    return d * lax.rsqrt(var + LN_EPS) * g + b


def _softplus(z):
    return jnp.maximum(z, 0.0) + jnp.log(1.0 + jnp.exp(-jnp.abs(z)))


def _sigmoid(z):
    return 1.0 / (1.0 + jnp.exp(-z))


def _silu(z):
    return z * _sigmoid(z)


def _ada_kernel(c_ref, w_ref, b_ref, o_ref):
    o_ref[...] = _dot3(_silu(c_ref[...]), w_ref[...]) + b_ref[...]


def _ada_mods(c_all, ada_w, ada_b):
    depth, d, d6 = ada_w.shape
    bsz = c_all.shape[0]
    tn = d
    out = pl.pallas_call(
        _ada_kernel,
        grid=(depth, d6 // tn),
        in_specs=[
            pl.BlockSpec((bsz, d), lambda l, j: (0, 0)),
            pl.BlockSpec((None, d, tn), lambda l, j: (l, 0, j)),
            pl.BlockSpec((None, 1, tn), lambda l, j: (l, 0, j)),
        ],
        out_specs=pl.BlockSpec((None, bsz, tn), lambda l, j: (l, 0, j)),
        out_shape=jax.ShapeDtypeStruct((depth, bsz, d6), F32),
        compiler_params=_params(("parallel", "parallel")),
        name="ada_mods",
    )(c_all, ada_w, ada_b.reshape(depth, 1, d6))
    return out.reshape(depth, bsz, 6, d)


def _rwkv_proj_kernel(x_ref, xp_ref, shift_ref, mods_ref, mu_ref, vec_ref, wrkv_ref, w1_ref, w2_ref,
                      a1_ref, a2_ref, g1_ref, g2_ref, hd_ref, hu_ref,
                      r_ref, lw_ref, k_ref, v_ref, kk_ref, b_ref, g_ref, last_ref):
    tm = x_ref.shape[0]
    sh1 = mods_ref[0:1, :]
    sc1 = mods_ref[1:2, :]
    h = x_ref[...] * (1.0 + sc1) + sh1
    h_prev_tile = xp_ref[7:8, :] * (1.0 + sc1) + sh1
    prev_row = jnp.where(pl.program_id(1) == 0, shift_ref[...], h_prev_tile)
    row = lax.broadcasted_iota(jnp.int32, (tm, 1), 0)
    xx = jnp.where(row == 0, prev_row, pltpu.roll(h, 1, axis=0)) - h

    def mix(i):
        return h + xx * mu_ref[i:i + 1, :]

    w0, a0 = vec_ref[0:1, :], vec_ref[1:2, :]
    k_k, k_a = vec_ref[2:3, :], vec_ref[3:4, :]
    r = _bdot(mix(0), wrkv_ref[0])
    k = _bdot(mix(1), wrkv_ref[1])
    v = _bdot(mix(2), wrkv_ref[2])
    ww = w0 + _bdot(jnp.tanh(_bdot(mix(3), w1_ref[...])), w2_ref[...])
    w_log = -_softplus(-ww) - 0.5
    a = _sigmoid(a0 + _bdot(_bdot(mix(4), a1_ref[...]), a2_ref[...]))
    g = _bdot(_sigmoid(_bdot(mix(5), g1_ref[...])), g2_ref[...])
    kk = k * k_k
    head_sq = _bdot(kk * kk, hd_ref[...])
    hi = head_sq.astype(BF16)
    lo = (head_sq - hi.astype(F32)).astype(BF16)
    ss = _dg(hi, hu_ref[...], 1, 0) + _dg(lo, hu_ref[...], 1, 0)
    kk = kk * lax.rsqrt(jnp.maximum(ss, 1e-24))
    r_ref[...] = r.astype(BF16)
    lw_ref[...] = -jnp.exp(w_log)
    k_ref[...] = (k * (1.0 + (a - 1.0) * k_a)).astype(BF16)
    v_ref[...] = v.astype(BF16)
    kk_ref[...] = kk.astype(BF16)
    b_ref[...] = (kk * a).astype(BF16)
    g_ref[...] = g.astype(BF16)
    last_ref[...] = h[tm - 1:tm, :]


def _rwkv_proj(x2d, shift_prev, mods_l, wl, bsz, t, tm):
    n, d = x2d.shape
    nt = t // tm
    tok = pl.BlockSpec((tm, d), lambda b, i: (b * nt + i, 0))
    full = lambda a: pl.BlockSpec(a.shape, lambda b, i: (0,) * a.ndim)
    weights = [wl["mu"], wl["vec"], wl["w_rkv"], wl["w1"], wl["w2"], wl["a1"], wl["a2"], wl["g1"],
               wl["g2"], wl["head_down"], wl["head_up"]]
    outs = pl.pallas_call(
        _rwkv_proj_kernel,
        grid=(bsz, nt),
        in_specs=[
            tok,
            pl.BlockSpec((8, d), lambda b, i: (jnp.maximum((b * nt + i) * (tm // 8) - 1, 0), 0)),
            pl.BlockSpec((None, 1, d), lambda b, i: (b, 0, 0)),
            pl.BlockSpec((None, 6, d), lambda b, i: (b, 0, 0)),
        ] + [full(a) for a in weights],
        out_specs=[tok] * 7 + [pl.BlockSpec((None, 1, d), lambda b, i: (b, 0, 0))],
        out_shape=[jax.ShapeDtypeStruct((n, d), F32 if i == 1 else BF16) for i in range(7)]
        + [jax.ShapeDtypeStruct((bsz, 1, d), F32)],
        compiler_params=_params(("parallel", "arbitrary")),
        name="rwkv_proj",
    )(x2d, x2d, shift_prev.reshape(bsz, 1, d), mods_l, *weights)
    return outs[:7], outs[7].reshape(bsz, d)


def _wkv_kernel(c, r_ref, lw_ref, k_ref, v_ref, kk_ref, b_ref, s0_ref, vec_ref, tri_ref, ones_ref,
                y_ref, s_out_ref, state_ref):
    n_chunks = r_ref.shape[0] // c
    n_pairs = r_ref.shape[1] // LANES
    first = pl.program_id(1) == 0

    @pl.when(first)
    def _():
        state_ref[...] = s0_ref[...]

    lane = lax.broadcasted_iota(jnp.int32, (c, LANES), 1)
    low = lane < HEAD_DIM
    ri = lax.broadcasted_iota(jnp.int32, (2 * c, 2 * c), 0)
    ci = lax.broadcasted_iota(jnp.int32, (2 * c, 2 * c), 1)
    same = (ri >= c) == (ci >= c)
    strict = same & (ci < ri)
    incl = same & (ci <= ri)
    eye = (ri == ci).astype(F32)
    tri = tri_ref[...]
    ones_bd = ones_ref[...]

    def stack(z):
        return jnp.concatenate([jnp.where(low, z, 0.0), jnp.where(low, 0.0, z)], axis=0)

    pairs = range(n_pairs)
    slabs = [slice(p * LANES, (p + 1) * LANES) for p in pairs]

    def rows(z):
        return jnp.concatenate([z[:, s] for s in slabs], 0)

    def lanes(z):
        return jnp.concatenate([z[p * c:(p + 1) * c] for p in pairs], 1)

    def prepare(ch):
        rs = slice(ch * c, (ch + 1) * c)
        return _wkv_prepare(c, lw_ref[rs, :], r_ref[rs, :], k_ref[rs, :], v_ref[rs, :], kk_ref[rs, :],
                            b_ref[rs, :], vec_ref, tri, ones_bd, stack, rows, lanes, slabs, strict, incl, eye)

    s_prev = [state_ref[p] for p in pairs]
    ready = prepare(0)
    for ch in range(n_chunks):
        upcoming = prepare(ch + 1) if ch + 1 < n_chunks else None
        s_prev, y_ref[ch * c:(ch + 1) * c, :] = _wkv_apply(c, ready, s_prev, vec_ref, ones_bd, lanes, slabs)
        ready = upcoming
    state_ref[...] = jnp.stack(s_prev)

    @pl.when(pl.program_id(1) == pl.num_programs(1) - 1)
    def _():
        s_out_ref[...] = state_ref[...]


def _wkv_prepare(c, lw_all, r_bf, k_bf, v_bf, kk_bf, b_bf, vec_ref, tri, ones_bd, stack, rows, lanes,
                 slabs, strict, incl, eye):
    cum_all = _dot_exact_lhs(tri, lw_all)
    prev_all = cum_all - lw_all
    mid_all = cum_all[c // 2 - 1:c // 2, :]
    end_all = cum_all[c - 1:c, :]
    e_in = jnp.exp(mid_all - cum_all)
    e_out = jnp.exp(end_all - cum_all)
    e_end = jnp.exp(end_all)
    r_all, k_all, v_all = r_bf.astype(F32), k_bf.astype(F32), v_bf.astype(F32)
    a_all, b_all = -kk_bf.astype(F32), b_bf.astype(F32)
    aq, rq = a_all * jnp.exp(prev_all - mid_all), r_all * jnp.exp(cum_all - mid_all)
    bi, ki = b_all * e_in, k_all * e_in
    a0, r0 = a_all * jnp.exp(prev_all), r_all * jnp.exp(cum_all)
    bo, ko = b_all * e_out, k_all * e_out

    gram = [_bdot(jnp.concatenate([stack(aq[:, s]), stack(rq[:, s])], 0),
                  jnp.concatenate([stack(bi[:, s]), stack(ki[:, s])], 0), 1, 1) for s in slabs]
    a_ab = [jnp.where(strict, g[:2 * c, :2 * c], 0.0) for g in gram]
    a_ak = [jnp.where(strict, g[:2 * c, 2 * c:], 0.0) for g in gram]
    a_r = [jnp.concatenate([jnp.where(incl, g[2 * c:, :2 * c], 0.0),
                            jnp.where(incl, g[2 * c:, 2 * c:], 0.0)], 1) for g in gram]
    tinv = [eye + m for m in a_ab]
    pw = a_ab
    for _ in range(max(c.bit_length() - 2, 0)):
        pw = [_inv_dot(m, m) for m in pw]
        tinv = [t_ + _inv_dot(t_, m) for t_, m in zip(tinv, pw)]
    v_st = [stack(v_all[:, s]) for s in slabs]
    return dict(
        state_lhs=[jnp.concatenate([_inv_dot(t_, stack(a0[:, s])), stack(r0[:, s])], 0).astype(BF16)
                   for t_, s in zip(tinv, slabs)],
        u_free=[_inv_dot(t_, _bdot(m, vs)) for t_, m, vs in zip(tinv, a_ak, v_st)],
        a_r=[m.astype(BF16) for m in a_r],
        v_st=[vs.astype(BF16) for vs in v_st],
        decay_rows=[jnp.concatenate([stack(bo[:, s]), stack(ko[:, s])], 0).astype(BF16) for s in slabs],
        e_end=e_end,
        bonus=lanes(_bdot(rows(r_all * k_all * vec_ref[0:1, :]), ones_bd)) * v_all)


def _wkv_apply(c, prep, s_prev, vec_ref, ones_bd, lanes, slabs):
    from_state = [_bdot(lhs, sp, 1, 1) for lhs, sp in zip(prep["state_lhs"], s_prev)]
    uv = [jnp.concatenate([(fs[:2 * c] + uf).astype(BF16), vs], 0)
          for fs, uf, vs in zip(from_state, prep["u_free"], prep["v_st"])]
    o_st = [fs[2 * c:] + _dg(m, x, 1, 0) for fs, m, x in zip(from_state, prep["a_r"], uv)]
    o = [x[:c] + x[c:] for x in o_st]
    new_state = [sp * prep["e_end"][:, s] + _dg(x, rows_e, 0, 0)
                 for s, sp, x, rows_e in zip(slabs, s_prev, uv, prep["decay_rows"])]
    o_rows = jnp.concatenate(o, 0)
    dev = o_rows - _bdot(o_rows, ones_bd) * (1.0 / HEAD_DIM)
    var = _bdot(dev * dev, ones_bd) * (1.0 / HEAD_DIM)
    y = lanes(dev * lax.rsqrt(var + GN_EPS)) * vec_ref[1:2, :] + vec_ref[2:3, :] + prep["bonus"]
    return new_state, y


def _pair_states(s):
    bsz, nh, n, _ = s.shape
    s = s.reshape(bsz, nh // 2, 2, n, n)
    z = jnp.zeros_like(s[:, :, 0])
    top = jnp.concatenate([s[:, :, 0], z], -1)
    bot = jnp.concatenate([z, s[:, :, 1]], -1)
    return jnp.concatenate([top, bot], -2)


def _unpair_states(sp):
    bsz, npair, _, _ = sp.shape
    n = HEAD_DIM
    return jnp.stack([sp[:, :, :n, :n], sp[:, :, n:, n:]], 2).reshape(bsz, 2 * npair, n, n)


def _wkv(proj, s0, wl, bsz, t):
    r, lw, k, v, kk, b = proj
    n, d = r.shape
    c = min(WKV_CHUNK, t)
    rows_per_step = min(WKV_CHUNKS_PER_STEP * c, t)
    nc = t // rows_per_step
    npair = d // LANES
    tok = pl.BlockSpec((rows_per_step, d), lambda bb, i: (bb * nc + i, 0))
    st = pl.BlockSpec((None, npair, LANES, LANES), lambda bb, i: (bb, 0, 0, 0))
    tri = jnp.tril(jnp.ones((c, c), BF16))
    hid = jnp.arange(LANES) // HEAD_DIM
    ones_bd = (hid[:, None] == hid[None, :]).astype(BF16)
    y, s_out = pl.pallas_call(
        functools.partial(_wkv_kernel, c),
        grid=(bsz, nc),
        in_specs=[tok] * 6 + [st,
                              pl.BlockSpec((8, d), lambda bb, i: (0, 0)),
                              pl.BlockSpec((c, c), lambda bb, i: (0, 0)),
                              pl.BlockSpec((LANES, LANES), lambda bb, i: (0, 0))],
        out_specs=[tok, st],
        out_shape=[jax.ShapeDtypeStruct((n, d), F32),
                   jax.ShapeDtypeStruct((bsz, npair, LANES, LANES), F32)],
        scratch_shapes=[pltpu.VMEM((npair, LANES, LANES), F32)],
        compiler_params=_params(("parallel", "arbitrary")),
        name="wkv_scan",
    )(r, lw, k, v, kk, b, _pair_states(s0.astype(F32)), wl["scan_vec"], tri, ones_bd)
    return y, _unpair_states(s_out)


def _router(logits):
    row_i = lax.broadcasted_iota(jnp.int32, logits.shape, 0)
    row = row_i.astype(F32)
    far = 1e9
    is_g = row_i < N_GROUPS
    gl = jnp.where(is_g, logits, NEG_BIG)
    gmax = jnp.max(gl, 0, keepdims=True)
    gsel = jnp.min(jnp.where(gl == gmax, row, far), 0, keepdims=True)
    gprob = 1.0 / jnp.sum(jnp.where(is_g, jnp.exp(gl - gmax), 0.0), 0, keepdims=True)
    group_of = lax.shift_right_arithmetic(row_i - N_GROUPS, 2).astype(F32)
    in_group = (row_i >= N_GROUPS) & (row_i < N_GROUPS + N_EXPERTS) & (group_of == gsel)
    el = jnp.where(in_group, logits, NEG_BIG)
    v1 = jnp.max(el, 0, keepdims=True)
    i1 = jnp.min(jnp.where(el == v1, row, far), 0, keepdims=True)
    el2 = jnp.where(row == i1, NEG_BIG, el)
    v2 = jnp.max(el2, 0, keepdims=True)
    i2 = jnp.min(jnp.where(el2 == v2, row, far), 0, keepdims=True)
    e2 = jnp.exp(v2 - v1)
    w1 = gprob / (1.0 + e2)
    w2 = gprob * e2 / (1.0 + e2)
    return jnp.where(row == i1, w1, 0.0) + jnp.where(row == i2, w2, 0.0)


def _post_kernel(gated, alpha, *refs):
    if gated:
        x_ref, y_ref, g_ref, mods_ref, wo_ref, ln_ref, rw_ref, rb_ref, x1_ref, comb_ref = refs
        y = y_ref[...] * g_ref[...].astype(F32)
    else:
        x_ref, y_ref, mods_ref, wo_ref, ln_ref, rw_ref, rb_ref, x1_ref, comb_ref = refs
        y = y_ref[...]
    gt1, sh2, sc2 = mods_ref[2:3, :], mods_ref[3:4, :], mods_ref[4:5, :]
    x1 = _layer_norm(alpha * x_ref[...] + gt1 * _bdot(y, wo_ref[...]), ln_ref[0:1, :], ln_ref[1:2, :])
    x1_ref[...] = x1
    h2 = x1 * (1.0 + sc2) + sh2
    comb_t = _router(_dot3(rw_ref[...], h2, 1, 1) + rb_ref[...])
    pad_rows = jnp.zeros((LANES - comb_t.shape[0], comb_t.shape[1]), F32)
    comb_ref[...] = jnp.concatenate([comb_t, pad_rows], 0).T


def _post_mixer(x2d, y2d, g2d, mods_l, w_o, ln_pack, router_w, router_b, alpha, bsz, t, tm):
    n, d = x2d.shape
    nt = t // tm
    tok = pl.BlockSpec((tm, d), lambda b, i: (b * nt + i, 0))
    full = lambda a: pl.BlockSpec(a.shape, lambda b, i: (0,) * a.ndim)
    gated = g2d is not None
    acts = [x2d, y2d] + ([g2d] if gated else [])
    consts = [w_o, ln_pack, router_w, router_b]
    return pl.pallas_call(
        functools.partial(_post_kernel, gated, alpha),
        grid=(bsz, nt),
        in_specs=[tok] * len(acts) + [pl.BlockSpec((None, 6, d), lambda b, i: (b, 0, 0))]
        + [full(a) for a in consts],
        out_specs=[tok, pl.BlockSpec((tm, LANES), lambda b, i: (b * nt + i, 0))],
        out_shape=[jax.ShapeDtypeStruct((n, d), F32), jax.ShapeDtypeStruct((n, LANES), F32)],
        compiler_params=_params(("parallel", "parallel")),
        name="post_mixer",
    )(*acts, mods_l, *consts)


def _moe_kernel(alpha, x1_ref, comb_ref, mods_ref, ln_ref, wg_ref, wu_ref, wd_ref, x2_ref, h2_ref, acc_ref):
    e = pl.program_id(1)

    @pl.when(e == 0)
    def _():
        sh2, sc2 = mods_ref[3], mods_ref[4]
        h2_ref[...] = (x1_ref[...] * (1.0 + sc2) + sh2).astype(BF16)
        acc_ref[...] = jnp.zeros_like(acc_ref)

    h2 = h2_ref[...]
    per_step, f, d = wd_ref.shape
    lane = lax.broadcasted_iota(jnp.int32, comb_ref.shape, 1)
    comb = comb_ref[...]
    hidden = []
    for i in range(per_step):
        hg = _dg(h2, wg_ref[i].astype(BF16), 1, 0)
        hu = _dg(h2, wu_ref[i].astype(BF16), 1, 0)
        ce = jnp.sum(jnp.where(lane == e * per_step + (i + N_GROUPS), comb, 0.0), -1, keepdims=True)
        hidden.append((_silu(hg) * hu * ce).astype(BF16))
    acc_ref[...] += _dg(jnp.concatenate(hidden, 1), wd_ref[...].reshape(per_step * f, d).astype(BF16), 1, 0)

    @pl.when(e == pl.num_programs(1) - 1)
    def _():
        gt2 = mods_ref[5]
        x2_ref[...] = _layer_norm(alpha * x1_ref[...] + gt2 * acc_ref[...], ln_ref[0:1, :], ln_ref[1:2, :])


def _moe(x1, comb, mods_l, ln_pack, wg, wu, wd, layer, alpha, bsz, t):
    n, d = x1.shape
    _, ne, _, f = wg.shape
    per_step = MOE_EXPERTS_PER_STEP
    if t >= MOE_TILE:
        tm = MOE_TILE
        mods = mods_l.reshape(bsz, 6, 1, d)
        mods_spec = pl.BlockSpec((None, 6, 1, d), lambda i, e: (i // (t // tm), 0, 0, 0))
    else:
        tm = min(n, MOE_TILE)
        mods = jnp.repeat(mods_l.transpose(1, 0, 2), t, axis=1)
        mods_spec = pl.BlockSpec((6, tm, d), lambda i, e: (0, i, 0))
    return pl.pallas_call(
        functools.partial(_moe_kernel, alpha),
        grid=(n // tm, ne // per_step),
        in_specs=[
            pl.BlockSpec((tm, d), lambda i, e: (i, 0)),
            pl.BlockSpec((tm, LANES), lambda i, e: (i, 0)),
            mods_spec,
            pl.BlockSpec(ln_pack.shape, lambda i, e: (0, 0)),
            pl.BlockSpec((None, per_step, d, f), lambda i, e: (layer, e, 0, 0)),
            pl.BlockSpec((None, per_step, d, f), lambda i, e: (layer, e, 0, 0)),
            pl.BlockSpec((None, per_step, f, d), lambda i, e: (layer, e, 0, 0)),
        ],
        out_specs=pl.BlockSpec((tm, d), lambda i, e: (i, 0)),
        out_shape=jax.ShapeDtypeStruct((n, d), F32),
        scratch_shapes=[pltpu.VMEM((tm, d), BF16), pltpu.VMEM((tm, d), F32)],
        compiler_params=_params(("parallel", "arbitrary")),
        name="hmoe",
    )(x1, comb, mods, ln_pack, wg, wu, wd)


def _fox_bias_placement(d):
    place_q = np.zeros((LANES, d), np.float32)
    place_k = np.zeros((LANES, d), np.float32)
    const_q = np.zeros((1, d), np.float32)
    const_k = np.zeros((1, d), np.float32)
    for h in range(d // HEAD_DIM):
        base = (h // 2) * LANES + (HEAD_DIM if h % 2 == 0 else 0)
        for term in range(3):
            place_q[term * TERM_STRIDE + h, base + term] = 1.0
            place_k[term * TERM_STRIDE + h, base + 3 + term] = -1.0
            const_q[0, base + 3 + term] = 1.0
            const_k[0, base + term] = 1.0
    return (jnp.asarray(place_q, BF16), jnp.asarray(place_k, BF16), jnp.asarray(const_q),
            jnp.asarray(const_k))


def _kv_kernel(x_ref, kvw_ref, fw_ref, fb_ref, tri_ref, pq_ref, pk_ref, cq_ref, ck_ref,
               k_ref, v_ref, kb_ref, vb_ref, lf_ref, fc_ref, qbias_ref, kbias_ref, vbt_ref, carry_ref):
    d = x_ref.shape[1]

    @pl.when(pl.program_id(1) == 0)
    def _():
        carry_ref[...] = jnp.zeros_like(carry_ref)

    x = x_ref[...]
    kv = _bdot(x, kvw_ref[...])
    k, v = kv[:, :d], kv[:, d:]
    k_ref[...] = k.T
    v_t = v.T
    v_ref[...] = v_t
    vbt_ref[...] = v_t.astype(BF16)
    kb_ref[...] = k.astype(BF16)
    vb_ref[...] = v.astype(BF16)
    z = _dot3(x, fw_ref[...]) + fb_ref[...]
    lf = -_softplus(-z)
    lf_ref[...] = lf
    fc = _dot_exact_lhs(tri_ref[...], lf) + carry_ref[...]
    fc_ref[...] = fc
    carry_ref[...] = fc[fc.shape[0] - 1:, :]
    lane = lax.broadcasted_iota(jnp.int32, fc.shape, 1)
    hi, mid, lo = _split3(jnp.where(lane < d // HEAD_DIM, fc * LOG2E, 0.0))
    terms = (hi.astype(F32) + pltpu.roll(mid.astype(F32), TERM_STRIDE, axis=1)
             + pltpu.roll(lo.astype(F32), 2 * TERM_STRIDE, axis=1)).astype(BF16)
    qbias_ref[...] = (_dg(terms, pq_ref[...], 1, 0) + cq_ref[...]).astype(BF16)
    kbias_ref[...] = (_dg(terms, pk_ref[...], 1, 0) + ck_ref[...]).astype(BF16)


def _kv_proj(x2d, kv_w, f_w, f_b, bsz, t, tm):
    n, d = x2d.shape
    nt = t // tm
    tok = pl.BlockSpec((tm, d), lambda b, i: (b * nt + i, 0))
    nar = pl.BlockSpec((tm, LANES), lambda b, i: (b * nt + i, 0))
    feature_major = pl.BlockSpec((None, d, tm), lambda b, i: (b, 0, i))
    tri = jnp.tril(jnp.ones((tm, tm), BF16))
    full = lambda a: pl.BlockSpec(a.shape, lambda b, i: (0,) * a.ndim)
    consts = [kv_w, f_w, f_b, tri, *_fox_bias_placement(d)]
    return pl.pallas_call(
        _kv_kernel,
        grid=(bsz, nt),
        in_specs=[tok] + [full(a) for a in consts],
        out_specs=[feature_major, feature_major, tok, tok, nar, nar, tok, tok, feature_major],
        out_shape=[jax.ShapeDtypeStruct((bsz, d, t), F32), jax.ShapeDtypeStruct((bsz, d, t), F32),
                   jax.ShapeDtypeStruct((n, d), BF16), jax.ShapeDtypeStruct((n, d), BF16),
                   jax.ShapeDtypeStruct((n, LANES), F32), jax.ShapeDtypeStruct((n, LANES), F32),
                   jax.ShapeDtypeStruct((n, d), BF16), jax.ShapeDtypeStruct((n, d), BF16),
                   jax.ShapeDtypeStruct((bsz, d, t), BF16)],
        scratch_shapes=[pltpu.VMEM((1, LANES), F32)],
        compiler_params=_params(("parallel", "arbitrary")),
        name="kv_proj",
    )(x2d, *consts)


def _q_kernel(scale, x_ref, mods_ref, wq_ref, q_ref):
    h = x_ref[...] * (1.0 + mods_ref[1:2, :]) + mods_ref[0:1, :]
    q_ref[...] = (_bdot(h, wq_ref[...]) * scale).astype(BF16)


def _q_proj(x2d, mods_l, wq, scale, bsz, t, tm):
    n, d = x2d.shape
    nt = t // tm
    tok = pl.BlockSpec((tm, d), lambda b, i: (b * nt + i, 0))
    return pl.pallas_call(
        functools.partial(_q_kernel, scale),
        grid=(bsz, nt),
        in_specs=[tok, pl.BlockSpec((None, 6, d), lambda b, i: (b, 0, 0)),
                  pl.BlockSpec(wq.shape, lambda b, i: (0, 0))],
        out_specs=tok,
        out_shape=jax.ShapeDtypeStruct((n, d), BF16),
        compiler_params=_params(("parallel", "parallel")),
        name="q_proj",
    )(x2d, mods_l, wq)


def _fox_prompt_kernel(tq, q_ref, k_ref, vt_ref, qb_ref, kb_ref, o_ref):
    t = q_ref.shape[0]
    lane = lax.broadcasted_iota(jnp.int32, (1, LANES), 1)
    half = tq // 2
    key = lax.broadcasted_iota(jnp.int32, (half, tq), 0)
    qry = lax.broadcasted_iota(jnp.int32, (half, tq), 1)
    key_b = lax.broadcasted_iota(jnp.int32, (half, half), 0)
    qry_b = lax.broadcasted_iota(jnp.int32, (half, half), 1)
    own = [lane < HEAD_DIM, lane >= HEAD_DIM]
    k_aug = [jnp.where(own[hh], k_ref[...], kb_ref[...]) for hh in range(2)]
    v_t = vt_ref[...]
    for qi in range(t // tq):
        lo, mid, hi = qi * tq, qi * tq + half, (qi + 1) * tq
        heads = []
        for hh in range(2):
            q = jnp.where(own[hh], q_ref[lo:hi, :], qb_ref[lo:hi, :])
            v_h = v_t[hh * HEAD_DIM:(hh + 1) * HEAD_DIM]
            s_a = jnp.where(key <= qry, _dg(k_aug[hh][lo:mid], q, 1, 1), NEG_BIG)
            s_b = jnp.where(key_b <= qry_b, _dg(k_aug[hh][mid:hi], q[half:], 1, 1), NEG_BIG)
            m = jnp.max(s_a, 0, keepdims=True)
            if qi > 0:
                s_past = _dg(k_aug[hh][:lo], q, 1, 1)
                m = jnp.maximum(m, jnp.max(s_past, 0, keepdims=True))
            m_b = jnp.maximum(m[:, half:], jnp.max(s_b, 0, keepdims=True))
            m = jnp.concatenate([m[:, :half], m_b], 1)
            p = jnp.exp2(s_a - m)
            num = _dg(v_h[:, lo:mid], p.astype(BF16), 1, 0)
            den = jnp.sum(p, 0, keepdims=True)
            if qi > 0:
                p = jnp.exp2(s_past - m)
                num = num + _dg(v_h[:, :lo], p.astype(BF16), 1, 0)
                den = den + jnp.sum(p, 0, keepdims=True)
            p = jnp.exp2(s_b - m_b)
            num_b = num[:, half:] + _dg(v_h[:, mid:hi], p.astype(BF16), 1, 0)
            den_b = den[:, half:] + jnp.sum(p, 0, keepdims=True)
            heads.append(jnp.concatenate([num[:, :half] / den[:, :half], num_b / den_b], 1))
        o_ref[lo:hi, :] = jnp.concatenate(heads, 0).T


def _fox_prompt(q, kb, vb_t, q_bias, k_bias, bsz, t, tq):
    n, d = q.shape
    seq = pl.BlockSpec((t, LANES), lambda b, p: (b, p))
    return pl.pallas_call(
        functools.partial(_fox_prompt_kernel, tq),
        grid=(bsz, d // LANES),
        in_specs=[seq, seq, pl.BlockSpec((None, LANES, t), lambda b, p: (b, p, 0)), seq, seq],
        out_specs=seq,
        out_shape=jax.ShapeDtypeStruct((n, d), F32),
        compiler_params=_params(("parallel", "parallel")),
        name="fox_prompt",
    )(q, kb, vb_t, q_bias, k_bias)


def _fox_sample_kernel(q_ref, kn_ref, vn_ref, fn_ref, fnt_ref, ck_ref, cv_ref, clf_ref, upper_ref,
                       o_ref, m_ref, l_ref, acc_ref, carry_ref):
    step = pl.program_id(1)
    t, d = q_ref.shape
    nh = d // HEAD_DIM

    @pl.when(step == 0)
    def _():
        m_ref[...] = jnp.full_like(m_ref, NEG_BIG)
        l_ref[...] = jnp.zeros_like(l_ref)
        acc_ref[...] = jnp.zeros_like(acc_ref)
        carry_ref[...] = jnp.zeros_like(carry_ref)

    heads = range(nh)
    cols = [slice(h * HEAD_DIM, (h + 1) * HEAD_DIM) for h in heads]
    q = [q_ref[:, c] for c in cols]

    def attend(keys, values, feature_major, bias, mask):
        s = [_dg(q[h], keys[h], 1, 0 if feature_major else 1) + bias[h] for h in heads]
        if mask is not None:
            s = [jnp.where(mask, x, NEG_BIG) for x in s]
        m_old = m_ref[...]
        m_new = jnp.maximum(m_old, jnp.stack([jnp.max(x, -1, keepdims=True) for x in s]))
        alpha = jnp.exp(m_old - m_new)
        p = [jnp.exp(x - m_new[h]) for h, x in enumerate(s)]
        l_ref[...] = alpha * l_ref[...] + jnp.stack([jnp.sum(x, -1, keepdims=True) for x in p])
        acc_ref[...] = alpha * acc_ref[...] + jnp.stack(
            [_dg(p[h].astype(BF16), values[h], 1, 1 if feature_major else 0) for h in heads])
        m_ref[...] = m_new

    clf = clf_ref[...]
    suf = _dot_exact_rhs(clf, upper_ref[...]) + carry_ref[...]
    carry_ref[...] = carry_ref[...] + jnp.sum(clf, -1, keepdims=True)
    fn = fn_ref[...]
    attend([ck_ref[h].astype(BF16) for h in heads], [cv_ref[h].astype(BF16) for h in heads], True,
           [fn[:, h:h + 1] + suf[h:h + 1, :] for h in heads], None)

    @pl.when(step == pl.num_programs(1) - 1)
    def _():
        rr = lax.broadcasted_iota(jnp.int32, (t, t), 0)
        cc = lax.broadcasted_iota(jnp.int32, (t, t), 1)
        fnt = fnt_ref[...]
        attend([kn_ref[:, c] for c in cols], [vn_ref[:, c] for c in cols], False,
               [fn[:, h:h + 1] - fnt[h:h + 1, :] for h in heads], cc <= rr)
        out = acc_ref[...] / l_ref[...]
        o_ref[...] = jnp.concatenate([out[h] for h in heads], 1)


def _fox_sample(q, kb, vb, fcum, cache_k, cache_v, cache_logf, bsz, t, tk):
    n, d = q.shape
    nh = d // HEAD_DIM
    plen = cache_logf.shape[1]
    nk = plen // tk
    ck = cache_k.transpose(0, 2, 3, 1)
    cv = cache_v.transpose(0, 2, 3, 1)
    clf_t = cache_logf.astype(F32).transpose(0, 2, 1)
    fn_t = fcum[:, :nh].reshape(bsz, t, nh).transpose(0, 2, 1)
    upper = (jnp.arange(tk)[:, None] > jnp.arange(tk)[None, :]).astype(BF16)
    tok = pl.BlockSpec((t, d), lambda b, j: (b, 0))
    past = pl.BlockSpec((None, nh, HEAD_DIM, tk), lambda b, j: (b, 0, 0, nk - 1 - j))
    return pl.pallas_call(
        _fox_sample_kernel,
        grid=(bsz, nk),
        in_specs=[tok, tok, tok,
                  pl.BlockSpec((t, LANES), lambda b, j: (b, 0)),
                  pl.BlockSpec((None, nh, t), lambda b, j: (b, 0, 0)),
                  past, past,
                  pl.BlockSpec((None, nh, tk), lambda b, j: (b, 0, nk - 1 - j)),
                  pl.BlockSpec((tk, tk), lambda b, j: (0, 0))],
        out_specs=tok,
        out_shape=jax.ShapeDtypeStruct((n, d), F32),
        scratch_shapes=[pltpu.VMEM((nh, t, 1), F32), pltpu.VMEM((nh, t, 1), F32),
                        pltpu.VMEM((nh, t, HEAD_DIM), F32), pltpu.VMEM((nh, 1), F32)],
        compiler_params=_params(("parallel", "arbitrary")),
        name="fox_sample",
    )(q, kb, vb, fcum, fn_t, ck, cv, clf_t, upper)


def _tile(t, cap):
    return min(t, cap)


def _trunk(x, mods, shift_in, wkv_in, cache, w):
    bsz, t, d = x.shape
    depth = mods.shape[0]
    n_a = w["n_a"]
    nh = d // HEAD_DIM
    x2d = x.reshape(bsz * t, d)
    shifts, states = [], []
    kv = None
    for l in range(depth):
        mods_l = mods[l]
        if l < n_a:
            wl = w["rwkv"][l]
            proj, last = _rwkv_proj(x2d, shift_in[l], mods_l, wl, bsz, t, _tile(t, 512))
            y, s_new = _wkv(proj[:6], wkv_in[l], wl, bsz, t)
            shifts.append(last)
            states.append(s_new.astype(wkv_in.dtype))
            mixer_out, gate, w_o = y, proj[6], wl["w_o"]
        else:
            if kv is None:
                kv = _kv_proj(x2d, w["kv_w"], w["f_w"], w["f_b"], bsz, t, _tile(t, 512))
            k_sh, v_sh, kb, vb, lf, fcum, q_bias, k_bias, vb_t = kv
            j = l - n_a
            q_scale = HEAD_DIM ** -0.5 * (LOG2E if cache is None else 1.0)
            q = _q_proj(x2d, mods_l, w["b_wq"][j], q_scale, bsz, t, _tile(t, 1024))
            if cache is None:
                mixer_out = _fox_prompt(q, kb, vb_t, q_bias, k_bias, bsz, t, _tile(t, 1024))
            else:
                mixer_out = _fox_sample(q, kb, vb, fcum, cache[0], cache[1], cache[2], bsz, t,
                                        _tile(cache[0].shape[1], 1024))
            gate, w_o = None, w["b_wo"][j]
        x1, comb = _post_mixer(x2d, mixer_out, gate, mods_l, w_o, w["ln"][l][0], w["router_w"][l],
                               w["router_b"][l], w["alpha"], bsz, t, _tile(t, 1024))
        x2d = _moe(x1, comb, mods_l, w["ln"][l][1], w["exp_wg"], w["exp_wu"], w["exp_wd"], l,
                   w["alpha"], bsz, t)
    k_sh, v_sh, lf = kv[0], kv[1], kv[4]
    by_head = lambda z: z.reshape(bsz, nh, HEAD_DIM, t).transpose(0, 3, 1, 2)
    return (x2d.reshape(bsz, t, d), jnp.stack(shifts), jnp.stack(states), by_head(k_sh), by_head(v_sh),
            lf[:, :nh].reshape(bsz, t, nh).astype(x.dtype))


def _prepare_weights(ln_g, ln_b, a_mu, a_w_rkv, a_w0, a_w1, a_w2, a_a0, a_a1, a_a2, a_g1, a_g2, a_k_k,
                     a_k_a, a_r_k, a_lnx_g, a_lnx_b, a_w_o, kv_w, f_w, f_b, b_wq, b_wo, rg_w, rg_b,
                     re_w, re_b, exp_wg, exp_wu, exp_wd):
    depth, _, d = ln_g.shape
    n_a = a_mu.shape[0]
    nh = d // HEAD_DIM
    alpha = (2.0 * depth) ** 0.25
    zrow = jnp.zeros((d,), F32)
    head_of = jnp.arange(d) // HEAD_DIM
    head_down = (head_of[:, None] == jnp.arange(LANES)[None, :]).astype(BF16)
    rwkv = []
    for l in range(n_a):
        rwkv.append(dict(
            mu=a_mu[l],
            vec=jnp.stack([a_w0[l], a_a0[l], a_k_k[l], a_k_a[l], zrow, zrow, zrow, zrow]),
            scan_vec=jnp.stack([a_r_k[l].reshape(d), a_lnx_g[l], a_lnx_b[l], zrow, zrow, zrow, zrow, zrow]),
            w_rkv=a_w_rkv[l].astype(BF16), w1=a_w1[l].astype(BF16), w2=a_w2[l].astype(BF16),
            a1=a_a1[l].astype(BF16), a2=a_a2[l].astype(BF16), g1=a_g1[l].astype(BF16),
            g2=a_g2[l].astype(BF16), w_o=a_w_o[l].astype(BF16),
            head_down=head_down, head_up=head_down.T))
    ln = [[jnp.stack([ln_g[l, s], ln_b[l, s], zrow, zrow, zrow, zrow, zrow, zrow])
           for s in range(2)] for l in range(depth)]
    pad = ROUTER_ROWS - N_GROUPS - N_EXPERTS
    router_w = [jnp.concatenate([rg_w[l], re_w[l], jnp.zeros((d, pad), F32)], 1).T for l in range(depth)]
    router_b = [jnp.concatenate([rg_b[l], re_b[l], jnp.zeros((pad,), F32)])[:, None] for l in range(depth)]
    return dict(
        n_a=n_a, alpha=alpha, rwkv=rwkv, ln=ln, router_w=router_w, router_b=router_b,
        kv_w=kv_w.astype(BF16),
        f_w=jnp.concatenate([f_w, jnp.zeros((d, LANES - nh), F32)], 1),
        f_b=jnp.concatenate([f_b, jnp.zeros((LANES - nh,), F32)])[None, :],
        b_wq=b_wq.astype(BF16), b_wo=b_wo.astype(BF16),
        exp_wg=exp_wg, exp_wu=exp_wu, exp_wd=exp_wd)


def kernel(x_prompt, x_sample, state_shift, state_wkv, cache_k, cache_v, cache_logf, c_prompt, c_sample,
           ada_w, ada_b, ln_g, ln_b, a_mu, a_w_rkv, a_w0, a_w1, a_w2, a_a0, a_a1, a_a2, a_g1, a_g2, a_k_k,
           a_k_a, a_r_k, a_lnx_g, a_lnx_b, a_w_o, kv_w, f_w, f_b, b_wq, b_wo, rg_w, rg_b, re_w, re_b,
           exp_wg, exp_wu, exp_wd):
    w = _prepare_weights(ln_g, ln_b, a_mu, a_w_rkv, a_w0, a_w1, a_w2, a_a0, a_a1, a_a2, a_g1, a_g2,
                         a_k_k, a_k_a, a_r_k, a_lnx_g, a_lnx_b, a_w_o, kv_w, f_w, f_b, b_wq, b_wo,
                         rg_w, rg_b, re_w, re_b, exp_wg, exp_wu, exp_wd)
    bp, _, d = x_prompt.shape
    n_a = a_mu.shape[0]
    nh = d // HEAD_DIM
    mods = _ada_mods(jnp.concatenate([c_prompt, c_sample], 0), ada_w, ada_b)
    zero_shift = jnp.zeros((n_a, bp, d), x_prompt.dtype)
    zero_wkv = jnp.zeros((n_a, bp, nh, HEAD_DIM, HEAD_DIM), x_prompt.dtype)
    outs_p = _trunk(x_prompt, mods[:, :bp], zero_shift, zero_wkv, None, w)
    outs_s = _trunk(x_sample, mods[:, bp:], state_shift, state_wkv, (cache_k, cache_v, cache_logf), w)
    y_p, p_shift, p_wkv, p_k, p_v, p_logf = outs_p
    y_s, s_shift, s_wkv, s_k, s_v, s_logf = outs_s
    return (y_p, y_s, p_shift, p_wkv, p_k, p_v, p_logf, s_shift, s_wkv, s_k, s_v, s_logf)
```

```python
import functools

import jax
import jax.numpy as jnp
import numpy as np
from jax import lax
from jax.experimental import pallas as pl
from jax.experimental.pallas import tpu as pltpu

F32 = jnp.float32
BF16 = jnp.bfloat16

HEAD_DIM = 64
LANES = 128
N_GROUPS = 4
EXP_PER_GROUP = 4
N_EXPERTS = N_GROUPS * EXP_PER_GROUP
LN_EPS = 1e-5
GN_EPS = 64e-5
WKV_CHUNK = 64
WKV_CHUNKS_PER_STEP = 8
MOE_EXPERTS_PER_STEP = 4
MOE_TILE = 1024
ROUTER_ROWS = 32
LOG2E = 1.4426950408889634
TERM_STRIDE = 32
VMEM_LIMIT = 56 * 1024 * 1024
NEG_BIG = -1e30


def _params(sem):
    return pltpu.CompilerParams(dimension_semantics=sem, vmem_limit_bytes=VMEM_LIMIT)


def _dg(a, b, ca, cb):
    return lax.dot_general(a, b, (((ca,), (cb,)), ((), ())), preferred_element_type=F32)


def _bdot(a, b, ca=1, cb=0):
    return _dg(a.astype(BF16), b.astype(BF16), ca, cb)


def _split3(x):
    hi = x.astype(BF16)
    r1 = x - hi.astype(F32)
    mid = r1.astype(BF16)
    lo = (r1 - mid.astype(F32)).astype(BF16)
    return hi, mid, lo


def _dot_exact_rhs(a, b_exact, ca=1, cb=0):
    hi, mid, lo = _split3(a)
    bb = b_exact.astype(BF16)
    return _dg(hi, bb, ca, cb) + _dg(mid, bb, ca, cb) + _dg(lo, bb, ca, cb)


def _dot_exact_lhs(a_exact, b, ca=1, cb=0):
    hi, mid, lo = _split3(b)
    aa = a_exact.astype(BF16)
    return _dg(aa, hi, ca, cb) + _dg(aa, mid, ca, cb) + _dg(aa, lo, ca, cb)


def _dot3(a, b, ca=1, cb=0):
    ah = a.astype(BF16)
    al = (a - ah.astype(F32)).astype(BF16)
    bh = b.astype(BF16)
    bl = (b - bh.astype(F32)).astype(BF16)
    return _dg(ah, bh, ca, cb) + _dg(ah, bl, ca, cb) + _dg(al, bh, ca, cb)


_inv_dot = _bdot


def _layer_norm(z, g, b):
    mu = jnp.mean(z, -1, keepdims=True)
    d = z - mu
    var = jnp.mean(d * d, -1, keepdims=True)
    return d * lax.rsqrt(var + LN_EPS) * g + b


def _softplus(z):
    return jnp.maximum(z, 0.0) + jnp.log(1.0 + jnp.exp(-jnp.abs(z)))


def _sigmoid(z):
    return 1.0 / (1.0 + jnp.exp(-z))


def _silu(z):
    return z * _sigmoid(z)


def _ada_kernel(c_ref, w_ref, b_ref, o_ref):
    o_ref[...] = _dot3(_silu(c_ref[...]), w_ref[...]) + b_ref[...]


def _ada_mods(c_all, ada_w, ada_b):
    depth, d, d6 = ada_w.shape
    bsz = c_all.shape[0]
    tn = d
    out = pl.pallas_call(
        _ada_kernel,
        grid=(depth, d6 // tn),
        in_specs=[
            pl.BlockSpec((bsz, d), lambda l, j: (0, 0)),
            pl.BlockSpec((None, d, tn), lambda l, j: (l, 0, j)),
            pl.BlockSpec((None, 1, tn), lambda l, j: (l, 0, j)),
        ],
        out_specs=pl.BlockSpec((None, bsz, tn), lambda l, j: (l, 0, j)),
        out_shape=jax.ShapeDtypeStruct((depth, bsz, d6), F32),
        compiler_params=_params(("parallel", "parallel")),
        name="ada_mods",
    )(c_all, ada_w, ada_b.reshape(depth, 1, d6))
    return out.reshape(depth, bsz, 6, d)


def _rwkv_proj_kernel(x_ref, xp_ref, shift_ref, mods_ref, mu_ref, vec_ref, wrkv_ref, w1_ref, w2_ref,
                      a1_ref, a2_ref, g1_ref, g2_ref, hd_ref, hu_ref,
                      r_ref, lw_ref, k_ref, v_ref, kk_ref, b_ref, g_ref, last_ref):
    tm = x_ref.shape[0]
    sh1 = mods_ref[0:1, :]
    sc1 = mods_ref[1:2, :]
    h = x_ref[...] * (1.0 + sc1) + sh1
    h_prev_tile = xp_ref[7:8, :] * (1.0 + sc1) + sh1
    prev_row = jnp.where(pl.program_id(1) == 0, shift_ref[...], h_prev_tile)
    row = lax.broadcasted_iota(jnp.int32, (tm, 1), 0)
    xx = jnp.where(row == 0, prev_row, pltpu.roll(h, 1, axis=0)) - h

    def mix(i):
        return h + xx * mu_ref[i:i + 1, :]

    w0, a0 = vec_ref[0:1, :], vec_ref[1:2, :]
    k_k, k_a = vec_ref[2:3, :], vec_ref[3:4, :]
    r = _bdot(mix(0), wrkv_ref[0])
    k = _bdot(mix(1), wrkv_ref[1])
    v = _bdot(mix(2), wrkv_ref[2])
    ww = w0 + _bdot(jnp.tanh(_bdot(mix(3), w1_ref[...])), w2_ref[...])
    w_log = -_softplus(-ww) - 0.5
    a = _sigmoid(a0 + _bdot(_bdot(mix(4), a1_ref[...]), a2_ref[...]))
    g = _bdot(_sigmoid(_bdot(mix(5), g1_ref[...])), g2_ref[...])
    kk = k * k_k
    head_sq = _bdot(kk * kk, hd_ref[...])
    hi = head_sq.astype(BF16)
    lo = (head_sq - hi.astype(F32)).astype(BF16)
    ss = _dg(hi, hu_ref[...], 1, 0) + _dg(lo, hu_ref[...], 1, 0)
    kk = kk * lax.rsqrt(jnp.maximum(ss, 1e-24))
    r_ref[...] = r.astype(BF16)
    lw_ref[...] = -jnp.exp(w_log)
    k_ref[...] = (k * (1.0 + (a - 1.0) * k_a)).astype(BF16)
    v_ref[...] = v.astype(BF16)
    kk_ref[...] = kk.astype(BF16)
    b_ref[...] = (kk * a).astype(BF16)
    g_ref[...] = g.astype(BF16)
    last_ref[...] = h[tm - 1:tm, :]


def _rwkv_proj(x2d, shift_prev, mods_l, wl, bsz, t, tm):
    n, d = x2d.shape
    nt = t // tm
    tok = pl.BlockSpec((tm, d), lambda b, i: (b * nt + i, 0))
    full = lambda a: pl.BlockSpec(a.shape, lambda b, i: (0,) * a.ndim)
    weights = [wl["mu"], wl["vec"], wl["w_rkv"], wl["w1"], wl["w2"], wl["a1"], wl["a2"], wl["g1"],
               wl["g2"], wl["head_down"], wl["head_up"]]
    outs = pl.pallas_call(
        _rwkv_proj_kernel,
        grid=(bsz, nt),
        in_specs=[
            tok,
            pl.BlockSpec((8, d), lambda b, i: (jnp.maximum((b * nt + i) * (tm // 8) - 1, 0), 0)),
            pl.BlockSpec((None, 1, d), lambda b, i: (b, 0, 0)),
            pl.BlockSpec((None, 6, d), lambda b, i: (b, 0, 0)),
        ] + [full(a) for a in weights],
        out_specs=[tok] * 7 + [pl.BlockSpec((None, 1, d), lambda b, i: (b, 0, 0))],
        out_shape=[jax.ShapeDtypeStruct((n, d), F32 if i == 1 else BF16) for i in range(7)]
        + [jax.ShapeDtypeStruct((bsz, 1, d), F32)],
        compiler_params=_params(("parallel", "arbitrary")),
        name="rwkv_proj",
    )(x2d, x2d, shift_prev.reshape(bsz, 1, d), mods_l, *weights)
    return outs[:7], outs[7].reshape(bsz, d)


def _wkv_kernel(c, r_ref, lw_ref, k_ref, v_ref, kk_ref, b_ref, s0_ref, vec_ref, tri_ref, ones_ref,
                y_ref, s_out_ref, state_ref):
    n_chunks = r_ref.shape[0] // c
    n_pairs = r_ref.shape[1] // LANES
    first = pl.program_id(1) == 0

    @pl.when(first)
    def _():
        state_ref[...] = s0_ref[...]

    lane = lax.broadcasted_iota(jnp.int32, (c, LANES), 1)
    low = lane < HEAD_DIM
    ri = lax.broadcasted_iota(jnp.int32, (2 * c, 2 * c), 0)
    ci = lax.broadcasted_iota(jnp.int32, (2 * c, 2 * c), 1)
    same = (ri >= c) == (ci >= c)
    strict = same & (ci < ri)
    incl = same & (ci <= ri)
    eye = (ri == ci).astype(F32)
    tri = tri_ref[...]
    ones_bd = ones_ref[...]

    def stack(z):
        return jnp.concatenate([jnp.where(low, z, 0.0), jnp.where(low, 0.0, z)], axis=0)

    pairs = range(n_pairs)
    slabs = [slice(p * LANES, (p + 1) * LANES) for p in pairs]

    def rows(z):
        return jnp.concatenate([z[:, s] for s in slabs], 0)

    def lanes(z):
        return jnp.concatenate([z[p * c:(p + 1) * c] for p in pairs], 1)

    def prepare(ch):
        rs = slice(ch * c, (ch + 1) * c)
        return _wkv_prepare(c, lw_ref[rs, :], r_ref[rs, :], k_ref[rs, :], v_ref[rs, :], kk_ref[rs, :],
                            b_ref[rs, :], vec_ref, tri, ones_bd, stack, rows, lanes, slabs, strict, incl, eye)

    s_prev = [state_ref[p] for p in pairs]
    ready = prepare(0)
    for ch in range(n_chunks):
        upcoming = prepare(ch + 1) if ch + 1 < n_chunks else None
        s_prev, y_ref[ch * c:(ch + 1) * c, :] = _wkv_apply(c, ready, s_prev, vec_ref, ones_bd, lanes, slabs)
        ready = upcoming
    state_ref[...] = jnp.stack(s_prev)

    @pl.when(pl.program_id(1) == pl.num_programs(1) - 1)
    def _():
        s_out_ref[...] = state_ref[...]


def _wkv_prepare(c, lw_all, r_bf, k_bf, v_bf, kk_bf, b_bf, vec_ref, tri, ones_bd, stack, rows, lanes,
                 slabs, strict, incl, eye):
    cum_all = _dot_exact_lhs(tri, lw_all)
    prev_all = cum_all - lw_all
    mid_all = cum_all[c // 2 - 1:c // 2, :]
    end_all = cum_all[c - 1:c, :]
    e_in = jnp.exp(mid_all - cum_all)
    e_out = jnp.exp(end_all - cum_all)
    e_end = jnp.exp(end_all)
    r_all, k_all, v_all = r_bf.astype(F32), k_bf.astype(F32), v_bf.astype(F32)
    a_all, b_all = -kk_bf.astype(F32), b_bf.astype(F32)
    aq, rq = a_all * jnp.exp(prev_all - mid_all), r_all * jnp.exp(cum_all - mid_all)
    bi, ki = b_all * e_in, k_all * e_in
    a0, r0 = a_all * jnp.exp(prev_all), r_all * jnp.exp(cum_all)
    bo, ko = b_all * e_out, k_all * e_out

    gram = [_bdot(jnp.concatenate([stack(aq[:, s]), stack(rq[:, s])], 0),
                  jnp.concatenate([stack(bi[:, s]), stack(ki[:, s])], 0), 1, 1) for s in slabs]
    a_ab = [jnp.where(strict, g[:2 * c, :2 * c], 0.0) for g in gram]
    a_ak = [jnp.where(strict, g[:2 * c, 2 * c:], 0.0) for g in gram]
    a_r = [jnp.concatenate([jnp.where(incl, g[2 * c:, :2 * c], 0.0),
                            jnp.where(incl, g[2 * c:, 2 * c:], 0.0)], 1) for g in gram]
    tinv = [eye + m for m in a_ab]
    pw = a_ab
    for _ in range(max(c.bit_length() - 2, 0)):
        pw = [_inv_dot(m, m) for m in pw]
        tinv = [t_ + _inv_dot(t_, m) for t_, m in zip(tinv, pw)]
    v_st = [stack(v_all[:, s]) for s in slabs]
    return dict(
        state_lhs=[jnp.concatenate([_inv_dot(t_, stack(a0[:, s])), stack(r0[:, s])], 0).astype(BF16)
                   for t_, s in zip(tinv, slabs)],
        u_free=[_inv_dot(t_, _bdot(m, vs)) for t_, m, vs in zip(tinv, a_ak, v_st)],
        a_r=[m.astype(BF16) for m in a_r],
        v_st=[vs.astype(BF16) for vs in v_st],
        decay_rows=[jnp.concatenate([stack(bo[:, s]), stack(ko[:, s])], 0).astype(BF16) for s in slabs],
        e_end=e_end,
        bonus=lanes(_bdot(rows(r_all * k_all * vec_ref[0:1, :]), ones_bd)) * v_all)


def _wkv_apply(c, prep, s_prev, vec_ref, ones_bd, lanes, slabs):
    from_state = [_bdot(lhs, sp, 1, 1) for lhs, sp in zip(prep["state_lhs"], s_prev)]
    uv = [jnp.concatenate([(fs[:2 * c] + uf).astype(BF16), vs], 0)
          for fs, uf, vs in zip(from_state, prep["u_free"], prep["v_st"])]
    o_st = [fs[2 * c:] + _dg(m, x, 1, 0) for fs, m, x in zip(from_state, prep["a_r"], uv)]
    o = [x[:c] + x[c:] for x in o_st]
    new_state = [sp * prep["e_end"][:, s] + _dg(x, rows_e, 0, 0)
                 for s, sp, x, rows_e in zip(slabs, s_prev, uv, prep["decay_rows"])]
    o_rows = jnp.concatenate(o, 0)
    dev = o_rows - _bdot(o_rows, ones_bd) * (1.0 / HEAD_DIM)
    var = _bdot(dev * dev, ones_bd) * (1.0 / HEAD_DIM)
    y = lanes(dev * lax.rsqrt(var + GN_EPS)) * vec_ref[1:2, :] + vec_ref[2:3, :] + prep["bonus"]
    return new_state, y


def _pair_states(s):
    bsz, nh, n, _ = s.shape
    s = s.reshape(bsz, nh // 2, 2, n, n)
    z = jnp.zeros_like(s[:, :, 0])
    top = jnp.concatenate([s[:, :, 0], z], -1)
    bot = jnp.concatenate([z, s[:, :, 1]], -1)
    return jnp.concatenate([top, bot], -2)


def _unpair_states(sp):
    bsz, npair, _, _ = sp.shape
    n = HEAD_DIM
    return jnp.stack([sp[:, :, :n, :n], sp[:, :, n:, n:]], 2).reshape(bsz, 2 * npair, n, n)


def _wkv(proj, s0, wl, bsz, t):
    r, lw, k, v, kk, b = proj
    n, d = r.shape
    c = min(WKV_CHUNK, t)
    rows_per_step = min(WKV_CHUNKS_PER_STEP * c, t)
    nc = t // rows_per_step
    npair = d // LANES
    tok = pl.BlockSpec((rows_per_step, d), lambda bb, i: (bb * nc + i, 0))
    st = pl.BlockSpec((None, npair, LANES, LANES), lambda bb, i: (bb, 0, 0, 0))
    tri = jnp.tril(jnp.ones((c, c), BF16))
    hid = jnp.arange(LANES) // HEAD_DIM
    ones_bd = (hid[:, None] == hid[None, :]).astype(BF16)
    y, s_out = pl.pallas_call(
        functools.partial(_wkv_kernel, c),
        grid=(bsz, nc),
        in_specs=[tok] * 6 + [st,
                              pl.BlockSpec((8, d), lambda bb, i: (0, 0)),
                              pl.BlockSpec((c, c), lambda bb, i: (0, 0)),
                              pl.BlockSpec((LANES, LANES), lambda bb, i: (0, 0))],
        out_specs=[tok, st],
        out_shape=[jax.ShapeDtypeStruct((n, d), F32),
                   jax.ShapeDtypeStruct((bsz, npair, LANES, LANES), F32)],
        scratch_shapes=[pltpu.VMEM((npair, LANES, LANES), F32)],
        compiler_params=_params(("parallel", "arbitrary")),
        name="wkv_scan",
    )(r, lw, k, v, kk, b, _pair_states(s0.astype(F32)), wl["scan_vec"], tri, ones_bd)
    return y, _unpair_states(s_out)


def _router(logits):
    row_i = lax.broadcasted_iota(jnp.int32, logits.shape, 0)
    row = row_i.astype(F32)
    far = 1e9
    is_g = row_i < N_GROUPS
    gl = jnp.where(is_g, logits, NEG_BIG)
    gmax = jnp.max(gl, 0, keepdims=True)
    gsel = jnp.min(jnp.where(gl == gmax, row, far), 0, keepdims=True)
    gprob = 1.0 / jnp.sum(jnp.where(is_g, jnp.exp(gl - gmax), 0.0), 0, keepdims=True)
    group_of = lax.shift_right_arithmetic(row_i - N_GROUPS, 2).astype(F32)
    in_group = (row_i >= N_GROUPS) & (row_i < N_GROUPS + N_EXPERTS) & (group_of == gsel)
    el = jnp.where(in_group, logits, NEG_BIG)
    v1 = jnp.max(el, 0, keepdims=True)
    i1 = jnp.min(jnp.where(el == v1, row, far), 0, keepdims=True)
    el2 = jnp.where(row == i1, NEG_BIG, el)
    v2 = jnp.max(el2, 0, keepdims=True)
    i2 = jnp.min(jnp.where(el2 == v2, row, far), 0, keepdims=True)
    e2 = jnp.exp(v2 - v1)
    w1 = gprob / (1.0 + e2)
    w2 = gprob * e2 / (1.0 + e2)
    return jnp.where(row == i1, w1, 0.0) + jnp.where(row == i2, w2, 0.0)


def _post_kernel(gated, alpha, *refs):
    if gated:
        x_ref, y_ref, g_ref, mods_ref, wo_ref, ln_ref, rw_ref, rb_ref, x1_ref, comb_ref = refs
        y = y_ref[...] * g_ref[...].astype(F32)
    else:
        x_ref, y_ref, mods_ref, wo_ref, ln_ref, rw_ref, rb_ref, x1_ref, comb_ref = refs
        y = y_ref[...]
    gt1, sh2, sc2 = mods_ref[2:3, :], mods_ref[3:4, :], mods_ref[4:5, :]
    x1 = _layer_norm(alpha * x_ref[...] + gt1 * _bdot(y, wo_ref[...]), ln_ref[0:1, :], ln_ref[1:2, :])
    x1_ref[...] = x1
    h2 = x1 * (1.0 + sc2) + sh2
    comb_t = _router(_dot3(rw_ref[...], h2, 1, 1) + rb_ref[...])
    pad_rows = jnp.zeros((LANES - comb_t.shape[0], comb_t.shape[1]), F32)
    comb_ref[...] = jnp.concatenate([comb_t, pad_rows], 0).T


def _post_mixer(x2d, y2d, g2d, mods_l, w_o, ln_pack, router_w, router_b, alpha, bsz, t, tm):
    n, d = x2d.shape
    nt = t // tm
    tok = pl.BlockSpec((tm, d), lambda b, i: (b * nt + i, 0))
    full = lambda a: pl.BlockSpec(a.shape, lambda b, i: (0,) * a.ndim)
    gated = g2d is not None
    acts = [x2d, y2d] + ([g2d] if gated else [])
    consts = [w_o, ln_pack, router_w, router_b]
    return pl.pallas_call(
        functools.partial(_post_kernel, gated, alpha),
        grid=(bsz, nt),
        in_specs=[tok] * len(acts) + [pl.BlockSpec((None, 6, d), lambda b, i: (b, 0, 0))]
        + [full(a) for a in consts],
        out_specs=[tok, pl.BlockSpec((tm, LANES), lambda b, i: (b * nt + i, 0))],
        out_shape=[jax.ShapeDtypeStruct((n, d), F32), jax.ShapeDtypeStruct((n, LANES), F32)],
        compiler_params=_params(("parallel", "parallel")),
        name="post_mixer",
    )(*acts, mods_l, *consts)


def _moe_kernel(alpha, x1_ref, comb_ref, mods_ref, ln_ref, wg_ref, wu_ref, wd_ref, x2_ref, h2_ref, acc_ref):
    e = pl.program_id(1)

    @pl.when(e == 0)
    def _():
        sh2, sc2 = mods_ref[3], mods_ref[4]
        h2_ref[...] = (x1_ref[...] * (1.0 + sc2) + sh2).astype(BF16)
        acc_ref[...] = jnp.zeros_like(acc_ref)

    h2 = h2_ref[...]
    per_step, f, d = wd_ref.shape
    lane = lax.broadcasted_iota(jnp.int32, comb_ref.shape, 1)
    comb = comb_ref[...]
    hidden = []
    for i in range(per_step):
        hg = _dg(h2, wg_ref[i].astype(BF16), 1, 0)
        hu = _dg(h2, wu_ref[i].astype(BF16), 1, 0)
        ce = jnp.sum(jnp.where(lane == e * per_step + (i + N_GROUPS), comb, 0.0), -1, keepdims=True)
        hidden.append((_silu(hg) * hu * ce).astype(BF16))
    acc_ref[...] += _dg(jnp.concatenate(hidden, 1), wd_ref[...].reshape(per_step * f, d).astype(BF16), 1, 0)

    @pl.when(e == pl.num_programs(1) - 1)
    def _():
        gt2 = mods_ref[5]
        x2_ref[...] = _layer_norm(alpha * x1_ref[...] + gt2 * acc_ref[...], ln_ref[0:1, :], ln_ref[1:2, :])


def _moe(x1, comb, mods_l, ln_pack, wg, wu, wd, layer, alpha, bsz, t):
    n, d = x1.shape
    _, ne, _, f = wg.shape
    per_step = MOE_EXPERTS_PER_STEP
    if t >= MOE_TILE:
        tm = MOE_TILE
        mods = mods_l.reshape(bsz, 6, 1, d)
        mods_spec = pl.BlockSpec((None, 6, 1, d), lambda i, e: (i // (t // tm), 0, 0, 0))
    else:
        tm = min(n, MOE_TILE)
        mods = jnp.repeat(mods_l.transpose(1, 0, 2), t, axis=1)
        mods_spec = pl.BlockSpec((6, tm, d), lambda i, e: (0, i, 0))
    return pl.pallas_call(
        functools.partial(_moe_kernel, alpha),
        grid=(n // tm, ne // per_step),
        in_specs=[
            pl.BlockSpec((tm, d), lambda i, e: (i, 0)),
            pl.BlockSpec((tm, LANES), lambda i, e: (i, 0)),
            mods_spec,
            pl.BlockSpec(ln_pack.shape, lambda i, e: (0, 0)),
            pl.BlockSpec((None, per_step, d, f), lambda i, e: (layer, e, 0, 0)),
            pl.BlockSpec((None, per_step, d, f), lambda i, e: (layer, e, 0, 0)),
            pl.BlockSpec((None, per_step, f, d), lambda i, e: (layer, e, 0, 0)),
        ],
        out_specs=pl.BlockSpec((tm, d), lambda i, e: (i, 0)),
        out_shape=jax.ShapeDtypeStruct((n, d), F32),
        scratch_shapes=[pltpu.VMEM((tm, d), BF16), pltpu.VMEM((tm, d), F32)],
        compiler_params=_params(("parallel", "arbitrary")),
        name="hmoe",
    )(x1, comb, mods, ln_pack, wg, wu, wd)


def _fox_bias_placement(d):
    place_q = np.zeros((LANES, d), np.float32)
    place_k = np.zeros((LANES, d), np.float32)
    const_q = np.zeros((1, d), np.float32)
    const_k = np.zeros((1, d), np.float32)
    for h in range(d // HEAD_DIM):
        base = (h // 2) * LANES + (HEAD_DIM if h % 2 == 0 else 0)
        for term in range(3):
            place_q[term * TERM_STRIDE + h, base + term] = 1.0
            place_k[term * TERM_STRIDE + h, base + 3 + term] = -1.0
            const_q[0, base + 3 + term] = 1.0
            const_k[0, base + term] = 1.0
    return (jnp.asarray(place_q, BF16), jnp.asarray(place_k, BF16), jnp.asarray(const_q),
            jnp.asarray(const_k))


def _kv_kernel(x_ref, kvw_ref, fw_ref, fb_ref, tri_ref, pq_ref, pk_ref, cq_ref, ck_ref,
               k_ref, v_ref, kb_ref, vb_ref, lf_ref, fc_ref, qbias_ref, kbias_ref, vbt_ref, carry_ref):
    d = x_ref.shape[1]

    @pl.when(pl.program_id(1) == 0)
    def _():
        carry_ref[...] = jnp.zeros_like(carry_ref)

    x = x_ref[...]
    kv = _bdot(x, kvw_ref[...])
    k, v = kv[:, :d], kv[:, d:]
    k_ref[...] = k.T
    v_t = v.T
    v_ref[...] = v_t
    vbt_ref[...] = v_t.astype(BF16)
    kb_ref[...] = k.astype(BF16)
    vb_ref[...] = v.astype(BF16)
    z = _dot3(x, fw_ref[...]) + fb_ref[...]
    lf = -_softplus(-z)
    lf_ref[...] = lf
    fc = _dot_exact_lhs(tri_ref[...], lf) + carry_ref[...]
    fc_ref[...] = fc
    carry_ref[...] = fc[fc.shape[0] - 1:, :]
    lane = lax.broadcasted_iota(jnp.int32, fc.shape, 1)
    hi, mid, lo = _split3(jnp.where(lane < d // HEAD_DIM, fc * LOG2E, 0.0))
    terms = (hi.astype(F32) + pltpu.roll(mid.astype(F32), TERM_STRIDE, axis=1)
             + pltpu.roll(lo.astype(F32), 2 * TERM_STRIDE, axis=1)).astype(BF16)
    qbias_ref[...] = (_dg(terms, pq_ref[...], 1, 0) + cq_ref[...]).astype(BF16)
    kbias_ref[...] = (_dg(terms, pk_ref[...], 1, 0) + ck_ref[...]).astype(BF16)


def _kv_proj(x2d, kv_w, f_w, f_b, bsz, t, tm):
    n, d = x2d.shape
    nt = t // tm
    tok = pl.BlockSpec((tm, d), lambda b, i: (b * nt + i, 0))
    nar = pl.BlockSpec((tm, LANES), lambda b, i: (b * nt + i, 0))
    feature_major = pl.BlockSpec((None, d, tm), lambda b, i: (b, 0, i))
    tri = jnp.tril(jnp.ones((tm, tm), BF16))
    full = lambda a: pl.BlockSpec(a.shape, lambda b, i: (0,) * a.ndim)
    consts = [kv_w, f_w, f_b, tri, *_fox_bias_placement(d)]
    return pl.pallas_call(
        _kv_kernel,
        grid=(bsz, nt),
        in_specs=[tok] + [full(a) for a in consts],
        out_specs=[feature_major, feature_major, tok, tok, nar, nar, tok, tok, feature_major],
        out_shape=[jax.ShapeDtypeStruct((bsz, d, t), F32), jax.ShapeDtypeStruct((bsz, d, t), F32),
                   jax.ShapeDtypeStruct((n, d), BF16), jax.ShapeDtypeStruct((n, d), BF16),
                   jax.ShapeDtypeStruct((n, LANES), F32), jax.ShapeDtypeStruct((n, LANES), F32),
                   jax.ShapeDtypeStruct((n, d), BF16), jax.ShapeDtypeStruct((n, d), BF16),
                   jax.ShapeDtypeStruct((bsz, d, t), BF16)],
        scratch_shapes=[pltpu.VMEM((1, LANES), F32)],
        compiler_params=_params(("parallel", "arbitrary")),
        name="kv_proj",
    )(x2d, *consts)


def _q_kernel(scale, x_ref, mods_ref, wq_ref, q_ref):
    h = x_ref[...] * (1.0 + mods_ref[1:2, :]) + mods_ref[0:1, :]
    q_ref[...] = (_bdot(h, wq_ref[...]) * scale).astype(BF16)


def _q_proj(x2d, mods_l, wq, scale, bsz, t, tm):
    n, d = x2d.shape
    nt = t // tm
    tok = pl.BlockSpec((tm, d), lambda b, i: (b * nt + i, 0))
    return pl.pallas_call(
        functools.partial(_q_kernel, scale),
        grid=(bsz, nt),
        in_specs=[tok, pl.BlockSpec((None, 6, d), lambda b, i: (b, 0, 0)),
                  pl.BlockSpec(wq.shape, lambda b, i: (0, 0))],
        out_specs=tok,
        out_shape=jax.ShapeDtypeStruct((n, d), BF16),
        compiler_params=_params(("parallel", "parallel")),
        name="q_proj",
    )(x2d, mods_l, wq)


def _fox_prompt_kernel(tq, q_ref, k_ref, vt_ref, qb_ref, kb_ref, o_ref):
    t = q_ref.shape[0]
    lane = lax.broadcasted_iota(jnp.int32, (1, LANES), 1)
    half = tq // 2
    key = lax.broadcasted_iota(jnp.int32, (half, tq), 0)
    qry = lax.broadcasted_iota(jnp.int32, (half, tq), 1)
    key_b = lax.broadcasted_iota(jnp.int32, (half, half), 0)
    qry_b = lax.broadcasted_iota(jnp.int32, (half, half), 1)
    own = [lane < HEAD_DIM, lane >= HEAD_DIM]
    k_aug = [jnp.where(own[hh], k_ref[...], kb_ref[...]) for hh in range(2)]
    v_t = vt_ref[...]
    for qi in range(t // tq):
        lo, mid, hi = qi * tq, qi * tq + half, (qi + 1) * tq
        heads = []
        for hh in range(2):
            q = jnp.where(own[hh], q_ref[lo:hi, :], qb_ref[lo:hi, :])
            v_h = v_t[hh * HEAD_DIM:(hh + 1) * HEAD_DIM]
            s_a = jnp.where(key <= qry, _dg(k_aug[hh][lo:mid], q, 1, 1), NEG_BIG)
            s_b = jnp.where(key_b <= qry_b, _dg(k_aug[hh][mid:hi], q[half:], 1, 1), NEG_BIG)
            m = jnp.max(s_a, 0, keepdims=True)
            if qi > 0:
                s_past = _dg(k_aug[hh][:lo], q, 1, 1)
                m = jnp.maximum(m, jnp.max(s_past, 0, keepdims=True))
            m_b = jnp.maximum(m[:, half:], jnp.max(s_b, 0, keepdims=True))
            m = jnp.concatenate([m[:, :half], m_b], 1)
            p = jnp.exp2(s_a - m)
            num = _dg(v_h[:, lo:mid], p.astype(BF16), 1, 0)
            den = jnp.sum(p, 0, keepdims=True)
            if qi > 0:
                p = jnp.exp2(s_past - m)
                num = num + _dg(v_h[:, :lo], p.astype(BF16), 1, 0)
                den = den + jnp.sum(p, 0, keepdims=True)
            p = jnp.exp2(s_b - m_b)
            num_b = num[:, half:] + _dg(v_h[:, mid:hi], p.astype(BF16), 1, 0)
            den_b = den[:, half:] + jnp.sum(p, 0, keepdims=True)
            heads.append(jnp.concatenate([num[:, :half] / den[:, :half], num_b / den_b], 1))
        o_ref[lo:hi, :] = jnp.concatenate(heads, 0).T


def _fox_prompt(q, kb, vb_t, q_bias, k_bias, bsz, t, tq):
    n, d = q.shape
    seq = pl.BlockSpec((t, LANES), lambda b, p: (b, p))
    return pl.pallas_call(
        functools.partial(_fox_prompt_kernel, tq),
        grid=(bsz, d // LANES),
        in_specs=[seq, seq, pl.BlockSpec((None, LANES, t), lambda b, p: (b, p, 0)), seq, seq],
        out_specs=seq,
        out_shape=jax.ShapeDtypeStruct((n, d), F32),
        compiler_params=_params(("parallel", "parallel")),
        name="fox_prompt",
    )(q, kb, vb_t, q_bias, k_bias)


def _fox_sample_kernel(q_ref, kn_ref, vn_ref, fn_ref, fnt_ref, ck_ref, cv_ref, clf_ref, upper_ref,
                       o_ref, m_ref, l_ref, acc_ref, carry_ref):
    step = pl.program_id(1)
    t, d = q_ref.shape
    nh = d // HEAD_DIM

    @pl.when(step == 0)
    def _():
        m_ref[...] = jnp.full_like(m_ref, NEG_BIG)
        l_ref[...] = jnp.zeros_like(l_ref)
        acc_ref[...] = jnp.zeros_like(acc_ref)
        carry_ref[...] = jnp.zeros_like(carry_ref)

    heads = range(nh)
    cols = [slice(h * HEAD_DIM, (h + 1) * HEAD_DIM) for h in heads]
    q = [q_ref[:, c] for c in cols]

    def attend(keys, values, feature_major, bias, mask):
        s = [_dg(q[h], keys[h], 1, 0 if feature_major else 1) + bias[h] for h in heads]
        if mask is not None:
            s = [jnp.where(mask, x, NEG_BIG) for x in s]
        m_old = m_ref[...]
        m_new = jnp.maximum(m_old, jnp.stack([jnp.max(x, -1, keepdims=True) for x in s]))
        alpha = jnp.exp(m_old - m_new)
        p = [jnp.exp(x - m_new[h]) for h, x in enumerate(s)]
        l_ref[...] = alpha * l_ref[...] + jnp.stack([jnp.sum(x, -1, keepdims=True) for x in p])
        acc_ref[...] = alpha * acc_ref[...] + jnp.stack(
            [_dg(p[h].astype(BF16), values[h], 1, 1 if feature_major else 0) for h in heads])
        m_ref[...] = m_new

    clf = clf_ref[...]
    suf = _dot_exact_rhs(clf, upper_ref[...]) + carry_ref[...]
    carry_ref[...] = carry_ref[...] + jnp.sum(clf, -1, keepdims=True)
    fn = fn_ref[...]
    attend([ck_ref[h].astype(BF16) for h in heads], [cv_ref[h].astype(BF16) for h in heads], True,
           [fn[:, h:h + 1] + suf[h:h + 1, :] for h in heads], None)

    @pl.when(step == pl.num_programs(1) - 1)
    def _():
        rr = lax.broadcasted_iota(jnp.int32, (t, t), 0)
        cc = lax.broadcasted_iota(jnp.int32, (t, t), 1)
        fnt = fnt_ref[...]
        attend([kn_ref[:, c] for c in cols], [vn_ref[:, c] for c in cols], False,
               [fn[:, h:h + 1] - fnt[h:h + 1, :] for h in heads], cc <= rr)
        out = acc_ref[...] / l_ref[...]
        o_ref[...] = jnp.concatenate([out[h] for h in heads], 1)


def _fox_sample(q, kb, vb, fcum, cache_k, cache_v, cache_logf, bsz, t, tk):
    n, d = q.shape
    nh = d // HEAD_DIM
    plen = cache_logf.shape[1]
    nk = plen // tk
    ck = cache_k.transpose(0, 2, 3, 1)
    cv = cache_v.transpose(0, 2, 3, 1)
    clf_t = cache_logf.astype(F32).transpose(0, 2, 1)
    fn_t = fcum[:, :nh].reshape(bsz, t, nh).transpose(0, 2, 1)
    upper = (jnp.arange(tk)[:, None] > jnp.arange(tk)[None, :]).astype(BF16)
    tok = pl.BlockSpec((t, d), lambda b, j: (b, 0))
    past = pl.BlockSpec((None, nh, HEAD_DIM, tk), lambda b, j: (b, 0, 0, nk - 1 - j))
    return pl.pallas_call(
        _fox_sample_kernel,
        grid=(bsz, nk),
        in_specs=[tok, tok, tok,
                  pl.BlockSpec((t, LANES), lambda b, j: (b, 0)),
                  pl.BlockSpec((None, nh, t), lambda b, j: (b, 0, 0)),
                  past, past,
                  pl.BlockSpec((None, nh, tk), lambda b, j: (b, 0, nk - 1 - j)),
                  pl.BlockSpec((tk, tk), lambda b, j: (0, 0))],
        out_specs=tok,
        out_shape=jax.ShapeDtypeStruct((n, d), F32),
        scratch_shapes=[pltpu.VMEM((nh, t, 1), F32), pltpu.VMEM((nh, t, 1), F32),
                        pltpu.VMEM((nh, t, HEAD_DIM), F32), pltpu.VMEM((nh, 1), F32)],
        compiler_params=_params(("parallel", "arbitrary")),
        name="fox_sample",
    )(q, kb, vb, fcum, fn_t, ck, cv, clf_t, upper)


def _tile(t, cap):
    return min(t, cap)


def _trunk(x, mods, shift_in, wkv_in, cache, w):
    bsz, t, d = x.shape
    depth = mods.shape[0]
    n_a = w["n_a"]
    nh = d // HEAD_DIM
    x2d = x.reshape(bsz * t, d)
    shifts, states = [], []
    kv = None
    for l in range(depth):
        mods_l = mods[l]
        if l < n_a:
            wl = w["rwkv"][l]
            proj, last = _rwkv_proj(x2d, shift_in[l], mods_l, wl, bsz, t, _tile(t, 512))
            y, s_new = _wkv(proj[:6], wkv_in[l], wl, bsz, t)
            shifts.append(last)
            states.append(s_new.astype(wkv_in.dtype))
            mixer_out, gate, w_o = y, proj[6], wl["w_o"]
        else:
            if kv is None:
                kv = _kv_proj(x2d, w["kv_w"], w["f_w"], w["f_b"], bsz, t, _tile(t, 512))
            k_sh, v_sh, kb, vb, lf, fcum, q_bias, k_bias, vb_t = kv
            j = l - n_a
            q_scale = HEAD_DIM ** -0.5 * (LOG2E if cache is None else 1.0)
            q = _q_proj(x2d, mods_l, w["b_wq"][j], q_scale, bsz, t, _tile(t, 1024))
            if cache is None:
                mixer_out = _fox_prompt(q, kb, vb_t, q_bias, k_bias, bsz, t, _tile(t, 1024))
            else:
                mixer_out = _fox_sample(q, kb, vb, fcum, cache[0], cache[1], cache[2], bsz, t,
                                        _tile(cache[0].shape[1], 1024))
            gate, w_o = None, w["b_wo"][j]
        x1, comb = _post_mixer(x2d, mixer_out, gate, mods_l, w_o, w["ln"][l][0], w["router_w"][l],
                               w["router_b"][l], w["alpha"], bsz, t, _tile(t, 1024))
        x2d = _moe(x1, comb, mods_l, w["ln"][l][1], w["exp_wg"], w["exp_wu"], w["exp_wd"], l,
                   w["alpha"], bsz, t)
    k_sh, v_sh, lf = kv[0], kv[1], kv[4]
    by_head = lambda z: z.reshape(bsz, nh, HEAD_DIM, t).transpose(0, 3, 1, 2)
    return (x2d.reshape(bsz, t, d), jnp.stack(shifts), jnp.stack(states), by_head(k_sh), by_head(v_sh),
            lf[:, :nh].reshape(bsz, t, nh).astype(x.dtype))


def _prepare_weights(ln_g, ln_b, a_mu, a_w_rkv, a_w0, a_w1, a_w2, a_a0, a_a1, a_a2, a_g1, a_g2, a_k_k,
                     a_k_a, a_r_k, a_lnx_g, a_lnx_b, a_w_o, kv_w, f_w, f_b, b_wq, b_wo, rg_w, rg_b,
                     re_w, re_b, exp_wg, exp_wu, exp_wd):
    depth, _, d = ln_g.shape
    n_a = a_mu.shape[0]
    nh = d // HEAD_DIM
    alpha = (2.0 * depth) ** 0.25
    zrow = jnp.zeros((d,), F32)
    head_of = jnp.arange(d) // HEAD_DIM
    head_down = (head_of[:, None] == jnp.arange(LANES)[None, :]).astype(BF16)
    rwkv = []
    for l in range(n_a):
        rwkv.append(dict(
            mu=a_mu[l],
            vec=jnp.stack([a_w0[l], a_a0[l], a_k_k[l], a_k_a[l], zrow, zrow, zrow, zrow]),
            scan_vec=jnp.stack([a_r_k[l].reshape(d), a_lnx_g[l], a_lnx_b[l], zrow, zrow, zrow, zrow, zrow]),
            w_rkv=a_w_rkv[l].astype(BF16), w1=a_w1[l].astype(BF16), w2=a_w2[l].astype(BF16),
            a1=a_a1[l].astype(BF16), a2=a_a2[l].astype(BF16), g1=a_g1[l].astype(BF16),
            g2=a_g2[l].astype(BF16), w_o=a_w_o[l].astype(BF16),
            head_down=head_down, head_up=head_down.T))
    ln = [[jnp.stack([ln_g[l, s], ln_b[l, s], zrow, zrow, zrow, zrow, zrow, zrow])
           for s in range(2)] for l in range(depth)]
    pad = ROUTER_ROWS - N_GROUPS - N_EXPERTS
    router_w = [jnp.concatenate([rg_w[l], re_w[l], jnp.zeros((d, pad), F32)], 1).T for l in range(depth)]
    router_b = [jnp.concatenate([rg_b[l], re_b[l], jnp.zeros((pad,), F32)])[:, None] for l in range(depth)]
    return dict(
        n_a=n_a, alpha=alpha, rwkv=rwkv, ln=ln, router_w=router_w, router_b=router_b,
        kv_w=kv_w.astype(BF16),
        f_w=jnp.concatenate([f_w, jnp.zeros((d, LANES - nh), F32)], 1),
        f_b=jnp.concatenate([f_b, jnp.zeros((LANES - nh,), F32)])[None, :],
        b_wq=b_wq.astype(BF16), b_wo=b_wo.astype(BF16),
        exp_wg=exp_wg, exp_wu=exp_wu, exp_wd=exp_wd)


def kernel(x_prompt, x_sample, state_shift, state_wkv, cache_k, cache_v, cache_logf, c_prompt, c_sample,
           ada_w, ada_b, ln_g, ln_b, a_mu, a_w_rkv, a_w0, a_w1, a_w2, a_a0, a_a1, a_a2, a_g1, a_g2, a_k_k,
           a_k_a, a_r_k, a_lnx_g, a_lnx_b, a_w_o, kv_w, f_w, f_b, b_wq, b_wo, rg_w, rg_b, re_w, re_b,
           exp_wg, exp_wu, exp_wd):
    w = _prepare_weights(ln_g, ln_b, a_mu, a_w_rkv, a_w0, a_w1, a_w2, a_a0, a_a1, a_a2, a_g1, a_g2,
                         a_k_k, a_k_a, a_r_k, a_lnx_g, a_lnx_b, a_w_o, kv_w, f_w, f_b, b_wq, b_wo,
                         rg_w, rg_b, re_w, re_b, exp_wg, exp_wu, exp_wd)
    bp, _, d = x_prompt.shape
    n_a = a_mu.shape[0]
    nh = d // HEAD_DIM
    mods = _ada_mods(jnp.concatenate([c_prompt, c_sample], 0), ada_w, ada_b)
    zero_shift = jnp.zeros((n_a, bp, d), x_prompt.dtype)
    zero_wkv = jnp.zeros((n_a, bp, nh, HEAD_DIM, HEAD_DIM), x_prompt.dtype)
    outs_p = _trunk(x_prompt, mods[:, :bp], zero_shift, zero_wkv, None, w)
    outs_s = _trunk(x_sample, mods[:, bp:], state_shift, state_wkv, (cache_k, cache_v, cache_logf), w)
    y_p, p_shift, p_wkv, p_k, p_v, p_logf = outs_p
    y_s, s_shift, s_wkv, s_k, s_v, s_logf = outs_s
    return (y_p, y_s, p_shift, p_wkv, p_k, p_v, p_logf, s_shift, s_wkv, s_k, s_v, s_logf)
```

```python
import functools

import jax
import jax.numpy as jnp
import numpy as np
from jax import lax
from jax.experimental import pallas as pl
from jax.experimental.pallas import tpu as pltpu

F32 = jnp.float32
BF16 = jnp.bfloat16

HEAD_DIM = 64
LANES = 128
N_GROUPS = 4
EXP_PER_GROUP = 4
N_EXPERTS = N_GROUPS * EXP_PER_GROUP
LN_EPS = 1e-5
GN_EPS = 64e-5
WKV_CHUNK = 64
WKV_CHUNKS_PER_STEP = 16
MOE_EXPERTS_PER_STEP = 4
MOE_TILE = 1024
ROUTER_ROWS = 32
LOG2E = 1.4426950408889634
TERM_STRIDE = 32
VMEM_LIMIT = 56 * 1024 * 1024
NEG_BIG = -1e30


def _params(sem):
    return pltpu.CompilerParams(dimension_semantics=sem, vmem_limit_bytes=VMEM_LIMIT)


def _dg(a, b, ca, cb):
    return lax.dot_general(a, b, (((ca,), (cb,)), ((), ())), preferred_element_type=F32)


def _bdot(a, b, ca=1, cb=0):
    return _dg(a.astype(BF16), b.astype(BF16), ca, cb)


def _split3(x):
    hi = x.astype(BF16)
    r1 = x - hi.astype(F32)
    mid = r1.astype(BF16)
    lo = (r1 - mid.astype(F32)).astype(BF16)
    return hi, mid, lo


def _dot_exact_rhs(a, b_exact, ca=1, cb=0):
    hi, mid, lo = _split3(a)
    bb = b_exact.astype(BF16)
    return _dg(hi, bb, ca, cb) + _dg(mid, bb, ca, cb) + _dg(lo, bb, ca, cb)


def _dot_exact_lhs(a_exact, b, ca=1, cb=0):
    hi, mid, lo = _split3(b)
    aa = a_exact.astype(BF16)
    return _dg(aa, hi, ca, cb) + _dg(aa, mid, ca, cb) + _dg(aa, lo, ca, cb)


def _dot3(a, b, ca=1, cb=0):
    ah = a.astype(BF16)
    al = (a - ah.astype(F32)).astype(BF16)
    bh = b.astype(BF16)
    bl = (b - bh.astype(F32)).astype(BF16)
    return _dg(ah, bh, ca, cb) + _dg(ah, bl, ca, cb) + _dg(al, bh, ca, cb)


_inv_dot = _bdot


def _layer_norm(z, g, b):
    mu = jnp.mean(z, -1, keepdims=True)
    d = z - mu
    var = jnp.mean(d * d, -1, keepdims=True)
    return d * lax.rsqrt(var + LN_EPS) * g + b


def _softplus(z):
    return jnp.maximum(z, 0.0) + jnp.log(1.0 + jnp.exp(-jnp.abs(z)))


def _sigmoid(z):
    return 1.0 / (1.0 + jnp.exp(-z))


def _silu(z):
    return z * _sigmoid(z)


def _ada_kernel(c_ref, w_ref, b_ref, o_ref):
    o_ref[...] = _dot3(_silu(c_ref[...]), w_ref[...]) + b_ref[...]


def _ada_mods(c_all, ada_w, ada_b):
    depth, d, d6 = ada_w.shape
    bsz = c_all.shape[0]
    tn = d
    out = pl.pallas_call(
        _ada_kernel,
        grid=(depth, d6 // tn),
        in_specs=[
            pl.BlockSpec((bsz, d), lambda l, j: (0, 0)),
            pl.BlockSpec((None, d, tn), lambda l, j: (l, 0, j)),
            pl.BlockSpec((None, 1, tn), lambda l, j: (l, 0, j)),
        ],
        out_specs=pl.BlockSpec((None, bsz, tn), lambda l, j: (l, 0, j)),
        out_shape=jax.ShapeDtypeStruct((depth, bsz, d6), F32),
        compiler_params=_params(("parallel", "parallel")),
        name="ada_mods",
    )(c_all, ada_w, ada_b.reshape(depth, 1, d6))
    return out.reshape(depth, bsz, 6, d)


def _rwkv_proj_kernel(x_ref, xp_ref, shift_ref, mods_ref, mu_ref, vec_ref, wrkv_ref, w1_ref, w2_ref,
                      a1_ref, a2_ref, g1_ref, g2_ref, hd_ref, hu_ref,
                      r_ref, lw_ref, k_ref, v_ref, kk_ref, b_ref, g_ref, last_ref):
    tm = x_ref.shape[0]
    sh1 = mods_ref[0:1, :]
    sc1 = mods_ref[1:2, :]
    h = x_ref[...] * (1.0 + sc1) + sh1
    h_prev_tile = xp_ref[7:8, :] * (1.0 + sc1) + sh1
    prev_row = jnp.where(pl.program_id(1) == 0, shift_ref[...], h_prev_tile)
    row = lax.broadcasted_iota(jnp.int32, (tm, 1), 0)
    xx = jnp.where(row == 0, prev_row, pltpu.roll(h, 1, axis=0)) - h

    def mix(i):
        return h + xx * mu_ref[i:i + 1, :]

    w0, a0 = vec_ref[0:1, :], vec_ref[1:2, :]
    k_k, k_a = vec_ref[2:3, :], vec_ref[3:4, :]
    r = _bdot(mix(0), wrkv_ref[0])
    k = _bdot(mix(1), wrkv_ref[1])
    v = _bdot(mix(2), wrkv_ref[2])
    ww = w0 + _bdot(jnp.tanh(_bdot(mix(3), w1_ref[...])), w2_ref[...])
    w_log = -_softplus(-ww) - 0.5
    a = _sigmoid(a0 + _bdot(_bdot(mix(4), a1_ref[...]), a2_ref[...]))
    g = _bdot(_sigmoid(_bdot(mix(5), g1_ref[...])), g2_ref[...])
    kk = k * k_k
    head_sq = _bdot(kk * kk, hd_ref[...])
    hi = head_sq.astype(BF16)
    lo = (head_sq - hi.astype(F32)).astype(BF16)
    ss = _dg(hi, hu_ref[...], 1, 0) + _dg(lo, hu_ref[...], 1, 0)
    kk = kk * lax.rsqrt(jnp.maximum(ss, 1e-24))
    r_ref[...] = r.astype(BF16)
    lw_ref[...] = -jnp.exp(w_log)
    k_ref[...] = (k * (1.0 + (a - 1.0) * k_a)).astype(BF16)
    v_ref[...] = v.astype(BF16)
    kk_ref[...] = kk.astype(BF16)
    b_ref[...] = (kk * a).astype(BF16)
    g_ref[...] = g.astype(BF16)
    last_ref[...] = h[tm - 1:tm, :]


def _rwkv_proj(x2d, shift_prev, mods_l, wl, bsz, t, tm):
    n, d = x2d.shape
    nt = t // tm
    tok = pl.BlockSpec((tm, d), lambda b, i: (b * nt + i, 0))
    full = lambda a: pl.BlockSpec(a.shape, lambda b, i: (0,) * a.ndim)
    weights = [wl["mu"], wl["vec"], wl["w_rkv"], wl["w1"], wl["w2"], wl["a1"], wl["a2"], wl["g1"],
               wl["g2"], wl["head_down"], wl["head_up"]]
    outs = pl.pallas_call(
        _rwkv_proj_kernel,
        grid=(bsz, nt),
        in_specs=[
            tok,
            pl.BlockSpec((8, d), lambda b, i: (jnp.maximum((b * nt + i) * (tm // 8) - 1, 0), 0)),
            pl.BlockSpec((None, 1, d), lambda b, i: (b, 0, 0)),
            pl.BlockSpec((None, 6, d), lambda b, i: (b, 0, 0)),
        ] + [full(a) for a in weights],
        out_specs=[tok] * 7 + [pl.BlockSpec((None, 1, d), lambda b, i: (b, 0, 0))],
        out_shape=[jax.ShapeDtypeStruct((n, d), F32 if i == 1 else BF16) for i in range(7)]
        + [jax.ShapeDtypeStruct((bsz, 1, d), F32)],
        compiler_params=_params(("parallel", "arbitrary")),
        name="rwkv_proj",
    )(x2d, x2d, shift_prev.reshape(bsz, 1, d), mods_l, *weights)
    return outs[:7], outs[7].reshape(bsz, d)


def _wkv_kernel(c, r_ref, lw_ref, k_ref, v_ref, kk_ref, b_ref, s0_ref, vec_ref, tri_ref, ones_ref,
                y_ref, s_out_ref, state_ref):
    n_chunks = r_ref.shape[0] // c
    n_pairs = r_ref.shape[1] // LANES
    first = pl.program_id(1) == 0

    @pl.when(first)
    def _():
        state_ref[...] = s0_ref[...]

    lane = lax.broadcasted_iota(jnp.int32, (c, LANES), 1)
    low = lane < HEAD_DIM
    ri = lax.broadcasted_iota(jnp.int32, (2 * c, 2 * c), 0)
    ci = lax.broadcasted_iota(jnp.int32, (2 * c, 2 * c), 1)
    same = (ri >= c) == (ci >= c)
    strict = same & (ci < ri)
    incl = same & (ci <= ri)
    eye = (ri == ci).astype(F32)
    tri = tri_ref[...]
    ones_bd = ones_ref[...]

    def stack(z):
        return jnp.concatenate([jnp.where(low, z, 0.0), jnp.where(low, 0.0, z)], axis=0)

    pairs = range(n_pairs)
    slabs = [slice(p * LANES, (p + 1) * LANES) for p in pairs]

    def rows(z):
        return jnp.concatenate([z[:, s] for s in slabs], 0)

    def lanes(z):
        return jnp.concatenate([z[p * c:(p + 1) * c] for p in pairs], 1)

    def prepare(ch):
        rs = slice(ch * c, (ch + 1) * c)
        return _wkv_prepare(c, lw_ref[rs, :], r_ref[rs, :], k_ref[rs, :], v_ref[rs, :], kk_ref[rs, :],
                            b_ref[rs, :], vec_ref, tri, ones_bd, stack, rows, lanes, slabs, strict, incl, eye)

    s_prev = [state_ref[p] for p in pairs]
    ready = prepare(0)
    for ch in range(n_chunks):
        upcoming = prepare(ch + 1) if ch + 1 < n_chunks else None
        s_prev, y_ref[ch * c:(ch + 1) * c, :] = _wkv_apply(c, ready, s_prev, vec_ref, ones_bd, lanes, slabs)
        ready = upcoming
    state_ref[...] = jnp.stack(s_prev)

    @pl.when(pl.program_id(1) == pl.num_programs(1) - 1)
    def _():
        s_out_ref[...] = state_ref[...]


def _wkv_prepare(c, lw_all, r_bf, k_bf, v_bf, kk_bf, b_bf, vec_ref, tri, ones_bd, stack, rows, lanes,
                 slabs, strict, incl, eye):
    cum_all = _dot_exact_lhs(tri, lw_all)
    prev_all = cum_all - lw_all
    mid_all = cum_all[c // 2 - 1:c // 2, :]
    end_all = cum_all[c - 1:c, :]
    e_in = jnp.exp(mid_all - cum_all)
    e_out = jnp.exp(end_all - cum_all)
    e_end = jnp.exp(end_all)
    r_all, k_all, v_all = r_bf.astype(F32), k_bf.astype(F32), v_bf.astype(F32)
    a_all, b_all = -kk_bf.astype(F32), b_bf.astype(F32)
    aq, rq = a_all * jnp.exp(prev_all - mid_all), r_all * jnp.exp(cum_all - mid_all)
    bi, ki = b_all * e_in, k_all * e_in
    a0, r0 = a_all * jnp.exp(prev_all), r_all * jnp.exp(cum_all)
    bo, ko = b_all * e_out, k_all * e_out

    gram = [_bdot(jnp.concatenate([stack(aq[:, s]), stack(rq[:, s])], 0),
                  jnp.concatenate([stack(bi[:, s]), stack(ki[:, s])], 0), 1, 1) for s in slabs]
    a_ab = [jnp.where(strict, g[:2 * c, :2 * c], 0.0) for g in gram]
    a_ak = [jnp.where(strict, g[:2 * c, 2 * c:], 0.0) for g in gram]
    a_r = [jnp.concatenate([jnp.where(incl, g[2 * c:, :2 * c], 0.0),
                            jnp.where(incl, g[2 * c:, 2 * c:], 0.0)], 1) for g in gram]
    tinv = [eye + m for m in a_ab]
    pw = a_ab
    for _ in range(max(c.bit_length() - 2, 0)):
        pw = [_inv_dot(m, m) for m in pw]
        tinv = [t_ + _inv_dot(t_, m) for t_, m in zip(tinv, pw)]
    v_st = [stack(v_all[:, s]) for s in slabs]
    return dict(
        state_lhs=[jnp.concatenate([_inv_dot(t_, stack(a0[:, s])), stack(r0[:, s])], 0).astype(BF16)
                   for t_, s in zip(tinv, slabs)],
        u_free=[_inv_dot(t_, _bdot(m, vs)) for t_, m, vs in zip(tinv, a_ak, v_st)],
        a_r=[m.astype(BF16) for m in a_r],
        v_st=[vs.astype(BF16) for vs in v_st],
        decay_rows=[jnp.concatenate([stack(bo[:, s]), stack(ko[:, s])], 0).astype(BF16) for s in slabs],
        e_end=e_end,
        bonus=lanes(_bdot(rows(r_all * k_all * vec_ref[0:1, :]), ones_bd)) * v_all)


def _wkv_apply(c, prep, s_prev, vec_ref, ones_bd, lanes, slabs):
    from_state = [_bdot(lhs, sp, 1, 1) for lhs, sp in zip(prep["state_lhs"], s_prev)]
    uv = [jnp.concatenate([(fs[:2 * c] + uf).astype(BF16), vs], 0)
          for fs, uf, vs in zip(from_state, prep["u_free"], prep["v_st"])]
    o_st = [fs[2 * c:] + _dg(m, x, 1, 0) for fs, m, x in zip(from_state, prep["a_r"], uv)]
    o = [x[:c] + x[c:] for x in o_st]
    new_state = [sp * prep["e_end"][:, s] + _dg(x, rows_e, 0, 0)
                 for s, sp, x, rows_e in zip(slabs, s_prev, uv, prep["decay_rows"])]
    o_rows = jnp.concatenate(o, 0)
    dev = o_rows - _bdot(o_rows, ones_bd) * (1.0 / HEAD_DIM)
    var = _bdot(dev * dev, ones_bd) * (1.0 / HEAD_DIM)
    y = lanes(dev * lax.rsqrt(var + GN_EPS)) * vec_ref[1:2, :] + vec_ref[2:3, :] + prep["bonus"]
    return new_state, y


def _pair_states(s):
    bsz, nh, n, _ = s.shape
    s = s.reshape(bsz, nh // 2, 2, n, n)
    z = jnp.zeros_like(s[:, :, 0])
    top = jnp.concatenate([s[:, :, 0], z], -1)
    bot = jnp.concatenate([z, s[:, :, 1]], -1)
    return jnp.concatenate([top, bot], -2)


def _unpair_states(sp):
    bsz, npair, _, _ = sp.shape
    n = HEAD_DIM
    return jnp.stack([sp[:, :, :n, :n], sp[:, :, n:, n:]], 2).reshape(bsz, 2 * npair, n, n)


def _wkv(proj, s0, wl, bsz, t):
    r, lw, k, v, kk, b = proj
    n, d = r.shape
    c = min(WKV_CHUNK, t)
    rows_per_step = min(WKV_CHUNKS_PER_STEP * c, t)
    nc = t // rows_per_step
    npair = d // LANES
    tok = pl.BlockSpec((rows_per_step, d), lambda bb, i: (bb * nc + i, 0))
    st = pl.BlockSpec((None, npair, LANES, LANES), lambda bb, i: (bb, 0, 0, 0))
    tri = jnp.tril(jnp.ones((c, c), BF16))
    hid = jnp.arange(LANES) // HEAD_DIM
    ones_bd = (hid[:, None] == hid[None, :]).astype(BF16)
    y, s_out = pl.pallas_call(
        functools.partial(_wkv_kernel, c),
        grid=(bsz, nc),
        in_specs=[tok] * 6 + [st,
                              pl.BlockSpec((8, d), lambda bb, i: (0, 0)),
                              pl.BlockSpec((c, c), lambda bb, i: (0, 0)),
                              pl.BlockSpec((LANES, LANES), lambda bb, i: (0, 0))],
        out_specs=[tok, st],
        out_shape=[jax.ShapeDtypeStruct((n, d), F32),
                   jax.ShapeDtypeStruct((bsz, npair, LANES, LANES), F32)],
        scratch_shapes=[pltpu.VMEM((npair, LANES, LANES), F32)],
        compiler_params=_params(("parallel", "arbitrary")),
        name="wkv_scan",
    )(r, lw, k, v, kk, b, _pair_states(s0.astype(F32)), wl["scan_vec"], tri, ones_bd)
    return y, _unpair_states(s_out)


def _router(logits):
    row_i = lax.broadcasted_iota(jnp.int32, logits.shape, 0)
    row = row_i.astype(F32)
    far = 1e9
    is_g = row_i < N_GROUPS
    gl = jnp.where(is_g, logits, NEG_BIG)
    gmax = jnp.max(gl, 0, keepdims=True)
    gsel = jnp.min(jnp.where(gl == gmax, row, far), 0, keepdims=True)
    gprob = 1.0 / jnp.sum(jnp.where(is_g, jnp.exp(gl - gmax), 0.0), 0, keepdims=True)
    group_of = lax.shift_right_arithmetic(row_i - N_GROUPS, 2).astype(F32)
    in_group = (row_i >= N_GROUPS) & (row_i < N_GROUPS + N_EXPERTS) & (group_of == gsel)
    el = jnp.where(in_group, logits, NEG_BIG)
    v1 = jnp.max(el, 0, keepdims=True)
    i1 = jnp.min(jnp.where(el == v1, row, far), 0, keepdims=True)
    el2 = jnp.where(row == i1, NEG_BIG, el)
    v2 = jnp.max(el2, 0, keepdims=True)
    i2 = jnp.min(jnp.where(el2 == v2, row, far), 0, keepdims=True)
    e2 = jnp.exp(v2 - v1)
    w1 = gprob / (1.0 + e2)
    w2 = gprob * e2 / (1.0 + e2)
    return jnp.where(row == i1, w1, 0.0) + jnp.where(row == i2, w2, 0.0)


def _post_kernel(gated, alpha, *refs):
    if gated:
        x_ref, y_ref, g_ref, mods_ref, wo_ref, ln_ref, rw_ref, rb_ref, x1_ref, comb_ref = refs
        y = y_ref[...] * g_ref[...].astype(F32)
    else:
        x_ref, y_ref, mods_ref, wo_ref, ln_ref, rw_ref, rb_ref, x1_ref, comb_ref = refs
        y = y_ref[...]
    gt1, sh2, sc2 = mods_ref[2:3, :], mods_ref[3:4, :], mods_ref[4:5, :]
    x1 = _layer_norm(alpha * x_ref[...] + gt1 * _bdot(y, wo_ref[...]), ln_ref[0:1, :], ln_ref[1:2, :])
    x1_ref[...] = x1
    h2 = x1 * (1.0 + sc2) + sh2
    comb_t = _router(_dot3(rw_ref[...], h2, 1, 1) + rb_ref[...])
    pad_rows = jnp.zeros((LANES - comb_t.shape[0], comb_t.shape[1]), F32)
    comb_ref[...] = jnp.concatenate([comb_t, pad_rows], 0).T


def _post_mixer(x2d, y2d, g2d, mods_l, w_o, ln_pack, router_w, router_b, alpha, bsz, t, tm):
    n, d = x2d.shape
    nt = t // tm
    tok = pl.BlockSpec((tm, d), lambda b, i: (b * nt + i, 0))
    full = lambda a: pl.BlockSpec(a.shape, lambda b, i: (0,) * a.ndim)
    gated = g2d is not None
    acts = [x2d, y2d] + ([g2d] if gated else [])
    consts = [w_o, ln_pack, router_w, router_b]
    return pl.pallas_call(
        functools.partial(_post_kernel, gated, alpha),
        grid=(bsz, nt),
        in_specs=[tok] * len(acts) + [pl.BlockSpec((None, 6, d), lambda b, i: (b, 0, 0))]
        + [full(a) for a in consts],
        out_specs=[tok, pl.BlockSpec((tm, LANES), lambda b, i: (b * nt + i, 0))],
        out_shape=[jax.ShapeDtypeStruct((n, d), F32), jax.ShapeDtypeStruct((n, LANES), F32)],
        compiler_params=_params(("parallel", "parallel")),
        name="post_mixer",
    )(*acts, mods_l, *consts)


def _moe_kernel(alpha, x1_ref, comb_ref, mods_ref, ln_ref, wg_ref, wu_ref, wd_ref, x2_ref, h2_ref, acc_ref):
    e = pl.program_id(1)

    @pl.when(e == 0)
    def _():
        sh2, sc2 = mods_ref[3], mods_ref[4]
        h2_ref[...] = (x1_ref[...] * (1.0 + sc2) + sh2).astype(BF16)
        acc_ref[...] = jnp.zeros_like(acc_ref)

    h2 = h2_ref[...]
    per_step, f, d = wd_ref.shape
    lane = lax.broadcasted_iota(jnp.int32, comb_ref.shape, 1)
    comb = comb_ref[...]
    hidden = []
    for i in range(per_step):
        hg = _dg(h2, wg_ref[i].astype(BF16), 1, 0)
        hu = _dg(h2, wu_ref[i].astype(BF16), 1, 0)
        ce = jnp.sum(jnp.where(lane == e * per_step + (i + N_GROUPS), comb, 0.0), -1, keepdims=True)
        hidden.append((_silu(hg) * hu * ce).astype(BF16))
    acc_ref[...] += _dg(jnp.concatenate(hidden, 1), wd_ref[...].reshape(per_step * f, d).astype(BF16), 1, 0)

    @pl.when(e == pl.num_programs(1) - 1)
    def _():
        gt2 = mods_ref[5]
        x2_ref[...] = _layer_norm(alpha * x1_ref[...] + gt2 * acc_ref[...], ln_ref[0:1, :], ln_ref[1:2, :])


def _moe(x1, comb, mods_l, ln_pack, wg, wu, wd, layer, alpha, bsz, t):
    n, d = x1.shape
    _, ne, _, f = wg.shape
    per_step = MOE_EXPERTS_PER_STEP
    if t >= MOE_TILE:
        tm = MOE_TILE
        mods = mods_l.reshape(bsz, 6, 1, d)
        mods_spec = pl.BlockSpec((None, 6, 1, d), lambda i, e: (i // (t // tm), 0, 0, 0))
    else:
        tm = min(n, MOE_TILE)
        mods = jnp.repeat(mods_l.transpose(1, 0, 2), t, axis=1)
        mods_spec = pl.BlockSpec((6, tm, d), lambda i, e: (0, i, 0))
    return pl.pallas_call(
        functools.partial(_moe_kernel, alpha),
        grid=(n // tm, ne // per_step),
        in_specs=[
            pl.BlockSpec((tm, d), lambda i, e: (i, 0)),
            pl.BlockSpec((tm, LANES), lambda i, e: (i, 0)),
            mods_spec,
            pl.BlockSpec(ln_pack.shape, lambda i, e: (0, 0)),
            pl.BlockSpec((None, per_step, d, f), lambda i, e: (layer, e, 0, 0)),
            pl.BlockSpec((None, per_step, d, f), lambda i, e: (layer, e, 0, 0)),
            pl.BlockSpec((None, per_step, f, d), lambda i, e: (layer, e, 0, 0)),
        ],
        out_specs=pl.BlockSpec((tm, d), lambda i, e: (i, 0)),
        out_shape=jax.ShapeDtypeStruct((n, d), F32),
        scratch_shapes=[pltpu.VMEM((tm, d), BF16), pltpu.VMEM((tm, d), F32)],
        compiler_params=_params(("parallel", "arbitrary")),
        name="hmoe",
    )(x1, comb, mods, ln_pack, wg, wu, wd)


def _fox_bias_placement(d):
    place_q = np.zeros((LANES, d), np.float32)
    place_k = np.zeros((LANES, d), np.float32)
    const_q = np.zeros((1, d), np.float32)
    const_k = np.zeros((1, d), np.float32)
    for h in range(d // HEAD_DIM):
        base = (h // 2) * LANES + (HEAD_DIM if h % 2 == 0 else 0)
        for term in range(3):
            place_q[term * TERM_STRIDE + h, base + term] = 1.0
            place_k[term * TERM_STRIDE + h, base + 3 + term] = -1.0
            const_q[0, base + 3 + term] = 1.0
            const_k[0, base + term] = 1.0
    return (jnp.asarray(place_q, BF16), jnp.asarray(place_k, BF16), jnp.asarray(const_q),
            jnp.asarray(const_k))


def _kv_kernel(x_ref, kvw_ref, fw_ref, fb_ref, tri_ref, pq_ref, pk_ref, cq_ref, ck_ref,
               k_ref, v_ref, kb_ref, vb_ref, lf_ref, fc_ref, qbias_ref, kbias_ref, vbt_ref, carry_ref):
    d = x_ref.shape[1]

    @pl.when(pl.program_id(1) == 0)
    def _():
        carry_ref[...] = jnp.zeros_like(carry_ref)

    x = x_ref[...]
    kv = _bdot(x, kvw_ref[...])
    k, v = kv[:, :d], kv[:, d:]
    k_ref[...] = k.T
    v_t = v.T
    v_ref[...] = v_t
    vbt_ref[...] = v_t.astype(BF16)
    kb_ref[...] = k.astype(BF16)
    vb_ref[...] = v.astype(BF16)
    z = _dot3(x, fw_ref[...]) + fb_ref[...]
    lf = -_softplus(-z)
    lf_ref[...] = lf
    fc = _dot_exact_lhs(tri_ref[...], lf) + carry_ref[...]
    fc_ref[...] = fc
    carry_ref[...] = fc[fc.shape[0] - 1:, :]
    lane = lax.broadcasted_iota(jnp.int32, fc.shape, 1)
    hi, mid, lo = _split3(jnp.where(lane < d // HEAD_DIM, fc * LOG2E, 0.0))
    terms = (hi.astype(F32) + pltpu.roll(mid.astype(F32), TERM_STRIDE, axis=1)
             + pltpu.roll(lo.astype(F32), 2 * TERM_STRIDE, axis=1)).astype(BF16)
    qbias_ref[...] = (_dg(terms, pq_ref[...], 1, 0) + cq_ref[...]).astype(BF16)
    kbias_ref[...] = (_dg(terms, pk_ref[...], 1, 0) + ck_ref[...]).astype(BF16)


def _kv_proj(x2d, kv_w, f_w, f_b, bsz, t, tm):
    n, d = x2d.shape
    nt = t // tm
    tok = pl.BlockSpec((tm, d), lambda b, i: (b * nt + i, 0))
    nar = pl.BlockSpec((tm, LANES), lambda b, i: (b * nt + i, 0))
    feature_major = pl.BlockSpec((None, d, tm), lambda b, i: (b, 0, i))
    tri = jnp.tril(jnp.ones((tm, tm), BF16))
    full = lambda a: pl.BlockSpec(a.shape, lambda b, i: (0,) * a.ndim)
    consts = [kv_w, f_w, f_b, tri, *_fox_bias_placement(d)]
    return pl.pallas_call(
        _kv_kernel,
        grid=(bsz, nt),
        in_specs=[tok] + [full(a) for a in consts],
        out_specs=[feature_major, feature_major, tok, tok, nar, nar, tok, tok, feature_major],
        out_shape=[jax.ShapeDtypeStruct((bsz, d, t), F32), jax.ShapeDtypeStruct((bsz, d, t), F32),
                   jax.ShapeDtypeStruct((n, d), BF16), jax.ShapeDtypeStruct((n, d), BF16),
                   jax.ShapeDtypeStruct((n, LANES), F32), jax.ShapeDtypeStruct((n, LANES), F32),
                   jax.ShapeDtypeStruct((n, d), BF16), jax.ShapeDtypeStruct((n, d), BF16),
                   jax.ShapeDtypeStruct((bsz, d, t), BF16)],
        scratch_shapes=[pltpu.VMEM((1, LANES), F32)],
        compiler_params=_params(("parallel", "arbitrary")),
        name="kv_proj",
    )(x2d, *consts)


def _q_kernel(scale, x_ref, mods_ref, wq_ref, q_ref):
    h = x_ref[...] * (1.0 + mods_ref[1:2, :]) + mods_ref[0:1, :]
    q_ref[...] = (_bdot(h, wq_ref[...]) * scale).astype(BF16)


def _q_proj(x2d, mods_l, wq, scale, bsz, t, tm):
    n, d = x2d.shape
    nt = t // tm
    tok = pl.BlockSpec((tm, d), lambda b, i: (b * nt + i, 0))
    return pl.pallas_call(
        functools.partial(_q_kernel, scale),
        grid=(bsz, nt),
        in_specs=[tok, pl.BlockSpec((None, 6, d), lambda b, i: (b, 0, 0)),
                  pl.BlockSpec(wq.shape, lambda b, i: (0, 0))],
        out_specs=tok,
        out_shape=jax.ShapeDtypeStruct((n, d), BF16),
        compiler_params=_params(("parallel", "parallel")),
        name="q_proj",
    )(x2d, mods_l, wq)


def _fox_prompt_kernel(tq, q_ref, k_ref, vt_ref, qb_ref, kb_ref, o_ref):
    t = q_ref.shape[0]
    lane = lax.broadcasted_iota(jnp.int32, (1, LANES), 1)
    half = tq // 2
    key = lax.broadcasted_iota(jnp.int32, (half, tq), 0)
    qry = lax.broadcasted_iota(jnp.int32, (half, tq), 1)
    key_b = lax.broadcasted_iota(jnp.int32, (half, half), 0)
    qry_b = lax.broadcasted_iota(jnp.int32, (half, half), 1)
    own = [lane < HEAD_DIM, lane >= HEAD_DIM]
    k_aug = [jnp.where(own[hh], k_ref[...], kb_ref[...]) for hh in range(2)]
    v_t = vt_ref[...]
    for qi in range(t // tq):
        lo, mid, hi = qi * tq, qi * tq + half, (qi + 1) * tq
        heads = []
        for hh in range(2):
            q = jnp.where(own[hh], q_ref[lo:hi, :], qb_ref[lo:hi, :])
            v_h = v_t[hh * HEAD_DIM:(hh + 1) * HEAD_DIM]
            s_a = jnp.where(key <= qry, _dg(k_aug[hh][lo:mid], q, 1, 1), NEG_BIG)
            s_b = jnp.where(key_b <= qry_b, _dg(k_aug[hh][mid:hi], q[half:], 1, 1), NEG_BIG)
            m = jnp.max(s_a, 0, keepdims=True)
            if qi > 0:
                s_past = _dg(k_aug[hh][:lo], q, 1, 1)
                m = jnp.maximum(m, jnp.max(s_past, 0, keepdims=True))
            m_b = jnp.maximum(m[:, half:], jnp.max(s_b, 0, keepdims=True))
            m = jnp.concatenate([m[:, :half], m_b], 1)
            p = jnp.exp2(s_a - m)
            num = _dg(v_h[:, lo:mid], p.astype(BF16), 1, 0)
            den = jnp.sum(p, 0, keepdims=True)
            if qi > 0:
                p = jnp.exp2(s_past - m)
                num = num + _dg(v_h[:, :lo], p.astype(BF16), 1, 0)
                den = den + jnp.sum(p, 0, keepdims=True)
            p = jnp.exp2(s_b - m_b)
            num_b = num[:, half:] + _dg(v_h[:, mid:hi], p.astype(BF16), 1, 0)
            den_b = den[:, half:] + jnp.sum(p, 0, keepdims=True)
            heads.append(jnp.concatenate([num[:, :half] / den[:, :half], num_b / den_b], 1))
        o_ref[lo:hi, :] = jnp.concatenate(heads, 0).T


def _fox_prompt(q, kb, vb_t, q_bias, k_bias, bsz, t, tq):
    n, d = q.shape
    seq = pl.BlockSpec((t, LANES), lambda b, p: (b, p))
    return pl.pallas_call(
        functools.partial(_fox_prompt_kernel, tq),
        grid=(bsz, d // LANES),
        in_specs=[seq, seq, pl.BlockSpec((None, LANES, t), lambda b, p: (b, p, 0)), seq, seq],
        out_specs=seq,
        out_shape=jax.ShapeDtypeStruct((n, d), F32),
        compiler_params=_params(("parallel", "parallel")),
        name="fox_prompt",
    )(q, kb, vb_t, q_bias, k_bias)


def _fox_sample_kernel(q_ref, kn_ref, vn_ref, fn_ref, fnt_ref, ck_ref, cv_ref, clf_ref, upper_ref,
                       o_ref, m_ref, l_ref, acc_ref, carry_ref):
    step = pl.program_id(1)
    t, d = q_ref.shape
    nh = d // HEAD_DIM

    @pl.when(step == 0)
    def _():
        m_ref[...] = jnp.full_like(m_ref, NEG_BIG)
        l_ref[...] = jnp.zeros_like(l_ref)
        acc_ref[...] = jnp.zeros_like(acc_ref)
        carry_ref[...] = jnp.zeros_like(carry_ref)

    heads = range(nh)
    cols = [slice(h * HEAD_DIM, (h + 1) * HEAD_DIM) for h in heads]
    q = [q_ref[:, c] for c in cols]

    def attend(keys, values, feature_major, bias, mask):
        s = [_dg(q[h], keys[h], 1, 0 if feature_major else 1) + bias[h] for h in heads]
        if mask is not None:
            s = [jnp.where(mask, x, NEG_BIG) for x in s]
        m_old = m_ref[...]
        m_new = jnp.maximum(m_old, jnp.stack([jnp.max(x, -1, keepdims=True) for x in s]))
        alpha = jnp.exp(m_old - m_new)
        p = [jnp.exp(x - m_new[h]) for h, x in enumerate(s)]
        l_ref[...] = alpha * l_ref[...] + jnp.stack([jnp.sum(x, -1, keepdims=True) for x in p])
        acc_ref[...] = alpha * acc_ref[...] + jnp.stack(
            [_dg(p[h].astype(BF16), values[h], 1, 1 if feature_major else 0) for h in heads])
        m_ref[...] = m_new

    clf = clf_ref[...]
    suf = _dot_exact_rhs(clf, upper_ref[...]) + carry_ref[...]
    carry_ref[...] = carry_ref[...] + jnp.sum(clf, -1, keepdims=True)
    fn = fn_ref[...]
    attend([ck_ref[h].astype(BF16) for h in heads], [cv_ref[h].astype(BF16) for h in heads], True,
           [fn[:, h:h + 1] + suf[h:h + 1, :] for h in heads], None)

    @pl.when(step == pl.num_programs(1) - 1)
    def _():
        rr = lax.broadcasted_iota(jnp.int32, (t, t), 0)
        cc = lax.broadcasted_iota(jnp.int32, (t, t), 1)
        fnt = fnt_ref[...]
        attend([kn_ref[:, c] for c in cols], [vn_ref[:, c] for c in cols], False,
               [fn[:, h:h + 1] - fnt[h:h + 1, :] for h in heads], cc <= rr)
        out = acc_ref[...] / l_ref[...]
        o_ref[...] = jnp.concatenate([out[h] for h in heads], 1)


def _fox_sample(q, kb, vb, fcum, cache_k, cache_v, cache_logf, bsz, t, tk):
    n, d = q.shape
    nh = d // HEAD_DIM
    plen = cache_logf.shape[1]
    nk = plen // tk
    ck = cache_k.transpose(0, 2, 3, 1)
    cv = cache_v.transpose(0, 2, 3, 1)
    clf_t = cache_logf.astype(F32).transpose(0, 2, 1)
    fn_t = fcum[:, :nh].reshape(bsz, t, nh).transpose(0, 2, 1)
    upper = (jnp.arange(tk)[:, None] > jnp.arange(tk)[None, :]).astype(BF16)
    tok = pl.BlockSpec((t, d), lambda b, j: (b, 0))
    past = pl.BlockSpec((None, nh, HEAD_DIM, tk), lambda b, j: (b, 0, 0, nk - 1 - j))
    return pl.pallas_call(
        _fox_sample_kernel,
        grid=(bsz, nk),
        in_specs=[tok, tok, tok,
                  pl.BlockSpec((t, LANES), lambda b, j: (b, 0)),
                  pl.BlockSpec((None, nh, t), lambda b, j: (b, 0, 0)),
                  past, past,
                  pl.BlockSpec((None, nh, tk), lambda b, j: (b, 0, nk - 1 - j)),
                  pl.BlockSpec((tk, tk), lambda b, j: (0, 0))],
        out_specs=tok,
        out_shape=jax.ShapeDtypeStruct((n, d), F32),
        scratch_shapes=[pltpu.VMEM((nh, t, 1), F32), pltpu.VMEM((nh, t, 1), F32),
                        pltpu.VMEM((nh, t, HEAD_DIM), F32), pltpu.VMEM((nh, 1), F32)],
        compiler_params=_params(("parallel", "arbitrary")),
        name="fox_sample",
    )(q, kb, vb, fcum, fn_t, ck, cv, clf_t, upper)


def _tile(t, cap):
    return min(t, cap)


def _trunk(x, mods, shift_in, wkv_in, cache, w):
    bsz, t, d = x.shape
    depth = mods.shape[0]
    n_a = w["n_a"]
    nh = d // HEAD_DIM
    x2d = x.reshape(bsz * t, d)
    shifts, states = [], []
    kv = None
    for l in range(depth):
        mods_l = mods[l]
        if l < n_a:
            wl = w["rwkv"][l]
            proj, last = _rwkv_proj(x2d, shift_in[l], mods_l, wl, bsz, t, _tile(t, 512))
            y, s_new = _wkv(proj[:6], wkv_in[l], wl, bsz, t)
            shifts.append(last)
            states.append(s_new.astype(wkv_in.dtype))
            mixer_out, gate, w_o = y, proj[6], wl["w_o"]
        else:
            if kv is None:
                kv = _kv_proj(x2d, w["kv_w"], w["f_w"], w["f_b"], bsz, t, _tile(t, 512))
            k_sh, v_sh, kb, vb, lf, fcum, q_bias, k_bias, vb_t = kv
            j = l - n_a
            q_scale = HEAD_DIM ** -0.5 * (LOG2E if cache is None else 1.0)
            q = _q_proj(x2d, mods_l, w["b_wq"][j], q_scale, bsz, t, _tile(t, 1024))
            if cache is None:
                mixer_out = _fox_prompt(q, kb, vb_t, q_bias, k_bias, bsz, t, _tile(t, 1024))
            else:
                mixer_out = _fox_sample(q, kb, vb, fcum, cache[0], cache[1], cache[2], bsz, t,
                                        _tile(cache[0].shape[1], 1024))
            gate, w_o = None, w["b_wo"][j]
        x1, comb = _post_mixer(x2d, mixer_out, gate, mods_l, w_o, w["ln"][l][0], w["router_w"][l],
                               w["router_b"][l], w["alpha"], bsz, t, _tile(t, 1024))
        x2d = _moe(x1, comb, mods_l, w["ln"][l][1], w["exp_wg"], w["exp_wu"], w["exp_wd"], l,
                   w["alpha"], bsz, t)
    k_sh, v_sh, lf = kv[0], kv[1], kv[4]
    by_head = lambda z: z.reshape(bsz, nh, HEAD_DIM, t).transpose(0, 3, 1, 2)
    return (x2d.reshape(bsz, t, d), jnp.stack(shifts), jnp.stack(states), by_head(k_sh), by_head(v_sh),
            lf[:, :nh].reshape(bsz, t, nh).astype(x.dtype))


def _prepare_weights(ln_g, ln_b, a_mu, a_w_rkv, a_w0, a_w1, a_w2, a_a0, a_a1, a_a2, a_g1, a_g2, a_k_k,
                     a_k_a, a_r_k, a_lnx_g, a_lnx_b, a_w_o, kv_w, f_w, f_b, b_wq, b_wo, rg_w, rg_b,
                     re_w, re_b, exp_wg, exp_wu, exp_wd):
    depth, _, d = ln_g.shape
    n_a = a_mu.shape[0]
    nh = d // HEAD_DIM
    alpha = (2.0 * depth) ** 0.25
    zrow = jnp.zeros((d,), F32)
    head_of = jnp.arange(d) // HEAD_DIM
    head_down = (head_of[:, None] == jnp.arange(LANES)[None, :]).astype(BF16)
    rwkv = []
    for l in range(n_a):
        rwkv.append(dict(
            mu=a_mu[l],
            vec=jnp.stack([a_w0[l], a_a0[l], a_k_k[l], a_k_a[l], zrow, zrow, zrow, zrow]),
            scan_vec=jnp.stack([a_r_k[l].reshape(d), a_lnx_g[l], a_lnx_b[l], zrow, zrow, zrow, zrow, zrow]),
            w_rkv=a_w_rkv[l].astype(BF16), w1=a_w1[l].astype(BF16), w2=a_w2[l].astype(BF16),
            a1=a_a1[l].astype(BF16), a2=a_a2[l].astype(BF16), g1=a_g1[l].astype(BF16),
            g2=a_g2[l].astype(BF16), w_o=a_w_o[l].astype(BF16),
            head_down=head_down, head_up=head_down.T))
    ln = [[jnp.stack([ln_g[l, s], ln_b[l, s], zrow, zrow, zrow, zrow, zrow, zrow])
           for s in range(2)] for l in range(depth)]
    pad = ROUTER_ROWS - N_GROUPS - N_EXPERTS
    router_w = [jnp.concatenate([rg_w[l], re_w[l], jnp.zeros((d, pad), F32)], 1).T for l in range(depth)]
    router_b = [jnp.concatenate([rg_b[l], re_b[l], jnp.zeros((pad,), F32)])[:, None] for l in range(depth)]
    return dict(
        n_a=n_a, alpha=alpha, rwkv=rwkv, ln=ln, router_w=router_w, router_b=router_b,
        kv_w=kv_w.astype(BF16),
        f_w=jnp.concatenate([f_w, jnp.zeros((d, LANES - nh), F32)], 1),
        f_b=jnp.concatenate([f_b, jnp.zeros((LANES - nh,), F32)])[None, :],
        b_wq=b_wq.astype(BF16), b_wo=b_wo.astype(BF16),
        exp_wg=exp_wg, exp_wu=exp_wu, exp_wd=exp_wd)


def kernel(x_prompt, x_sample, state_shift, state_wkv, cache_k, cache_v, cache_logf, c_prompt, c_sample,
           ada_w, ada_b, ln_g, ln_b, a_mu, a_w_rkv, a_w0, a_w1, a_w2, a_a0, a_a1, a_a2, a_g1, a_g2, a_k_k,
           a_k_a, a_r_k, a_lnx_g, a_lnx_b, a_w_o, kv_w, f_w, f_b, b_wq, b_wo, rg_w, rg_b, re_w, re_b,
           exp_wg, exp_wu, exp_wd):
    w = _prepare_weights(ln_g, ln_b, a_mu, a_w_rkv, a_w0, a_w1, a_w2, a_a0, a_a1, a_a2, a_g1, a_g2,
                         a_k_k, a_k_a, a_r_k, a_lnx_g, a_lnx_b, a_w_o, kv_w, f_w, f_b, b_wq, b_wo,
                         rg_w, rg_b, re_w, re_b, exp_wg, exp_wu, exp_wd)
    bp, _, d = x_prompt.shape
    n_a = a_mu.shape[0]
    nh = d // HEAD_DIM
    mods = _ada_mods(jnp.concatenate([c_prompt, c_sample], 0), ada_w, ada_b)
    zero_shift = jnp.zeros((n_a, bp, d), x_prompt.dtype)
    zero_wkv = jnp.zeros((n_a, bp, nh, HEAD_DIM, HEAD_DIM), x_prompt.dtype)
    outs_p = _trunk(x_prompt, mods[:, :bp], zero_shift, zero_wkv, None, w)
    outs_s = _trunk(x_sample, mods[:, bp:], state_shift, state_wkv, (cache_k, cache_v, cache_logf), w)
    y_p, p_shift, p_wkv, p_k, p_v, p_logf = outs_p
    y_s, s_shift, s_wkv, s_k, s_v, s_logf = outs_s
    return (y_p, y_s, p_shift, p_wkv, p_k, p_v, p_logf, s_shift, s_wkv, s_k, s_v, s_logf)
```
